```python
import jax
import jax.numpy as jnp
from jax import lax
import numpy as np

D_MODEL = 1024
BATCH = 2
SEQ = 8192
DEPTH = 2
DEC_BATCH = 32
DEC_SEQ = 4
PAST_LEN = 8192
PAGE_SIZE = 128

CHUNK = 128
A_GROUPS = 4
A_GROUP_DIM = 128
A_WIDTH = A_GROUPS * A_GROUP_DIM
HEAD_DIM = 64
KV_HEADS = 4
DIL_PAIRS = ((128, 1), (512, 4), (2048, 16))
N_DIL = len(DIL_PAIRS)
Q_HEADS = N_DIL * KV_HEADS
Q_W = Q_HEADS * HEAD_DIM
KV_W = KV_HEADS * HEAD_DIM
BAND = 128
MAX_WINDOW = 2048
C_WIDTH = 512
CONV_W = 3
D_FF = -(-(8 * D_MODEL) // (3 * 256)) * 256
EPS = 1e-6

SPLIT_SIZES = (A_WIDTH, A_WIDTH, Q_W, KV_W, KV_W, C_WIDTH, C_WIDTH, C_WIDTH, D_MODEL, D_MODEL, D_MODEL)
SPLIT_IDX = tuple(sum(SPLIT_SIZES[:i + 1]) for i in range(len(SPLIT_SIZES) - 1))
IN_W = sum(SPLIT_SIZES)

kernel_name = 'hybrid_gated_gmlp_dilated_conv_decoder_step'


def rms_norm(x, g):
    xf = x.astype(jnp.float32)
    y = xf * lax.rsqrt(jnp.mean(xf * xf, axis=-1, keepdims=True) + EPS)
    return (y * g).astype(x.dtype)


def layer_norm(x, g, b):
    xf = x.astype(jnp.float32)
    mu = jnp.mean(xf, axis=-1, keepdims=True)
    xc = xf - mu
    y = xc * lax.rsqrt(jnp.mean(xc * xc, axis=-1, keepdims=True) + EPS)
    return (y * g + b).astype(x.dtype)


def alibi_slopes():
    s = [2.0 ** (-8.0 * (i + 1) / Q_HEADS) for i in range(Q_HEADS)]
    return jnp.asarray(s, jnp.float32).reshape(N_DIL, KV_HEADS)


def split_proj(h, w_in):
    z = jnp.einsum('bsd,de->bse', h, w_in)
    return jnp.split(z, SPLIT_IDX, axis=-1)


def gmlp_inputs(u_raw, v_raw, ln_g, ln_b):
    return jax.nn.gelu(u_raw), layer_norm(jax.nn.gelu(v_raw), ln_g, ln_b)


def spatial_gate(v, w_s, b_s):
    L = v.shape[-3]
    w = jnp.tril(w_s[:, :L, :L])
    return jnp.einsum('gts,...sgc->...tgc', w, v) + jnp.transpose(b_s[:, :L])[:, :, None]


def banded_group(q, k, v, slopes, dil):
    Bn, S, H, D = q.shape
    span = dil * BAND
    Sp = -(-S // span) * span
    J = Sp // dil
    nb = J // BAND

    def to_blocks(a):
        a = jnp.pad(a, ((0, 0), (0, Sp - S), (0, 0), (0, 0)))
        a = a.reshape(Bn, J, dil, H, D).transpose(0, 2, 1, 3, 4)
        return a.reshape(Bn, dil, nb, BAND, H, D)

    def with_prev(a):
        prev = jnp.pad(a, ((0, 0), (0, 0), (1, 0), (0, 0), (0, 0), (0, 0)))[:, :, :-1]
        return jnp.concatenate([prev, a], axis=3)

    qb = to_blocks(q)
    kb = with_prev(to_blocks(k))
    vb = with_prev(to_blocks(v))
    s = jnp.einsum('brnqhd,brnkhd->brnhqk', qb, kb, preferred_element_type=jnp.float32) * (HEAD_DIM ** -0.5)
    dist = (jnp.arange(BAND)[:, None] + BAND) - jnp.arange(2 * BAND)[None, :]
    valid = ((dist >= 0) & (dist <= BAND))[None] & (
        (jnp.arange(nb)[:, None, None] > 0) | (jnp.arange(2 * BAND)[None, None, :] >= BAND))
    s = s - (slopes * dil)[:, None, None] * dist.astype(jnp.float32)
    s = jnp.where(valid[None, None, :, None], s, -jnp.inf)
    m = jnp.max(s, axis=-1, keepdims=True)
    p = jnp.exp(s - m)
    den = jnp.sum(p, axis=-1)
    o = jnp.einsum('brnhqk,brnkhd->brnqhd', p, vb) / jnp.transpose(den, (0, 1, 2, 4, 3))[..., None]
    lse = m[..., 0] + jnp.log(den)
    o = o.reshape(Bn, dil, J, H, D).transpose(0, 2, 1, 3, 4).reshape(Bn, Sp, H, D)[:, :S]
    lse = jnp.transpose(lse, (0, 1, 2, 4, 3)).reshape(Bn, dil, J, H).transpose(0, 2, 1, 3).reshape(Bn, Sp, H)[:, :S]
    return o, lse


def combine_groups(outs, lses):
    w = jax.nn.softmax(jnp.stack(lses), axis=0)
    return jnp.sum(w[..., None] * jnp.stack(outs), axis=0)


def dilated_attn_prompt(q, k, v):
    slopes = alibi_slopes()
    outs, lses = [], []
    for g, (_, dil) in enumerate(DIL_PAIRS):
        o, l = banded_group(q[:, :, g], k, v, slopes[g], dil)
        outs.append(o)
        lses.append(l)
    return combine_groups(outs, lses).astype(v.dtype)


def dilated_attn_sample(q, k_all, v_all, L):
    T = q.shape[1]
    slopes = alibi_slopes()
    taps = jnp.arange(BAND + 1)
    outs, lses = [], []
    for g, (_, dil) in enumerate(DIL_PAIRS):
        idx = L + jnp.arange(T)[:, None] - dil * taps[None, :]
        valid = idx >= 0
        idx = jnp.maximum(idx, 0)
        kg = k_all[:, idx]
        vg = v_all[:, idx]
        s = jnp.einsum('bthd,btkhd->bthk', q[:, :, g], kg, preferred_element_type=jnp.float32) * (HEAD_DIM ** -0.5)
        s = s - (slopes[g] * dil)[:, None] * taps.astype(jnp.float32)[None, :]
        s = jnp.where(valid[None, :, None, :], s, -jnp.inf)
        m = jnp.max(s, axis=-1, keepdims=True)
        p = jnp.exp(s - m)
        den = jnp.sum(p, axis=-1)
        outs.append(jnp.einsum('bthk,btkhd->bthd', p, vg) / den[..., None])
        lses.append(m[..., 0] + jnp.log(den))
    return combine_groups(outs, lses).astype(v_all.dtype)


def short_conv(z, prev, w, b):
    T = z.shape[1]
    zp = jnp.concatenate([prev.astype(z.dtype), z], axis=1)
    y = b + sum(w[i] * zp[:, i:i + T] for i in range(CONV_W))
    return y, zp[:, T:]


def merge_branches(out_a, out_b, out_c, ga, gb, gc, b_gate, w_br_a, w_br_b, w_br_c, w_o):
    bga, bgb, bgc = jnp.split(b_gate, 3)
    merged = (jax.nn.sigmoid(ga + bga) * (out_a @ w_br_a)
              + jax.nn.sigmoid(gb + bgb) * (out_b @ w_br_b)
              + jax.nn.sigmoid(gc + bgc) * (out_c @ w_br_c))
    return merged @ w_o


def mixer_prompt(h, w_in, a_ln_g, a_ln_b, a_ws, a_bs, c_conv_w, c_conv_b, w_br_a, w_br_b, w_br_c, b_gate, w_o):
    Bn, S, _ = h.shape
    ua, va, q, k, v, cx, cb, cc, ga, gb, gc = split_proj(h, w_in)
    u, vn = gmlp_inputs(ua, va, a_ln_g, a_ln_b)
    sa = spatial_gate(vn.reshape(Bn, S // CHUNK, CHUNK, A_GROUPS, A_GROUP_DIM), a_ws, a_bs).reshape(Bn, S, A_WIDTH)
    out_a = u * sa
    k = k.reshape(Bn, S, KV_HEADS, HEAD_DIM)
    v = v.reshape(Bn, S, KV_HEADS, HEAD_DIM)
    out_b = dilated_attn_prompt(q.reshape(Bn, S, N_DIL, KV_HEADS, HEAD_DIM), k, v).reshape(Bn, S, KV_W)
    conv, conv_state = short_conv(cc * cx, jnp.zeros((Bn, CONV_W - 1, C_WIDTH), h.dtype), c_conv_w, c_conv_b)
    out_c = cb * conv
    y = merge_branches(out_a, out_b, out_c, ga, gb, gc, b_gate, w_br_a, w_br_b, w_br_c, w_o)
    win = min(MAX_WINDOW, S)
    return y, k[:, S - win:], v[:, S - win:], conv_state


def mixer_sample(h, cache_k, cache_v, conv_prev, w_in, a_ln_g, a_ln_b, a_ws, a_bs, c_conv_w, c_conv_b,
                 w_br_a, w_br_b, w_br_c, b_gate, w_o):
    Bn, T, _ = h.shape
    ua, va, q, k, v, cx, cb, cc, ga, gb, gc = split_proj(h, w_in)
    u, vn = gmlp_inputs(ua, va, a_ln_g, a_ln_b)
    sa = spatial_gate(vn.reshape(Bn, T, A_GROUPS, A_GROUP_DIM), a_ws, a_bs).reshape(Bn, T, A_WIDTH)
    out_a = u * sa
    k = k.reshape(Bn, T, KV_HEADS, HEAD_DIM)
    v = v.reshape(Bn, T, KV_HEADS, HEAD_DIM)
    L = cache_k.shape[1]
    k_all = jnp.concatenate([cache_k.astype(k.dtype), k], axis=1)
    v_all = jnp.concatenate([cache_v.astype(v.dtype), v], axis=1)
    out_b = dilated_attn_sample(q.reshape(Bn, T, N_DIL, KV_HEADS, HEAD_DIM), k_all, v_all, L).reshape(Bn, T, KV_W)
    conv, conv_state = short_conv(cc * cx, conv_prev, c_conv_w, c_conv_b)
    out_c = cb * conv
    y = merge_branches(out_a, out_b, out_c, ga, gb, gc, b_gate, w_br_a, w_br_b, w_br_c, w_o)
    return y, k, v, conv_state, vn


def swiglu(h, wg, wu, wd):
    return (jax.nn.silu(h @ wg) * (h @ wu)) @ wd


def setup_inputs(seed: int = 0) -> dict:
    key = jax.random.key(seed)
    ks = jax.random.split(key, 24)
    f32 = jnp.float32

    def nrm(i, shape, scale):
        return jax.random.normal(ks[i], shape, f32) * scale

    win = min(MAX_WINDOW, PAST_LEN)
    return {
        'x_prompt': nrm(0, (BATCH, SEQ, D_MODEL), 1.0),
        'x_sample': nrm(1, (DEC_BATCH, DEC_SEQ, D_MODEL), 1.0),
        'cache_k_win': nrm(2, (DEPTH, DEC_BATCH, win, KV_HEADS, HEAD_DIM), 1.0),
        'cache_v_win': nrm(3, (DEPTH, DEC_BATCH, win, KV_HEADS, HEAD_DIM), 1.0),
        'state_conv': nrm(4, (DEPTH, DEC_BATCH, CONV_W - 1, C_WIDTH), 1.0),
        'g_pre_mix': 1.0 + nrm(5, (DEPTH, D_MODEL), 0.02),
        'g_post_mix': 1.0 + nrm(6, (DEPTH, D_MODEL), 0.02),
        'g_pre_ffn': 1.0 + nrm(7, (DEPTH, D_MODEL), 0.02),
        'g_post_ffn': 1.0 + nrm(8, (DEPTH, D_MODEL), 0.02),
        'w_in': nrm(9, (DEPTH, D_MODEL, IN_W), D_MODEL ** -0.5),
        'a_ln_g': 1.0 + nrm(10, (DEPTH, A_WIDTH), 0.02),
        'a_ln_b': nrm(11, (DEPTH, A_WIDTH), 0.02),
        'a_ws': nrm(12, (DEPTH, A_GROUPS, CHUNK, CHUNK), CHUNK ** -0.5),
        'a_bs': 1.0 + nrm(13, (DEPTH, A_GROUPS, CHUNK), 0.02),
        'c_conv_w': nrm(14, (DEPTH, CONV_W, C_WIDTH), CONV_W ** -0.5),
        'c_conv_b': nrm(15, (DEPTH, C_WIDTH), 0.02),
        'w_br_a': nrm(16, (DEPTH, A_WIDTH, D_MODEL), A_WIDTH ** -0.5),
        'w_br_b': nrm(17, (DEPTH, KV_W, D_MODEL), KV_W ** -0.5),
        'w_br_c': nrm(18, (DEPTH, C_WIDTH, D_MODEL), C_WIDTH ** -0.5),
        'b_gate': nrm(19, (DEPTH, 3 * D_MODEL), 0.02),
        'w_o': nrm(20, (DEPTH, D_MODEL, D_MODEL), D_MODEL ** -0.5),
        'w_ff_gate': nrm(21, (DEPTH, D_MODEL, D_FF), D_MODEL ** -0.5),
        'w_ff_up': nrm(22, (DEPTH, D_MODEL, D_FF), D_MODEL ** -0.5),
        'w_ff_down': nrm(23, (DEPTH, D_FF, D_MODEL), D_FF ** -0.5),
    }


def reference(x_prompt, x_sample, cache_k_win, cache_v_win, state_conv,
              g_pre_mix, g_post_mix, g_pre_ffn, g_post_ffn, w_in,
              a_ln_g, a_ln_b, a_ws, a_bs, c_conv_w, c_conv_b,
              w_br_a, w_br_b, w_br_c, b_gate, w_o, w_ff_gate, w_ff_up, w_ff_down):
    xp, xs = x_prompt, x_sample
    kp_l, vp_l, ks_l, vs_l, cp_l, cs_l, av_l = [], [], [], [], [], [], []
    for l in range(DEPTH):
        mix_w = (w_in[l], a_ln_g[l], a_ln_b[l], a_ws[l], a_bs[l], c_conv_w[l], c_conv_b[l],
                 w_br_a[l], w_br_b[l], w_br_c[l], b_gate[l], w_o[l])
        mp, kp, vp, cp = mixer_prompt(rms_norm(xp, g_pre_mix[l]), *mix_w)
        ms, kn, vnw, cs, av = mixer_sample(rms_norm(xs, g_pre_mix[l]), cache_k_win[l], cache_v_win[l],
                                           state_conv[l], *mix_w)
        xp = xp + rms_norm(mp, g_post_mix[l])
        xs = xs + rms_norm(ms, g_post_mix[l])
        xp = xp + rms_norm(swiglu(rms_norm(xp, g_pre_ffn[l]), w_ff_gate[l], w_ff_up[l], w_ff_down[l]), g_post_ffn[l])
        xs = xs + rms_norm(swiglu(rms_norm(xs, g_pre_ffn[l]), w_ff_gate[l], w_ff_up[l], w_ff_down[l]), g_post_ffn[l])
        kp_l.append(kp)
        vp_l.append(vp)
        ks_l.append(kn)
        vs_l.append(vnw)
        cp_l.append(cp)
        cs_l.append(cs)
        av_l.append(av)
    return (xp, xs, jnp.stack(kp_l), jnp.stack(vp_l), jnp.stack(ks_l), jnp.stack(vs_l),
            jnp.stack(cp_l), jnp.stack(cs_l), jnp.stack(av_l))
```

```python
import functools

import jax
import jax.numpy as jnp
from jax import lax
from jax.experimental import pallas as pl
from jax.experimental.pallas import tpu as pltpu

F32 = jnp.float32
BF16 = jnp.bfloat16

D_MODEL = 1024
BATCH = 2
SEQ = 8192
DEPTH = 2
DEC_BATCH = 32
DEC_SEQ = 4
CHUNK = 128
A_GROUPS = 4
A_GROUP_DIM = 128
A_WIDTH = A_GROUPS * A_GROUP_DIM
HEAD_DIM = 64
KV_HEADS = 4
DILS = (1, 4, 16)
N_DIL = len(DILS)
Q_W = N_DIL * KV_HEADS * HEAD_DIM
KV_W = KV_HEADS * HEAD_DIM
BAND = 128
MAX_WINDOW = 2048
C_WIDTH = 512
CONV_W = 3
D_FF = 2816
EPS = 1e-6

OFF_U = 0
OFF_V = OFF_U + A_WIDTH
OFF_Q = OFF_V + A_WIDTH
OFF_K = OFF_Q + Q_W
OFF_VV = OFF_K + KV_W
OFF_CX = OFF_VV + KV_W
OFF_CB = OFF_CX + C_WIDTH
OFF_CC = OFF_CB + C_WIDTH
OFF_G = OFF_CC + C_WIDTH
IN_W = OFF_G + 3 * D_MODEL

LANES = 128
SUBLANES = 8
VMEM_LIMIT = 56 * 1024 * 1024
NEG = -1e30

SAMPLE_ROWS = DEC_BATCH * DEC_SEQ
SPAN = BAND * DILS[-1]
PROMPT_TM = 512

SLOPES = tuple(
    tuple(2.0 ** (-8.0 * (g * KV_HEADS + h + 1) / (N_DIL * KV_HEADS)) for h in range(KV_HEADS))
    for g in range(N_DIL)
)


def _rms(x, g):
    return x * lax.rsqrt(jnp.mean(x * x, axis=-1, keepdims=True) + EPS) * g


def _layer_norm(x, g, b):
    mu = jnp.mean(x, axis=-1, keepdims=True)
    xc = x - mu
    return xc * lax.rsqrt(jnp.mean(xc * xc, axis=-1, keepdims=True) + EPS) * g + b


def _dot(a, b):
    return jnp.dot(a, b, preferred_element_type=F32)


def _dot_nt(a, b):
    return lax.dot_general(a, b, (((1,), (1,)), ((), ())), preferred_element_type=F32)


def _select_rows(hid, vals):
    out = vals[-1]
    for h in range(len(vals) - 2, -1, -1):
        out = jnp.where(hid == h, vals[h], out)
    return out


def _resident(shape):
    nd = len(shape)
    return pl.BlockSpec(shape, lambda *_: (0,) * nd, pipeline_mode=pl.Buffered(1))


def _params(sem):
    return pltpu.CompilerParams(dimension_semantics=sem, vmem_limit_bytes=VMEM_LIMIT)


def _inproj_prompt_kernel(x_ref, gpre_ref, w_ref, lng_ref, lnb_ref, ws_ref, bs_ref, cw_ref, cbias_ref, bg_ref,
                          oa_ref, q_ref, k_ref, v_ref, oc_ref, gate_ref, cs_ref, prev_ref):
    tm = x_ref.shape[0]
    i = pl.program_id(1)
    h = _rms(x_ref[...], gpre_ref[...]).astype(BF16)

    def proj(off, width):
        return _dot(h, w_ref[:, off:off + width])

    u = jax.nn.gelu(proj(OFF_U, A_WIDTH))
    vn = _layer_norm(jax.nn.gelu(proj(OFF_V, A_WIDTH)), lng_ref[...], lnb_ref[...])
    row = lax.broadcasted_iota(jnp.int32, (CHUNK, CHUNK), 0)
    col = lax.broadcasted_iota(jnp.int32, (CHUNK, CHUNK), 1)
    for g in range(A_GROUPS):
        wt = jnp.where(row >= col, ws_ref[g], 0.0).astype(BF16)
        cs = slice(g * A_GROUP_DIM, (g + 1) * A_GROUP_DIM)
        for c in range(tm // CHUNK):
            rs = slice(c * CHUNK, (c + 1) * CHUNK)
            sa = _dot(wt, vn[rs, cs].astype(BF16)) + bs_ref[:, cs]
            oa_ref[rs, cs] = (u[rs, cs] * sa).astype(BF16)

    qv = proj(OFF_Q, Q_W)
    for c in range(Q_W // LANES):
        q_ref[c] = qv[:, c * LANES:(c + 1) * LANES]
    kv = proj(OFF_K, 2 * KV_W)
    for c in range(KV_W // LANES):
        k_ref[c] = kv[:, c * LANES:(c + 1) * LANES]
        v_ref[c] = kv[:, KV_W + c * LANES:KV_W + (c + 1) * LANES]

    cz = proj(OFF_CC, C_WIDTH) * proj(OFF_CX, C_WIDTH)

    @pl.when(i == 0)
    def _():
        prev_ref[...] = jnp.zeros_like(prev_ref)

    p2 = prev_ref[SUBLANES - 2:SUBLANES - 1, :]
    p1 = prev_ref[SUBLANES - 1:SUBLANES, :]
    r = lax.broadcasted_iota(jnp.int32, (tm, 1), 0)
    z1 = jnp.where(r == 0, p1, pltpu.roll(cz, 1, 0))
    z2 = jnp.where(r == 0, p2, jnp.where(r == 1, p1, pltpu.roll(cz, 2, 0)))
    y = cbias_ref[...] + cw_ref[0:1, :] * z2 + cw_ref[1:2, :] * z1 + cw_ref[2:3, :] * cz
    oc_ref[...] = (proj(OFF_CB, C_WIDTH) * y).astype(BF16)
    tail = cz[tm - SUBLANES:tm, :]
    prev_ref[...] = tail
    cs_ref[0] = tail

    for j in range(3):
        cs = slice(j * D_MODEL, (j + 1) * D_MODEL)
        gate_ref[:, cs] = jax.nn.sigmoid(proj(OFF_G + j * D_MODEL, D_MODEL) + bg_ref[:, cs]).astype(BF16)


def _inproj_prompt(x, gpre, w_in, lng, lnb, ws, bs_full, cw, cbias, bg):
    tm = PROMPT_TM
    nt = SEQ // tm
    rows = BATCH * SEQ

    def rowblk(width):
        return pl.BlockSpec((tm, width), lambda b, i: (b * nt + i, 0))

    def slab(n):
        return pl.BlockSpec((n, tm, LANES), lambda b, i: (0, b * nt + i, 0))

    return pl.pallas_call(
        _inproj_prompt_kernel,
        grid=(BATCH, nt),
        in_specs=[
            rowblk(D_MODEL),
            _resident((1, D_MODEL)),
            _resident((D_MODEL, IN_W)),
            _resident((1, A_WIDTH)),
            _resident((1, A_WIDTH)),
            _resident((A_GROUPS, CHUNK, CHUNK)),
            _resident((CHUNK, A_WIDTH)),
            _resident((CONV_W, C_WIDTH)),
            _resident((1, C_WIDTH)),
            _resident((1, 3 * D_MODEL)),
        ],
        out_specs=[
            rowblk(A_WIDTH),
            slab(Q_W // LANES),
            slab(KV_W // LANES),
            slab(KV_W // LANES),
            rowblk(C_WIDTH),
            rowblk(3 * D_MODEL),
            pl.BlockSpec((1, SUBLANES, C_WIDTH), lambda b, i: (b, 0, 0)),
        ],
        out_shape=[
            jax.ShapeDtypeStruct((rows, A_WIDTH), BF16),
            jax.ShapeDtypeStruct((Q_W // LANES, rows, LANES), F32),
            jax.ShapeDtypeStruct((KV_W // LANES, rows, LANES), F32),
            jax.ShapeDtypeStruct((KV_W // LANES, rows, LANES), F32),
            jax.ShapeDtypeStruct((rows, C_WIDTH), BF16),
            jax.ShapeDtypeStruct((rows, 3 * D_MODEL), BF16),
            jax.ShapeDtypeStruct((BATCH, SUBLANES, C_WIDTH), F32),
        ],
        scratch_shapes=[pltpu.VMEM((SUBLANES, C_WIDTH), F32)],
        compiler_params=_params(("arbitrary", "arbitrary")),
        name="inproj_prompt",
    )(x, gpre, w_in, lng, lnb, ws, bs_full, cw, cbias, bg)


def _attn_prompt_kernel(q_ref, k_ref, v_ref, o_ref, kbuf, vbuf, acc, mrun, lrun, bias_ref):
    b = pl.program_id(0)
    i = pl.program_id(1)
    first = i == 0
    nslab = KV_W // LANES

    @pl.when(jnp.logical_and(b == 0, first))
    def _():
        rr = lax.broadcasted_iota(jnp.int32, (KV_HEADS * BAND, 2 * BAND), 0)
        kk = lax.broadcasted_iota(jnp.int32, (KV_HEADS * BAND, 2 * BAND), 1)
        dist = (rr & (BAND - 1)) + BAND - kk
        hid = rr >> 7
        for g, dil in enumerate(DILS):
            coef = _select_rows(hid, [jnp.float32(SLOPES[g][h] * dil) for h in range(KV_HEADS)])
            ok = jnp.logical_and(dist >= 0, dist <= BAND)
            bias_ref[g] = jnp.where(ok, -(coef * dist.astype(F32)), NEG)

    @pl.when(first)
    def _():
        kbuf[:, 0:SPAN, :] = jnp.zeros((nslab, SPAN, LANES), F32)
        vbuf[:, 0:SPAN, :] = jnp.zeros((nslab, SPAN, LANES), F32)

    @pl.when(i > 0)
    def _():
        kbuf[:, 0:SPAN, :] = kbuf[:, SPAN:2 * SPAN, :]
        vbuf[:, 0:SPAN, :] = vbuf[:, SPAN:2 * SPAN, :]

    kbuf[:, SPAN:2 * SPAN, :] = k_ref[...]
    vbuf[:, SPAN:2 * SPAN, :] = v_ref[...]

    lane = lax.broadcasted_iota(jnp.int32, (1, KV_W), 1)
    hid = lane >> 6
    qmask = [jnp.where(hid == h, HEAD_DIM ** -0.5, 0.0).astype(F32) for h in range(KV_HEADS)]
    keylane = lax.broadcasted_iota(jnp.int32, (1, 2 * BAND), 1)

    def unit(g, q_start, k_start, pen_on, init):
        stride = DILS[g]

        def rows(start, n):
            return pl.ds(start, n, stride=stride) if stride > 1 else pl.ds(start, n)

        qr = rows(q_start, BAND)
        kr = rows(k_start, 2 * BAND)
        qb = jnp.concatenate([q_ref[nslab * g + c, qr, :] for c in range(nslab)], axis=1)
        kb = jnp.concatenate([kbuf[c, kr, :] for c in range(nslab)], axis=1).astype(BF16)
        vb = jnp.concatenate([vbuf[c, kr, :] for c in range(nslab)], axis=1).astype(BF16)
        qs = jnp.concatenate([(qb * qmask[h]).astype(BF16) for h in range(KV_HEADS)], axis=0)
        s = _dot_nt(qs, kb) + bias_ref[g]
        pen = jnp.where(keylane < BAND, jnp.where(pen_on, NEG, 0.0), 0.0)
        s = s + pen
        m = jnp.max(s, axis=-1, keepdims=True)
        p = jnp.exp(s - m)
        l = jnp.sum(p, axis=-1, keepdims=True)
        r = _dot(p.astype(BF16), vb)

        def per_head(x):
            return _select_rows(hid, [x[h * BAND:(h + 1) * BAND] for h in range(KV_HEADS)])

        o_u = per_head(r)
        m_u = per_head(m)
        l_u = per_head(l)
        if init:
            o_n, m_n, l_n = o_u, m_u, l_u
        else:
            m_old = jnp.concatenate([mrun[c, qr, :] for c in range(nslab)], axis=1)
            o_old = jnp.concatenate([acc[c, qr, :] for c in range(nslab)], axis=1)
            l_old = jnp.concatenate([lrun[c, qr, :] for c in range(nslab)], axis=1)
            m_n = jnp.maximum(m_old, m_u)
            a_old = jnp.exp(m_old - m_n)
            a_new = jnp.exp(m_u - m_n)
            o_n = o_old * a_old + o_u * a_new
            l_n = l_old * a_old + l_u * a_new
        for c in range(nslab):
            ls = slice(c * LANES, (c + 1) * LANES)
            acc[c, qr, :] = o_n[:, ls]
            mrun[c, qr, :] = m_n[:, ls]
            lrun[c, qr, :] = l_n[:, ls]

    nblk = SPAN // BAND

    def body0(n, carry):
        q0 = pl.multiple_of(n * BAND, BAND)
        unit(0, q0, q0 + (SPAN - BAND), jnp.logical_and(first, n == 0), True)
        return carry

    lax.fori_loop(0, nblk, body0, 0)

    def body1(t, carry):
        sub = t >> 2
        res = t & 3
        q0 = sub * (BAND * DILS[1]) + res
        unit(1, q0, q0 + (SPAN - BAND * DILS[1]), jnp.logical_and(first, sub == 0), False)
        return carry

    lax.fori_loop(0, nblk, body1, 0)

    def body2(res, carry):
        unit(2, res, res, first, False)
        return carry

    lax.fori_loop(0, nblk, body2, 0)

    for c in range(nslab):
        o_ref[:, c * LANES:(c + 1) * LANES] = (acc[c] / lrun[c]).astype(BF16)


def _attn_prompt(q, k, v):
    ns = SEQ // SPAN
    nslab = KV_W // LANES
    rows = BATCH * SEQ

    def slab(n):
        return pl.BlockSpec((n, SPAN, LANES), lambda b, i: (0, b * ns + i, 0))

    return pl.pallas_call(
        _attn_prompt_kernel,
        grid=(BATCH, ns),
        in_specs=[slab(Q_W // LANES), slab(nslab), slab(nslab)],
        out_specs=pl.BlockSpec((SPAN, KV_W), lambda b, i: (b * ns + i, 0)),
        out_shape=jax.ShapeDtypeStruct((rows, KV_W), BF16),
        scratch_shapes=[
            pltpu.VMEM((nslab, 2 * SPAN, LANES), F32),
            pltpu.VMEM((nslab, 2 * SPAN, LANES), F32),
            pltpu.VMEM((nslab, SPAN, LANES), F32),
            pltpu.VMEM((nslab, SPAN, LANES), F32),
            pltpu.VMEM((nslab, SPAN, LANES), F32),
            pltpu.VMEM((N_DIL, KV_HEADS * BAND, 2 * BAND), F32),
        ],
        compiler_params=_params(("arbitrary", "arbitrary")),
        name="attn_prompt",
    )(q, k, v)


def _merge_kernel(x_ref, oa_ref, ob_ref, oc_ref, gate_ref, wa_ref, wb_ref, wc_ref, wo_ref, gpost_ref, o_ref):
    def gate(j):
        return gate_ref[:, j * D_MODEL:(j + 1) * D_MODEL].astype(F32)

    merged = (gate(0) * _dot(oa_ref[...], wa_ref[...])
              + gate(1) * _dot(ob_ref[...].astype(BF16), wb_ref[...])
              + gate(2) * _dot(oc_ref[...], wc_ref[...]))
    y = _dot(merged.astype(BF16), wo_ref[...])
    o_ref[...] = x_ref[...] + _rms(y, gpost_ref[...])


def _merge(x, oa, ob, oc, gates, wa, wb, wc, wo, gpost, tm):
    rows = x.shape[0]

    def rowblk(width):
        return pl.BlockSpec((tm, width), lambda i: (i, 0))

    return pl.pallas_call(
        _merge_kernel,
        grid=(rows // tm,),
        in_specs=[
            rowblk(D_MODEL), rowblk(A_WIDTH), rowblk(KV_W), rowblk(C_WIDTH), rowblk(3 * D_MODEL),
            _resident((A_WIDTH, D_MODEL)), _resident((KV_W, D_MODEL)), _resident((C_WIDTH, D_MODEL)),
            _resident((D_MODEL, D_MODEL)), _resident((1, D_MODEL)),
        ],
        out_specs=rowblk(D_MODEL),
        out_shape=jax.ShapeDtypeStruct((rows, D_MODEL), F32),
        compiler_params=_params(("parallel",)),
        name="merge",
    )(x, oa, ob, oc, gates, wa, wb, wc, wo, gpost)


def _ffn_kernel(x_ref, gpre_ref, wg_ref, wu_ref, wd_ref, gpost_ref, o_ref):
    x = x_ref[...]
    h = _rms(x, gpre_ref[...]).astype(BF16)
    act = (jax.nn.silu(_dot(h, wg_ref[...])) * _dot(h, wu_ref[...])).astype(BF16)
    o_ref[...] = x + _rms(_dot(act, wd_ref[...]), gpost_ref[...])


def _ffn(x, gpre, wg, wu, wd, gpost, tm):
    rows = x.shape[0]
    rowblk = pl.BlockSpec((tm, D_MODEL), lambda i: (i, 0))
    return pl.pallas_call(
        _ffn_kernel,
        grid=(rows // tm,),
        in_specs=[rowblk, _resident((1, D_MODEL)), _resident((D_MODEL, D_FF)), _resident((D_MODEL, D_FF)),
                  _resident((D_FF, D_MODEL)), _resident((1, D_MODEL))],
        out_specs=rowblk,
        out_shape=jax.ShapeDtypeStruct((rows, D_MODEL), F32),
        compiler_params=_params(("parallel",)),
        name="ffn",
    )(x, gpre, wg, wu, wd, gpost)


def _inproj_sample_kernel(x_ref, gpre_ref, w_ref, lng_ref, lnb_ref, mix_ref, bs_ref, cw_ref, cbias_ref, bg_ref,
                          p1_ref, p2_ref,
                          oa_ref, q_ref, k_ref, v_ref, oc_ref, gate_ref, vn_ref, cz_ref):
    n = SAMPLE_ROWS
    h = _rms(x_ref[...], gpre_ref[...]).astype(BF16)

    def proj(off, width):
        return _dot(h, w_ref[:, off:off + width])

    u = jax.nn.gelu(proj(OFF_U, A_WIDTH))
    vn = _layer_norm(jax.nn.gelu(proj(OFF_V, A_WIDTH)), lng_ref[...], lnb_ref[...])
    vn_ref[...] = vn
    row = lax.broadcasted_iota(jnp.int32, (n, n), 0)
    col = lax.broadcasted_iota(jnp.int32, (n, n), 1)
    keep = jnp.logical_and((row >> 2) == (col >> 2), (col & 3) <= (row & 3))
    for g in range(A_GROUPS):
        cs = slice(g * A_GROUP_DIM, (g + 1) * A_GROUP_DIM)
        wt = jnp.where(keep, mix_ref[g], 0.0).astype(BF16)
        sa = _dot(wt, vn[:, cs].astype(BF16)) + bs_ref[:, cs]
        oa_ref[:, cs] = (u[:, cs] * sa).astype(BF16)

    q_ref[...] = proj(OFF_Q, Q_W)
    kv = proj(OFF_K, 2 * KV_W)
    k_ref[...] = kv[:, 0:KV_W]
    v_ref[...] = kv[:, KV_W:2 * KV_W]

    cz = proj(OFF_CC, C_WIDTH) * proj(OFF_CX, C_WIDTH)
    cz_ref[...] = cz
    t = lax.broadcasted_iota(jnp.int32, (n, 1), 0) & (DEC_SEQ - 1)
    z1 = jnp.where(t >= 1, pltpu.roll(cz, 1, 0), p1_ref[...])
    z2 = jnp.where(t >= 2, pltpu.roll(cz, 2, 0), p2_ref[...])
    y = cbias_ref[...] + cw_ref[0:1, :] * z2 + cw_ref[1:2, :] * z1 + cw_ref[2:3, :] * cz
    oc_ref[...] = (proj(OFF_CB, C_WIDTH) * y).astype(BF16)

    for j in range(3):
        cs = slice(j * D_MODEL, (j + 1) * D_MODEL)
        gate_ref[:, cs] = jax.nn.sigmoid(proj(OFF_G + j * D_MODEL, D_MODEL) + bg_ref[:, cs]).astype(BF16)


def _inproj_sample(x, gpre, w_in, lng, lnb, mix, bs_full, cw, cbias, bg, p1, p2):
    n = SAMPLE_ROWS

    def full(shape):
        nd = len(shape)
        return pl.BlockSpec(shape, lambda i: (0,) * nd)

    widths_dtypes = [(A_WIDTH, BF16), (Q_W, F32), (KV_W, F32), (KV_W, F32), (C_WIDTH, BF16),
                     (3 * D_MODEL, BF16), (A_WIDTH, F32), (C_WIDTH, F32)]
    return pl.pallas_call(
        _inproj_sample_kernel,
        grid=(1,),
        in_specs=[full((n, D_MODEL)), full((1, D_MODEL)), full((D_MODEL, IN_W)), full((1, A_WIDTH)),
                  full((1, A_WIDTH)), full((A_GROUPS, n, n)), full((n, A_WIDTH)), full((CONV_W, C_WIDTH)),
                  full((1, C_WIDTH)), full((1, 3 * D_MODEL)), full((n, C_WIDTH)), full((n, C_WIDTH))],
        out_specs=[full((n, w)) for w, _ in widths_dtypes],
        out_shape=[jax.ShapeDtypeStruct((n, w), dt) for w, dt in widths_dtypes],
        compiler_params=_params(("arbitrary",)),
        name="inproj_sample",
    )(x, gpre, w_in, lng, lnb, mix, bs_full, cw, cbias, bg, p1, p2)


def _attn_sample_kernel(qe_ref, kne_ref, vne_ref, ck0_ref, cv0_ref, ck1_ref, cv1_ref, ck2_ref, cv2_ref, o_ref):
    nr = DEC_SEQ * KV_HEADS
    wide = DEC_SEQ * KV_W
    ridx = lax.broadcasted_iota(jnp.int32, (nr, 1), 0)
    t_of = ridx >> 2
    h_of = ridx & 3
    lane_w = lax.broadcasted_iota(jnp.int32, (1, wide), 1)
    blockdiag = jnp.logical_and((lane_w >> 8) == t_of, ((lane_w >> 6) & 3) == h_of)
    lane_n = lax.broadcasted_iota(jnp.int32, (1, KV_W), 1)
    headlanes = (lane_n >> 6) == h_of
    scale = HEAD_DIM ** -0.5

    qe = qe_ref[0]
    kne = kne_ref[0]
    vne = vne_ref[0]
    kne_w = jnp.concatenate([kne] * DEC_SEQ, axis=1)
    vne_w = jnp.concatenate([vne] * DEC_SEQ, axis=1)
    cidx = lax.broadcasted_iota(jnp.int32, (1, BAND), 1)

    def slope_rows(g):
        return _select_rows(h_of, [jnp.float32(SLOPES[g][h] * DILS[g]) for h in range(KV_HEADS)])

    q0 = jnp.where(headlanes, qe[:, 0:KV_W] * scale, 0.0)
    dist = BAND + t_of - cidx
    s0 = _dot_nt(q0.astype(BF16), ck0_ref[0].astype(BF16))
    s0 = jnp.where(cidx >= t_of, s0 - slope_rows(0) * dist.astype(F32), NEG)
    sn = []
    for n in range(DEC_SEQ):
        raw = jnp.sum(q0 * kne_ref[0, KV_HEADS * n:KV_HEADS * n + 1, :], axis=-1, keepdims=True)
        dn = t_of - n
        sn.append(jnp.where(dn >= 0, raw - slope_rows(0) * dn.astype(F32), NEG))
    m0 = jnp.max(s0, axis=-1, keepdims=True)
    for x in sn:
        m0 = jnp.maximum(m0, x)
    p0 = jnp.exp(s0 - m0)
    l0 = jnp.sum(p0, axis=-1, keepdims=True)
    r0 = _dot(p0.astype(BF16), cv0_ref[0].astype(BF16))
    for n in range(DEC_SEQ):
        pn = jnp.exp(sn[n] - m0)
        l0 = l0 + pn
        r0 = r0 + pn * vne_ref[0, KV_HEADS * n:KV_HEADS * n + 1, :]
    o0 = jnp.where(blockdiag, jnp.concatenate([r0] * DEC_SEQ, axis=1), 0.0)
    parts = [(o0, m0, l0)]

    tap = (BAND - cidx).astype(F32)
    for g, (ck_ref, cv_ref) in ((1, (ck1_ref, cv1_ref)), (2, (ck2_ref, cv2_ref))):
        qg = jnp.concatenate([qe[:, g * KV_W:(g + 1) * KV_W]] * DEC_SEQ, axis=1)
        qg = jnp.where(blockdiag, qg * scale, 0.0)
        s = _dot_nt(qg.astype(BF16), ck_ref[0].astype(BF16)) - slope_rows(g) * tap
        s_new = jnp.sum(qg * kne_w, axis=-1, keepdims=True)
        m = jnp.maximum(jnp.max(s, axis=-1, keepdims=True), s_new)
        p = jnp.exp(s - m)
        p_new = jnp.exp(s_new - m)
        l = jnp.sum(p, axis=-1, keepdims=True) + p_new
        r = _dot(p.astype(BF16), cv_ref[0].astype(BF16)) + p_new * vne_w
        parts.append((jnp.where(blockdiag, r, 0.0), m, l))

    m_all = jnp.maximum(jnp.maximum(parts[0][1], parts[1][1]), parts[2][1])
    o_all = jnp.zeros((nr, wide), F32)
    l_all = jnp.zeros((nr, 1), F32)
    for o, m, l in parts:
        w = jnp.exp(m - m_all)
        o_all = o_all + o * w
        l_all = l_all + l * w
    o_ref[0] = jnp.sum(o_all / l_all, axis=0, keepdims=True)


def _attn_sample(layer, qe, kne, vne, ck, cv):
    nr = DEC_SEQ * KV_HEADS
    wide = DEC_SEQ * KV_W
    base = layer * DEC_BATCH
    views = []
    specs = []
    shapes = ((MAX_WINDOW, KV_W), (MAX_WINDOW // DILS[1], KV_W * DILS[1]), (MAX_WINDOW // DILS[2], KV_W * DILS[2]))
    blocks = ((BAND, KV_W), (BAND, wide), (BAND, wide))
    for (r, w), (br, bw) in zip(shapes, blocks):
        last_row_block = r // br - 1
        for c in (ck, cv):
            views.append(c.reshape(DEPTH * DEC_BATCH, r, w))
            specs.append(pl.BlockSpec((1, br, bw), lambda b, rb=last_row_block: (base + b, rb, 0)))
    return pl.pallas_call(
        _attn_sample_kernel,
        grid=(DEC_BATCH,),
        in_specs=[pl.BlockSpec((1, nr, Q_W), lambda b: (b, 0, 0)),
                  pl.BlockSpec((1, nr, KV_W), lambda b: (b, 0, 0)),
                  pl.BlockSpec((1, nr, KV_W), lambda b: (b, 0, 0))] + specs,
        out_specs=pl.BlockSpec((1, 1, wide), lambda b: (b, 0, 0)),
        out_shape=jax.ShapeDtypeStruct((DEC_BATCH, 1, wide), F32),
        compiler_params=_params(("parallel",)),
        name="attn_sample",
    )(qe, kne, vne, *views)


def kernel(x_prompt, x_sample, cache_k_win, cache_v_win, state_conv, g_pre_mix, g_post_mix, g_pre_ffn, g_post_ffn,
           w_in, a_ln_g, a_ln_b, a_ws, a_bs, c_conv_w, c_conv_b, w_br_a, w_br_b, w_br_c, b_gate, w_o,
           w_ff_gate, w_ff_up, w_ff_down):
    xp = x_prompt.reshape(BATCH * SEQ, D_MODEL)
    xs = x_sample.reshape(SAMPLE_ROWS, D_MODEL)
    win = min(MAX_WINDOW, SEQ)
    kp_l, vp_l, ks_l, vs_l, cp_l, cs_l, av_l = [], [], [], [], [], [], []
    for l in range(DEPTH):
        def row(a):
            return a[l].reshape(1, -1)

        w_in_b = w_in[l].astype(BF16)
        wa, wb, wc, wo = (w[l].astype(BF16) for w in (w_br_a, w_br_b, w_br_c, w_o))
        wg, wu, wd = (w[l].astype(BF16) for w in (w_ff_gate, w_ff_up, w_ff_down))
        bs_full = jnp.repeat(a_bs[l].T, A_GROUP_DIM, axis=1)

        oa, q, k, v, oc, gates, cs8 = _inproj_prompt(
            xp, row(g_pre_mix), w_in_b, row(a_ln_g), row(a_ln_b), a_ws[l], bs_full,
            c_conv_w[l], row(c_conv_b), row(b_gate))
        ob = _attn_prompt(q, k, v)
        xp = _merge(xp, oa, ob, oc, gates, wa, wb, wc, wo, row(g_post_mix), PROMPT_TM)
        xp = _ffn(xp, row(g_pre_ffn), wg, wu, wd, row(g_post_ffn), PROMPT_TM)

        def window(slabs):
            a = slabs.reshape(KV_W // LANES, BATCH, SEQ, LANES)[:, :, SEQ - win:]
            return jnp.transpose(a, (1, 2, 0, 3)).reshape(BATCH, win, KV_HEADS, HEAD_DIM)

        kp_l.append(window(k))
        vp_l.append(window(v))
        cp_l.append(cs8[:, SUBLANES - (CONV_W - 1):])

        mix = jnp.tile(a_ws[l][:, :DEC_SEQ, :DEC_SEQ], (1, DEC_BATCH, DEC_BATCH))
        bs_s = jnp.tile(bs_full[:DEC_SEQ], (DEC_BATCH, 1))
        prev = state_conv[l]
        zeros = jnp.zeros((DEC_BATCH, 1, C_WIDTH), F32)
        p1 = jnp.concatenate([prev[:, 1:2], zeros, zeros, zeros], axis=1).reshape(SAMPLE_ROWS, C_WIDTH)
        p2 = jnp.concatenate([prev[:, 0:1], prev[:, 1:2], zeros, zeros], axis=1).reshape(SAMPLE_ROWS, C_WIDTH)
        oa_s, q_s, k_s, v_s, oc_s, gates_s, vn_s, cz_s = _inproj_sample(
            xs, row(g_pre_mix), w_in_b, row(a_ln_g), row(a_ln_b), mix, bs_s,
            c_conv_w[l], row(c_conv_b), row(b_gate), p1, p2)

        def expand(a):
            return jnp.repeat(a.reshape(DEC_BATCH, DEC_SEQ, -1), KV_HEADS, axis=1)

        ob_s = _attn_sample(l, expand(q_s), expand(k_s), expand(v_s), cache_k_win, cache_v_win)
        ob_s = ob_s.reshape(SAMPLE_ROWS, KV_W)
        xs = _merge(xs, oa_s, ob_s, oc_s, gates_s, wa, wb, wc, wo, row(g_post_mix), SAMPLE_ROWS)
        xs = _ffn(xs, row(g_pre_ffn), wg, wu, wd, row(g_post_ffn), SAMPLE_ROWS)

        ks_l.append(k_s.reshape(DEC_BATCH, DEC_SEQ, KV_HEADS, HEAD_DIM))
        vs_l.append(v_s.reshape(DEC_BATCH, DEC_SEQ, KV_HEADS, HEAD_DIM))
        cs_l.append(cz_s.reshape(DEC_BATCH, DEC_SEQ, C_WIDTH)[:, DEC_SEQ - (CONV_W - 1):])
        av_l.append(vn_s.reshape(DEC_BATCH, DEC_SEQ, A_WIDTH))

    return (xp.reshape(BATCH, SEQ, D_MODEL), xs.reshape(DEC_BATCH, DEC_SEQ, D_MODEL),
            jnp.stack(kp_l), jnp.stack(vp_l), jnp.stack(ks_l), jnp.stack(vs_l),
            jnp.stack(cp_l), jnp.stack(cs_l), jnp.stack(av_l))
```

```python
import functools

import jax
import jax.numpy as jnp
from jax import lax
from jax.experimental import pallas as pl
from jax.experimental.pallas import tpu as pltpu

F32 = jnp.float32
BF16 = jnp.bfloat16

D_MODEL = 1024
BATCH = 2
SEQ = 8192
DEPTH = 2
DEC_BATCH = 32
DEC_SEQ = 4
CHUNK = 128
A_GROUPS = 4
A_GROUP_DIM = 128
A_WIDTH = A_GROUPS * A_GROUP_DIM
HEAD_DIM = 64
KV_HEADS = 4
DILS = (1, 4, 16)
N_DIL = len(DILS)
Q_W = N_DIL * KV_HEADS * HEAD_DIM
KV_W = KV_HEADS * HEAD_DIM
BAND = 128
MAX_WINDOW = 2048
C_WIDTH = 512
CONV_W = 3
D_FF = 2816
EPS = 1e-6

OFF_U = 0
OFF_V = OFF_U + A_WIDTH
OFF_Q = OFF_V + A_WIDTH
OFF_K = OFF_Q + Q_W
OFF_VV = OFF_K + KV_W
OFF_CX = OFF_VV + KV_W
OFF_CB = OFF_CX + C_WIDTH
OFF_CC = OFF_CB + C_WIDTH
OFF_G = OFF_CC + C_WIDTH
IN_W = OFF_G + 3 * D_MODEL

LANES = 128
SUBLANES = 8
VMEM_LIMIT = 56 * 1024 * 1024
NEG = -1e30

SAMPLE_ROWS = DEC_BATCH * DEC_SEQ
WIN = min(MAX_WINDOW, SEQ)
SPAN = BAND * DILS[-1]
PROMPT_TM = 512

SLOPES = tuple(
    tuple(2.0 ** (-8.0 * (g * KV_HEADS + h + 1) / (N_DIL * KV_HEADS)) for h in range(KV_HEADS))
    for g in range(N_DIL)
)


def _rms(x, g):
    return x * lax.rsqrt(jnp.mean(x * x, axis=-1, keepdims=True) + EPS) * g


def _layer_norm(x, g, b):
    mu = jnp.mean(x, axis=-1, keepdims=True)
    xc = x - mu
    return xc * lax.rsqrt(jnp.mean(xc * xc, axis=-1, keepdims=True) + EPS) * g + b


def _dot(a, b):
    return jnp.dot(a, b, preferred_element_type=F32)


def _dot_nt(a, b):
    return lax.dot_general(a, b, (((1,), (1,)), ((), ())), preferred_element_type=F32)


def _select_rows(hid, vals):
    out = vals[-1]
    for h in range(len(vals) - 2, -1, -1):
        out = jnp.where(hid == h, vals[h], out)
    return out


def _resident(shape):
    nd = len(shape)
    return pl.BlockSpec(shape, lambda *_: (0,) * nd, pipeline_mode=pl.Buffered(1))


def _params(sem):
    return pltpu.CompilerParams(dimension_semantics=sem, vmem_limit_bytes=VMEM_LIMIT)


def _inproj_prompt_kernel(x_ref, gpre_ref, w_ref, wkt_ref, wvt_ref, lng_ref, lnb_ref, ws_ref, bs_ref, cw_ref,
                          cbias_ref, bg_ref,
                          oa_ref, q_ref, k_ref, v_ref, kt_ref, vt_ref, oc_ref, gate_ref, cs_ref, prev_ref):
    tm = x_ref.shape[0]
    i = pl.program_id(1)
    h = _rms(x_ref[...], gpre_ref[...]).astype(BF16)

    @pl.when(i >= pl.num_programs(1) - WIN // tm)
    def _():
        kt_ref[0] = _dot_nt(wkt_ref[...], h)
        vt_ref[0] = _dot_nt(wvt_ref[...], h)

    def proj(off, width):
        return _dot(h, w_ref[:, off:off + width])

    u = jax.nn.gelu(proj(OFF_U, A_WIDTH))
    vn = _layer_norm(jax.nn.gelu(proj(OFF_V, A_WIDTH)), lng_ref[...], lnb_ref[...])
    row = lax.broadcasted_iota(jnp.int32, (CHUNK, CHUNK), 0)
    col = lax.broadcasted_iota(jnp.int32, (CHUNK, CHUNK), 1)
    for g in range(A_GROUPS):
        wt = jnp.where(row >= col, ws_ref[g], 0.0).astype(BF16)
        cs = slice(g * A_GROUP_DIM, (g + 1) * A_GROUP_DIM)
        for c in range(tm // CHUNK):
            rs = slice(c * CHUNK, (c + 1) * CHUNK)
            sa = _dot(wt, vn[rs, cs].astype(BF16)) + bs_ref[:, cs]
            oa_ref[rs, cs] = (u[rs, cs] * sa).astype(BF16)

    qv = proj(OFF_Q, Q_W)
    for c in range(Q_W // LANES):
        q_ref[c] = qv[:, c * LANES:(c + 1) * LANES]
    kv = proj(OFF_K, 2 * KV_W)
    for c in range(KV_W // LANES):
        k_ref[c] = kv[:, c * LANES:(c + 1) * LANES]
        v_ref[c] = kv[:, KV_W + c * LANES:KV_W + (c + 1) * LANES]

    cz = proj(OFF_CC, C_WIDTH) * proj(OFF_CX, C_WIDTH)

    @pl.when(i == 0)
    def _():
        prev_ref[...] = jnp.zeros_like(prev_ref)

    p2 = prev_ref[SUBLANES - 2:SUBLANES - 1, :]
    p1 = prev_ref[SUBLANES - 1:SUBLANES, :]
    r = lax.broadcasted_iota(jnp.int32, (tm, 1), 0)
    z1 = jnp.where(r == 0, p1, pltpu.roll(cz, 1, 0))
    z2 = jnp.where(r == 0, p2, jnp.where(r == 1, p1, pltpu.roll(cz, 2, 0)))
    y = cbias_ref[...] + cw_ref[0:1, :] * z2 + cw_ref[1:2, :] * z1 + cw_ref[2:3, :] * cz
    oc_ref[...] = (proj(OFF_CB, C_WIDTH) * y).astype(BF16)
    tail = cz[tm - SUBLANES:tm, :]
    prev_ref[...] = tail
    cs_ref[0] = tail

    for j in range(3):
        cs = slice(j * D_MODEL, (j + 1) * D_MODEL)
        gate_ref[:, cs] = jax.nn.sigmoid(proj(OFF_G + j * D_MODEL, D_MODEL) + bg_ref[:, cs]).astype(BF16)


def _inproj_prompt(x, gpre, w_in, wkt, wvt, lng, lnb, ws, bs_full, cw, cbias, bg):
    tm = PROMPT_TM
    nt = SEQ // tm
    rows = BATCH * SEQ
    win_spec = pl.BlockSpec((1, KV_W, tm), lambda b, i: (b, 0, jnp.maximum(i - (nt - WIN // tm), 0)))

    def rowblk(width):
        return pl.BlockSpec((tm, width), lambda b, i: (b * nt + i, 0))

    def slab(n):
        return pl.BlockSpec((n, tm, LANES), lambda b, i: (0, b * nt + i, 0))

    return pl.pallas_call(
        _inproj_prompt_kernel,
        grid=(BATCH, nt),
        in_specs=[
            rowblk(D_MODEL),
            _resident((1, D_MODEL)),
            _resident((D_MODEL, IN_W)),
            _resident((KV_W, D_MODEL)),
            _resident((KV_W, D_MODEL)),
            _resident((1, A_WIDTH)),
            _resident((1, A_WIDTH)),
            _resident((A_GROUPS, CHUNK, CHUNK)),
            _resident((CHUNK, A_WIDTH)),
            _resident((CONV_W, C_WIDTH)),
            _resident((1, C_WIDTH)),
            _resident((1, 3 * D_MODEL)),
        ],
        out_specs=[
            rowblk(A_WIDTH),
            slab(Q_W // LANES),
            slab(KV_W // LANES),
            slab(KV_W // LANES),
            win_spec,
            win_spec,
            rowblk(C_WIDTH),
            rowblk(3 * D_MODEL),
            pl.BlockSpec((1, SUBLANES, C_WIDTH), lambda b, i: (b, 0, 0)),
        ],
        out_shape=[
            jax.ShapeDtypeStruct((rows, A_WIDTH), BF16),
            jax.ShapeDtypeStruct((Q_W // LANES, rows, LANES), F32),
            jax.ShapeDtypeStruct((KV_W // LANES, rows, LANES), F32),
            jax.ShapeDtypeStruct((KV_W // LANES, rows, LANES), F32),
            jax.ShapeDtypeStruct((BATCH, KV_W, WIN), F32),
            jax.ShapeDtypeStruct((BATCH, KV_W, WIN), F32),
            jax.ShapeDtypeStruct((rows, C_WIDTH), BF16),
            jax.ShapeDtypeStruct((rows, 3 * D_MODEL), BF16),
            jax.ShapeDtypeStruct((BATCH, SUBLANES, C_WIDTH), F32),
        ],
        scratch_shapes=[pltpu.VMEM((SUBLANES, C_WIDTH), F32)],
        compiler_params=_params(("arbitrary", "arbitrary")),
        name="inproj_prompt",
    )(x, gpre, w_in, wkt, wvt, lng, lnb, ws, bs_full, cw, cbias, bg)


def _attn_prompt_kernel(q_ref, k_ref, v_ref, o_ref, kbuf, vbuf, acc, mrun, lrun, bias_ref):
    b = pl.program_id(0)
    i = pl.program_id(1)
    first = i == 0
    nslab = KV_W // LANES

    @pl.when(jnp.logical_and(b == 0, first))
    def _():
        rr = lax.broadcasted_iota(jnp.int32, (KV_HEADS * BAND, 2 * BAND), 0)
        kk = lax.broadcasted_iota(jnp.int32, (KV_HEADS * BAND, 2 * BAND), 1)
        dist = (rr & (BAND - 1)) + BAND - kk
        hid = rr >> 7
        for g, dil in enumerate(DILS):
            coef = _select_rows(hid, [jnp.float32(SLOPES[g][h] * dil) for h in range(KV_HEADS)])
            ok = jnp.logical_and(dist >= 0, dist <= BAND)
            bias_ref[g] = jnp.where(ok, -(coef * dist.astype(F32)), NEG)

    @pl.when(first)
    def _():
        kbuf[:, 0:SPAN, :] = jnp.zeros((nslab, SPAN, LANES), F32)
        vbuf[:, 0:SPAN, :] = jnp.zeros((nslab, SPAN, LANES), F32)

    @pl.when(i > 0)
    def _():
        kbuf[:, 0:SPAN, :] = kbuf[:, SPAN:2 * SPAN, :]
        vbuf[:, 0:SPAN, :] = vbuf[:, SPAN:2 * SPAN, :]

    kbuf[:, SPAN:2 * SPAN, :] = k_ref[...]
    vbuf[:, SPAN:2 * SPAN, :] = v_ref[...]

    lane = lax.broadcasted_iota(jnp.int32, (1, KV_W), 1)
    hid = lane >> 6
    qmask = [jnp.where(hid == h, HEAD_DIM ** -0.5, 0.0).astype(F32) for h in range(KV_HEADS)]
    keylane = lax.broadcasted_iota(jnp.int32, (1, 2 * BAND), 1)

    def unit(g, q_start, k_start, pen_on, init):
        stride = DILS[g]

        def rows(start, n):
            return pl.ds(start, n, stride=stride) if stride > 1 else pl.ds(start, n)

        qr = rows(q_start, BAND)
        kr = rows(k_start, 2 * BAND)
        qb = jnp.concatenate([q_ref[nslab * g + c, qr, :] for c in range(nslab)], axis=1)
        kb = jnp.concatenate([kbuf[c, kr, :] for c in range(nslab)], axis=1).astype(BF16)
        vb = jnp.concatenate([vbuf[c, kr, :] for c in range(nslab)], axis=1).astype(BF16)
        qs = jnp.concatenate([(qb * qmask[h]).astype(BF16) for h in range(KV_HEADS)], axis=0)
        s = _dot_nt(qs, kb) + bias_ref[g]
        pen = jnp.where(keylane < BAND, jnp.where(pen_on, NEG, 0.0), 0.0)
        s = s + pen
        m = jnp.max(s, axis=-1, keepdims=True)
        p = jnp.exp(s - m)
        l = jnp.sum(p, axis=-1, keepdims=True)
        r = _dot(p.astype(BF16), vb)

        def per_head(x):
            return _select_rows(hid, [x[h * BAND:(h + 1) * BAND] for h in range(KV_HEADS)])

        o_u = per_head(r)
        m_u = per_head(m)
        l_u = per_head(l)
        if init:
            o_n, m_n, l_n = o_u, m_u, l_u
        else:
            m_old = jnp.concatenate([mrun[c, qr, :] for c in range(nslab)], axis=1)
            o_old = jnp.concatenate([acc[c, qr, :] for c in range(nslab)], axis=1)
            l_old = jnp.concatenate([lrun[c, qr, :] for c in range(nslab)], axis=1)
            m_n = jnp.maximum(m_old, m_u)
            a_old = jnp.exp(m_old - m_n)
            a_new = jnp.exp(m_u - m_n)
            o_n = o_old * a_old + o_u * a_new
            l_n = l_old * a_old + l_u * a_new
        for c in range(nslab):
            ls = slice(c * LANES, (c + 1) * LANES)
            acc[c, qr, :] = o_n[:, ls]
            mrun[c, qr, :] = m_n[:, ls]
            lrun[c, qr, :] = l_n[:, ls]

    nblk = SPAN // BAND

    def body0(n, carry):
        q0 = pl.multiple_of(n * BAND, BAND)
        unit(0, q0, q0 + (SPAN - BAND), jnp.logical_and(first, n == 0), True)
        return carry

    lax.fori_loop(0, nblk, body0, 0)

    def body1(t, carry):
        sub = t >> 2
        res = t & 3
        q0 = sub * (BAND * DILS[1]) + res
        unit(1, q0, q0 + (SPAN - BAND * DILS[1]), jnp.logical_and(first, sub == 0), False)
        return carry

    lax.fori_loop(0, nblk, body1, 0)

    def body2(res, carry):
        unit(2, res, res, first, False)
        return carry

    lax.fori_loop(0, nblk, body2, 0)

    for c in range(nslab):
        o_ref[:, c * LANES:(c + 1) * LANES] = (acc[c] / lrun[c]).astype(BF16)


def _attn_prompt(q, k, v):
    ns = SEQ // SPAN
    nslab = KV_W // LANES
    rows = BATCH * SEQ

    def slab(n):
        return pl.BlockSpec((n, SPAN, LANES), lambda b, i: (0, b * ns + i, 0))

    return pl.pallas_call(
        _attn_prompt_kernel,
        grid=(BATCH, ns),
        in_specs=[slab(Q_W // LANES), slab(nslab), slab(nslab)],
        out_specs=pl.BlockSpec((SPAN, KV_W), lambda b, i: (b * ns + i, 0)),
        out_shape=jax.ShapeDtypeStruct((rows, KV_W), BF16),
        scratch_shapes=[
            pltpu.VMEM((nslab, 2 * SPAN, LANES), F32),
            pltpu.VMEM((nslab, 2 * SPAN, LANES), F32),
            pltpu.VMEM((nslab, SPAN, LANES), F32),
            pltpu.VMEM((nslab, SPAN, LANES), F32),
            pltpu.VMEM((nslab, SPAN, LANES), F32),
            pltpu.VMEM((N_DIL, KV_HEADS * BAND, 2 * BAND), F32),
        ],
        compiler_params=_params(("arbitrary", "arbitrary")),
        name="attn_prompt",
    )(q, k, v)


def _merge_kernel(x_ref, oa_ref, ob_ref, oc_ref, gate_ref, wa_ref, wb_ref, wc_ref, wo_ref, gpost_ref, o_ref):
    def gate(j):
        return gate_ref[:, j * D_MODEL:(j + 1) * D_MODEL].astype(F32)

    merged = (gate(0) * _dot(oa_ref[...], wa_ref[...])
              + gate(1) * _dot(ob_ref[...].astype(BF16), wb_ref[...])
              + gate(2) * _dot(oc_ref[...], wc_ref[...]))
    y = _dot(merged.astype(BF16), wo_ref[...])
    o_ref[...] = x_ref[...] + _rms(y, gpost_ref[...])


def _merge(x, oa, ob, oc, gates, wa, wb, wc, wo, gpost, tm):
    rows = x.shape[0]

    def rowblk(width):
        return pl.BlockSpec((tm, width), lambda i: (i, 0))

    return pl.pallas_call(
        _merge_kernel,
        grid=(rows // tm,),
        in_specs=[
            rowblk(D_MODEL), rowblk(A_WIDTH), rowblk(KV_W), rowblk(C_WIDTH), rowblk(3 * D_MODEL),
            _resident((A_WIDTH, D_MODEL)), _resident((KV_W, D_MODEL)), _resident((C_WIDTH, D_MODEL)),
            _resident((D_MODEL, D_MODEL)), _resident((1, D_MODEL)),
        ],
        out_specs=rowblk(D_MODEL),
        out_shape=jax.ShapeDtypeStruct((rows, D_MODEL), F32),
        compiler_params=_params(("parallel",)),
        name="merge",
    )(x, oa, ob, oc, gates, wa, wb, wc, wo, gpost)


def _ffn_kernel(x_ref, gpre_ref, wg_ref, wu_ref, wd_ref, gpost_ref, o_ref):
    x = x_ref[...]
    h = _rms(x, gpre_ref[...]).astype(BF16)
    act = (jax.nn.silu(_dot(h, wg_ref[...])) * _dot(h, wu_ref[...])).astype(BF16)
    o_ref[...] = x + _rms(_dot(act, wd_ref[...]), gpost_ref[...])


def _ffn(x, gpre, wg, wu, wd, gpost, tm):
    rows = x.shape[0]
    rowblk = pl.BlockSpec((tm, D_MODEL), lambda i: (i, 0))
    return pl.pallas_call(
        _ffn_kernel,
        grid=(rows // tm,),
        in_specs=[rowblk, _resident((1, D_MODEL)), _resident((D_MODEL, D_FF)), _resident((D_MODEL, D_FF)),
                  _resident((D_FF, D_MODEL)), _resident((1, D_MODEL))],
        out_specs=rowblk,
        out_shape=jax.ShapeDtypeStruct((rows, D_MODEL), F32),
        compiler_params=_params(("parallel",)),
        name="ffn",
    )(x, gpre, wg, wu, wd, gpost)


def _inproj_sample_kernel(x_ref, gpre_ref, w_ref, lng_ref, lnb_ref, mix_ref, bs_ref, cw_ref, cbias_ref, bg_ref,
                          p1_ref, p2_ref,
                          oa_ref, q_ref, k_ref, v_ref, oc_ref, gate_ref, vn_ref, cz_ref):
    n = SAMPLE_ROWS
    h = _rms(x_ref[...], gpre_ref[...]).astype(BF16)

    def proj(off, width):
        return _dot(h, w_ref[:, off:off + width])

    u = jax.nn.gelu(proj(OFF_U, A_WIDTH))
    vn = _layer_norm(jax.nn.gelu(proj(OFF_V, A_WIDTH)), lng_ref[...], lnb_ref[...])
    vn_ref[...] = vn
    row = lax.broadcasted_iota(jnp.int32, (n, n), 0)
    col = lax.broadcasted_iota(jnp.int32, (n, n), 1)
    keep = jnp.logical_and((row >> 2) == (col >> 2), (col & 3) <= (row & 3))
    for g in range(A_GROUPS):
        cs = slice(g * A_GROUP_DIM, (g + 1) * A_GROUP_DIM)
        wt = jnp.where(keep, mix_ref[g], 0.0).astype(BF16)
        sa = _dot(wt, vn[:, cs].astype(BF16)) + bs_ref[:, cs]
        oa_ref[:, cs] = (u[:, cs] * sa).astype(BF16)

    q_ref[...] = proj(OFF_Q, Q_W)
    kv = proj(OFF_K, 2 * KV_W)
    k_ref[...] = kv[:, 0:KV_W]
    v_ref[...] = kv[:, KV_W:2 * KV_W]

    cz = proj(OFF_CC, C_WIDTH) * proj(OFF_CX, C_WIDTH)
    cz_ref[...] = cz
    t = lax.broadcasted_iota(jnp.int32, (n, 1), 0) & (DEC_SEQ - 1)
    z1 = jnp.where(t >= 1, pltpu.roll(cz, 1, 0), p1_ref[...])
    z2 = jnp.where(t >= 2, pltpu.roll(cz, 2, 0), p2_ref[...])
    y = cbias_ref[...] + cw_ref[0:1, :] * z2 + cw_ref[1:2, :] * z1 + cw_ref[2:3, :] * cz
    oc_ref[...] = (proj(OFF_CB, C_WIDTH) * y).astype(BF16)

    for j in range(3):
        cs = slice(j * D_MODEL, (j + 1) * D_MODEL)
        gate_ref[:, cs] = jax.nn.sigmoid(proj(OFF_G + j * D_MODEL, D_MODEL) + bg_ref[:, cs]).astype(BF16)


def _inproj_sample(x, gpre, w_in, lng, lnb, mix, bs_full, cw, cbias, bg, p1, p2):
    n = SAMPLE_ROWS

    def full(shape):
        nd = len(shape)
        return pl.BlockSpec(shape, lambda i: (0,) * nd)

    widths_dtypes = [(A_WIDTH, BF16), (Q_W, F32), (KV_W, F32), (KV_W, F32), (C_WIDTH, BF16),
                     (3 * D_MODEL, BF16), (A_WIDTH, F32), (C_WIDTH, F32)]
    return pl.pallas_call(
        _inproj_sample_kernel,
        grid=(1,),
        in_specs=[full((n, D_MODEL)), full((1, D_MODEL)), full((D_MODEL, IN_W)), full((1, A_WIDTH)),
                  full((1, A_WIDTH)), full((A_GROUPS, n, n)), full((n, A_WIDTH)), full((CONV_W, C_WIDTH)),
                  full((1, C_WIDTH)), full((1, 3 * D_MODEL)), full((n, C_WIDTH)), full((n, C_WIDTH))],
        out_specs=[full((n, w)) for w, _ in widths_dtypes],
        out_shape=[jax.ShapeDtypeStruct((n, w), dt) for w, dt in widths_dtypes],
        compiler_params=_params(("arbitrary",)),
        name="inproj_sample",
    )(x, gpre, w_in, lng, lnb, mix, bs_full, cw, cbias, bg, p1, p2)


def _attn_sample_kernel(qh_ref, kn_ref, vn_ref, kt_ref, vt_ref, o_ref, delta_ref, pen_ref):
    nr = N_DIL * SUBLANES
    width = MAX_WINDOW

    @pl.when(pl.program_id(0) == 0)
    def _():
        row = lax.broadcasted_iota(jnp.int32, (nr, width), 0)
        pos = lax.broadcasted_iota(jnp.int32, (nr, width), 1)
        grp = row >> 3
        delta = width + (row & (SUBLANES - 1)) - pos
        dil_m1 = _select_rows(grp, [d - 1 for d in DILS])
        reach = _select_rows(grp, [BAND * d for d in DILS])
        ok = jnp.logical_and((delta & dil_m1) == 0, delta <= reach)
        delta_ref[...] = delta.astype(F32)
        pen_ref[...] = jnp.where(ok, 0.0, NEG)

    ridx = lax.broadcasted_iota(jnp.int32, (nr, 1), 0)
    g_of = ridx >> 3
    t_of = ridx & (SUBLANES - 1)
    scale = HEAD_DIM ** -0.5
    for h in range(KV_HEADS):
        slope = _select_rows(g_of, [jnp.float32(SLOPES[g][h]) for g in range(N_DIL)])
        q = qh_ref[0, h] * scale
        s = _dot(q.astype(BF16), kt_ref[0, 0, h].astype(BF16)) + (pen_ref[...] - slope * delta_ref[...])
        m = jnp.max(s, axis=-1, keepdims=True)
        s_new = []
        for n in range(DEC_SEQ):
            raw = jnp.sum(q * kn_ref[0, h, n:n + 1, :], axis=-1, keepdims=True)
            dn = t_of - n
            ok = jnp.logical_or(dn == 0, jnp.logical_and(g_of == 0, dn > 0))
            sn = jnp.where(ok, raw - slope * dn.astype(F32), NEG)
            s_new.append(sn)
            m = jnp.maximum(m, sn)
        p = jnp.exp(s - m)
        l = jnp.sum(p, axis=-1, keepdims=True)
        r = _dot_nt(p.astype(BF16), vt_ref[0, 0, h].astype(BF16))
        for n in range(DEC_SEQ):
            pn = jnp.exp(s_new[n] - m)
            l = l + pn
            r = r + pn * vn_ref[0, h, n:n + 1, :]
        parts = [(r[g * SUBLANES:(g + 1) * SUBLANES], m[g * SUBLANES:(g + 1) * SUBLANES],
                  l[g * SUBLANES:(g + 1) * SUBLANES]) for g in range(N_DIL)]
        m_all = jnp.maximum(jnp.maximum(parts[0][1], parts[1][1]), parts[2][1])
        num = jnp.zeros((SUBLANES, HEAD_DIM), F32)
        den = jnp.zeros((SUBLANES, 1), F32)
        for o_g, m_g, l_g in parts:
            w = jnp.exp(m_g - m_all)
            num = num + o_g * w
            den = den + l_g * w
        o_ref[0, :, h * HEAD_DIM:(h + 1) * HEAD_DIM] = (num / den)[0:DEC_SEQ]


def _attn_sample(layer, qh, knh, vnh, ckt, cvt):
    nr = N_DIL * SUBLANES
    cache_spec = pl.BlockSpec((1, 1, KV_HEADS, HEAD_DIM, MAX_WINDOW), lambda b: (layer, b, 0, 0, 0))
    new_spec = pl.BlockSpec((1, KV_HEADS, SUBLANES, HEAD_DIM), lambda b: (b, 0, 0, 0))
    return pl.pallas_call(
        _attn_sample_kernel,
        grid=(DEC_BATCH,),
        in_specs=[pl.BlockSpec((1, KV_HEADS, nr, HEAD_DIM), lambda b: (b, 0, 0, 0)),
                  new_spec, new_spec, cache_spec, cache_spec],
        out_specs=pl.BlockSpec((1, DEC_SEQ, KV_W), lambda b: (b, 0, 0)),
        out_shape=jax.ShapeDtypeStruct((DEC_BATCH, DEC_SEQ, KV_W), F32),
        scratch_shapes=[pltpu.VMEM((nr, MAX_WINDOW), F32), pltpu.VMEM((nr, MAX_WINDOW), F32)],
        compiler_params=_params(("arbitrary",)),
        name="attn_sample",
    )(qh, knh, vnh, ckt, cvt)


def kernel(x_prompt, x_sample, cache_k_win, cache_v_win, state_conv, g_pre_mix, g_post_mix, g_pre_ffn, g_post_ffn,
           w_in, a_ln_g, a_ln_b, a_ws, a_bs, c_conv_w, c_conv_b, w_br_a, w_br_b, w_br_c, b_gate, w_o,
           w_ff_gate, w_ff_up, w_ff_down):
    xp = x_prompt.reshape(BATCH * SEQ, D_MODEL)
    xs = x_sample.reshape(SAMPLE_ROWS, D_MODEL)
    ckt = jnp.transpose(cache_k_win, (0, 1, 3, 4, 2))
    cvt = jnp.transpose(cache_v_win, (0, 1, 3, 4, 2))
    kp_l, vp_l, ks_l, vs_l, cp_l, cs_l, av_l = [], [], [], [], [], [], []
    for l in range(DEPTH):
        def row(a):
            return a[l].reshape(1, -1)

        w_in_b = w_in[l].astype(BF16)
        wa, wb, wc, wo = (w[l].astype(BF16) for w in (w_br_a, w_br_b, w_br_c, w_o))
        wg, wu, wd = (w[l].astype(BF16) for w in (w_ff_gate, w_ff_up, w_ff_down))
        bs_full = jnp.repeat(a_bs[l].T, A_GROUP_DIM, axis=1)

        wkt = jnp.transpose(w_in[l][:, OFF_K:OFF_K + KV_W]).astype(BF16)
        wvt = jnp.transpose(w_in[l][:, OFF_VV:OFF_VV + KV_W]).astype(BF16)
        oa, q, k, v, kt, vt, oc, gates, cs8 = _inproj_prompt(
            xp, row(g_pre_mix), w_in_b, wkt, wvt, row(a_ln_g), row(a_ln_b), a_ws[l], bs_full,
            c_conv_w[l], row(c_conv_b), row(b_gate))
        ob = _attn_prompt(q, k, v)
        xp = _merge(xp, oa, ob, oc, gates, wa, wb, wc, wo, row(g_post_mix), PROMPT_TM)
        xp = _ffn(xp, row(g_pre_ffn), wg, wu, wd, row(g_post_ffn), PROMPT_TM)

        def window(t):
            return jnp.transpose(t.reshape(BATCH, KV_HEADS, HEAD_DIM, WIN), (0, 3, 1, 2))

        kp_l.append(window(kt))
        vp_l.append(window(vt))
        cp_l.append(cs8[:, SUBLANES - (CONV_W - 1):])

        mix = jnp.tile(a_ws[l][:, :DEC_SEQ, :DEC_SEQ], (1, DEC_BATCH, DEC_BATCH))
        bs_s = jnp.tile(bs_full[:DEC_SEQ], (DEC_BATCH, 1))
        prev = state_conv[l]
        zeros = jnp.zeros((DEC_BATCH, 1, C_WIDTH), F32)
        p1 = jnp.concatenate([prev[:, 1:2], zeros, zeros, zeros], axis=1).reshape(SAMPLE_ROWS, C_WIDTH)
        p2 = jnp.concatenate([prev[:, 0:1], prev[:, 1:2], zeros, zeros], axis=1).reshape(SAMPLE_ROWS, C_WIDTH)
        oa_s, q_s, k_s, v_s, oc_s, gates_s, vn_s, cz_s = _inproj_sample(
            xs, row(g_pre_mix), w_in_b, row(a_ln_g), row(a_ln_b), mix, bs_s,
            c_conv_w[l], row(c_conv_b), row(b_gate), p1, p2)

        pad_t = ((0, 0), (0, 0), (0, 0), (0, SUBLANES - DEC_SEQ), (0, 0))
        qh = q_s.reshape(DEC_BATCH, DEC_SEQ, N_DIL, KV_HEADS, HEAD_DIM).transpose(0, 3, 2, 1, 4)
        qh = jnp.pad(qh, pad_t).reshape(DEC_BATCH, KV_HEADS, N_DIL * SUBLANES, HEAD_DIM)

        def new_rows(a):
            a = a.reshape(DEC_BATCH, DEC_SEQ, KV_HEADS, HEAD_DIM).transpose(0, 2, 1, 3)
            return jnp.pad(a, pad_t[1:])

        ob_s = _attn_sample(l, qh, new_rows(k_s), new_rows(v_s), ckt, cvt)
        ob_s = ob_s.reshape(SAMPLE_ROWS, KV_W)
        xs = _merge(xs, oa_s, ob_s, oc_s, gates_s, wa, wb, wc, wo, row(g_post_mix), SAMPLE_ROWS)
        xs = _ffn(xs, row(g_pre_ffn), wg, wu, wd, row(g_post_ffn), SAMPLE_ROWS)

        ks_l.append(k_s.reshape(DEC_BATCH, DEC_SEQ, KV_HEADS, HEAD_DIM))
        vs_l.append(v_s.reshape(DEC_BATCH, DEC_SEQ, KV_HEADS, HEAD_DIM))
        cs_l.append(cz_s.reshape(DEC_BATCH, DEC_SEQ, C_WIDTH)[:, DEC_SEQ - (CONV_W - 1):])
        av_l.append(vn_s.reshape(DEC_BATCH, DEC_SEQ, A_WIDTH))

    return (xp.reshape(BATCH, SEQ, D_MODEL), xs.reshape(DEC_BATCH, DEC_SEQ, D_MODEL),
            jnp.stack(kp_l), jnp.stack(vp_l), jnp.stack(ks_l), jnp.stack(vs_l),
            jnp.stack(cp_l), jnp.stack(cs_l), jnp.stack(av_l))
```

```python
import functools

import jax
import jax.numpy as jnp
from jax import lax
from jax.experimental import pallas as pl
from jax.experimental.pallas import tpu as pltpu

F32 = jnp.float32
BF16 = jnp.bfloat16

D_MODEL = 1024
BATCH = 2
SEQ = 8192
DEPTH = 2
DEC_BATCH = 32
DEC_SEQ = 4
CHUNK = 128
A_GROUPS = 4
A_GROUP_DIM = 128
A_WIDTH = A_GROUPS * A_GROUP_DIM
HEAD_DIM = 64
KV_HEADS = 4
DILS = (1, 4, 16)
N_DIL = len(DILS)
Q_W = N_DIL * KV_HEADS * HEAD_DIM
KV_W = KV_HEADS * HEAD_DIM
BAND = 128
MAX_WINDOW = 2048
C_WIDTH = 512
CONV_W = 3
D_FF = 2816
EPS = 1e-6

OFF_U = 0
OFF_V = OFF_U + A_WIDTH
OFF_Q = OFF_V + A_WIDTH
OFF_K = OFF_Q + Q_W
OFF_VV = OFF_K + KV_W
OFF_CX = OFF_VV + KV_W
OFF_CB = OFF_CX + C_WIDTH
OFF_CC = OFF_CB + C_WIDTH
OFF_G = OFF_CC + C_WIDTH
IN_W = OFF_G + 3 * D_MODEL

LANES = 128
SUBLANES = 8
VMEM_LIMIT = 56 * 1024 * 1024
NEG = -1e30

SAMPLE_ROWS = DEC_BATCH * DEC_SEQ
WIN = min(MAX_WINDOW, SEQ)
SPAN = BAND * DILS[-1]
RES = DILS[1]
PROMPT_TM = 512
LOG2E = 1.4426950408889634
QSCALE = LOG2E * HEAD_DIM ** -0.5

SLOPES = tuple(
    tuple(2.0 ** (-8.0 * (g * KV_HEADS + h + 1) / (N_DIL * KV_HEADS)) for h in range(KV_HEADS))
    for g in range(N_DIL)
)


def _rms(x, g):
    return x * lax.rsqrt(jnp.mean(x * x, axis=-1, keepdims=True) + EPS) * g


def _layer_norm(x, g, b):
    mu = jnp.mean(x, axis=-1, keepdims=True)
    xc = x - mu
    return xc * lax.rsqrt(jnp.mean(xc * xc, axis=-1, keepdims=True) + EPS) * g + b


def _dot(a, b):
    return jnp.dot(a, b, preferred_element_type=F32)


def _dot_nt(a, b):
    return lax.dot_general(a, b, (((1,), (1,)), ((), ())), preferred_element_type=F32)


def _select_rows(hid, vals):
    out = vals[-1]
    for h in range(len(vals) - 2, -1, -1):
        out = jnp.where(hid == h, vals[h], out)
    return out


def _resident(shape):
    nd = len(shape)
    return pl.BlockSpec(shape, lambda *_: (0,) * nd, pipeline_mode=pl.Buffered(1))


def _params(sem):
    return pltpu.CompilerParams(dimension_semantics=sem, vmem_limit_bytes=VMEM_LIMIT)


def _inproj_prompt_kernel(x_ref, gpre_ref, w_ref, wkt_ref, wvt_ref, lng_ref, lnb_ref, ws_ref, bs_ref, cw_ref,
                          cbias_ref, bg_ref,
                          oa_ref, q_ref, k_ref, v_ref, kt_ref, vt_ref, oc_ref, gate_ref, cs_ref, prev_ref):
    tm = x_ref.shape[0]
    i = pl.program_id(1)

    @pl.when(i == 0)
    def _():
        prev_ref[...] = jnp.zeros_like(prev_ref)

    h = _rms(x_ref[...], gpre_ref[...]).astype(BF16)

    def proj(off, width):
        return _dot(h, w_ref[:, off:off + width])

    u = jax.nn.gelu(proj(OFF_U, A_WIDTH))
    vn = _layer_norm(jax.nn.gelu(proj(OFF_V, A_WIDTH)), lng_ref[...], lnb_ref[...])
    row = lax.broadcasted_iota(jnp.int32, (CHUNK, CHUNK), 0)
    col = lax.broadcasted_iota(jnp.int32, (CHUNK, CHUNK), 1)
    for g in range(A_GROUPS):
        wt = jnp.where(row >= col, ws_ref[g], 0.0).astype(BF16)
        cs = slice(g * A_GROUP_DIM, (g + 1) * A_GROUP_DIM)
        for c in range(tm // CHUNK):
            rs = slice(c * CHUNK, (c + 1) * CHUNK)
            sa = _dot(wt, vn[rs, cs].astype(BF16)) + bs_ref[:, cs]
            oa_ref[rs, cs] = (u[rs, cs] * sa).astype(BF16)

    qv = proj(OFF_Q, Q_W)
    for c in range(Q_W // LANES):
        q_ref[c] = qv[:, c * LANES:(c + 1) * LANES] * QSCALE
    kv = proj(OFF_K, 2 * KV_W)
    for c in range(KV_W // LANES):
        k_ref[c] = kv[:, c * LANES:(c + 1) * LANES]
        v_ref[c] = kv[:, KV_W + c * LANES:KV_W + (c + 1) * LANES]

    cz = proj(OFF_CC, C_WIDTH) * proj(OFF_CX, C_WIDTH)
    p2 = prev_ref[SUBLANES - 2:SUBLANES - 1, :]
    p1 = prev_ref[SUBLANES - 1:SUBLANES, :]
    r = lax.broadcasted_iota(jnp.int32, (tm, 1), 0)
    z1 = jnp.where(r == 0, p1, pltpu.roll(cz, 1, 0))
    z2 = jnp.where(r == 0, p2, jnp.where(r == 1, p1, pltpu.roll(cz, 2, 0)))
    y = cbias_ref[...] + cw_ref[0:1, :] * z2 + cw_ref[1:2, :] * z1 + cw_ref[2:3, :] * cz
    oc_ref[...] = (proj(OFF_CB, C_WIDTH) * y).astype(BF16)
    tail = cz[tm - SUBLANES:tm, :]
    prev_ref[...] = tail
    cs_ref[0] = tail

    for j in range(3):
        cs = slice(j * D_MODEL, (j + 1) * D_MODEL)
        gate_ref[:, cs] = jax.nn.sigmoid(proj(OFF_G + j * D_MODEL, D_MODEL) + bg_ref[:, cs]).astype(BF16)

    @pl.when(i >= pl.num_programs(1) - WIN // tm)
    def _():
        kt_ref[0] = _dot_nt(wkt_ref[...], h)
        vt_ref[0] = _dot_nt(wvt_ref[...], h)


def _inproj_prompt(x, gpre, w_in, wkt, wvt, lng, lnb, ws, bs_full, cw, cbias, bg):
    tm = PROMPT_TM
    nt = SEQ // tm
    rows = BATCH * SEQ
    win_spec = pl.BlockSpec((1, KV_W, tm), lambda b, i: (b, 0, jnp.maximum(i - (nt - WIN // tm), 0)))

    def rowblk(width):
        return pl.BlockSpec((tm, width), lambda b, i: (b * nt + i, 0))

    def slab(n):
        return pl.BlockSpec((n, tm, LANES), lambda b, i: (0, b * nt + i, 0))

    return pl.pallas_call(
        _inproj_prompt_kernel,
        grid=(BATCH, nt),
        in_specs=[
            rowblk(D_MODEL),
            _resident((1, D_MODEL)),
            _resident((D_MODEL, IN_W)),
            _resident((KV_W, D_MODEL)),
            _resident((KV_W, D_MODEL)),
            _resident((1, A_WIDTH)),
            _resident((1, A_WIDTH)),
            _resident((A_GROUPS, CHUNK, CHUNK)),
            _resident((CHUNK, A_WIDTH)),
            _resident((CONV_W, C_WIDTH)),
            _resident((1, C_WIDTH)),
            _resident((1, 3 * D_MODEL)),
        ],
        out_specs=[
            rowblk(A_WIDTH),
            slab(Q_W // LANES),
            slab(KV_W // LANES),
            slab(KV_W // LANES),
            win_spec,
            win_spec,
            rowblk(C_WIDTH),
            rowblk(3 * D_MODEL),
            pl.BlockSpec((1, SUBLANES, C_WIDTH), lambda b, i: (b, 0, 0)),
        ],
        out_shape=[
            jax.ShapeDtypeStruct((rows, A_WIDTH), BF16),
            jax.ShapeDtypeStruct((Q_W // LANES, rows, LANES), F32),
            jax.ShapeDtypeStruct((KV_W // LANES, rows, LANES), F32),
            jax.ShapeDtypeStruct((KV_W // LANES, rows, LANES), F32),
            jax.ShapeDtypeStruct((BATCH, KV_W, WIN), F32),
            jax.ShapeDtypeStruct((BATCH, KV_W, WIN), F32),
            jax.ShapeDtypeStruct((rows, C_WIDTH), BF16),
            jax.ShapeDtypeStruct((rows, 3 * D_MODEL), BF16),
            jax.ShapeDtypeStruct((BATCH, SUBLANES, C_WIDTH), F32),
        ],
        scratch_shapes=[pltpu.VMEM((SUBLANES, C_WIDTH), F32)],
        compiler_params=_params(("arbitrary", "arbitrary")),
        name="inproj_prompt",
    )(x, gpre, w_in, wkt, wvt, lng, lnb, ws, bs_full, cw, cbias, bg)


def _attn_prompt_kernel(q_ref, k_ref, v_ref, o_ref, ktail, vtail, k4, v4, q24, acc, mrun, lrun, tmp, bias_ref):
    b = pl.program_id(0)
    i = pl.program_id(1)
    first = i == 0
    nslab = KV_W // LANES
    quarter = SPAN // RES

    @pl.when(jnp.logical_and(b == 0, first))
    def _():
        rr = lax.broadcasted_iota(jnp.int32, (KV_HEADS * BAND, 2 * BAND), 0)
        kk = lax.broadcasted_iota(jnp.int32, (KV_HEADS * BAND, 2 * BAND), 1)
        dist = (rr & (BAND - 1)) + BAND - kk
        hrow = rr >> 7
        ok = jnp.logical_and(dist >= 0, dist <= BAND)
        for g, dil in enumerate(DILS):
            coef = _select_rows(hrow, [jnp.float32(SLOPES[g][h] * dil * LOG2E) for h in range(KV_HEADS)])
            full = jnp.where(ok, -(coef * dist.astype(F32)), NEG)
            bias_ref[2 * g] = full
            bias_ref[2 * g + 1] = jnp.where(kk >= BAND, full, NEG)

    @pl.when(first)
    def _():
        for r in range(RES):
            base = r * 2 * quarter
            k4[:, base:base + quarter, :] = jnp.zeros((nslab, quarter, LANES), F32)
            v4[:, base:base + quarter, :] = jnp.zeros((nslab, quarter, LANES), F32)
        ktail[...] = jnp.zeros_like(ktail)
        vtail[...] = jnp.zeros_like(vtail)

    @pl.when(i > 0)
    def _():
        for r in range(RES):
            base = r * 2 * quarter
            k4[:, base:base + quarter, :] = k4[:, base + quarter:base + 2 * quarter, :]
            v4[:, base:base + quarter, :] = v4[:, base + quarter:base + 2 * quarter, :]

    for r in range(RES):
        base = r * 2 * quarter + quarter
        for c in range(nslab):
            k4[c, base:base + quarter, :] = k_ref[c, pl.ds(r, quarter, stride=RES), :]
            v4[c, base:base + quarter, :] = v_ref[c, pl.ds(r, quarter, stride=RES), :]
            q24[c, r * quarter:(r + 1) * quarter, :] = q_ref[2 * nslab + c, pl.ds(r, quarter, stride=RES), :]

    lane = lax.broadcasted_iota(jnp.int32, (1, KV_W), 1)
    hid = lane >> 6
    headmask = [jnp.where(hid == h, 1.0, 0.0).astype(BF16) for h in range(KV_HEADS)]

    def wide(ref, rows, slab0=0):
        return jnp.concatenate([ref[slab0 + c, rows, :] for c in range(nslab)], axis=1)

    def softmax_unit(qb, kb, vb, bias):
        qb16 = qb.astype(BF16)
        qs = jnp.concatenate([qb16 * headmask[h] for h in range(KV_HEADS)], axis=0)
        s = _dot_nt(qs, kb.astype(BF16)) + bias
        m = jnp.max(s, axis=-1, keepdims=True)
        p = jnp.exp2(s - m)
        l = jnp.sum(p, axis=-1, keepdims=True)
        r = _dot(p.astype(BF16), vb.astype(BF16))

        def per_head(x):
            return _select_rows(hid, [x[h * BAND:(h + 1) * BAND] for h in range(KV_HEADS)])

        return per_head(r), per_head(m), per_head(l)

    def store_stats(rows, o, m, l):
        for c in range(nslab):
            ls = slice(c * LANES, (c + 1) * LANES)
            acc[c, rows, :] = o[:, ls]
            mrun[c, rows, :] = m[:, ls]
            lrun[c, rows, :] = l[:, ls]

    def merge_stats(rows, o_u, m_u, l_u):
        m_old = wide(mrun, rows)
        m_n = jnp.maximum(m_old, m_u)
        a_old = jnp.exp2(m_old - m_n)
        a_new = jnp.exp2(m_u - m_n)
        store_stats(rows, wide(acc, rows) * a_old + o_u * a_new, m_n, wide(lrun, rows) * a_old + l_u * a_new)

    def unit0(n, kb, vb, bias):
        q0 = n * BAND if isinstance(n, int) else pl.multiple_of(n * BAND, BAND)
        stats = softmax_unit(wide(q_ref, pl.ds(q0, BAND)), kb, vb, bias)
        sub = BAND // RES
        for j, x in enumerate(stats):
            for c in range(nslab):
                tmp[j * nslab + c, pl.ds(q0, BAND), :] = x[:, c * LANES:(c + 1) * LANES]
        for r in range(RES):
            src = pl.ds(q0 + r, sub, stride=RES)
            dst = pl.ds(r * quarter + n * sub, sub)
            for c in range(nslab):
                acc[c, dst, :] = tmp[c, src, :]
                mrun[c, dst, :] = tmp[nslab + c, src, :]
                lrun[c, dst, :] = tmp[2 * nslab + c, src, :]

    kb0 = jnp.concatenate([wide(ktail, pl.ds(0, BAND)), wide(k_ref, pl.ds(0, BAND))], axis=0)
    vb0 = jnp.concatenate([wide(vtail, pl.ds(0, BAND)), wide(v_ref, pl.ds(0, BAND))], axis=0)
    unit0(0, kb0, vb0, bias_ref[jnp.where(first, 1, 0)])

    def body0(n, carry):
        keys = pl.ds(pl.multiple_of((n - 1) * BAND, BAND), 2 * BAND)
        unit0(n, wide(k_ref, keys), wide(v_ref, keys), bias_ref[0])
        return carry

    lax.fori_loop(1, SPAN // BAND, body0, 0, unroll=3)

    def body1(s, carry):
        bias = bias_ref[2 + jnp.where(jnp.logical_and(first, s == 0), 1, 0)]
        for r in range(RES):
            qb = wide(q_ref, pl.ds(s * (BAND * RES) + r, BAND, stride=RES), nslab)
            keys = pl.ds(r * 2 * quarter + quarter + (s - 1) * BAND, 2 * BAND)
            o_u, m_u, l_u = softmax_unit(qb, wide(k4, keys), wide(v4, keys), bias)
            merge_stats(pl.ds(r * quarter + s * BAND, BAND), o_u, m_u, l_u)
        return carry

    lax.fori_loop(0, quarter // BAND, body1, 0)

    def body2(r, carry):
        bias = bias_ref[4 + jnp.where(first, 1, 0)]
        for a in range(DILS[2] // RES):
            rows = pl.ds(r * quarter + a, BAND, stride=RES)
            keys = pl.ds(r * 2 * quarter + a, 2 * BAND, stride=RES)
            o_u, m_u, l_u = softmax_unit(wide(q24, rows), wide(k4, keys), wide(v4, keys), bias)
            merge_stats(rows, o_u, m_u, l_u)
        return carry

    lax.fori_loop(0, RES, body2, 0)

    for r in range(RES):
        rows = pl.ds(r * quarter, quarter)
        for c in range(nslab):
            tmp[c, pl.ds(r, quarter, stride=RES), :] = acc[c, rows, :] / lrun[c, rows, :]
    for c in range(nslab):
        o_ref[:, c * LANES:(c + 1) * LANES] = tmp[c].astype(BF16)
    ktail[...] = k_ref[:, SPAN - BAND:SPAN, :]
    vtail[...] = v_ref[:, SPAN - BAND:SPAN, :]


def _attn_prompt(q, k, v):
    ns = SEQ // SPAN
    nslab = KV_W // LANES
    rows = BATCH * SEQ

    def slab(n):
        return pl.BlockSpec((n, SPAN, LANES), lambda b, i: (0, b * ns + i, 0))

    return pl.pallas_call(
        _attn_prompt_kernel,
        grid=(BATCH, ns),
        in_specs=[slab(Q_W // LANES), slab(nslab), slab(nslab)],
        out_specs=pl.BlockSpec((SPAN, KV_W), lambda b, i: (b * ns + i, 0)),
        out_shape=jax.ShapeDtypeStruct((rows, KV_W), BF16),
        scratch_shapes=[
            pltpu.VMEM((nslab, BAND, LANES), F32),
            pltpu.VMEM((nslab, BAND, LANES), F32),
            pltpu.VMEM((nslab, 2 * SPAN, LANES), F32),
            pltpu.VMEM((nslab, 2 * SPAN, LANES), F32),
            pltpu.VMEM((nslab, SPAN, LANES), F32),
            pltpu.VMEM((nslab, SPAN, LANES), F32),
            pltpu.VMEM((nslab, SPAN, LANES), F32),
            pltpu.VMEM((nslab, SPAN, LANES), F32),
            pltpu.VMEM((3 * nslab, SPAN, LANES), F32),
            pltpu.VMEM((2 * N_DIL, KV_HEADS * BAND, 2 * BAND), F32),
        ],
        compiler_params=_params(("arbitrary", "arbitrary")),
        name="attn_prompt",
    )(q, k, v)


def _merge_kernel(x_ref, oa_ref, ob_ref, oc_ref, gate_ref, wa_ref, wb_ref, wc_ref, wo_ref, gpost_ref, o_ref):
    def gate(j):
        return gate_ref[:, j * D_MODEL:(j + 1) * D_MODEL].astype(F32)

    merged = (gate(0) * _dot(oa_ref[...], wa_ref[...])
              + gate(1) * _dot(ob_ref[...].astype(BF16), wb_ref[...])
              + gate(2) * _dot(oc_ref[...], wc_ref[...]))
    y = _dot(merged.astype(BF16), wo_ref[...])
    o_ref[...] = x_ref[...] + _rms(y, gpost_ref[...])


def _merge(x, oa, ob, oc, gates, wa, wb, wc, wo, gpost, tm):
    rows = x.shape[0]

    def rowblk(width):
        return pl.BlockSpec((tm, width), lambda i: (i, 0))

    return pl.pallas_call(
        _merge_kernel,
        grid=(rows // tm,),
        in_specs=[
            rowblk(D_MODEL), rowblk(A_WIDTH), rowblk(KV_W), rowblk(C_WIDTH), rowblk(3 * D_MODEL),
            _resident((A_WIDTH, D_MODEL)), _resident((KV_W, D_MODEL)), _resident((C_WIDTH, D_MODEL)),
            _resident((D_MODEL, D_MODEL)), _resident((1, D_MODEL)),
        ],
        out_specs=rowblk(D_MODEL),
        out_shape=jax.ShapeDtypeStruct((rows, D_MODEL), F32),
        compiler_params=_params(("parallel",)),
        name="merge",
    )(x, oa, ob, oc, gates, wa, wb, wc, wo, gpost)


def _ffn_kernel(x_ref, gpre_ref, wg_ref, wu_ref, wd_ref, gpost_ref, o_ref):
    x = x_ref[...]
    h = _rms(x, gpre_ref[...]).astype(BF16)
    act = (jax.nn.silu(_dot(h, wg_ref[...])) * _dot(h, wu_ref[...])).astype(BF16)
    o_ref[...] = x + _rms(_dot(act, wd_ref[...]), gpost_ref[...])


def _ffn(x, gpre, wg, wu, wd, gpost, tm):
    rows = x.shape[0]
    rowblk = pl.BlockSpec((tm, D_MODEL), lambda i: (i, 0))
    return pl.pallas_call(
        _ffn_kernel,
        grid=(rows // tm,),
        in_specs=[rowblk, _resident((1, D_MODEL)), _resident((D_MODEL, D_FF)), _resident((D_MODEL, D_FF)),
                  _resident((D_FF, D_MODEL)), _resident((1, D_MODEL))],
        out_specs=rowblk,
        out_shape=jax.ShapeDtypeStruct((rows, D_MODEL), F32),
        compiler_params=_params(("parallel",)),
        name="ffn",
    )(x, gpre, wg, wu, wd, gpost)


def _inproj_sample_kernel(x_ref, gpre_ref, w_ref, lng_ref, lnb_ref, mix_ref, bs_ref, cw_ref, cbias_ref, bg_ref,
                          p1_ref, p2_ref,
                          oa_ref, q_ref, k_ref, v_ref, oc_ref, gate_ref, vn_ref, cz_ref):
    n = SAMPLE_ROWS
    h = _rms(x_ref[...], gpre_ref[...]).astype(BF16)

    def proj(off, width):
        return _dot(h, w_ref[:, off:off + width])

    u = jax.nn.gelu(proj(OFF_U, A_WIDTH))
    vn = _layer_norm(jax.nn.gelu(proj(OFF_V, A_WIDTH)), lng_ref[...], lnb_ref[...])
    vn_ref[...] = vn
    row = lax.broadcasted_iota(jnp.int32, (n, n), 0)
    col = lax.broadcasted_iota(jnp.int32, (n, n), 1)
    keep = jnp.logical_and((row >> 2) == (col >> 2), (col & 3) <= (row & 3))
    for g in range(A_GROUPS):
        cs = slice(g * A_GROUP_DIM, (g + 1) * A_GROUP_DIM)
        wt = jnp.where(keep, mix_ref[g], 0.0).astype(BF16)
        sa = _dot(wt, vn[:, cs].astype(BF16)) + bs_ref[:, cs]
        oa_ref[:, cs] = (u[:, cs] * sa).astype(BF16)

    q_ref[...] = proj(OFF_Q, Q_W)
    kv = proj(OFF_K, 2 * KV_W)
    k_ref[...] = kv[:, 0:KV_W]
    v_ref[...] = kv[:, KV_W:2 * KV_W]

    cz = proj(OFF_CC, C_WIDTH) * proj(OFF_CX, C_WIDTH)
    cz_ref[...] = cz
    t = lax.broadcasted_iota(jnp.int32, (n, 1), 0) & (DEC_SEQ - 1)
    z1 = jnp.where(t >= 1, pltpu.roll(cz, 1, 0), p1_ref[...])
    z2 = jnp.where(t >= 2, pltpu.roll(cz, 2, 0), p2_ref[...])
    y = cbias_ref[...] + cw_ref[0:1, :] * z2 + cw_ref[1:2, :] * z1 + cw_ref[2:3, :] * cz
    oc_ref[...] = (proj(OFF_CB, C_WIDTH) * y).astype(BF16)

    for j in range(3):
        cs = slice(j * D_MODEL, (j + 1) * D_MODEL)
        gate_ref[:, cs] = jax.nn.sigmoid(proj(OFF_G + j * D_MODEL, D_MODEL) + bg_ref[:, cs]).astype(BF16)


def _inproj_sample(x, gpre, w_in, lng, lnb, mix, bs_full, cw, cbias, bg, p1, p2):
    n = SAMPLE_ROWS

    def full(shape):
        nd = len(shape)
        return pl.BlockSpec(shape, lambda i: (0,) * nd)

    widths_dtypes = [(A_WIDTH, BF16), (Q_W, F32), (KV_W, F32), (KV_W, F32), (C_WIDTH, BF16),
                     (3 * D_MODEL, BF16), (A_WIDTH, F32), (C_WIDTH, F32)]
    return pl.pallas_call(
        _inproj_sample_kernel,
        grid=(1,),
        in_specs=[full((n, D_MODEL)), full((1, D_MODEL)), full((D_MODEL, IN_W)), full((1, A_WIDTH)),
                  full((1, A_WIDTH)), full((A_GROUPS, n, n)), full((n, A_WIDTH)), full((CONV_W, C_WIDTH)),
                  full((1, C_WIDTH)), full((1, 3 * D_MODEL)), full((n, C_WIDTH)), full((n, C_WIDTH))],
        out_specs=[full((n, w)) for w, _ in widths_dtypes],
        out_shape=[jax.ShapeDtypeStruct((n, w), dt) for w, dt in widths_dtypes],
        compiler_params=_params(("arbitrary",)),
        name="inproj_sample",
    )(x, gpre, w_in, lng, lnb, mix, bs_full, cw, cbias, bg, p1, p2)


def _attn_sample_kernel(qh_ref, kn_ref, vn_ref, kt_ref, vt_ref, o_ref, delta_ref, pen_ref):
    nr = N_DIL * SUBLANES
    width = MAX_WINDOW

    @pl.when(pl.program_id(0) == 0)
    def _():
        row = lax.broadcasted_iota(jnp.int32, (nr, width), 0)
        pos = lax.broadcasted_iota(jnp.int32, (nr, width), 1)
        grp = row >> 3
        delta = width + (row & (SUBLANES - 1)) - pos
        dil_m1 = _select_rows(grp, [d - 1 for d in DILS])
        reach = _select_rows(grp, [BAND * d for d in DILS])
        ok = jnp.logical_and((delta & dil_m1) == 0, delta <= reach)
        delta_ref[...] = delta.astype(F32)
        pen_ref[...] = jnp.where(ok, 0.0, NEG)

    ridx = lax.broadcasted_iota(jnp.int32, (nr, 1), 0)
    g_of = ridx >> 3
    t_of = ridx & (SUBLANES - 1)
    scale = HEAD_DIM ** -0.5
    for h in range(KV_HEADS):
        slope = _select_rows(g_of, [jnp.float32(SLOPES[g][h]) for g in range(N_DIL)])
        q = qh_ref[0, h] * scale
        s = _dot(q.astype(BF16), kt_ref[0, 0, h].astype(BF16)) + (pen_ref[...] - slope * delta_ref[...])
        m = jnp.max(s, axis=-1, keepdims=True)
        s_new = []
        for n in range(DEC_SEQ):
            raw = jnp.sum(q * kn_ref[0, h, n:n + 1, :], axis=-1, keepdims=True)
            dn = t_of - n
            ok = jnp.logical_or(dn == 0, jnp.logical_and(g_of == 0, dn > 0))
            sn = jnp.where(ok, raw - slope * dn.astype(F32), NEG)
            s_new.append(sn)
            m = jnp.maximum(m, sn)
        p = jnp.exp(s - m)
        l = jnp.sum(p, axis=-1, keepdims=True)
        r = _dot_nt(p.astype(BF16), vt_ref[0, 0, h].astype(BF16))
        for n in range(DEC_SEQ):
            pn = jnp.exp(s_new[n] - m)
            l = l + pn
            r = r + pn * vn_ref[0, h, n:n + 1, :]
        parts = [(r[g * SUBLANES:(g + 1) * SUBLANES], m[g * SUBLANES:(g + 1) * SUBLANES],
                  l[g * SUBLANES:(g + 1) * SUBLANES]) for g in range(N_DIL)]
        m_all = jnp.maximum(jnp.maximum(parts[0][1], parts[1][1]), parts[2][1])
        num = jnp.zeros((SUBLANES, HEAD_DIM), F32)
        den = jnp.zeros((SUBLANES, 1), F32)
        for o_g, m_g, l_g in parts:
            w = jnp.exp(m_g - m_all)
            num = num + o_g * w
            den = den + l_g * w
        o_ref[0, :, h * HEAD_DIM:(h + 1) * HEAD_DIM] = (num / den)[0:DEC_SEQ]


def _attn_sample(layer, qh, knh, vnh, ckt, cvt):
    nr = N_DIL * SUBLANES
    cache_spec = pl.BlockSpec((1, 1, KV_HEADS, HEAD_DIM, MAX_WINDOW), lambda b: (layer, b, 0, 0, 0))
    new_spec = pl.BlockSpec((1, KV_HEADS, SUBLANES, HEAD_DIM), lambda b: (b, 0, 0, 0))
    return pl.pallas_call(
        _attn_sample_kernel,
        grid=(DEC_BATCH,),
        in_specs=[pl.BlockSpec((1, KV_HEADS, nr, HEAD_DIM), lambda b: (b, 0, 0, 0)),
                  new_spec, new_spec, cache_spec, cache_spec],
        out_specs=pl.BlockSpec((1, DEC_SEQ, KV_W), lambda b: (b, 0, 0)),
        out_shape=jax.ShapeDtypeStruct((DEC_BATCH, DEC_SEQ, KV_W), F32),
        scratch_shapes=[pltpu.VMEM((nr, MAX_WINDOW), F32), pltpu.VMEM((nr, MAX_WINDOW), F32)],
        compiler_params=_params(("arbitrary",)),
        name="attn_sample",
    )(qh, knh, vnh, ckt, cvt)


def kernel(x_prompt, x_sample, cache_k_win, cache_v_win, state_conv, g_pre_mix, g_post_mix, g_pre_ffn, g_post_ffn,
           w_in, a_ln_g, a_ln_b, a_ws, a_bs, c_conv_w, c_conv_b, w_br_a, w_br_b, w_br_c, b_gate, w_o,
           w_ff_gate, w_ff_up, w_ff_down):
    xp = x_prompt.reshape(BATCH * SEQ, D_MODEL)
    xs = x_sample.reshape(SAMPLE_ROWS, D_MODEL)
    ckt = jnp.transpose(cache_k_win, (0, 1, 3, 4, 2))
    cvt = jnp.transpose(cache_v_win, (0, 1, 3, 4, 2))
    kp_l, vp_l, ks_l, vs_l, cp_l, cs_l, av_l = [], [], [], [], [], [], []
    for l in range(DEPTH):
        def row(a):
            return a[l].reshape(1, -1)

        w_in_b = w_in[l].astype(BF16)
        wa, wb, wc, wo = (w[l].astype(BF16) for w in (w_br_a, w_br_b, w_br_c, w_o))
        wg, wu, wd = (w[l].astype(BF16) for w in (w_ff_gate, w_ff_up, w_ff_down))
        bs_full = jnp.repeat(a_bs[l].T, A_GROUP_DIM, axis=1)

        wkt = jnp.transpose(w_in[l][:, OFF_K:OFF_K + KV_W]).astype(BF16)
        wvt = jnp.transpose(w_in[l][:, OFF_VV:OFF_VV + KV_W]).astype(BF16)
        oa, q, k, v, kt, vt, oc, gates, cs8 = _inproj_prompt(
            xp, row(g_pre_mix), w_in_b, wkt, wvt, row(a_ln_g), row(a_ln_b), a_ws[l], bs_full,
            c_conv_w[l], row(c_conv_b), row(b_gate))
        ob = _attn_prompt(q, k, v)
        xp = _merge(xp, oa, ob, oc, gates, wa, wb, wc, wo, row(g_post_mix), PROMPT_TM)
        xp = _ffn(xp, row(g_pre_ffn), wg, wu, wd, row(g_post_ffn), PROMPT_TM)

        def window(t):
            return jnp.transpose(t.reshape(BATCH, KV_HEADS, HEAD_DIM, WIN), (0, 3, 1, 2))

        kp_l.append(window(kt))
        vp_l.append(window(vt))
        cp_l.append(cs8[:, SUBLANES - (CONV_W - 1):])

        mix = jnp.tile(a_ws[l][:, :DEC_SEQ, :DEC_SEQ], (1, DEC_BATCH, DEC_BATCH))
        bs_s = jnp.tile(bs_full[:DEC_SEQ], (DEC_BATCH, 1))
        prev = state_conv[l]
        zeros = jnp.zeros((DEC_BATCH, 1, C_WIDTH), F32)
        p1 = jnp.concatenate([prev[:, 1:2], zeros, zeros, zeros], axis=1).reshape(SAMPLE_ROWS, C_WIDTH)
        p2 = jnp.concatenate([prev[:, 0:1], prev[:, 1:2], zeros, zeros], axis=1).reshape(SAMPLE_ROWS, C_WIDTH)
        oa_s, q_s, k_s, v_s, oc_s, gates_s, vn_s, cz_s = _inproj_sample(
            xs, row(g_pre_mix), w_in_b, row(a_ln_g), row(a_ln_b), mix, bs_s,
            c_conv_w[l], row(c_conv_b), row(b_gate), p1, p2)

        pad_t = ((0, 0), (0, 0), (0, 0), (0, SUBLANES - DEC_SEQ), (0, 0))
        qh = q_s.reshape(DEC_BATCH, DEC_SEQ, N_DIL, KV_HEADS, HEAD_DIM).transpose(0, 3, 2, 1, 4)
        qh = jnp.pad(qh, pad_t).reshape(DEC_BATCH, KV_HEADS, N_DIL * SUBLANES, HEAD_DIM)

        def new_rows(a):
            a = a.reshape(DEC_BATCH, DEC_SEQ, KV_HEADS, HEAD_DIM).transpose(0, 2, 1, 3)
            return jnp.pad(a, pad_t[1:])

        ob_s = _attn_sample(l, qh, new_rows(k_s), new_rows(v_s), ckt, cvt)
        ob_s = ob_s.reshape(SAMPLE_ROWS, KV_W)
        xs = _merge(xs, oa_s, ob_s, oc_s, gates_s, wa, wb, wc, wo, row(g_post_mix), SAMPLE_ROWS)
        xs = _ffn(xs, row(g_pre_ffn), wg, wu, wd, row(g_post_ffn), SAMPLE_ROWS)

        ks_l.append(k_s.reshape(DEC_BATCH, DEC_SEQ, KV_HEADS, HEAD_DIM))
        vs_l.append(v_s.reshape(DEC_BATCH, DEC_SEQ, KV_HEADS, HEAD_DIM))
        cs_l.append(cz_s.reshape(DEC_BATCH, DEC_SEQ, C_WIDTH)[:, DEC_SEQ - (CONV_W - 1):])
        av_l.append(vn_s.reshape(DEC_BATCH, DEC_SEQ, A_WIDTH))

    return (xp.reshape(BATCH, SEQ, D_MODEL), xs.reshape(DEC_BATCH, DEC_SEQ, D_MODEL),
            jnp.stack(kp_l), jnp.stack(vp_l), jnp.stack(ks_l), jnp.stack(vs_l),
            jnp.stack(cp_l), jnp.stack(cs_l), jnp.stack(av_l))
```

```python
import functools

import jax
import jax.numpy as jnp
from jax import lax
from jax.experimental import pallas as pl
from jax.experimental.pallas import tpu as pltpu

F32 = jnp.float32
BF16 = jnp.bfloat16

D_MODEL = 1024
BATCH = 2
SEQ = 8192
DEPTH = 2
DEC_BATCH = 32
DEC_SEQ = 4
CHUNK = 128
A_GROUPS = 4
A_GROUP_DIM = 128
A_WIDTH = A_GROUPS * A_GROUP_DIM
HEAD_DIM = 64
KV_HEADS = 4
DILS = (1, 4, 16)
N_DIL = len(DILS)
Q_W = N_DIL * KV_HEADS * HEAD_DIM
KV_W = KV_HEADS * HEAD_DIM
BAND = 128
MAX_WINDOW = 2048
C_WIDTH = 512
CONV_W = 3
D_FF = 2816
EPS = 1e-6

OFF_U = 0
OFF_V = OFF_U + A_WIDTH
OFF_Q = OFF_V + A_WIDTH
OFF_K = OFF_Q + Q_W
OFF_VV = OFF_K + KV_W
OFF_CX = OFF_VV + KV_W
OFF_CB = OFF_CX + C_WIDTH
OFF_CC = OFF_CB + C_WIDTH
OFF_G = OFF_CC + C_WIDTH
IN_W = OFF_G + 3 * D_MODEL

LANES = 128
SUBLANES = 8
VMEM_LIMIT = 56 * 1024 * 1024
NEG = -1e30

SAMPLE_ROWS = DEC_BATCH * DEC_SEQ
WIN = min(MAX_WINDOW, SEQ)
SPAN = BAND * DILS[-1]
RES = DILS[1]
PROMPT_TM = 512
INPROJ_SUB = 256
LOG2E = 1.4426950408889634
QSCALE = LOG2E * HEAD_DIM ** -0.5

SLOPES = tuple(
    tuple(2.0 ** (-8.0 * (g * KV_HEADS + h + 1) / (N_DIL * KV_HEADS)) for h in range(KV_HEADS))
    for g in range(N_DIL)
)


def _rms(x, g):
    return x * lax.rsqrt(jnp.mean(x * x, axis=-1, keepdims=True) + EPS) * g


def _layer_norm(x, g, b):
    mu = jnp.mean(x, axis=-1, keepdims=True)
    xc = x - mu
    return xc * lax.rsqrt(jnp.mean(xc * xc, axis=-1, keepdims=True) + EPS) * g + b


def _dot(a, b):
    return jnp.dot(a, b, preferred_element_type=F32)


def _dot_nt(a, b):
    return lax.dot_general(a, b, (((1,), (1,)), ((), ())), preferred_element_type=F32)


def _select_rows(hid, vals):
    out = vals[-1]
    for h in range(len(vals) - 2, -1, -1):
        out = jnp.where(hid == h, vals[h], out)
    return out


def _layer_spec(layer, shape):
    nd = len(shape)
    return pl.BlockSpec((None,) + tuple(shape), lambda *_: (layer,) + (0,) * nd, pipeline_mode=pl.Buffered(1))


def _params(sem):
    return pltpu.CompilerParams(dimension_semantics=sem, vmem_limit_bytes=VMEM_LIMIT)


def _inproj_prompt_kernel(x_ref, gpre_ref, w_ref, wkt_ref, wvt_ref, lng_ref, lnb_ref, ws_ref, bs_ref, cw_ref,
                          cbias_ref, bg_ref,
                          oa_ref, q_ref, k_ref, v_ref, kt_ref, vt_ref, oc_ref, gate_ref, cs_ref, prev_ref):
    tm = x_ref.shape[0]
    i = pl.program_id(1)

    @pl.when(i == 0)
    def _():
        prev_ref[...] = jnp.zeros_like(prev_ref)

    row = lax.broadcasted_iota(jnp.int32, (CHUNK, CHUNK), 0)
    col = lax.broadcasted_iota(jnp.int32, (CHUNK, CHUNK), 1)
    wt = [jnp.where(row >= col, ws_ref[g], 0.0).astype(BF16) for g in range(A_GROUPS)]
    rsub = lax.broadcasted_iota(jnp.int32, (INPROJ_SUB, 1), 0)

    def sub_tile(r0, tail_in):
        rows = slice(r0, r0 + INPROJ_SUB)
        h = _rms(x_ref[rows, :], gpre_ref[...]).astype(BF16)

        def proj(off, width):
            return _dot(h, w_ref[:, off:off + width])

        u = jax.nn.gelu(proj(OFF_U, A_WIDTH))
        vn = _layer_norm(jax.nn.gelu(proj(OFF_V, A_WIDTH)), lng_ref[...], lnb_ref[...])
        for g in range(A_GROUPS):
            cs = slice(g * A_GROUP_DIM, (g + 1) * A_GROUP_DIM)
            for c in range(INPROJ_SUB // CHUNK):
                rs = slice(c * CHUNK, (c + 1) * CHUNK)
                sa = _dot(wt[g], vn[rs, cs].astype(BF16)) + bs_ref[:, g:g + 1]
                oa_ref[r0 + c * CHUNK:r0 + (c + 1) * CHUNK, cs] = (u[rs, cs] * sa).astype(BF16)

        qv = proj(OFF_Q, Q_W)
        for c in range(Q_W // LANES):
            q_ref[c, rows, :] = qv[:, c * LANES:(c + 1) * LANES] * QSCALE
        kv = proj(OFF_K, 2 * KV_W)
        for c in range(KV_W // LANES):
            k_ref[c, rows, :] = kv[:, c * LANES:(c + 1) * LANES]
            v_ref[c, rows, :] = kv[:, KV_W + c * LANES:KV_W + (c + 1) * LANES]

        cz = proj(OFF_CC, C_WIDTH) * proj(OFF_CX, C_WIDTH)
        p2 = tail_in[SUBLANES - 2:SUBLANES - 1, :]
        p1 = tail_in[SUBLANES - 1:SUBLANES, :]
        z1 = jnp.where(rsub == 0, p1, pltpu.roll(cz, 1, 0))
        z2 = jnp.where(rsub == 0, p2, jnp.where(rsub == 1, p1, pltpu.roll(cz, 2, 0)))
        y = cbias_ref[...] + cw_ref[0:1, :] * z2 + cw_ref[1:2, :] * z1 + cw_ref[2:3, :] * cz
        oc_ref[rows, :] = (proj(OFF_CB, C_WIDTH) * y).astype(BF16)

        for j in range(3):
            cs = slice(j * D_MODEL, (j + 1) * D_MODEL)
            gate_ref[rows, cs] = jax.nn.sigmoid(proj(OFF_G + j * D_MODEL, D_MODEL) + bg_ref[:, cs]).astype(BF16)
        return h, cz[INPROJ_SUB - SUBLANES:INPROJ_SUB, :]

    tail = prev_ref[...]
    hs = []
    for r0 in range(0, tm, INPROJ_SUB):
        h, tail = sub_tile(r0, tail)
        hs.append(h)
    prev_ref[...] = tail
    cs_ref[0] = tail

    @pl.when(i >= pl.num_programs(1) - WIN // tm)
    def _():
        for j, h in enumerate(hs):
            cols = slice(j * INPROJ_SUB, (j + 1) * INPROJ_SUB)
            kt_ref[0, :, cols] = _dot_nt(wkt_ref[...], h)
            vt_ref[0, :, cols] = _dot_nt(wvt_ref[...], h)


def _inproj_prompt(layer, x, gpre, w_in, wkt, wvt, lng, lnb, ws, bs_t, cw, cbias, bg):
    tm = PROMPT_TM
    nt = SEQ // tm
    rows = BATCH * SEQ
    win_spec = pl.BlockSpec((1, KV_W, tm), lambda b, i: (b, 0, jnp.maximum(i - (nt - WIN // tm), 0)))

    def rowblk(width):
        return pl.BlockSpec((tm, width), lambda b, i: (b * nt + i, 0))

    def slab(n):
        return pl.BlockSpec((n, tm, LANES), lambda b, i: (0, b * nt + i, 0))

    return pl.pallas_call(
        _inproj_prompt_kernel,
        grid=(BATCH, nt),
        in_specs=[
            rowblk(D_MODEL),
            _layer_spec(layer, (1, D_MODEL)),
            _layer_spec(layer, (D_MODEL, IN_W)),
            _layer_spec(layer, (KV_W, D_MODEL)),
            _layer_spec(layer, (KV_W, D_MODEL)),
            _layer_spec(layer, (1, A_WIDTH)),
            _layer_spec(layer, (1, A_WIDTH)),
            _layer_spec(layer, (A_GROUPS, CHUNK, CHUNK)),
            _layer_spec(layer, (CHUNK, A_GROUPS)),
            _layer_spec(layer, (CONV_W, C_WIDTH)),
            _layer_spec(layer, (1, C_WIDTH)),
            _layer_spec(layer, (1, 3 * D_MODEL)),
        ],
        out_specs=[
            rowblk(A_WIDTH),
            slab(Q_W // LANES),
            slab(KV_W // LANES),
            slab(KV_W // LANES),
            win_spec,
            win_spec,
            rowblk(C_WIDTH),
            rowblk(3 * D_MODEL),
            pl.BlockSpec((1, SUBLANES, C_WIDTH), lambda b, i: (b, 0, 0)),
        ],
        out_shape=[
            jax.ShapeDtypeStruct((rows, A_WIDTH), BF16),
            jax.ShapeDtypeStruct((Q_W // LANES, rows, LANES), F32),
            jax.ShapeDtypeStruct((KV_W // LANES, rows, LANES), F32),
            jax.ShapeDtypeStruct((KV_W // LANES, rows, LANES), F32),
            jax.ShapeDtypeStruct((BATCH, KV_W, WIN), F32),
            jax.ShapeDtypeStruct((BATCH, KV_W, WIN), F32),
            jax.ShapeDtypeStruct((rows, C_WIDTH), BF16),
            jax.ShapeDtypeStruct((rows, 3 * D_MODEL), BF16),
            jax.ShapeDtypeStruct((BATCH, SUBLANES, C_WIDTH), F32),
        ],
        scratch_shapes=[pltpu.VMEM((SUBLANES, C_WIDTH), F32)],
        compiler_params=_params(("arbitrary", "arbitrary")),
        name="inproj_prompt",
    )(x, gpre, w_in, wkt, wvt, lng, lnb, ws, bs_t, cw, cbias, bg)


def _attn_prompt_kernel(q_ref, k_ref, v_ref, o_ref, ktail, vtail, k4, v4, q24, acc, mrun, lrun, tmp, bias_ref):
    b = pl.program_id(0)
    i = pl.program_id(1)
    first = i == 0
    nslab = KV_W // LANES
    quarter = SPAN // RES

    @pl.when(jnp.logical_and(b == 0, first))
    def _():
        rr = lax.broadcasted_iota(jnp.int32, (KV_HEADS * BAND, 2 * BAND), 0)
        kk = lax.broadcasted_iota(jnp.int32, (KV_HEADS * BAND, 2 * BAND), 1)
        dist = (rr & (BAND - 1)) + BAND - kk
        hrow = rr >> 7
        ok = jnp.logical_and(dist >= 0, dist <= BAND)
        for g, dil in enumerate(DILS):
            coef = _select_rows(hrow, [jnp.float32(SLOPES[g][h] * dil * LOG2E) for h in range(KV_HEADS)])
            full = jnp.where(ok, -(coef * dist.astype(F32)), NEG)
            bias_ref[2 * g] = full
            bias_ref[2 * g + 1] = jnp.where(kk >= BAND, full, NEG)

    @pl.when(first)
    def _():
        for r in range(RES):
            base = r * 2 * quarter
            k4[:, base:base + quarter, :] = jnp.zeros((nslab, quarter, LANES), F32)
            v4[:, base:base + quarter, :] = jnp.zeros((nslab, quarter, LANES), F32)
        ktail[...] = jnp.zeros_like(ktail)
        vtail[...] = jnp.zeros_like(vtail)

    @pl.when(i > 0)
    def _():
        for r in range(RES):
            base = r * 2 * quarter
            k4[:, base:base + quarter, :] = k4[:, base + quarter:base + 2 * quarter, :]
            v4[:, base:base + quarter, :] = v4[:, base + quarter:base + 2 * quarter, :]

    for r in range(RES):
        base = r * 2 * quarter + quarter
        for c in range(nslab):
            k4[c, base:base + quarter, :] = k_ref[c, pl.ds(r, quarter, stride=RES), :]
            v4[c, base:base + quarter, :] = v_ref[c, pl.ds(r, quarter, stride=RES), :]
            q24[c, r * quarter:(r + 1) * quarter, :] = q_ref[2 * nslab + c, pl.ds(r, quarter, stride=RES), :]

    lane = lax.broadcasted_iota(jnp.int32, (1, KV_W), 1)
    hid = lane >> 6
    headmask = [jnp.where(hid == h, 1.0, 0.0).astype(BF16) for h in range(KV_HEADS)]

    def wide(ref, rows, slab0=0):
        return jnp.concatenate([ref[slab0 + c, rows, :] for c in range(nslab)], axis=1)

    def softmax_unit(qb, kb, vb, bias):
        qb16 = qb.astype(BF16)
        qs = jnp.concatenate([qb16 * headmask[h] for h in range(KV_HEADS)], axis=0)
        s = _dot_nt(qs, kb.astype(BF16)) + bias
        m = jnp.max(s, axis=-1, keepdims=True)
        p = jnp.exp2(s - m)
        l = jnp.sum(p, axis=-1, keepdims=True)
        r = _dot(p.astype(BF16), vb.astype(BF16))

        def per_head(x):
            return _select_rows(hid, [x[h * BAND:(h + 1) * BAND] for h in range(KV_HEADS)])

        return per_head(r), per_head(m), per_head(l)

    def store_stats(rows, o, m, l):
        for c in range(nslab):
            ls = slice(c * LANES, (c + 1) * LANES)
            acc[c, rows, :] = o[:, ls]
            mrun[c, rows, :] = m[:, ls]
            lrun[c, rows, :] = l[:, ls]

    def merge_stats(rows, o_u, m_u, l_u):
        m_old = wide(mrun, rows)
        m_n = jnp.maximum(m_old, m_u)
        a_old = jnp.exp2(m_old - m_n)
        a_new = jnp.exp2(m_u - m_n)
        store_stats(rows, wide(acc, rows) * a_old + o_u * a_new, m_n, wide(lrun, rows) * a_old + l_u * a_new)

    def unit0(n, kb, vb, bias):
        q0 = n * BAND if isinstance(n, int) else pl.multiple_of(n * BAND, BAND)
        stats = softmax_unit(wide(q_ref, pl.ds(q0, BAND)), kb, vb, bias)
        sub = BAND // RES
        for j, x in enumerate(stats):
            for c in range(nslab):
                tmp[j * nslab + c, pl.ds(q0, BAND), :] = x[:, c * LANES:(c + 1) * LANES]
        for r in range(RES):
            src = pl.ds(q0 + r, sub, stride=RES)
            dst = pl.ds(r * quarter + n * sub, sub)
            for c in range(nslab):
                acc[c, dst, :] = tmp[c, src, :]
                mrun[c, dst, :] = tmp[nslab + c, src, :]
                lrun[c, dst, :] = tmp[2 * nslab + c, src, :]

    kb0 = jnp.concatenate([wide(ktail, pl.ds(0, BAND)), wide(k_ref, pl.ds(0, BAND))], axis=0)
    vb0 = jnp.concatenate([wide(vtail, pl.ds(0, BAND)), wide(v_ref, pl.ds(0, BAND))], axis=0)
    unit0(0, kb0, vb0, bias_ref[jnp.where(first, 1, 0)])

    def body0(n, carry):
        keys = pl.ds(pl.multiple_of((n - 1) * BAND, BAND), 2 * BAND)
        unit0(n, wide(k_ref, keys), wide(v_ref, keys), bias_ref[0])
        return carry

    lax.fori_loop(1, SPAN // BAND, body0, 0, unroll=3)

    def body1(s, carry):
        bias = bias_ref[2 + jnp.where(jnp.logical_and(first, s == 0), 1, 0)]
        for r in range(RES):
            qb = wide(q_ref, pl.ds(s * (BAND * RES) + r, BAND, stride=RES), nslab)
            keys = pl.ds(r * 2 * quarter + quarter + (s - 1) * BAND, 2 * BAND)
            o_u, m_u, l_u = softmax_unit(qb, wide(k4, keys), wide(v4, keys), bias)
            merge_stats(pl.ds(r * quarter + s * BAND, BAND), o_u, m_u, l_u)
        return carry

    lax.fori_loop(0, quarter // BAND, body1, 0)

    def body2(r, carry):
        bias = bias_ref[4 + jnp.where(first, 1, 0)]
        for a in range(DILS[2] // RES):
            rows = pl.ds(r * quarter + a, BAND, stride=RES)
            keys = pl.ds(r * 2 * quarter + a, 2 * BAND, stride=RES)
            o_u, m_u, l_u = softmax_unit(wide(q24, rows), wide(k4, keys), wide(v4, keys), bias)
            merge_stats(rows, o_u, m_u, l_u)
        return carry

    lax.fori_loop(0, RES, body2, 0)

    for r in range(RES):
        rows = pl.ds(r * quarter, quarter)
        for c in range(nslab):
            tmp[c, pl.ds(r, quarter, stride=RES), :] = acc[c, rows, :] / lrun[c, rows, :]
    for c in range(nslab):
        o_ref[:, c * LANES:(c + 1) * LANES] = tmp[c].astype(BF16)
    ktail[...] = k_ref[:, SPAN - BAND:SPAN, :]
    vtail[...] = v_ref[:, SPAN - BAND:SPAN, :]


def _attn_prompt(q, k, v):
    ns = SEQ // SPAN
    nslab = KV_W // LANES
    rows = BATCH * SEQ

    def slab(n):
        return pl.BlockSpec((n, SPAN, LANES), lambda b, i: (0, b * ns + i, 0))

    return pl.pallas_call(
        _attn_prompt_kernel,
        grid=(BATCH, ns),
        in_specs=[slab(Q_W // LANES), slab(nslab), slab(nslab)],
        out_specs=pl.BlockSpec((SPAN, KV_W), lambda b, i: (b * ns + i, 0)),
        out_shape=jax.ShapeDtypeStruct((rows, KV_W), BF16),
        scratch_shapes=[
            pltpu.VMEM((nslab, BAND, LANES), F32),
            pltpu.VMEM((nslab, BAND, LANES), F32),
            pltpu.VMEM((nslab, 2 * SPAN, LANES), F32),
            pltpu.VMEM((nslab, 2 * SPAN, LANES), F32),
            pltpu.VMEM((nslab, SPAN, LANES), F32),
            pltpu.VMEM((nslab, SPAN, LANES), F32),
            pltpu.VMEM((nslab, SPAN, LANES), F32),
            pltpu.VMEM((nslab, SPAN, LANES), F32),
            pltpu.VMEM((3 * nslab, SPAN, LANES), F32),
            pltpu.VMEM((2 * N_DIL, KV_HEADS * BAND, 2 * BAND), F32),
        ],
        compiler_params=_params(("arbitrary", "arbitrary")),
        name="attn_prompt",
    )(q, k, v)


def _merge_kernel(x_ref, oa_ref, ob_ref, oc_ref, gate_ref, wa_ref, wb_ref, wc_ref, wo_ref, gpost_ref, o_ref):
    def gate(j):
        return gate_ref[:, j * D_MODEL:(j + 1) * D_MODEL].astype(F32)

    merged = (gate(0) * _dot(oa_ref[...], wa_ref[...])
              + gate(1) * _dot(ob_ref[...].astype(BF16), wb_ref[...])
              + gate(2) * _dot(oc_ref[...], wc_ref[...]))
    y = _dot(merged.astype(BF16), wo_ref[...])
    o_ref[...] = x_ref[...] + _rms(y, gpost_ref[...])


def _merge(layer, x, oa, ob, oc, gates, wa, wb, wc, wo, gpost, tm):
    rows = x.shape[0]

    def rowblk(width):
        return pl.BlockSpec((tm, width), lambda i: (i, 0))

    return pl.pallas_call(
        _merge_kernel,
        grid=(rows // tm,),
        in_specs=[
            rowblk(D_MODEL), rowblk(A_WIDTH), rowblk(KV_W), rowblk(C_WIDTH), rowblk(3 * D_MODEL),
            _layer_spec(layer, (A_WIDTH, D_MODEL)), _layer_spec(layer, (KV_W, D_MODEL)),
            _layer_spec(layer, (C_WIDTH, D_MODEL)), _layer_spec(layer, (D_MODEL, D_MODEL)),
            _layer_spec(layer, (1, D_MODEL)),
        ],
        out_specs=rowblk(D_MODEL),
        out_shape=jax.ShapeDtypeStruct((rows, D_MODEL), F32),
        compiler_params=_params(("parallel",)),
        name="merge",
    )(x, oa, ob, oc, gates, wa, wb, wc, wo, gpost)


def _ffn_kernel(x_ref, gpre_ref, wg_ref, wu_ref, wd_ref, gpost_ref, o_ref):
    x = x_ref[...]
    h = _rms(x, gpre_ref[...]).astype(BF16)
    act = (jax.nn.silu(_dot(h, wg_ref[...])) * _dot(h, wu_ref[...])).astype(BF16)
    o_ref[...] = x + _rms(_dot(act, wd_ref[...]), gpost_ref[...])


def _ffn(layer, x, gpre, wg, wu, wd, gpost, tm):
    rows = x.shape[0]
    rowblk = pl.BlockSpec((tm, D_MODEL), lambda i: (i, 0))
    return pl.pallas_call(
        _ffn_kernel,
        grid=(rows // tm,),
        in_specs=[rowblk, _layer_spec(layer, (1, D_MODEL)), _layer_spec(layer, (D_MODEL, D_FF)),
                  _layer_spec(layer, (D_MODEL, D_FF)), _layer_spec(layer, (D_FF, D_MODEL)),
                  _layer_spec(layer, (1, D_MODEL))],
        out_specs=rowblk,
        out_shape=jax.ShapeDtypeStruct((rows, D_MODEL), F32),
        compiler_params=_params(("parallel",)),
        name="ffn",
    )(x, gpre, wg, wu, wd, gpost)


def _inproj_sample_kernel(ws_ref, bs_ref, x_ref, gpre_ref, w_ref, lng_ref, lnb_ref, cw_ref, cbias_ref, bg_ref,
                          p1_ref, p2_ref,
                          oa_ref, q_ref, k_ref, v_ref, oc_ref, gate_ref, vn_ref, cz_ref, *, layer):
    n = SAMPLE_ROWS
    h = _rms(x_ref[...], gpre_ref[...]).astype(BF16)

    def proj(off, width):
        return _dot(h, w_ref[:, off:off + width])

    u = jax.nn.gelu(proj(OFF_U, A_WIDTH))
    vn = _layer_norm(jax.nn.gelu(proj(OFF_V, A_WIDTH)), lng_ref[...], lnb_ref[...])
    vn_ref[...] = vn
    t = lax.broadcasted_iota(jnp.int32, (n, 1), 0) & (DEC_SEQ - 1)
    for g in range(A_GROUPS):
        cs = slice(g * A_GROUP_DIM, (g + 1) * A_GROUP_DIM)
        base = (layer * A_GROUPS + g) * DEC_SEQ
        vg = vn[:, cs]
        sa = _select_rows(t, [bs_ref[base + tt] for tt in range(DEC_SEQ)])
        for k in range(DEC_SEQ):
            coef = jnp.zeros((n, 1), F32)
            for tt in range(k, DEC_SEQ):
                coef = jnp.where(t == tt, ws_ref[(base + tt) * DEC_SEQ + tt - k], coef)
            sa = sa + coef * (vg if k == 0 else pltpu.roll(vg, k, 0))
        oa_ref[:, cs] = (u[:, cs] * sa).astype(BF16)

    q_ref[...] = proj(OFF_Q, Q_W)
    kv = proj(OFF_K, 2 * KV_W)
    k_ref[...] = kv[:, 0:KV_W]
    v_ref[...] = kv[:, KV_W:2 * KV_W]

    cz = proj(OFF_CC, C_WIDTH) * proj(OFF_CX, C_WIDTH)
    cz_ref[...] = cz
    z1 = jnp.where(t >= 1, pltpu.roll(cz, 1, 0), p1_ref[...])
    z2 = jnp.where(t >= 2, pltpu.roll(cz, 2, 0), p2_ref[...])
    y = cbias_ref[...] + cw_ref[0:1, :] * z2 + cw_ref[1:2, :] * z1 + cw_ref[2:3, :] * cz
    oc_ref[...] = (proj(OFF_CB, C_WIDTH) * y).astype(BF16)

    for j in range(3):
        cs = slice(j * D_MODEL, (j + 1) * D_MODEL)
        gate_ref[:, cs] = jax.nn.sigmoid(proj(OFF_G + j * D_MODEL, D_MODEL) + bg_ref[:, cs]).astype(BF16)


def _inproj_sample(layer, ws4, bs4, x, gpre, w_in, lng, lnb, cw, cbias, bg, p1, p2):
    n = SAMPLE_ROWS
    smem = pl.BlockSpec(memory_space=pltpu.SMEM)

    def full(shape):
        nd = len(shape)
        return pl.BlockSpec(shape, lambda i: (0,) * nd)

    widths_dtypes = [(A_WIDTH, BF16), (Q_W, F32), (KV_W, F32), (KV_W, F32), (C_WIDTH, BF16),
                     (3 * D_MODEL, BF16), (A_WIDTH, F32), (C_WIDTH, F32)]
    return pl.pallas_call(
        functools.partial(_inproj_sample_kernel, layer=layer),
        grid=(1,),
        in_specs=[smem, smem, full((n, D_MODEL)), _layer_spec(layer, (1, D_MODEL)),
                  _layer_spec(layer, (D_MODEL, IN_W)), _layer_spec(layer, (1, A_WIDTH)),
                  _layer_spec(layer, (1, A_WIDTH)), _layer_spec(layer, (CONV_W, C_WIDTH)),
                  _layer_spec(layer, (1, C_WIDTH)), _layer_spec(layer, (1, 3 * D_MODEL)),
                  _layer_spec(layer, (n, C_WIDTH)), _layer_spec(layer, (n, C_WIDTH))],
        out_specs=[full((n, w)) for w, _ in widths_dtypes],
        out_shape=[jax.ShapeDtypeStruct((n, w), dt) for w, dt in widths_dtypes],
        compiler_params=_params(("arbitrary",)),
        name="inproj_sample",
    )(ws4, bs4, x, gpre, w_in, lng, lnb, cw, cbias, bg, p1, p2)


def _attn_sample_kernel(qh_ref, kn_ref, vn_ref, kt_ref, vt_ref, o_ref, bias_ref):
    nr = N_DIL * SUBLANES
    width = MAX_WINDOW

    @pl.when(pl.program_id(0) == 0)
    def _():
        row = lax.broadcasted_iota(jnp.int32, (nr, width), 0)
        pos = lax.broadcasted_iota(jnp.int32, (nr, width), 1)
        grp = row >> 3
        delta = width + (row & (SUBLANES - 1)) - pos
        dil_m1 = _select_rows(grp, [d - 1 for d in DILS])
        reach = _select_rows(grp, [BAND * d for d in DILS])
        ok = jnp.logical_and((delta & dil_m1) == 0, delta <= reach)
        for h in range(KV_HEADS):
            slope = _select_rows(grp, [jnp.float32(SLOPES[g][h]) for g in range(N_DIL)])
            bias_ref[h * nr:(h + 1) * nr, :] = jnp.where(ok, -(slope * delta.astype(F32)), NEG)

    ridx = lax.broadcasted_iota(jnp.int32, (nr, 1), 0)
    g_of = ridx >> 3
    t_of = ridx & (SUBLANES - 1)
    scale = HEAD_DIM ** -0.5
    qs = [qh_ref[0, h] * scale for h in range(KV_HEADS)]
    s = jnp.concatenate([_dot(qs[h].astype(BF16), kt_ref[0, 0, h].astype(BF16)) for h in range(KV_HEADS)], axis=0)
    s = s + bias_ref[...]
    m = jnp.max(s, axis=-1, keepdims=True)
    s_new = []
    for n in range(DEC_SEQ):
        dn = t_of - n
        ok = jnp.logical_or(dn == 0, jnp.logical_and(g_of == 0, dn > 0))
        rows = []
        for h in range(KV_HEADS):
            slope = _select_rows(g_of, [jnp.float32(SLOPES[g][h]) for g in range(N_DIL)])
            raw = jnp.sum(qs[h] * kn_ref[0, h, n:n + 1, :], axis=-1, keepdims=True)
            rows.append(jnp.where(ok, raw - slope * dn.astype(F32), NEG))
        sn = jnp.concatenate(rows, axis=0)
        s_new.append(sn)
        m = jnp.maximum(m, sn)
    p = jnp.exp(s - m)
    l_all = jnp.sum(p, axis=-1, keepdims=True)
    p16 = p.astype(BF16)
    p_new = [jnp.exp(sn - m) for sn in s_new]
    for pn in p_new:
        l_all = l_all + pn
    for h in range(KV_HEADS):
        hr = slice(h * nr, (h + 1) * nr)
        r = _dot_nt(p16[hr], vt_ref[0, 0, h].astype(BF16))
        for n in range(DEC_SEQ):
            r = r + p_new[n][hr] * vn_ref[0, h, n:n + 1, :]
        m_h = m[hr]
        l_h = l_all[hr]
        parts = [(r[g * SUBLANES:(g + 1) * SUBLANES], m_h[g * SUBLANES:(g + 1) * SUBLANES],
                  l_h[g * SUBLANES:(g + 1) * SUBLANES]) for g in range(N_DIL)]
        m_all = jnp.maximum(jnp.maximum(parts[0][1], parts[1][1]), parts[2][1])
        num = jnp.zeros((SUBLANES, HEAD_DIM), F32)
        den = jnp.zeros((SUBLANES, 1), F32)
        for o_g, m_g, l_g in parts:
            w = jnp.exp(m_g - m_all)
            num = num + o_g * w
            den = den + l_g * w
        o_ref[0, :, h * HEAD_DIM:(h + 1) * HEAD_DIM] = (num / den)[0:DEC_SEQ]


def _attn_sample(layer, qh, knh, vnh, ckt, cvt):
    nr = N_DIL * SUBLANES
    cache_spec = pl.BlockSpec((1, 1, KV_HEADS, HEAD_DIM, MAX_WINDOW), lambda b: (layer, b, 0, 0, 0))
    new_spec = pl.BlockSpec((1, KV_HEADS, SUBLANES, HEAD_DIM), lambda b: (b, 0, 0, 0))
    return pl.pallas_call(
        _attn_sample_kernel,
        grid=(DEC_BATCH,),
        in_specs=[pl.BlockSpec((1, KV_HEADS, nr, HEAD_DIM), lambda b: (b, 0, 0, 0)),
                  new_spec, new_spec, cache_spec, cache_spec],
        out_specs=pl.BlockSpec((1, DEC_SEQ, KV_W), lambda b: (b, 0, 0)),
        out_shape=jax.ShapeDtypeStruct((DEC_BATCH, DEC_SEQ, KV_W), F32),
        scratch_shapes=[pltpu.VMEM((KV_HEADS * nr, MAX_WINDOW), F32)],
        compiler_params=_params(("arbitrary",)),
        name="attn_sample",
    )(qh, knh, vnh, ckt, cvt)


def kernel(x_prompt, x_sample, cache_k_win, cache_v_win, state_conv, g_pre_mix, g_post_mix, g_pre_ffn, g_post_ffn,
           w_in, a_ln_g, a_ln_b, a_ws, a_bs, c_conv_w, c_conv_b, w_br_a, w_br_b, w_br_c, b_gate, w_o,
           w_ff_gate, w_ff_up, w_ff_down):
    xp = x_prompt.reshape(BATCH * SEQ, D_MODEL)
    xs = x_sample.reshape(SAMPLE_ROWS, D_MODEL)
    ckt = jnp.transpose(cache_k_win, (0, 1, 3, 4, 2))
    cvt = jnp.transpose(cache_v_win, (0, 1, 3, 4, 2))

    def vec(a):
        return a.reshape(DEPTH, 1, -1)

    w_in_b = w_in.astype(BF16)
    wkt = jnp.swapaxes(w_in[:, :, OFF_K:OFF_K + KV_W], 1, 2).astype(BF16)
    wvt = jnp.swapaxes(w_in[:, :, OFF_VV:OFF_VV + KV_W], 1, 2).astype(BF16)
    wa, wb, wc, wo = (w.astype(BF16) for w in (w_br_a, w_br_b, w_br_c, w_o))
    wg, wu, wd = (w.astype(BF16) for w in (w_ff_gate, w_ff_up, w_ff_down))
    gpm, gqm, gpf, gqf = vec(g_pre_mix), vec(g_post_mix), vec(g_pre_ffn), vec(g_post_ffn)
    lng, lnb, cbias, bg = vec(a_ln_g), vec(a_ln_b), vec(c_conv_b), vec(b_gate)
    bs_t = jnp.swapaxes(a_bs, 1, 2)
    ws4 = a_ws[:, :, :DEC_SEQ, :DEC_SEQ].reshape(-1)
    bs4 = a_bs[:, :, :DEC_SEQ].reshape(-1)
    zeros = jnp.zeros((DEPTH, DEC_BATCH, 1, C_WIDTH), F32)
    p1 = jnp.concatenate([state_conv[:, :, 1:2], zeros, zeros, zeros], axis=2).reshape(DEPTH, SAMPLE_ROWS, C_WIDTH)
    p2 = jnp.concatenate([state_conv, zeros, zeros], axis=2).reshape(DEPTH, SAMPLE_ROWS, C_WIDTH)

    kp_l, vp_l, ks_l, vs_l, cp_l, cs_l, av_l = [], [], [], [], [], [], []
    for l in range(DEPTH):
        oa, q, k, v, kt, vt, oc, gates, cs8 = _inproj_prompt(
            l, xp, gpm, w_in_b, wkt, wvt, lng, lnb, a_ws, bs_t, c_conv_w, cbias, bg)
        ob = _attn_prompt(q, k, v)
        xp = _merge(l, xp, oa, ob, oc, gates, wa, wb, wc, wo, gqm, PROMPT_TM)
        xp = _ffn(l, xp, gpf, wg, wu, wd, gqf, PROMPT_TM)

        def window(t):
            return jnp.transpose(t.reshape(BATCH, KV_HEADS, HEAD_DIM, WIN), (0, 3, 1, 2))

        kp_l.append(window(kt))
        vp_l.append(window(vt))
        cp_l.append(cs8[:, SUBLANES - (CONV_W - 1):])

        oa_s, q_s, k_s, v_s, oc_s, gates_s, vn_s, cz_s = _inproj_sample(
            l, ws4, bs4, xs, gpm, w_in_b, lng, lnb, c_conv_w, cbias, bg, p1, p2)

        pad_t = ((0, 0), (0, 0), (0, 0), (0, SUBLANES - DEC_SEQ), (0, 0))
        qh = q_s.reshape(DEC_BATCH, DEC_SEQ, N_DIL, KV_HEADS, HEAD_DIM).transpose(0, 3, 2, 1, 4)
        qh = jnp.pad(qh, pad_t).reshape(DEC_BATCH, KV_HEADS, N_DIL * SUBLANES, HEAD_DIM)

        def new_rows(a):
            a = a.reshape(DEC_BATCH, DEC_SEQ, KV_HEADS, HEAD_DIM).transpose(0, 2, 1, 3)
            return jnp.pad(a, pad_t[1:])

        ob_s = _attn_sample(l, qh, new_rows(k_s), new_rows(v_s), ckt, cvt)
        ob_s = ob_s.reshape(SAMPLE_ROWS, KV_W)
        xs = _merge(l, xs, oa_s, ob_s, oc_s, gates_s, wa, wb, wc, wo, gqm, SAMPLE_ROWS)
        xs = _ffn(l, xs, gpf, wg, wu, wd, gqf, SAMPLE_ROWS)

        ks_l.append(k_s.reshape(DEC_BATCH, DEC_SEQ, KV_HEADS, HEAD_DIM))
        vs_l.append(v_s.reshape(DEC_BATCH, DEC_SEQ, KV_HEADS, HEAD_DIM))
        cs_l.append(cz_s.reshape(DEC_BATCH, DEC_SEQ, C_WIDTH)[:, DEC_SEQ - (CONV_W - 1):])
        av_l.append(vn_s.reshape(DEC_BATCH, DEC_SEQ, A_WIDTH))

    return (xp.reshape(BATCH, SEQ, D_MODEL), xs.reshape(DEC_BATCH, DEC_SEQ, D_MODEL),
            jnp.stack(kp_l), jnp.stack(vp_l), jnp.stack(ks_l), jnp.stack(vs_l),
            jnp.stack(cp_l), jnp.stack(cs_l), jnp.stack(av_l))
```

```python
import functools

import jax
import jax.numpy as jnp
from jax import lax
from jax.experimental import pallas as pl
from jax.experimental.pallas import tpu as pltpu

F32 = jnp.float32
BF16 = jnp.bfloat16

D_MODEL = 1024
BATCH = 2
SEQ = 8192
DEPTH = 2
DEC_BATCH = 32
DEC_SEQ = 4
CHUNK = 128
A_GROUPS = 4
A_GROUP_DIM = 128
A_WIDTH = A_GROUPS * A_GROUP_DIM
HEAD_DIM = 64
KV_HEADS = 4
DILS = (1, 4, 16)
N_DIL = len(DILS)
Q_W = N_DIL * KV_HEADS * HEAD_DIM
KV_W = KV_HEADS * HEAD_DIM
BAND = 128
MAX_WINDOW = 2048
C_WIDTH = 512
CONV_W = 3
D_FF = 2816
EPS = 1e-6

OFF_U = 0
OFF_V = OFF_U + A_WIDTH
OFF_Q = OFF_V + A_WIDTH
OFF_K = OFF_Q + Q_W
OFF_VV = OFF_K + KV_W
OFF_CX = OFF_VV + KV_W
OFF_CB = OFF_CX + C_WIDTH
OFF_CC = OFF_CB + C_WIDTH
OFF_G = OFF_CC + C_WIDTH
IN_W = OFF_G + 3 * D_MODEL

LANES = 128
SUBLANES = 8
VMEM_LIMIT = 56 * 1024 * 1024
NEG = -1e30

SAMPLE_ROWS = DEC_BATCH * DEC_SEQ
WIN = min(MAX_WINDOW, SEQ)
SPAN = BAND * DILS[-1]
RES = DILS[1]
PROMPT_TM = 512
INPROJ_SUB = 256
INPROJ_LAG = 2
MXU_N = 256
FF_CHUNKS = ((0, 6 * MXU_N), (6 * MXU_N, D_FF))
LOG2E = 1.4426950408889634
QSCALE = LOG2E * HEAD_DIM ** -0.5

SLOPES = tuple(
    tuple(2.0 ** (-8.0 * (g * KV_HEADS + h + 1) / (N_DIL * KV_HEADS)) for h in range(KV_HEADS))
    for g in range(N_DIL)
)


def _rms(x, g):
    return x * lax.rsqrt(jnp.mean(x * x, axis=-1, keepdims=True) + EPS) * g


def _layer_norm(x, g, b):
    mu = jnp.mean(x, axis=-1, keepdims=True)
    xc = x - mu
    return xc * lax.rsqrt(jnp.mean(xc * xc, axis=-1, keepdims=True) + EPS) * g + b


def _dot(a, b):
    return jnp.dot(a, b, preferred_element_type=F32)


def _dot_nt(a, b):
    return lax.dot_general(a, b, (((1,), (1,)), ((), ())), preferred_element_type=F32)


def _select_rows(hid, vals):
    out = vals[-1]
    for h in range(len(vals) - 2, -1, -1):
        out = jnp.where(hid == h, vals[h], out)
    return out


def _layer_spec(layer, shape):
    nd = len(shape)
    return pl.BlockSpec((None,) + tuple(shape), lambda *_: (layer,) + (0,) * nd, pipeline_mode=pl.Buffered(1))


def _params(sem):
    return pltpu.CompilerParams(dimension_semantics=sem, vmem_limit_bytes=VMEM_LIMIT)


def _inproj_prompt_kernel(x_ref, gpre_ref, w_ref, wkt_ref, wvt_ref, lng_ref, lnb_ref, ws_ref, bs_ref, cw_ref,
                          cbias_ref, bg_ref,
                          oa_ref, q_ref, k_ref, v_ref, kt_ref, vt_ref, oc_ref, gate_ref, cs_ref, prev_ref):
    tm = x_ref.shape[0]
    i = pl.program_id(1)

    @pl.when(i == 0)
    def _():
        prev_ref[...] = jnp.zeros_like(prev_ref)

    row = lax.broadcasted_iota(jnp.int32, (CHUNK, CHUNK), 0)
    col = lax.broadcasted_iota(jnp.int32, (CHUNK, CHUNK), 1)
    wt = [jnp.where(row >= col, ws_ref[g], 0.0).astype(BF16) for g in range(A_GROUPS)]
    rsub = lax.broadcasted_iota(jnp.int32, (INPROJ_SUB, 1), 0)

    carry = {"tail": prev_ref[...], "h": []}

    def sub_tile(r0):
        rows = slice(r0, r0 + INPROJ_SUB)
        h = _rms(x_ref[rows, :], gpre_ref[...]).astype(BF16)
        carry["h"].append(h)

        def proj(off, width):
            return _dot(h, w_ref[:, off:off + width])

        yield
        u = jax.nn.gelu(proj(OFF_U, A_WIDTH))
        yield
        vn = _layer_norm(jax.nn.gelu(proj(OFF_V, A_WIDTH)), lng_ref[...], lnb_ref[...])
        yield
        for g in range(A_GROUPS):
            cs = slice(g * A_GROUP_DIM, (g + 1) * A_GROUP_DIM)
            for c in range(INPROJ_SUB // CHUNK):
                rs = slice(c * CHUNK, (c + 1) * CHUNK)
                sa = _dot(wt[g], vn[rs, cs].astype(BF16)) + bs_ref[:, g:g + 1]
                oa_ref[r0 + c * CHUNK:r0 + (c + 1) * CHUNK, cs] = (u[rs, cs] * sa).astype(BF16)
        yield
        qv = proj(OFF_Q, Q_W)
        for c in range(Q_W // LANES):
            q_ref[c, rows, :] = qv[:, c * LANES:(c + 1) * LANES] * QSCALE
        yield
        kv = proj(OFF_K, 2 * KV_W)
        for c in range(KV_W // LANES):
            k_ref[c, rows, :] = kv[:, c * LANES:(c + 1) * LANES]
            v_ref[c, rows, :] = kv[:, KV_W + c * LANES:KV_W + (c + 1) * LANES]
        yield
        cz = proj(OFF_CC, C_WIDTH) * proj(OFF_CX, C_WIDTH)
        p2 = carry["tail"][SUBLANES - 2:SUBLANES - 1, :]
        p1 = carry["tail"][SUBLANES - 1:SUBLANES, :]
        carry["tail"] = cz[INPROJ_SUB - SUBLANES:INPROJ_SUB, :]
        z1 = jnp.where(rsub == 0, p1, pltpu.roll(cz, 1, 0))
        z2 = jnp.where(rsub == 0, p2, jnp.where(rsub == 1, p1, pltpu.roll(cz, 2, 0)))
        y = cbias_ref[...] + cw_ref[0:1, :] * z2 + cw_ref[1:2, :] * z1 + cw_ref[2:3, :] * cz
        yield
        oc_ref[rows, :] = (proj(OFF_CB, C_WIDTH) * y).astype(BF16)
        for j in range(3):
            yield
            cs = slice(j * D_MODEL, (j + 1) * D_MODEL)
            gate_ref[rows, cs] = jax.nn.sigmoid(proj(OFF_G + j * D_MODEL, D_MODEL) + bg_ref[:, cs]).astype(BF16)

    tiles = [sub_tile(r0) for r0 in range(0, tm, INPROJ_SUB)]
    live = [True] * len(tiles)
    tick = 0
    while any(live):
        for j, t in enumerate(tiles):
            if live[j] and tick >= j * INPROJ_LAG:
                live[j] = next(t, "done") != "done"
        tick += 1
    prev_ref[...] = carry["tail"]
    cs_ref[0] = carry["tail"]
    hs = carry["h"]

    @pl.when(i >= pl.num_programs(1) - WIN // tm)
    def _():
        for j, h in enumerate(hs):
            cols = slice(j * INPROJ_SUB, (j + 1) * INPROJ_SUB)
            kt_ref[0, :, cols] = _dot_nt(wkt_ref[...], h)
            vt_ref[0, :, cols] = _dot_nt(wvt_ref[...], h)


def _inproj_prompt(layer, x, gpre, w_in, wkt, wvt, lng, lnb, ws, bs_t, cw, cbias, bg):
    tm = PROMPT_TM
    nt = SEQ // tm
    rows = BATCH * SEQ
    win_spec = pl.BlockSpec((1, KV_W, tm), lambda b, i: (b, 0, jnp.maximum(i - (nt - WIN // tm), 0)))

    def rowblk(width):
        return pl.BlockSpec((tm, width), lambda b, i: (b * nt + i, 0))

    def slab(n):
        return pl.BlockSpec((n, tm, LANES), lambda b, i: (0, b * nt + i, 0))

    return pl.pallas_call(
        _inproj_prompt_kernel,
        grid=(BATCH, nt),
        in_specs=[
            rowblk(D_MODEL),
            _layer_spec(layer, (1, D_MODEL)),
            _layer_spec(layer, (D_MODEL, IN_W)),
            _layer_spec(layer, (KV_W, D_MODEL)),
            _layer_spec(layer, (KV_W, D_MODEL)),
            _layer_spec(layer, (1, A_WIDTH)),
            _layer_spec(layer, (1, A_WIDTH)),
            _layer_spec(layer, (A_GROUPS, CHUNK, CHUNK)),
            _layer_spec(layer, (CHUNK, A_GROUPS)),
            _layer_spec(layer, (CONV_W, C_WIDTH)),
            _layer_spec(layer, (1, C_WIDTH)),
            _layer_spec(layer, (1, 3 * D_MODEL)),
        ],
        out_specs=[
            rowblk(A_WIDTH),
            slab(Q_W // LANES),
            slab(KV_W // LANES),
            slab(KV_W // LANES),
            win_spec,
            win_spec,
            rowblk(C_WIDTH),
            rowblk(3 * D_MODEL),
            pl.BlockSpec((1, SUBLANES, C_WIDTH), lambda b, i: (b, 0, 0)),
        ],
        out_shape=[
            jax.ShapeDtypeStruct((rows, A_WIDTH), BF16),
            jax.ShapeDtypeStruct((Q_W // LANES, rows, LANES), F32),
            jax.ShapeDtypeStruct((KV_W // LANES, rows, LANES), F32),
            jax.ShapeDtypeStruct((KV_W // LANES, rows, LANES), F32),
            jax.ShapeDtypeStruct((BATCH, KV_W, WIN), F32),
            jax.ShapeDtypeStruct((BATCH, KV_W, WIN), F32),
            jax.ShapeDtypeStruct((rows, C_WIDTH), BF16),
            jax.ShapeDtypeStruct((rows, 3 * D_MODEL), BF16),
            jax.ShapeDtypeStruct((BATCH, SUBLANES, C_WIDTH), F32),
        ],
        scratch_shapes=[pltpu.VMEM((SUBLANES, C_WIDTH), F32)],
        compiler_params=_params(("arbitrary", "arbitrary")),
        name="inproj_prompt",
    )(x, gpre, w_in, wkt, wvt, lng, lnb, ws, bs_t, cw, cbias, bg)


def _attn_prompt_kernel(q_ref, k_ref, v_ref, o_ref, ktail, vtail, k4, v4, q24, acc, mrun, lrun, tmp, bias_ref):
    b = pl.program_id(0)
    i = pl.program_id(1)
    first = i == 0
    nslab = KV_W // LANES
    quarter = SPAN // RES

    @pl.when(jnp.logical_and(b == 0, first))
    def _():
        rr = lax.broadcasted_iota(jnp.int32, (KV_HEADS * BAND, 2 * BAND), 0)
        kk = lax.broadcasted_iota(jnp.int32, (KV_HEADS * BAND, 2 * BAND), 1)
        dist = (rr & (BAND - 1)) + BAND - kk
        hrow = rr >> 7
        ok = jnp.logical_and(dist >= 0, dist <= BAND)
        for g, dil in enumerate(DILS):
            coef = _select_rows(hrow, [jnp.float32(SLOPES[g][h] * dil * LOG2E) for h in range(KV_HEADS)])
            full = jnp.where(ok, -(coef * dist.astype(F32)), NEG)
            bias_ref[2 * g] = full
            bias_ref[2 * g + 1] = jnp.where(kk >= BAND, full, NEG)

    @pl.when(first)
    def _():
        for r in range(RES):
            base = r * 2 * quarter
            k4[:, base:base + quarter, :] = jnp.zeros((nslab, quarter, LANES), F32)
            v4[:, base:base + quarter, :] = jnp.zeros((nslab, quarter, LANES), F32)
        ktail[...] = jnp.zeros_like(ktail)
        vtail[...] = jnp.zeros_like(vtail)

    @pl.when(i > 0)
    def _():
        for r in range(RES):
            base = r * 2 * quarter
            k4[:, base:base + quarter, :] = k4[:, base + quarter:base + 2 * quarter, :]
            v4[:, base:base + quarter, :] = v4[:, base + quarter:base + 2 * quarter, :]

    for r in range(RES):
        base = r * 2 * quarter + quarter
        for c in range(nslab):
            k4[c, base:base + quarter, :] = k_ref[c, pl.ds(r, quarter, stride=RES), :]
            v4[c, base:base + quarter, :] = v_ref[c, pl.ds(r, quarter, stride=RES), :]
            q24[c, r * quarter:(r + 1) * quarter, :] = q_ref[2 * nslab + c, pl.ds(r, quarter, stride=RES), :]

    lane = lax.broadcasted_iota(jnp.int32, (1, KV_W), 1)
    hid = lane >> 6
    headmask = [jnp.where(hid == h, 1.0, 0.0).astype(BF16) for h in range(KV_HEADS)]

    def wide(ref, rows, slab0=0):
        return jnp.concatenate([ref[slab0 + c, rows, :] for c in range(nslab)], axis=1)

    def softmax_unit(qb, kb, vb, bias):
        qb16 = qb.astype(BF16)
        qs = jnp.concatenate([qb16 * headmask[h] for h in range(KV_HEADS)], axis=0)
        s = _dot_nt(qs, kb.astype(BF16))
        ps, ms, ls = [], [], []
        for h in range(KV_HEADS):
            hr = slice(h * BAND, (h + 1) * BAND)
            sh = s[hr] + bias_ref[bias, hr, :]
            mh = jnp.max(sh, axis=-1, keepdims=True)
            ph = jnp.exp2(sh - mh)
            ps.append(ph.astype(BF16))
            ms.append(mh)
            ls.append(jnp.sum(ph, axis=-1, keepdims=True))
        r = _dot(jnp.concatenate(ps, axis=0), vb.astype(BF16))
        o = _select_rows(hid, [r[h * BAND:(h + 1) * BAND] for h in range(KV_HEADS)])
        return o, _select_rows(hid, ms), _select_rows(hid, ls)

    def store_stats(rows, o, m, l):
        for c in range(nslab):
            ls = slice(c * LANES, (c + 1) * LANES)
            acc[c, rows, :] = o[:, ls]
            mrun[c, rows, :] = m[:, ls]
            lrun[c, rows, :] = l[:, ls]

    def merge_stats(rows, o_u, m_u, l_u):
        m_old = wide(mrun, rows)
        m_n = jnp.maximum(m_old, m_u)
        a_old = jnp.exp2(m_old - m_n)
        a_new = jnp.exp2(m_u - m_n)
        store_stats(rows, wide(acc, rows) * a_old + o_u * a_new, m_n, wide(lrun, rows) * a_old + l_u * a_new)

    def unit0(n, kb, vb, bias):
        q0 = n * BAND if isinstance(n, int) else pl.multiple_of(n * BAND, BAND)
        stats = softmax_unit(wide(q_ref, pl.ds(q0, BAND)), kb, vb, bias)
        sub = BAND // RES
        for j, x in enumerate(stats):
            for c in range(nslab):
                tmp[j * nslab + c, pl.ds(q0, BAND), :] = x[:, c * LANES:(c + 1) * LANES]
        for r in range(RES):
            src = pl.ds(q0 + r, sub, stride=RES)
            dst = pl.ds(r * quarter + n * sub, sub)
            for c in range(nslab):
                acc[c, dst, :] = tmp[c, src, :]
                mrun[c, dst, :] = tmp[nslab + c, src, :]
                lrun[c, dst, :] = tmp[2 * nslab + c, src, :]

    kb0 = jnp.concatenate([wide(ktail, pl.ds(0, BAND)), wide(k_ref, pl.ds(0, BAND))], axis=0)
    vb0 = jnp.concatenate([wide(vtail, pl.ds(0, BAND)), wide(v_ref, pl.ds(0, BAND))], axis=0)
    unit0(0, kb0, vb0, jnp.where(first, 1, 0))

    def body0(n, carry):
        keys = pl.ds(pl.multiple_of((n - 1) * BAND, BAND), 2 * BAND)
        unit0(n, wide(k_ref, keys), wide(v_ref, keys), 0)
        return carry

    lax.fori_loop(1, SPAN // BAND, body0, 0, unroll=3)

    def body1(s, carry):
        bias = 2 + jnp.where(jnp.logical_and(first, s == 0), 1, 0)
        for r in range(RES):
            qb = wide(q_ref, pl.ds(s * (BAND * RES) + r, BAND, stride=RES), nslab)
            keys = pl.ds(r * 2 * quarter + quarter + (s - 1) * BAND, 2 * BAND)
            o_u, m_u, l_u = softmax_unit(qb, wide(k4, keys), wide(v4, keys), bias)
            merge_stats(pl.ds(r * quarter + s * BAND, BAND), o_u, m_u, l_u)
        return carry

    lax.fori_loop(0, quarter // BAND, body1, 0)

    def body2(r, carry):
        bias = 4 + jnp.where(first, 1, 0)
        for a in range(DILS[2] // RES):
            rows = pl.ds(r * quarter + a, BAND, stride=RES)
            keys = pl.ds(r * 2 * quarter + a, 2 * BAND, stride=RES)
            o_u, m_u, l_u = softmax_unit(wide(q24, rows), wide(k4, keys), wide(v4, keys), bias)
            merge_stats(rows, o_u, m_u, l_u)
        return carry

    lax.fori_loop(0, RES, body2, 0)

    for r in range(RES):
        rows = pl.ds(r * quarter, quarter)
        for c in range(nslab):
            tmp[c, pl.ds(r, quarter, stride=RES), :] = acc[c, rows, :] / lrun[c, rows, :]
    for c in range(nslab):
        o_ref[:, c * LANES:(c + 1) * LANES] = tmp[c].astype(BF16)
    ktail[...] = k_ref[:, SPAN - BAND:SPAN, :]
    vtail[...] = v_ref[:, SPAN - BAND:SPAN, :]


def _attn_prompt(q, k, v):
    ns = SEQ // SPAN
    nslab = KV_W // LANES
    rows = BATCH * SEQ

    def slab(n):
        return pl.BlockSpec((n, SPAN, LANES), lambda b, i: (0, b * ns + i, 0))

    return pl.pallas_call(
        _attn_prompt_kernel,
        grid=(BATCH, ns),
        in_specs=[slab(Q_W // LANES), slab(nslab), slab(nslab)],
        out_specs=pl.BlockSpec((SPAN, KV_W), lambda b, i: (b * ns + i, 0)),
        out_shape=jax.ShapeDtypeStruct((rows, KV_W), BF16),
        scratch_shapes=[
            pltpu.VMEM((nslab, BAND, LANES), F32),
            pltpu.VMEM((nslab, BAND, LANES), F32),
            pltpu.VMEM((nslab, 2 * SPAN, LANES), F32),
            pltpu.VMEM((nslab, 2 * SPAN, LANES), F32),
            pltpu.VMEM((nslab, SPAN, LANES), F32),
            pltpu.VMEM((nslab, SPAN, LANES), F32),
            pltpu.VMEM((nslab, SPAN, LANES), F32),
            pltpu.VMEM((nslab, SPAN, LANES), F32),
            pltpu.VMEM((3 * nslab, SPAN, LANES), F32),
            pltpu.VMEM((2 * N_DIL, KV_HEADS * BAND, 2 * BAND), F32),
        ],
        compiler_params=_params(("arbitrary", "arbitrary")),
        name="attn_prompt",
    )(q, k, v)


def _merge_ffn_kernel(x_ref, oa_ref, ob_ref, oc_ref, gate_ref, wa_ref, wb_ref, wc_ref, wo_ref, gqm_ref,
                      gpf_ref, wg_ref, wu_ref, wd_ref, gqf_ref, o_ref):
    def gate(j):
        return gate_ref[:, j * D_MODEL:(j + 1) * D_MODEL].astype(F32)

    merged = (gate(0) * _dot(oa_ref[...], wa_ref[...])
              + gate(1) * _dot(ob_ref[...].astype(BF16), wb_ref[...])
              + gate(2) * _dot(oc_ref[...], wc_ref[...]))
    x = x_ref[...] + _rms(_dot(merged.astype(BF16), wo_ref[...]), gqm_ref[...])

    h = _rms(x, gpf_ref[...]).astype(BF16)
    y = None
    for lo, hi in FF_CHUNKS:
        act = (jax.nn.silu(_dot(h, wg_ref[:, lo:hi])) * _dot(h, wu_ref[:, lo:hi])).astype(BF16)
        part = _dot(act, wd_ref[lo:hi, :])
        y = part if y is None else y + part
    o_ref[...] = x + _rms(y, gqf_ref[...])


def _merge_ffn(layer, x, oa, ob, oc, gates, wa, wb, wc, wo, gqm, gpf, wg, wu, wd, gqf, tm):
    rows = x.shape[0]

    def rowblk(width):
        return pl.BlockSpec((tm, width), lambda i: (i, 0))

    return pl.pallas_call(
        _merge_ffn_kernel,
        grid=(rows // tm,),
        in_specs=[
            rowblk(D_MODEL), rowblk(A_WIDTH), rowblk(KV_W), rowblk(C_WIDTH), rowblk(3 * D_MODEL),
            _layer_spec(layer, (A_WIDTH, D_MODEL)), _layer_spec(layer, (KV_W, D_MODEL)),
            _layer_spec(layer, (C_WIDTH, D_MODEL)), _layer_spec(layer, (D_MODEL, D_MODEL)),
            _layer_spec(layer, (1, D_MODEL)),
            _layer_spec(layer, (1, D_MODEL)), _layer_spec(layer, (D_MODEL, D_FF)),
            _layer_spec(layer, (D_MODEL, D_FF)), _layer_spec(layer, (D_FF, D_MODEL)),
            _layer_spec(layer, (1, D_MODEL)),
        ],
        out_specs=rowblk(D_MODEL),
        out_shape=jax.ShapeDtypeStruct((rows, D_MODEL), F32),
        compiler_params=_params(("parallel",)),
        name="merge_ffn",
    )(x, oa, ob, oc, gates, wa, wb, wc, wo, gqm, gpf, wg, wu, wd, gqf)


def _inproj_sample_kernel(ws_ref, bs_ref, x_ref, gpre_ref, w_ref, lng_ref, lnb_ref, cw_ref, cbias_ref, bg_ref,
                          p1_ref, p2_ref,
                          oa_ref, q_ref, k_ref, v_ref, oc_ref, gate_ref, vn_ref, cz_ref, *, layer):
    n = SAMPLE_ROWS
    h = _rms(x_ref[...], gpre_ref[...]).astype(BF16)

    def proj(off, width):
        return _dot(h, w_ref[:, off:off + width])

    u = jax.nn.gelu(proj(OFF_U, A_WIDTH))
    vn = _layer_norm(jax.nn.gelu(proj(OFF_V, A_WIDTH)), lng_ref[...], lnb_ref[...])
    vn_ref[...] = vn
    t = lax.broadcasted_iota(jnp.int32, (n, 1), 0) & (DEC_SEQ - 1)
    for g in range(A_GROUPS):
        cs = slice(g * A_GROUP_DIM, (g + 1) * A_GROUP_DIM)
        base = (layer * A_GROUPS + g) * DEC_SEQ
        vg = vn[:, cs]
        sa = _select_rows(t, [bs_ref[base + tt] for tt in range(DEC_SEQ)])
        for k in range(DEC_SEQ):
            coef = jnp.zeros((n, 1), F32)
            for tt in range(k, DEC_SEQ):
                coef = jnp.where(t == tt, ws_ref[(base + tt) * DEC_SEQ + tt - k], coef)
            sa = sa + coef * (vg if k == 0 else pltpu.roll(vg, k, 0))
        oa_ref[:, cs] = (u[:, cs] * sa).astype(BF16)

    q_ref[...] = proj(OFF_Q, Q_W)
    kv = proj(OFF_K, 2 * KV_W)
    k_ref[...] = kv[:, 0:KV_W]
    v_ref[...] = kv[:, KV_W:2 * KV_W]

    cz = proj(OFF_CC, C_WIDTH) * proj(OFF_CX, C_WIDTH)
    cz_ref[...] = cz
    z1 = jnp.where(t >= 1, pltpu.roll(cz, 1, 0), p1_ref[...])
    z2 = jnp.where(t >= 2, pltpu.roll(cz, 2, 0), p2_ref[...])
    y = cbias_ref[...] + cw_ref[0:1, :] * z2 + cw_ref[1:2, :] * z1 + cw_ref[2:3, :] * cz
    oc_ref[...] = (proj(OFF_CB, C_WIDTH) * y).astype(BF16)

    for j in range(3):
        cs = slice(j * D_MODEL, (j + 1) * D_MODEL)
        gate_ref[:, cs] = jax.nn.sigmoid(proj(OFF_G + j * D_MODEL, D_MODEL) + bg_ref[:, cs]).astype(BF16)


def _inproj_sample(layer, ws4, bs4, x, gpre, w_in, lng, lnb, cw, cbias, bg, p1, p2):
    n = SAMPLE_ROWS
    smem = pl.BlockSpec(memory_space=pltpu.SMEM)

    def full(shape):
        nd = len(shape)
        return pl.BlockSpec(shape, lambda i: (0,) * nd)

    widths_dtypes = [(A_WIDTH, BF16), (Q_W, F32), (KV_W, F32), (KV_W, F32), (C_WIDTH, BF16),
                     (3 * D_MODEL, BF16), (A_WIDTH, F32), (C_WIDTH, F32)]
    return pl.pallas_call(
        functools.partial(_inproj_sample_kernel, layer=layer),
        grid=(1,),
        in_specs=[smem, smem, full((n, D_MODEL)), _layer_spec(layer, (1, D_MODEL)),
                  _layer_spec(layer, (D_MODEL, IN_W)), _layer_spec(layer, (1, A_WIDTH)),
                  _layer_spec(layer, (1, A_WIDTH)), _layer_spec(layer, (CONV_W, C_WIDTH)),
                  _layer_spec(layer, (1, C_WIDTH)), _layer_spec(layer, (1, 3 * D_MODEL)),
                  _layer_spec(layer, (n, C_WIDTH)), _layer_spec(layer, (n, C_WIDTH))],
        out_specs=[full((n, w)) for w, _ in widths_dtypes],
        out_shape=[jax.ShapeDtypeStruct((n, w), dt) for w, dt in widths_dtypes],
        compiler_params=_params(("arbitrary",)),
        name="inproj_sample",
    )(ws4, bs4, x, gpre, w_in, lng, lnb, cw, cbias, bg, p1, p2)


def _attn_sample_kernel(qh_ref, kn_ref, vn_ref, kt_ref, vt_ref, o_ref, bias_ref):
    nr = N_DIL * SUBLANES
    width = MAX_WINDOW

    @pl.when(pl.program_id(0) == 0)
    def _():
        row = lax.broadcasted_iota(jnp.int32, (nr, width), 0)
        pos = lax.broadcasted_iota(jnp.int32, (nr, width), 1)
        grp = row >> 3
        delta = width + (row & (SUBLANES - 1)) - pos
        dil_m1 = _select_rows(grp, [d - 1 for d in DILS])
        reach = _select_rows(grp, [BAND * d for d in DILS])
        ok = jnp.logical_and((delta & dil_m1) == 0, delta <= reach)
        for h in range(KV_HEADS):
            slope = _select_rows(grp, [jnp.float32(SLOPES[g][h]) for g in range(N_DIL)])
            bias_ref[h * nr:(h + 1) * nr, :] = jnp.where(ok, -(slope * delta.astype(F32)), NEG)

    ridx = lax.broadcasted_iota(jnp.int32, (nr, 1), 0)
    g_of = ridx >> 3
    t_of = ridx & (SUBLANES - 1)
    scale = HEAD_DIM ** -0.5
    qs = [qh_ref[0, h] * scale for h in range(KV_HEADS)]
    s = jnp.concatenate([_dot(qs[h].astype(BF16), kt_ref[0, 0, h].astype(BF16)) for h in range(KV_HEADS)], axis=0)
    s = s + bias_ref[...]
    m = jnp.max(s, axis=-1, keepdims=True)
    s_new = []
    for n in range(DEC_SEQ):
        dn = t_of - n
        ok = jnp.logical_or(dn == 0, jnp.logical_and(g_of == 0, dn > 0))
        rows = []
        for h in range(KV_HEADS):
            slope = _select_rows(g_of, [jnp.float32(SLOPES[g][h]) for g in range(N_DIL)])
            raw = jnp.sum(qs[h] * kn_ref[0, h, n:n + 1, :], axis=-1, keepdims=True)
            rows.append(jnp.where(ok, raw - slope * dn.astype(F32), NEG))
        sn = jnp.concatenate(rows, axis=0)
        s_new.append(sn)
        m = jnp.maximum(m, sn)
    p = jnp.exp(s - m)
    l_all = jnp.sum(p, axis=-1, keepdims=True)
    p16 = p.astype(BF16)
    p_new = [jnp.exp(sn - m) for sn in s_new]
    for pn in p_new:
        l_all = l_all + pn
    for h in range(KV_HEADS):
        hr = slice(h * nr, (h + 1) * nr)
        r = _dot_nt(p16[hr], vt_ref[0, 0, h].astype(BF16))
        for n in range(DEC_SEQ):
            r = r + p_new[n][hr] * vn_ref[0, h, n:n + 1, :]
        m_h = m[hr]
        l_h = l_all[hr]
        parts = [(r[g * SUBLANES:(g + 1) * SUBLANES], m_h[g * SUBLANES:(g + 1) * SUBLANES],
                  l_h[g * SUBLANES:(g + 1) * SUBLANES]) for g in range(N_DIL)]
        m_all = jnp.maximum(jnp.maximum(parts[0][1], parts[1][1]), parts[2][1])
        num = jnp.zeros((SUBLANES, HEAD_DIM), F32)
        den = jnp.zeros((SUBLANES, 1), F32)
        for o_g, m_g, l_g in parts:
            w = jnp.exp(m_g - m_all)
            num = num + o_g * w
            den = den + l_g * w
        o_ref[0, :, h * HEAD_DIM:(h + 1) * HEAD_DIM] = (num / den)[0:DEC_SEQ]


def _attn_sample(layer, qh, knh, vnh, ckt, cvt):
    nr = N_DIL * SUBLANES
    cache_spec = pl.BlockSpec((1, 1, KV_HEADS, HEAD_DIM, MAX_WINDOW), lambda b: (layer, b, 0, 0, 0))
    new_spec = pl.BlockSpec((1, KV_HEADS, SUBLANES, HEAD_DIM), lambda b: (b, 0, 0, 0))
    return pl.pallas_call(
        _attn_sample_kernel,
        grid=(DEC_BATCH,),
        in_specs=[pl.BlockSpec((1, KV_HEADS, nr, HEAD_DIM), lambda b: (b, 0, 0, 0)),
                  new_spec, new_spec, cache_spec, cache_spec],
        out_specs=pl.BlockSpec((1, DEC_SEQ, KV_W), lambda b: (b, 0, 0)),
        out_shape=jax.ShapeDtypeStruct((DEC_BATCH, DEC_SEQ, KV_W), F32),
        scratch_shapes=[pltpu.VMEM((KV_HEADS * nr, MAX_WINDOW), F32)],
        compiler_params=_params(("arbitrary",)),
        name="attn_sample",
    )(qh, knh, vnh, ckt, cvt)


def kernel(x_prompt, x_sample, cache_k_win, cache_v_win, state_conv, g_pre_mix, g_post_mix, g_pre_ffn, g_post_ffn,
           w_in, a_ln_g, a_ln_b, a_ws, a_bs, c_conv_w, c_conv_b, w_br_a, w_br_b, w_br_c, b_gate, w_o,
           w_ff_gate, w_ff_up, w_ff_down):
    xp = x_prompt.reshape(BATCH * SEQ, D_MODEL)
    xs = x_sample.reshape(SAMPLE_ROWS, D_MODEL)
    ckt = jnp.transpose(cache_k_win, (0, 1, 3, 4, 2))
    cvt = jnp.transpose(cache_v_win, (0, 1, 3, 4, 2))

    def vec(a):
        return a.reshape(DEPTH, 1, -1)

    w_in_b = w_in.astype(BF16)
    wkt = jnp.swapaxes(w_in[:, :, OFF_K:OFF_K + KV_W], 1, 2).astype(BF16)
    wvt = jnp.swapaxes(w_in[:, :, OFF_VV:OFF_VV + KV_W], 1, 2).astype(BF16)
    wa, wb, wc, wo = (w.astype(BF16) for w in (w_br_a, w_br_b, w_br_c, w_o))
    wg, wu, wd = (w.astype(BF16) for w in (w_ff_gate, w_ff_up, w_ff_down))
    gpm, gqm, gpf, gqf = vec(g_pre_mix), vec(g_post_mix), vec(g_pre_ffn), vec(g_post_ffn)
    lng, lnb, cbias, bg = vec(a_ln_g), vec(a_ln_b), vec(c_conv_b), vec(b_gate)
    bs_t = jnp.swapaxes(a_bs, 1, 2)
    ws4 = a_ws[:, :, :DEC_SEQ, :DEC_SEQ].reshape(-1)
    bs4 = a_bs[:, :, :DEC_SEQ].reshape(-1)
    zeros = jnp.zeros((DEPTH, DEC_BATCH, 1, C_WIDTH), F32)
    p1 = jnp.concatenate([state_conv[:, :, 1:2], zeros, zeros, zeros], axis=2).reshape(DEPTH, SAMPLE_ROWS, C_WIDTH)
    p2 = jnp.concatenate([state_conv, zeros, zeros], axis=2).reshape(DEPTH, SAMPLE_ROWS, C_WIDTH)

    kp_l, vp_l, ks_l, vs_l, cp_l, cs_l, av_l = [], [], [], [], [], [], []
    for l in range(DEPTH):
        oa, q, k, v, kt, vt, oc, gates, cs8 = _inproj_prompt(
            l, xp, gpm, w_in_b, wkt, wvt, lng, lnb, a_ws, bs_t, c_conv_w, cbias, bg)
        ob = _attn_prompt(q, k, v)
        xp = _merge_ffn(l, xp, oa, ob, oc, gates, wa, wb, wc, wo, gqm, gpf, wg, wu, wd, gqf, PROMPT_TM)

        def window(t):
            return jnp.transpose(t.reshape(BATCH, KV_HEADS, HEAD_DIM, WIN), (0, 3, 1, 2))

        kp_l.append(window(kt))
        vp_l.append(window(vt))
        cp_l.append(cs8[:, SUBLANES - (CONV_W - 1):])

        oa_s, q_s, k_s, v_s, oc_s, gates_s, vn_s, cz_s = _inproj_sample(
            l, ws4, bs4, xs, gpm, w_in_b, lng, lnb, c_conv_w, cbias, bg, p1, p2)

        pad_t = ((0, 0), (0, 0), (0, 0), (0, SUBLANES - DEC_SEQ), (0, 0))
        qh = q_s.reshape(DEC_BATCH, DEC_SEQ, N_DIL, KV_HEADS, HEAD_DIM).transpose(0, 3, 2, 1, 4)
        qh = jnp.pad(qh, pad_t).reshape(DEC_BATCH, KV_HEADS, N_DIL * SUBLANES, HEAD_DIM)

        def new_rows(a):
            a = a.reshape(DEC_BATCH, DEC_SEQ, KV_HEADS, HEAD_DIM).transpose(0, 2, 1, 3)
            return jnp.pad(a, pad_t[1:])

        ob_s = _attn_sample(l, qh, new_rows(k_s), new_rows(v_s), ckt, cvt)
        ob_s = ob_s.reshape(SAMPLE_ROWS, KV_W)
        xs = _merge_ffn(l, xs, oa_s, ob_s, oc_s, gates_s, wa, wb, wc, wo, gqm, gpf, wg, wu, wd, gqf, SAMPLE_ROWS)

        ks_l.append(k_s.reshape(DEC_BATCH, DEC_SEQ, KV_HEADS, HEAD_DIM))
        vs_l.append(v_s.reshape(DEC_BATCH, DEC_SEQ, KV_HEADS, HEAD_DIM))
        cs_l.append(cz_s.reshape(DEC_BATCH, DEC_SEQ, C_WIDTH)[:, DEC_SEQ - (CONV_W - 1):])
        av_l.append(vn_s.reshape(DEC_BATCH, DEC_SEQ, A_WIDTH))

    return (xp.reshape(BATCH, SEQ, D_MODEL), xs.reshape(DEC_BATCH, DEC_SEQ, D_MODEL),
            jnp.stack(kp_l), jnp.stack(vp_l), jnp.stack(ks_l), jnp.stack(vs_l),
            jnp.stack(cp_l), jnp.stack(cs_l), jnp.stack(av_l))
```

```python
import functools

import jax
import jax.numpy as jnp
from jax import lax
from jax.experimental import pallas as pl
from jax.experimental.pallas import tpu as pltpu

F32 = jnp.float32
BF16 = jnp.bfloat16

D_MODEL = 1024
BATCH = 2
SEQ = 8192
DEPTH = 2
DEC_BATCH = 32
DEC_SEQ = 4
CHUNK = 128
A_GROUPS = 4
A_GROUP_DIM = 128
A_WIDTH = A_GROUPS * A_GROUP_DIM
HEAD_DIM = 64
KV_HEADS = 4
DILS = (1, 4, 16)
N_DIL = len(DILS)
Q_W = N_DIL * KV_HEADS * HEAD_DIM
KV_W = KV_HEADS * HEAD_DIM
BAND = 128
MAX_WINDOW = 2048
C_WIDTH = 512
CONV_W = 3
D_FF = 2816
EPS = 1e-6

OFF_U = 0
OFF_V = OFF_U + A_WIDTH
OFF_Q = OFF_V + A_WIDTH
OFF_K = OFF_Q + Q_W
OFF_VV = OFF_K + KV_W
OFF_CX = OFF_VV + KV_W
OFF_CB = OFF_CX + C_WIDTH
OFF_CC = OFF_CB + C_WIDTH
OFF_G = OFF_CC + C_WIDTH
IN_W = OFF_G + 3 * D_MODEL

LANES = 128
SUBLANES = 8
VMEM_LIMIT = 56 * 1024 * 1024
NEG = -1e30

SAMPLE_ROWS = DEC_BATCH * DEC_SEQ
WIN = min(MAX_WINDOW, SEQ)
SPAN = BAND * DILS[-1]
RES = DILS[1]
PROMPT_TM = 512
ROW_SUB = 256
INPROJ_SUB = ROW_SUB
INPROJ_LAG = 2
MERGE_FFN_LAG = 3
MXU_N = 256
FF_CHUNKS = ((0, 6 * MXU_N), (6 * MXU_N, D_FF))
LOG2E = 1.4426950408889634
QSCALE = LOG2E * HEAD_DIM ** -0.5

SLOPES = tuple(
    tuple(2.0 ** (-8.0 * (g * KV_HEADS + h + 1) / (N_DIL * KV_HEADS)) for h in range(KV_HEADS))
    for g in range(N_DIL)
)


def _rms(x, g):
    return x * lax.rsqrt(jnp.mean(x * x, axis=-1, keepdims=True) + EPS) * g


def _layer_norm(x, g, b):
    mu = jnp.mean(x, axis=-1, keepdims=True)
    xc = x - mu
    return xc * lax.rsqrt(jnp.mean(xc * xc, axis=-1, keepdims=True) + EPS) * g + b


def _dot(a, b):
    return jnp.dot(a, b, preferred_element_type=F32)


def _dot_nt(a, b):
    return lax.dot_general(a, b, (((1,), (1,)), ((), ())), preferred_element_type=F32)


def _select_rows(hid, vals):
    out = vals[-1]
    for h in range(len(vals) - 2, -1, -1):
        out = jnp.where(hid == h, vals[h], out)
    return out


def _layer_spec(layer, shape):
    nd = len(shape)
    return pl.BlockSpec((None,) + tuple(shape), lambda *_: (layer,) + (0,) * nd, pipeline_mode=pl.Buffered(1))


def _trace_staggered(stage_gens, lag):
    live = [True] * len(stage_gens)
    tick = 0
    while any(live):
        for j, gen in enumerate(stage_gens):
            if live[j] and tick >= j * lag:
                live[j] = next(gen, "done") != "done"
        tick += 1


def _params(sem):
    return pltpu.CompilerParams(dimension_semantics=sem, vmem_limit_bytes=VMEM_LIMIT)


def _inproj_prompt_kernel(x_ref, gpre_ref, w_ref, wkt_ref, wvt_ref, lng_ref, lnb_ref, ws_ref, bs_ref, cw_ref,
                          cbias_ref, bg_ref,
                          oa_ref, q_ref, k_ref, v_ref, kt_ref, vt_ref, oc_ref, gate_ref, cs_ref, prev_ref):
    tm = x_ref.shape[0]
    i = pl.program_id(1)

    @pl.when(i == 0)
    def _():
        prev_ref[...] = jnp.zeros_like(prev_ref)

    row = lax.broadcasted_iota(jnp.int32, (CHUNK, CHUNK), 0)
    col = lax.broadcasted_iota(jnp.int32, (CHUNK, CHUNK), 1)
    wt = [jnp.where(row >= col, ws_ref[g], 0.0).astype(BF16) for g in range(A_GROUPS)]
    rsub = lax.broadcasted_iota(jnp.int32, (INPROJ_SUB, 1), 0)

    carry = {"tail": prev_ref[...], "h": []}

    def sub_tile(r0):
        rows = slice(r0, r0 + INPROJ_SUB)
        h = _rms(x_ref[rows, :], gpre_ref[...]).astype(BF16)
        carry["h"].append(h)

        def proj(off, width):
            return _dot(h, w_ref[:, off:off + width])

        yield
        u = jax.nn.gelu(proj(OFF_U, A_WIDTH))
        yield
        vn = _layer_norm(jax.nn.gelu(proj(OFF_V, A_WIDTH)), lng_ref[...], lnb_ref[...])
        for j in range(3):
            yield
            cs = slice(j * D_MODEL, (j + 1) * D_MODEL)
            gate_ref[rows, cs] = jax.nn.sigmoid(proj(OFF_G + j * D_MODEL, D_MODEL) + bg_ref[:, cs]).astype(BF16)
        yield
        for g in range(A_GROUPS):
            cs = slice(g * A_GROUP_DIM, (g + 1) * A_GROUP_DIM)
            for c in range(0, INPROJ_SUB // CHUNK, 2):
                pair = jnp.concatenate([vn[(c + j) * CHUNK:(c + j + 1) * CHUNK, cs] for j in range(2)], axis=1)
                sa2 = _dot(wt[g], pair.astype(BF16)) + bs_ref[:, g:g + 1]
                for j in range(2):
                    rs = slice((c + j) * CHUNK, (c + j + 1) * CHUNK)
                    sa = sa2[:, j * A_GROUP_DIM:(j + 1) * A_GROUP_DIM]
                    oa_ref[r0 + (c + j) * CHUNK:r0 + (c + j + 1) * CHUNK, cs] = (u[rs, cs] * sa).astype(BF16)
        yield
        cz = proj(OFF_CC, C_WIDTH) * proj(OFF_CX, C_WIDTH)
        p2 = carry["tail"][SUBLANES - 2:SUBLANES - 1, :]
        p1 = carry["tail"][SUBLANES - 1:SUBLANES, :]
        carry["tail"] = cz[INPROJ_SUB - SUBLANES:INPROJ_SUB, :]
        z1 = jnp.where(rsub == 0, p1, pltpu.roll(cz, 1, 0))
        z2 = jnp.where(rsub == 0, p2, jnp.where(rsub == 1, p1, pltpu.roll(cz, 2, 0)))
        y = cbias_ref[...] + cw_ref[0:1, :] * z2 + cw_ref[1:2, :] * z1 + cw_ref[2:3, :] * cz
        yield
        oc_ref[rows, :] = (proj(OFF_CB, C_WIDTH) * y).astype(BF16)
        yield
        qv = proj(OFF_Q, Q_W)
        for c in range(Q_W // LANES):
            q_ref[c, rows, :] = qv[:, c * LANES:(c + 1) * LANES] * QSCALE
        yield
        kv = proj(OFF_K, 2 * KV_W)
        for c in range(KV_W // LANES):
            k_ref[c, rows, :] = kv[:, c * LANES:(c + 1) * LANES]
            v_ref[c, rows, :] = kv[:, KV_W + c * LANES:KV_W + (c + 1) * LANES]

    _trace_staggered([sub_tile(r0) for r0 in range(0, tm, INPROJ_SUB)], INPROJ_LAG)
    prev_ref[...] = carry["tail"]
    cs_ref[0] = carry["tail"]
    hs = carry["h"]

    @pl.when(i >= pl.num_programs(1) - WIN // tm)
    def _():
        for j, h in enumerate(hs):
            cols = slice(j * INPROJ_SUB, (j + 1) * INPROJ_SUB)
            kt_ref[0, :, cols] = _dot_nt(wkt_ref[...], h)
            vt_ref[0, :, cols] = _dot_nt(wvt_ref[...], h)


def _inproj_prompt(layer, x, gpre, w_in, wkt, wvt, lng, lnb, ws, bs_t, cw, cbias, bg):
    tm = PROMPT_TM
    nt = SEQ // tm
    rows = BATCH * SEQ
    win_spec = pl.BlockSpec((1, KV_W, tm), lambda b, i: (b, 0, jnp.maximum(i - (nt - WIN // tm), 0)))

    def rowblk(width):
        return pl.BlockSpec((tm, width), lambda b, i: (b * nt + i, 0))

    def slab(n):
        return pl.BlockSpec((n, tm, LANES), lambda b, i: (0, b * nt + i, 0))

    return pl.pallas_call(
        _inproj_prompt_kernel,
        grid=(BATCH, nt),
        in_specs=[
            rowblk(D_MODEL),
            _layer_spec(layer, (1, D_MODEL)),
            _layer_spec(layer, (D_MODEL, IN_W)),
            _layer_spec(layer, (KV_W, D_MODEL)),
            _layer_spec(layer, (KV_W, D_MODEL)),
            _layer_spec(layer, (1, A_WIDTH)),
            _layer_spec(layer, (1, A_WIDTH)),
            _layer_spec(layer, (A_GROUPS, CHUNK, CHUNK)),
            _layer_spec(layer, (CHUNK, A_GROUPS)),
            _layer_spec(layer, (CONV_W, C_WIDTH)),
            _layer_spec(layer, (1, C_WIDTH)),
            _layer_spec(layer, (1, 3 * D_MODEL)),
        ],
        out_specs=[
            rowblk(A_WIDTH),
            slab(Q_W // LANES),
            slab(KV_W // LANES),
            slab(KV_W // LANES),
            win_spec,
            win_spec,
            rowblk(C_WIDTH),
            rowblk(3 * D_MODEL),
            pl.BlockSpec((1, SUBLANES, C_WIDTH), lambda b, i: (b, 0, 0)),
        ],
        out_shape=[
            jax.ShapeDtypeStruct((rows, A_WIDTH), BF16),
            jax.ShapeDtypeStruct((Q_W // LANES, rows, LANES), F32),
            jax.ShapeDtypeStruct((KV_W // LANES, rows, LANES), F32),
            jax.ShapeDtypeStruct((KV_W // LANES, rows, LANES), F32),
            jax.ShapeDtypeStruct((BATCH, KV_W, WIN), F32),
            jax.ShapeDtypeStruct((BATCH, KV_W, WIN), F32),
            jax.ShapeDtypeStruct((rows, C_WIDTH), BF16),
            jax.ShapeDtypeStruct((rows, 3 * D_MODEL), BF16),
            jax.ShapeDtypeStruct((BATCH, SUBLANES, C_WIDTH), F32),
        ],
        scratch_shapes=[pltpu.VMEM((SUBLANES, C_WIDTH), F32)],
        compiler_params=_params(("arbitrary", "arbitrary")),
        name="inproj_prompt",
    )(x, gpre, w_in, wkt, wvt, lng, lnb, ws, bs_t, cw, cbias, bg)


def _attn_prompt_kernel(q_ref, k_ref, v_ref, o_ref, ktail, vtail, k4, v4, q24, acc, mrun, lrun, tmp, bias_ref):
    b = pl.program_id(0)
    i = pl.program_id(1)
    first = i == 0
    nslab = KV_W // LANES
    quarter = SPAN // RES

    @pl.when(jnp.logical_and(b == 0, first))
    def _():
        rr = lax.broadcasted_iota(jnp.int32, (KV_HEADS * BAND, 2 * BAND), 0)
        kk = lax.broadcasted_iota(jnp.int32, (KV_HEADS * BAND, 2 * BAND), 1)
        dist = (rr & (BAND - 1)) + BAND - kk
        hrow = rr >> 7
        ok = jnp.logical_and(dist >= 0, dist <= BAND)
        for g, dil in enumerate(DILS):
            coef = _select_rows(hrow, [jnp.float32(SLOPES[g][h] * dil * LOG2E) for h in range(KV_HEADS)])
            full = jnp.where(ok, -(coef * dist.astype(F32)), NEG)
            bias_ref[2 * g] = full
            bias_ref[2 * g + 1] = jnp.where(kk >= BAND, full, NEG)

    @pl.when(first)
    def _():
        for r in range(RES):
            base = r * 2 * quarter
            k4[:, base:base + quarter, :] = jnp.zeros((nslab, quarter, LANES), F32)
            v4[:, base:base + quarter, :] = jnp.zeros((nslab, quarter, LANES), F32)
        ktail[...] = jnp.zeros_like(ktail)
        vtail[...] = jnp.zeros_like(vtail)

    @pl.when(i > 0)
    def _():
        for r in range(RES):
            base = r * 2 * quarter
            k4[:, base:base + quarter, :] = k4[:, base + quarter:base + 2 * quarter, :]
            v4[:, base:base + quarter, :] = v4[:, base + quarter:base + 2 * quarter, :]

    for r in range(RES):
        base = r * 2 * quarter + quarter
        for c in range(nslab):
            k4[c, base:base + quarter, :] = k_ref[c, pl.ds(r, quarter, stride=RES), :]
            v4[c, base:base + quarter, :] = v_ref[c, pl.ds(r, quarter, stride=RES), :]
            q24[c, r * quarter:(r + 1) * quarter, :] = q_ref[2 * nslab + c, pl.ds(r, quarter, stride=RES), :]

    lane = lax.broadcasted_iota(jnp.int32, (1, KV_W), 1)
    hid = lane >> 6
    headmask = [jnp.where(hid == h, 1.0, 0.0).astype(BF16) for h in range(KV_HEADS)]

    def wide(ref, rows, slab0=0):
        return jnp.concatenate([ref[slab0 + c, rows, :] for c in range(nslab)], axis=1)

    def softmax_unit(qb, kb, vb, bias):
        qb16 = qb.astype(BF16)
        qs = jnp.concatenate([qb16 * headmask[h] for h in range(KV_HEADS)], axis=0)
        s = _dot_nt(qs, kb.astype(BF16))
        ps, ms, ls = [], [], []
        for h in range(KV_HEADS):
            hr = slice(h * BAND, (h + 1) * BAND)
            sh = s[hr] + bias_ref[bias, hr, :]
            mh = jnp.max(sh, axis=-1, keepdims=True)
            ph = jnp.exp2(sh - mh)
            ps.append(ph.astype(BF16))
            ms.append(mh)
            ls.append(jnp.sum(ph, axis=-1, keepdims=True))
        r = _dot(jnp.concatenate(ps, axis=0), vb.astype(BF16))
        o = _select_rows(hid, [r[h * BAND:(h + 1) * BAND] for h in range(KV_HEADS)])
        return o, _select_rows(hid, ms), _select_rows(hid, ls)

    def store_stats(rows, o, m, l):
        for c in range(nslab):
            ls = slice(c * LANES, (c + 1) * LANES)
            acc[c, rows, :] = o[:, ls]
            mrun[c, rows, :] = m[:, ls]
            lrun[c, rows, :] = l[:, ls]

    def merge_stats(rows, o_u, m_u, l_u):
        m_old = wide(mrun, rows)
        m_n = jnp.maximum(m_old, m_u)
        a_old = jnp.exp2(m_old - m_n)
        a_new = jnp.exp2(m_u - m_n)
        store_stats(rows, wide(acc, rows) * a_old + o_u * a_new, m_n, wide(lrun, rows) * a_old + l_u * a_new)

    def unit0(n, kb, vb, bias):
        q0 = n * BAND if isinstance(n, int) else pl.multiple_of(n * BAND, BAND)
        stats = softmax_unit(wide(q_ref, pl.ds(q0, BAND)), kb, vb, bias)
        sub = BAND // RES
        for j, x in enumerate(stats):
            for c in range(nslab):
                tmp[j * nslab + c, pl.ds(q0, BAND), :] = x[:, c * LANES:(c + 1) * LANES]
        for r in range(RES):
            src = pl.ds(q0 + r, sub, stride=RES)
            dst = pl.ds(r * quarter + n * sub, sub)
            for c in range(nslab):
                acc[c, dst, :] = tmp[c, src, :]
                mrun[c, dst, :] = tmp[nslab + c, src, :]
                lrun[c, dst, :] = tmp[2 * nslab + c, src, :]

    kb0 = jnp.concatenate([wide(ktail, pl.ds(0, BAND)), wide(k_ref, pl.ds(0, BAND))], axis=0)
    vb0 = jnp.concatenate([wide(vtail, pl.ds(0, BAND)), wide(v_ref, pl.ds(0, BAND))], axis=0)
    unit0(0, kb0, vb0, jnp.where(first, 1, 0))

    def body0(n, carry):
        keys = pl.ds(pl.multiple_of((n - 1) * BAND, BAND), 2 * BAND)
        unit0(n, wide(k_ref, keys), wide(v_ref, keys), 0)
        return carry

    lax.fori_loop(1, SPAN // BAND, body0, 0, unroll=3)

    def body1(s, carry):
        bias = 2 + jnp.where(jnp.logical_and(first, s == 0), 1, 0)
        for r in range(RES):
            qb = wide(q_ref, pl.ds(s * (BAND * RES) + r, BAND, stride=RES), nslab)
            keys = pl.ds(r * 2 * quarter + quarter + (s - 1) * BAND, 2 * BAND)
            o_u, m_u, l_u = softmax_unit(qb, wide(k4, keys), wide(v4, keys), bias)
            merge_stats(pl.ds(r * quarter + s * BAND, BAND), o_u, m_u, l_u)
        return carry

    lax.fori_loop(0, quarter // BAND, body1, 0)

    def body2(r, carry):
        bias = 4 + jnp.where(first, 1, 0)
        for a in range(DILS[2] // RES):
            rows = pl.ds(r * quarter + a, BAND, stride=RES)
            keys = pl.ds(r * 2 * quarter + a, 2 * BAND, stride=RES)
            o_u, m_u, l_u = softmax_unit(wide(q24, rows), wide(k4, keys), wide(v4, keys), bias)
            merge_stats(rows, o_u, m_u, l_u)
        return carry

    lax.fori_loop(0, RES, body2, 0)

    for r in range(RES):
        rows = pl.ds(r * quarter, quarter)
        for c in range(nslab):
            tmp[c, pl.ds(r, quarter, stride=RES), :] = acc[c, rows, :] / lrun[c, rows, :]
    for c in range(nslab):
        o_ref[:, c * LANES:(c + 1) * LANES] = tmp[c].astype(BF16)
    ktail[...] = k_ref[:, SPAN - BAND:SPAN, :]
    vtail[...] = v_ref[:, SPAN - BAND:SPAN, :]


def _attn_prompt(q, k, v):
    ns = SEQ // SPAN
    nslab = KV_W // LANES
    rows = BATCH * SEQ

    def slab(n):
        return pl.BlockSpec((n, SPAN, LANES), lambda b, i: (0, b * ns + i, 0))

    return pl.pallas_call(
        _attn_prompt_kernel,
        grid=(BATCH, ns),
        in_specs=[slab(Q_W // LANES), slab(nslab), slab(nslab)],
        out_specs=pl.BlockSpec((SPAN, KV_W), lambda b, i: (b * ns + i, 0)),
        out_shape=jax.ShapeDtypeStruct((rows, KV_W), BF16),
        scratch_shapes=[
            pltpu.VMEM((nslab, BAND, LANES), F32),
            pltpu.VMEM((nslab, BAND, LANES), F32),
            pltpu.VMEM((nslab, 2 * SPAN, LANES), F32),
            pltpu.VMEM((nslab, 2 * SPAN, LANES), F32),
            pltpu.VMEM((nslab, SPAN, LANES), F32),
            pltpu.VMEM((nslab, SPAN, LANES), F32),
            pltpu.VMEM((nslab, SPAN, LANES), F32),
            pltpu.VMEM((nslab, SPAN, LANES), F32),
            pltpu.VMEM((3 * nslab, SPAN, LANES), F32),
            pltpu.VMEM((2 * N_DIL, KV_HEADS * BAND, 2 * BAND), F32),
        ],
        compiler_params=_params(("arbitrary", "arbitrary")),
        name="attn_prompt",
    )(q, k, v)


def _merge_ffn_kernel(x_ref, oa_ref, ob_ref, oc_ref, gate_ref, wa_ref, wb_ref, wc_ref, wo_ref, gqm_ref,
                      gpf_ref, wg_ref, wu_ref, wd_ref, gqf_ref, o_ref):
    tm = x_ref.shape[0]
    sub = min(tm, ROW_SUB)

    def sub_tile(r0):
        rows = slice(r0, r0 + sub)

        def gate(j):
            return gate_ref[rows, j * D_MODEL:(j + 1) * D_MODEL].astype(F32)

        merged = gate(0) * _dot(oa_ref[rows, :], wa_ref[...])
        yield
        merged = merged + gate(1) * _dot(ob_ref[rows, :].astype(BF16), wb_ref[...])
        yield
        merged = merged + gate(2) * _dot(oc_ref[rows, :], wc_ref[...])
        yield
        x = x_ref[rows, :] + _rms(_dot(merged.astype(BF16), wo_ref[...]), gqm_ref[...])
        yield
        h = _rms(x, gpf_ref[...]).astype(BF16)
        y = None
        for lo, hi in FF_CHUNKS:
            yield
            a = jax.nn.silu(_dot(h, wg_ref[:, lo:hi]))
            yield
            act = (a * _dot(h, wu_ref[:, lo:hi])).astype(BF16)
            yield
            part = _dot(act, wd_ref[lo:hi, :])
            y = part if y is None else y + part
        yield
        o_ref[rows, :] = x + _rms(y, gqf_ref[...])

    _trace_staggered([sub_tile(r0) for r0 in range(0, tm, sub)], MERGE_FFN_LAG)


def _merge_ffn(layer, x, oa, ob, oc, gates, wa, wb, wc, wo, gqm, gpf, wg, wu, wd, gqf, tm):
    rows = x.shape[0]

    def rowblk(width):
        return pl.BlockSpec((tm, width), lambda i: (i, 0))

    return pl.pallas_call(
        _merge_ffn_kernel,
        grid=(rows // tm,),
        in_specs=[
            rowblk(D_MODEL), rowblk(A_WIDTH), rowblk(KV_W), rowblk(C_WIDTH), rowblk(3 * D_MODEL),
            _layer_spec(layer, (A_WIDTH, D_MODEL)), _layer_spec(layer, (KV_W, D_MODEL)),
            _layer_spec(layer, (C_WIDTH, D_MODEL)), _layer_spec(layer, (D_MODEL, D_MODEL)),
            _layer_spec(layer, (1, D_MODEL)),
            _layer_spec(layer, (1, D_MODEL)), _layer_spec(layer, (D_MODEL, D_FF)),
            _layer_spec(layer, (D_MODEL, D_FF)), _layer_spec(layer, (D_FF, D_MODEL)),
            _layer_spec(layer, (1, D_MODEL)),
        ],
        out_specs=rowblk(D_MODEL),
        out_shape=jax.ShapeDtypeStruct((rows, D_MODEL), F32),
        compiler_params=_params(("parallel",)),
        name="merge_ffn",
    )(x, oa, ob, oc, gates, wa, wb, wc, wo, gqm, gpf, wg, wu, wd, gqf)


def _inproj_sample_kernel(ws_ref, bs_ref, x_ref, gpre_ref, w_ref, lng_ref, lnb_ref, cw_ref, cbias_ref, bg_ref,
                          p1_ref, p2_ref,
                          oa_ref, q_ref, k_ref, v_ref, oc_ref, gate_ref, vn_ref, cz_ref, *, layer):
    n = SAMPLE_ROWS
    h = _rms(x_ref[...], gpre_ref[...]).astype(BF16)

    def proj(off, width):
        return _dot(h, w_ref[:, off:off + width])

    u = jax.nn.gelu(proj(OFF_U, A_WIDTH))
    vn = _layer_norm(jax.nn.gelu(proj(OFF_V, A_WIDTH)), lng_ref[...], lnb_ref[...])
    vn_ref[...] = vn
    t = lax.broadcasted_iota(jnp.int32, (n, 1), 0) & (DEC_SEQ - 1)
    for g in range(A_GROUPS):
        cs = slice(g * A_GROUP_DIM, (g + 1) * A_GROUP_DIM)
        base = (layer * A_GROUPS + g) * DEC_SEQ
        vg = vn[:, cs]
        sa = _select_rows(t, [bs_ref[base + tt] for tt in range(DEC_SEQ)])
        for k in range(DEC_SEQ):
            coef = jnp.zeros((n, 1), F32)
            for tt in range(k, DEC_SEQ):
                coef = jnp.where(t == tt, ws_ref[(base + tt) * DEC_SEQ + tt - k], coef)
            sa = sa + coef * (vg if k == 0 else pltpu.roll(vg, k, 0))
        oa_ref[:, cs] = (u[:, cs] * sa).astype(BF16)

    q_ref[...] = proj(OFF_Q, Q_W)
    kv = proj(OFF_K, 2 * KV_W)
    k_ref[...] = kv[:, 0:KV_W]
    v_ref[...] = kv[:, KV_W:2 * KV_W]

    cz = proj(OFF_CC, C_WIDTH) * proj(OFF_CX, C_WIDTH)
    cz_ref[...] = cz
    z1 = jnp.where(t >= 1, pltpu.roll(cz, 1, 0), p1_ref[...])
    z2 = jnp.where(t >= 2, pltpu.roll(cz, 2, 0), p2_ref[...])
    y = cbias_ref[...] + cw_ref[0:1, :] * z2 + cw_ref[1:2, :] * z1 + cw_ref[2:3, :] * cz
    oc_ref[...] = (proj(OFF_CB, C_WIDTH) * y).astype(BF16)

    for j in range(3):
        cs = slice(j * D_MODEL, (j + 1) * D_MODEL)
        gate_ref[:, cs] = jax.nn.sigmoid(proj(OFF_G + j * D_MODEL, D_MODEL) + bg_ref[:, cs]).astype(BF16)


def _inproj_sample(layer, ws4, bs4, x, gpre, w_in, lng, lnb, cw, cbias, bg, p1, p2):
    n = SAMPLE_ROWS
    smem = pl.BlockSpec(memory_space=pltpu.SMEM)

    def full(shape):
        nd = len(shape)
        return pl.BlockSpec(shape, lambda i: (0,) * nd)

    widths_dtypes = [(A_WIDTH, BF16), (Q_W, F32), (KV_W, F32), (KV_W, F32), (C_WIDTH, BF16),
                     (3 * D_MODEL, BF16), (A_WIDTH, F32), (C_WIDTH, F32)]
    return pl.pallas_call(
        functools.partial(_inproj_sample_kernel, layer=layer),
        grid=(1,),
        in_specs=[smem, smem, full((n, D_MODEL)), _layer_spec(layer, (1, D_MODEL)),
                  _layer_spec(layer, (D_MODEL, IN_W)), _layer_spec(layer, (1, A_WIDTH)),
                  _layer_spec(layer, (1, A_WIDTH)), _layer_spec(layer, (CONV_W, C_WIDTH)),
                  _layer_spec(layer, (1, C_WIDTH)), _layer_spec(layer, (1, 3 * D_MODEL)),
                  _layer_spec(layer, (n, C_WIDTH)), _layer_spec(layer, (n, C_WIDTH))],
        out_specs=[full((n, w)) for w, _ in widths_dtypes],
        out_shape=[jax.ShapeDtypeStruct((n, w), dt) for w, dt in widths_dtypes],
        compiler_params=_params(("arbitrary",)),
        name="inproj_sample",
    )(ws4, bs4, x, gpre, w_in, lng, lnb, cw, cbias, bg, p1, p2)


def _attn_sample_kernel(qh_ref, kn_ref, vn_ref, kt_ref, vt_ref, o_ref, bias_ref):
    nr = N_DIL * SUBLANES
    width = MAX_WINDOW

    @pl.when(pl.program_id(0) == 0)
    def _():
        row = lax.broadcasted_iota(jnp.int32, (nr, width), 0)
        pos = lax.broadcasted_iota(jnp.int32, (nr, width), 1)
        grp = row >> 3
        delta = width + (row & (SUBLANES - 1)) - pos
        dil_m1 = _select_rows(grp, [d - 1 for d in DILS])
        reach = _select_rows(grp, [BAND * d for d in DILS])
        ok = jnp.logical_and((delta & dil_m1) == 0, delta <= reach)
        for h in range(KV_HEADS):
            slope = _select_rows(grp, [jnp.float32(SLOPES[g][h]) for g in range(N_DIL)])
            bias_ref[h * nr:(h + 1) * nr, :] = jnp.where(ok, -(slope * delta.astype(F32)), NEG)

    ridx = lax.broadcasted_iota(jnp.int32, (nr, 1), 0)
    g_of = ridx >> 3
    t_of = ridx & (SUBLANES - 1)
    scale = HEAD_DIM ** -0.5
    qs = [qh_ref[0, h] * scale for h in range(KV_HEADS)]
    s = jnp.concatenate([_dot(qs[h].astype(BF16), kt_ref[0, 0, h].astype(BF16)) for h in range(KV_HEADS)], axis=0)
    s = s + bias_ref[...]
    m = jnp.max(s, axis=-1, keepdims=True)
    s_new = []
    for n in range(DEC_SEQ):
        dn = t_of - n
        ok = jnp.logical_or(dn == 0, jnp.logical_and(g_of == 0, dn > 0))
        rows = []
        for h in range(KV_HEADS):
            slope = _select_rows(g_of, [jnp.float32(SLOPES[g][h]) for g in range(N_DIL)])
            raw = jnp.sum(qs[h] * kn_ref[0, h, n:n + 1, :], axis=-1, keepdims=True)
            rows.append(jnp.where(ok, raw - slope * dn.astype(F32), NEG))
        sn = jnp.concatenate(rows, axis=0)
        s_new.append(sn)
        m = jnp.maximum(m, sn)
    p = jnp.exp(s - m)
    l_all = jnp.sum(p, axis=-1, keepdims=True)
    p16 = p.astype(BF16)
    p_new = [jnp.exp(sn - m) for sn in s_new]
    for pn in p_new:
        l_all = l_all + pn
    for h in range(KV_HEADS):
        hr = slice(h * nr, (h + 1) * nr)
        r = _dot_nt(p16[hr], vt_ref[0, 0, h].astype(BF16))
        for n in range(DEC_SEQ):
            r = r + p_new[n][hr] * vn_ref[0, h, n:n + 1, :]
        m_h = m[hr]
        l_h = l_all[hr]
        parts = [(r[g * SUBLANES:(g + 1) * SUBLANES], m_h[g * SUBLANES:(g + 1) * SUBLANES],
                  l_h[g * SUBLANES:(g + 1) * SUBLANES]) for g in range(N_DIL)]
        m_all = jnp.maximum(jnp.maximum(parts[0][1], parts[1][1]), parts[2][1])
        num = jnp.zeros((SUBLANES, HEAD_DIM), F32)
        den = jnp.zeros((SUBLANES, 1), F32)
        for o_g, m_g, l_g in parts:
            w = jnp.exp(m_g - m_all)
            num = num + o_g * w
            den = den + l_g * w
        o_ref[0, :, h * HEAD_DIM:(h + 1) * HEAD_DIM] = (num / den)[0:DEC_SEQ]


def _attn_sample(layer, qh, knh, vnh, ckt, cvt):
    nr = N_DIL * SUBLANES
    cache_spec = pl.BlockSpec((1, 1, KV_HEADS, HEAD_DIM, MAX_WINDOW), lambda b: (layer, b, 0, 0, 0))
    new_spec = pl.BlockSpec((1, KV_HEADS, SUBLANES, HEAD_DIM), lambda b: (b, 0, 0, 0))
    return pl.pallas_call(
        _attn_sample_kernel,
        grid=(DEC_BATCH,),
        in_specs=[pl.BlockSpec((1, KV_HEADS, nr, HEAD_DIM), lambda b: (b, 0, 0, 0)),
                  new_spec, new_spec, cache_spec, cache_spec],
        out_specs=pl.BlockSpec((1, DEC_SEQ, KV_W), lambda b: (b, 0, 0)),
        out_shape=jax.ShapeDtypeStruct((DEC_BATCH, DEC_SEQ, KV_W), F32),
        scratch_shapes=[pltpu.VMEM((KV_HEADS * nr, MAX_WINDOW), F32)],
        compiler_params=_params(("arbitrary",)),
        name="attn_sample",
    )(qh, knh, vnh, ckt, cvt)


def kernel(x_prompt, x_sample, cache_k_win, cache_v_win, state_conv, g_pre_mix, g_post_mix, g_pre_ffn, g_post_ffn,
           w_in, a_ln_g, a_ln_b, a_ws, a_bs, c_conv_w, c_conv_b, w_br_a, w_br_b, w_br_c, b_gate, w_o,
           w_ff_gate, w_ff_up, w_ff_down):
    xp = x_prompt.reshape(BATCH * SEQ, D_MODEL)
    xs = x_sample.reshape(SAMPLE_ROWS, D_MODEL)
    ckt = jnp.transpose(cache_k_win, (0, 1, 3, 4, 2))
    cvt = jnp.transpose(cache_v_win, (0, 1, 3, 4, 2))

    def vec(a):
        return a.reshape(DEPTH, 1, -1)

    w_in_b = w_in.astype(BF16)
    wkt = jnp.swapaxes(w_in[:, :, OFF_K:OFF_K + KV_W], 1, 2).astype(BF16)
    wvt = jnp.swapaxes(w_in[:, :, OFF_VV:OFF_VV + KV_W], 1, 2).astype(BF16)
    wa, wb, wc, wo = (w.astype(BF16) for w in (w_br_a, w_br_b, w_br_c, w_o))
    wg, wu, wd = (w.astype(BF16) for w in (w_ff_gate, w_ff_up, w_ff_down))
    gpm, gqm, gpf, gqf = vec(g_pre_mix), vec(g_post_mix), vec(g_pre_ffn), vec(g_post_ffn)
    lng, lnb, cbias, bg = vec(a_ln_g), vec(a_ln_b), vec(c_conv_b), vec(b_gate)
    bs_t = jnp.swapaxes(a_bs, 1, 2)
    ws4 = a_ws[:, :, :DEC_SEQ, :DEC_SEQ].reshape(-1)
    bs4 = a_bs[:, :, :DEC_SEQ].reshape(-1)
    zeros = jnp.zeros((DEPTH, DEC_BATCH, 1, C_WIDTH), F32)
    p1 = jnp.concatenate([state_conv[:, :, 1:2], zeros, zeros, zeros], axis=2).reshape(DEPTH, SAMPLE_ROWS, C_WIDTH)
    p2 = jnp.concatenate([state_conv, zeros, zeros], axis=2).reshape(DEPTH, SAMPLE_ROWS, C_WIDTH)

    kp_l, vp_l, ks_l, vs_l, cp_l, cs_l, av_l = [], [], [], [], [], [], []
    for l in range(DEPTH):
        oa, q, k, v, kt, vt, oc, gates, cs8 = _inproj_prompt(
            l, xp, gpm, w_in_b, wkt, wvt, lng, lnb, a_ws, bs_t, c_conv_w, cbias, bg)
        ob = _attn_prompt(q, k, v)
        xp = _merge_ffn(l, xp, oa, ob, oc, gates, wa, wb, wc, wo, gqm, gpf, wg, wu, wd, gqf, PROMPT_TM)

        def window(t):
            return jnp.transpose(t.reshape(BATCH, KV_HEADS, HEAD_DIM, WIN), (0, 3, 1, 2))

        kp_l.append(window(kt))
        vp_l.append(window(vt))
        cp_l.append(cs8[:, SUBLANES - (CONV_W - 1):])

        oa_s, q_s, k_s, v_s, oc_s, gates_s, vn_s, cz_s = _inproj_sample(
            l, ws4, bs4, xs, gpm, w_in_b, lng, lnb, c_conv_w, cbias, bg, p1, p2)

        pad_t = ((0, 0), (0, 0), (0, 0), (0, SUBLANES - DEC_SEQ), (0, 0))
        qh = q_s.reshape(DEC_BATCH, DEC_SEQ, N_DIL, KV_HEADS, HEAD_DIM).transpose(0, 3, 2, 1, 4)
        qh = jnp.pad(qh, pad_t).reshape(DEC_BATCH, KV_HEADS, N_DIL * SUBLANES, HEAD_DIM)

        def new_rows(a):
            a = a.reshape(DEC_BATCH, DEC_SEQ, KV_HEADS, HEAD_DIM).transpose(0, 2, 1, 3)
            return jnp.pad(a, pad_t[1:])

        ob_s = _attn_sample(l, qh, new_rows(k_s), new_rows(v_s), ckt, cvt)
        ob_s = ob_s.reshape(SAMPLE_ROWS, KV_W)
        xs = _merge_ffn(l, xs, oa_s, ob_s, oc_s, gates_s, wa, wb, wc, wo, gqm, gpf, wg, wu, wd, gqf, SAMPLE_ROWS)

        ks_l.append(k_s.reshape(DEC_BATCH, DEC_SEQ, KV_HEADS, HEAD_DIM))
        vs_l.append(v_s.reshape(DEC_BATCH, DEC_SEQ, KV_HEADS, HEAD_DIM))
        cs_l.append(cz_s.reshape(DEC_BATCH, DEC_SEQ, C_WIDTH)[:, DEC_SEQ - (CONV_W - 1):])
        av_l.append(vn_s.reshape(DEC_BATCH, DEC_SEQ, A_WIDTH))

    return (xp.reshape(BATCH, SEQ, D_MODEL), xs.reshape(DEC_BATCH, DEC_SEQ, D_MODEL),
            jnp.stack(kp_l), jnp.stack(vp_l), jnp.stack(ks_l), jnp.stack(vs_l),
            jnp.stack(cp_l), jnp.stack(cs_l), jnp.stack(av_l))
```

```python
import functools

import jax
import jax.numpy as jnp
from jax import lax
from jax.experimental import pallas as pl
from jax.experimental.pallas import tpu as pltpu

F32 = jnp.float32
BF16 = jnp.bfloat16

D_MODEL = 1024
BATCH = 2
SEQ = 8192
DEPTH = 2
DEC_BATCH = 32
DEC_SEQ = 4
CHUNK = 128
A_GROUPS = 4
A_GROUP_DIM = 128
A_WIDTH = A_GROUPS * A_GROUP_DIM
HEAD_DIM = 64
KV_HEADS = 4
DILS = (1, 4, 16)
N_DIL = len(DILS)
Q_W = N_DIL * KV_HEADS * HEAD_DIM
KV_W = KV_HEADS * HEAD_DIM
BAND = 128
MAX_WINDOW = 2048
C_WIDTH = 512
CONV_W = 3
D_FF = 2816
EPS = 1e-6

OFF_U = 0
OFF_V = OFF_U + A_WIDTH
OFF_Q = OFF_V + A_WIDTH
OFF_K = OFF_Q + Q_W
OFF_VV = OFF_K + KV_W
OFF_CX = OFF_VV + KV_W
OFF_CB = OFF_CX + C_WIDTH
OFF_CC = OFF_CB + C_WIDTH
OFF_G = OFF_CC + C_WIDTH
IN_W = OFF_G + 3 * D_MODEL

LANES = 128
SUBLANES = 8
VMEM_LIMIT = 56 * 1024 * 1024
NEG = -1e30

SAMPLE_ROWS = DEC_BATCH * DEC_SEQ
WIN = min(MAX_WINDOW, SEQ)
SPAN = BAND * DILS[-1]
RES = DILS[1]
PROMPT_TM = 512
ROW_SUB = 256
INPROJ_SUB = ROW_SUB
INPROJ_LAG = 2
MERGE_FFN_LAG = 3
HEAD_STACK = KV_HEADS
SAMPLE_EB = 2
MXU_N = 256
FF_CHUNKS = ((0, 6 * MXU_N), (6 * MXU_N, D_FF))
LOG2E = 1.4426950408889634
QSCALE = LOG2E * HEAD_DIM ** -0.5

SLOPES = tuple(
    tuple(2.0 ** (-8.0 * (g * KV_HEADS + h + 1) / (N_DIL * KV_HEADS)) for h in range(KV_HEADS))
    for g in range(N_DIL)
)


def _rms(x, g):
    return x * lax.rsqrt(jnp.mean(x * x, axis=-1, keepdims=True) + EPS) * g


def _layer_norm(x, g, b):
    mu = jnp.mean(x, axis=-1, keepdims=True)
    xc = x - mu
    return xc * lax.rsqrt(jnp.mean(xc * xc, axis=-1, keepdims=True) + EPS) * g + b


def _dot(a, b):
    return jnp.dot(a, b, preferred_element_type=F32)


def _dot_nt(a, b):
    return lax.dot_general(a, b, (((1,), (1,)), ((), ())), preferred_element_type=F32)


def _select_rows(hid, vals):
    out = vals[-1]
    for h in range(len(vals) - 2, -1, -1):
        out = jnp.where(hid == h, vals[h], out)
    return out


def _layer_spec(layer, shape):
    nd = len(shape)
    return pl.BlockSpec((None,) + tuple(shape), lambda *_: (layer,) + (0,) * nd, pipeline_mode=pl.Buffered(1))


def _trace_staggered(stage_gens, lag):
    live = [True] * len(stage_gens)
    tick = 0
    while any(live):
        for j, gen in enumerate(stage_gens):
            if live[j] and tick >= j * lag:
                live[j] = next(gen, "done") != "done"
        tick += 1


def _params(sem):
    return pltpu.CompilerParams(dimension_semantics=sem, vmem_limit_bytes=VMEM_LIMIT)


def _inproj_prompt_kernel(x_ref, gpre_ref, w_ref, wkt_ref, wvt_ref, lng_ref, lnb_ref, ws_ref, bs_ref, cw_ref,
                          cbias_ref, bg_ref,
                          oa_ref, q_ref, k_ref, v_ref, kt_ref, vt_ref, oc_ref, gate_ref, cs_ref, prev_ref):
    tm = x_ref.shape[0]
    i = pl.program_id(1)

    @pl.when(i == 0)
    def _():
        prev_ref[...] = jnp.zeros_like(prev_ref)

    row = lax.broadcasted_iota(jnp.int32, (CHUNK, CHUNK), 0)
    col = lax.broadcasted_iota(jnp.int32, (CHUNK, CHUNK), 1)
    wt = [jnp.where(row >= col, ws_ref[g], 0.0).astype(BF16) for g in range(A_GROUPS)]
    rsub = lax.broadcasted_iota(jnp.int32, (INPROJ_SUB, 1), 0)

    carry = {"tail": prev_ref[...], "h": []}

    def sub_tile(r0):
        rows = slice(r0, r0 + INPROJ_SUB)
        h = _rms(x_ref[rows, :], gpre_ref[...]).astype(BF16)
        carry["h"].append(h)

        def proj(off, width):
            return _dot(h, w_ref[:, off:off + width])

        yield
        u = jax.nn.gelu(proj(OFF_U, A_WIDTH))
        yield
        vn = _layer_norm(jax.nn.gelu(proj(OFF_V, A_WIDTH)), lng_ref[...], lnb_ref[...])
        for j in range(3):
            yield
            cs = slice(j * D_MODEL, (j + 1) * D_MODEL)
            gate_ref[rows, cs] = jax.nn.sigmoid(proj(OFF_G + j * D_MODEL, D_MODEL) + bg_ref[:, cs]).astype(BF16)
        yield
        for g in range(A_GROUPS):
            cs = slice(g * A_GROUP_DIM, (g + 1) * A_GROUP_DIM)
            for c in range(0, INPROJ_SUB // CHUNK, 2):
                pair = jnp.concatenate([vn[(c + j) * CHUNK:(c + j + 1) * CHUNK, cs] for j in range(2)], axis=1)
                sa2 = _dot(wt[g], pair.astype(BF16)) + bs_ref[:, g:g + 1]
                for j in range(2):
                    rs = slice((c + j) * CHUNK, (c + j + 1) * CHUNK)
                    sa = sa2[:, j * A_GROUP_DIM:(j + 1) * A_GROUP_DIM]
                    oa_ref[r0 + (c + j) * CHUNK:r0 + (c + j + 1) * CHUNK, cs] = (u[rs, cs] * sa).astype(BF16)
        yield
        cz = proj(OFF_CC, C_WIDTH) * proj(OFF_CX, C_WIDTH)
        p2 = carry["tail"][SUBLANES - 2:SUBLANES - 1, :]
        p1 = carry["tail"][SUBLANES - 1:SUBLANES, :]
        carry["tail"] = cz[INPROJ_SUB - SUBLANES:INPROJ_SUB, :]
        z1 = jnp.where(rsub == 0, p1, pltpu.roll(cz, 1, 0))
        z2 = jnp.where(rsub == 0, p2, jnp.where(rsub == 1, p1, pltpu.roll(cz, 2, 0)))
        y = cbias_ref[...] + cw_ref[0:1, :] * z2 + cw_ref[1:2, :] * z1 + cw_ref[2:3, :] * cz
        yield
        oc_ref[rows, :] = (proj(OFF_CB, C_WIDTH) * y).astype(BF16)
        yield
        qv = proj(OFF_Q, Q_W)
        for c in range(Q_W // LANES):
            q_ref[c, rows, :] = qv[:, c * LANES:(c + 1) * LANES] * QSCALE
        yield
        kv = proj(OFF_K, 2 * KV_W)
        for c in range(KV_W // LANES):
            k_ref[c, rows, :] = kv[:, c * LANES:(c + 1) * LANES]
            v_ref[c, rows, :] = kv[:, KV_W + c * LANES:KV_W + (c + 1) * LANES]

    _trace_staggered([sub_tile(r0) for r0 in range(0, tm, INPROJ_SUB)], INPROJ_LAG)
    prev_ref[...] = carry["tail"]
    cs_ref[0] = carry["tail"]
    hs = carry["h"]

    @pl.when(i >= pl.num_programs(1) - WIN // tm)
    def _():
        for j, h in enumerate(hs):
            cols = slice(j * INPROJ_SUB, (j + 1) * INPROJ_SUB)
            kt_ref[0, :, cols] = _dot_nt(wkt_ref[...], h)
            vt_ref[0, :, cols] = _dot_nt(wvt_ref[...], h)


def _inproj_prompt(layer, x, gpre, w_in, wkt, wvt, lng, lnb, ws, bs_t, cw, cbias, bg):
    tm = PROMPT_TM
    nt = SEQ // tm
    rows = BATCH * SEQ
    win_spec = pl.BlockSpec((1, KV_W, tm), lambda b, i: (b, 0, jnp.maximum(i - (nt - WIN // tm), 0)))

    def rowblk(width):
        return pl.BlockSpec((tm, width), lambda b, i: (b * nt + i, 0))

    def slab(n):
        return pl.BlockSpec((n, tm, LANES), lambda b, i: (0, b * nt + i, 0))

    return pl.pallas_call(
        _inproj_prompt_kernel,
        grid=(BATCH, nt),
        in_specs=[
            rowblk(D_MODEL),
            _layer_spec(layer, (1, D_MODEL)),
            _layer_spec(layer, (D_MODEL, IN_W)),
            _layer_spec(layer, (KV_W, D_MODEL)),
            _layer_spec(layer, (KV_W, D_MODEL)),
            _layer_spec(layer, (1, A_WIDTH)),
            _layer_spec(layer, (1, A_WIDTH)),
            _layer_spec(layer, (A_GROUPS, CHUNK, CHUNK)),
            _layer_spec(layer, (CHUNK, A_GROUPS)),
            _layer_spec(layer, (CONV_W, C_WIDTH)),
            _layer_spec(layer, (1, C_WIDTH)),
            _layer_spec(layer, (1, 3 * D_MODEL)),
        ],
        out_specs=[
            rowblk(A_WIDTH),
            slab(Q_W // LANES),
            slab(KV_W // LANES),
            slab(KV_W // LANES),
            win_spec,
            win_spec,
            rowblk(C_WIDTH),
            rowblk(3 * D_MODEL),
            pl.BlockSpec((1, SUBLANES, C_WIDTH), lambda b, i: (b, 0, 0)),
        ],
        out_shape=[
            jax.ShapeDtypeStruct((rows, A_WIDTH), BF16),
            jax.ShapeDtypeStruct((Q_W // LANES, rows, LANES), F32),
            jax.ShapeDtypeStruct((KV_W // LANES, rows, LANES), F32),
            jax.ShapeDtypeStruct((KV_W // LANES, rows, LANES), F32),
            jax.ShapeDtypeStruct((BATCH, KV_W, WIN), F32),
            jax.ShapeDtypeStruct((BATCH, KV_W, WIN), F32),
            jax.ShapeDtypeStruct((rows, C_WIDTH), BF16),
            jax.ShapeDtypeStruct((rows, 3 * D_MODEL), BF16),
            jax.ShapeDtypeStruct((BATCH, SUBLANES, C_WIDTH), F32),
        ],
        scratch_shapes=[pltpu.VMEM((SUBLANES, C_WIDTH), F32)],
        compiler_params=_params(("arbitrary", "arbitrary")),
        name="inproj_prompt",
    )(x, gpre, w_in, wkt, wvt, lng, lnb, ws, bs_t, cw, cbias, bg)


def _attn_prompt_kernel(q_ref, k_ref, v_ref, o_ref, ktail, vtail, k4, v4, q24, acc, mrun, lrun, tmp, bias_ref):
    b = pl.program_id(0)
    i = pl.program_id(1)
    first = i == 0
    nslab = KV_W // LANES
    quarter = SPAN // RES

    @pl.when(jnp.logical_and(b == 0, first))
    def _():
        rr = lax.broadcasted_iota(jnp.int32, (KV_HEADS * BAND, 2 * BAND), 0)
        kk = lax.broadcasted_iota(jnp.int32, (KV_HEADS * BAND, 2 * BAND), 1)
        dist = (rr & (BAND - 1)) + BAND - kk
        hrow = rr >> 7
        ok = jnp.logical_and(dist >= 0, dist <= BAND)
        for g, dil in enumerate(DILS):
            coef = _select_rows(hrow, [jnp.float32(SLOPES[g][h] * dil * LOG2E) for h in range(KV_HEADS)])
            full = jnp.where(ok, -(coef * dist.astype(F32)), NEG)
            bias_ref[2 * g] = full
            bias_ref[2 * g + 1] = jnp.where(kk >= BAND, full, NEG)

    @pl.when(first)
    def _():
        for r in range(RES):
            base = r * 2 * quarter
            k4[:, base:base + quarter, :] = jnp.zeros((nslab, quarter, LANES), F32)
            v4[:, base:base + quarter, :] = jnp.zeros((nslab, quarter, LANES), F32)
        ktail[...] = jnp.zeros_like(ktail)
        vtail[...] = jnp.zeros_like(vtail)

    @pl.when(i > 0)
    def _():
        for r in range(RES):
            base = r * 2 * quarter
            k4[:, base:base + quarter, :] = k4[:, base + quarter:base + 2 * quarter, :]
            v4[:, base:base + quarter, :] = v4[:, base + quarter:base + 2 * quarter, :]

    for r in range(RES):
        base = r * 2 * quarter + quarter
        for c in range(nslab):
            k4[c, base:base + quarter, :] = k_ref[c, pl.ds(r, quarter, stride=RES), :]
            v4[c, base:base + quarter, :] = v_ref[c, pl.ds(r, quarter, stride=RES), :]
            q24[c, r * quarter:(r + 1) * quarter, :] = q_ref[2 * nslab + c, pl.ds(r, quarter, stride=RES), :]

    lane = lax.broadcasted_iota(jnp.int32, (1, KV_W), 1)
    hid = lane >> 6
    headmask = [jnp.where(hid == h, 1.0, 0.0).astype(BF16) for h in range(KV_HEADS)]

    def wide(ref, rows, slab0=0):
        return jnp.concatenate([ref[slab0 + c, rows, :] for c in range(nslab)], axis=1)

    def softmax_unit(qb, kb, vb, bias):
        qb16 = qb.astype(BF16)
        kb16 = kb.astype(BF16)
        vb16 = vb.astype(BF16)
        rs, ms, ls = [], [], []
        for h0 in range(0, KV_HEADS, HEAD_STACK):
            qs = jnp.concatenate([qb16 * headmask[h0 + j] for j in range(HEAD_STACK)], axis=0)
            s = _dot_nt(qs, kb16)
            ps = []
            for j in range(HEAD_STACK):
                sh = s[j * BAND:(j + 1) * BAND] + bias_ref[bias, (h0 + j) * BAND:(h0 + j + 1) * BAND, :]
                mh = jnp.max(sh, axis=-1, keepdims=True)
                ph = jnp.exp2(sh - mh)
                ps.append(ph.astype(BF16))
                ms.append(mh)
                ls.append(jnp.sum(ph, axis=-1, keepdims=True))
            r = _dot(jnp.concatenate(ps, axis=0), vb16)
            rs += [r[j * BAND:(j + 1) * BAND] for j in range(HEAD_STACK)]
        return _select_rows(hid, rs), _select_rows(hid, ms), _select_rows(hid, ls)

    def store_stats(rows, o, m, l):
        for c in range(nslab):
            ls = slice(c * LANES, (c + 1) * LANES)
            acc[c, rows, :] = o[:, ls]
            mrun[c, rows, :] = m[:, ls]
            lrun[c, rows, :] = l[:, ls]

    def merge_stats(rows, o_u, m_u, l_u):
        m_old = wide(mrun, rows)
        m_n = jnp.maximum(m_old, m_u)
        a_old = jnp.exp2(m_old - m_n)
        a_new = jnp.exp2(m_u - m_n)
        store_stats(rows, wide(acc, rows) * a_old + o_u * a_new, m_n, wide(lrun, rows) * a_old + l_u * a_new)

    def unit0(n, kb, vb, bias):
        q0 = n * BAND if isinstance(n, int) else pl.multiple_of(n * BAND, BAND)
        stats = softmax_unit(wide(q_ref, pl.ds(q0, BAND)), kb, vb, bias)
        sub = BAND // RES
        for j, x in enumerate(stats):
            for c in range(nslab):
                tmp[j * nslab + c, pl.ds(q0, BAND), :] = x[:, c * LANES:(c + 1) * LANES]
        for r in range(RES):
            src = pl.ds(q0 + r, sub, stride=RES)
            dst = pl.ds(r * quarter + n * sub, sub)
            for c in range(nslab):
                acc[c, dst, :] = tmp[c, src, :]
                mrun[c, dst, :] = tmp[nslab + c, src, :]
                lrun[c, dst, :] = tmp[2 * nslab + c, src, :]

    kb0 = jnp.concatenate([wide(ktail, pl.ds(0, BAND)), wide(k_ref, pl.ds(0, BAND))], axis=0)
    vb0 = jnp.concatenate([wide(vtail, pl.ds(0, BAND)), wide(v_ref, pl.ds(0, BAND))], axis=0)
    unit0(0, kb0, vb0, jnp.where(first, 1, 0))

    def body0(n, carry):
        keys = pl.ds(pl.multiple_of((n - 1) * BAND, BAND), 2 * BAND)
        unit0(n, wide(k_ref, keys), wide(v_ref, keys), 0)
        return carry

    lax.fori_loop(1, SPAN // BAND, body0, 0, unroll=5)

    def body1(s, carry):
        bias = 2 + jnp.where(jnp.logical_and(first, s == 0), 1, 0)
        for r in range(RES):
            qb = wide(q_ref, pl.ds(s * (BAND * RES) + r, BAND, stride=RES), nslab)
            keys = pl.ds(r * 2 * quarter + quarter + (s - 1) * BAND, 2 * BAND)
            o_u, m_u, l_u = softmax_unit(qb, wide(k4, keys), wide(v4, keys), bias)
            merge_stats(pl.ds(r * quarter + s * BAND, BAND), o_u, m_u, l_u)
        return carry

    lax.fori_loop(0, quarter // BAND, body1, 0, unroll=2)

    def body2(r, carry):
        bias = 4 + jnp.where(first, 1, 0)
        for a in range(DILS[2] // RES):
            rows = pl.ds(r * quarter + a, BAND, stride=RES)
            keys = pl.ds(r * 2 * quarter + a, 2 * BAND, stride=RES)
            o_u, m_u, l_u = softmax_unit(wide(q24, rows), wide(k4, keys), wide(v4, keys), bias)
            merge_stats(rows, o_u, m_u, l_u)
        return carry

    lax.fori_loop(0, RES, body2, 0, unroll=2)

    for r in range(RES):
        rows = pl.ds(r * quarter, quarter)
        for c in range(nslab):
            tmp[c, pl.ds(r, quarter, stride=RES), :] = acc[c, rows, :] / lrun[c, rows, :]
    for c in range(nslab):
        o_ref[:, c * LANES:(c + 1) * LANES] = tmp[c].astype(BF16)
    ktail[...] = k_ref[:, SPAN - BAND:SPAN, :]
    vtail[...] = v_ref[:, SPAN - BAND:SPAN, :]


def _attn_prompt(q, k, v):
    ns = SEQ // SPAN
    nslab = KV_W // LANES
    rows = BATCH * SEQ

    def slab(n):
        return pl.BlockSpec((n, SPAN, LANES), lambda b, i: (0, b * ns + i, 0))

    return pl.pallas_call(
        _attn_prompt_kernel,
        grid=(BATCH, ns),
        in_specs=[slab(Q_W // LANES), slab(nslab), slab(nslab)],
        out_specs=pl.BlockSpec((SPAN, KV_W), lambda b, i: (b * ns + i, 0)),
        out_shape=jax.ShapeDtypeStruct((rows, KV_W), BF16),
        scratch_shapes=[
            pltpu.VMEM((nslab, BAND, LANES), F32),
            pltpu.VMEM((nslab, BAND, LANES), F32),
            pltpu.VMEM((nslab, 2 * SPAN, LANES), F32),
            pltpu.VMEM((nslab, 2 * SPAN, LANES), F32),
            pltpu.VMEM((nslab, SPAN, LANES), F32),
            pltpu.VMEM((nslab, SPAN, LANES), F32),
            pltpu.VMEM((nslab, SPAN, LANES), F32),
            pltpu.VMEM((nslab, SPAN, LANES), F32),
            pltpu.VMEM((3 * nslab, SPAN, LANES), F32),
            pltpu.VMEM((2 * N_DIL, KV_HEADS * BAND, 2 * BAND), F32),
        ],
        compiler_params=_params(("arbitrary", "arbitrary")),
        name="attn_prompt",
    )(q, k, v)


def _merge_ffn_kernel(x_ref, oa_ref, ob_ref, oc_ref, gate_ref, wa_ref, wb_ref, wc_ref, wo_ref, gqm_ref,
                      gpf_ref, wg_ref, wu_ref, wd_ref, gqf_ref, o_ref):
    tm = x_ref.shape[0]
    sub = min(tm, ROW_SUB)

    def sub_tile(r0):
        rows = slice(r0, r0 + sub)

        def gate(j):
            return gate_ref[rows, j * D_MODEL:(j + 1) * D_MODEL].astype(F32)

        merged = gate(0) * _dot(oa_ref[rows, :], wa_ref[...])
        yield
        merged = merged + gate(1) * _dot(ob_ref[rows, :].astype(BF16), wb_ref[...])
        yield
        merged = merged + gate(2) * _dot(oc_ref[rows, :], wc_ref[...])
        yield
        x = x_ref[rows, :] + _rms(_dot(merged.astype(BF16), wo_ref[...]), gqm_ref[...])
        yield
        h = _rms(x, gpf_ref[...]).astype(BF16)
        y = None
        for lo, hi in FF_CHUNKS:
            yield
            a = jax.nn.silu(_dot(h, wg_ref[:, lo:hi]))
            yield
            act = (a * _dot(h, wu_ref[:, lo:hi])).astype(BF16)
            yield
            part = _dot(act, wd_ref[lo:hi, :])
            y = part if y is None else y + part
        yield
        o_ref[rows, :] = x + _rms(y, gqf_ref[...])

    _trace_staggered([sub_tile(r0) for r0 in range(0, tm, sub)], MERGE_FFN_LAG)


def _merge_ffn(layer, x, oa, ob, oc, gates, wa, wb, wc, wo, gqm, gpf, wg, wu, wd, gqf, tm):
    rows = x.shape[0]

    def rowblk(width):
        return pl.BlockSpec((tm, width), lambda i: (i, 0))

    return pl.pallas_call(
        _merge_ffn_kernel,
        grid=(rows // tm,),
        in_specs=[
            rowblk(D_MODEL), rowblk(A_WIDTH), rowblk(KV_W), rowblk(C_WIDTH), rowblk(3 * D_MODEL),
            _layer_spec(layer, (A_WIDTH, D_MODEL)), _layer_spec(layer, (KV_W, D_MODEL)),
            _layer_spec(layer, (C_WIDTH, D_MODEL)), _layer_spec(layer, (D_MODEL, D_MODEL)),
            _layer_spec(layer, (1, D_MODEL)),
            _layer_spec(layer, (1, D_MODEL)), _layer_spec(layer, (D_MODEL, D_FF)),
            _layer_spec(layer, (D_MODEL, D_FF)), _layer_spec(layer, (D_FF, D_MODEL)),
            _layer_spec(layer, (1, D_MODEL)),
        ],
        out_specs=rowblk(D_MODEL),
        out_shape=jax.ShapeDtypeStruct((rows, D_MODEL), F32),
        compiler_params=_params(("parallel",)),
        name="merge_ffn",
    )(x, oa, ob, oc, gates, wa, wb, wc, wo, gqm, gpf, wg, wu, wd, gqf)


def _inproj_sample_kernel(ws_ref, bs_ref, x_ref, gpre_ref, w_ref, lng_ref, lnb_ref, cw_ref, cbias_ref, bg_ref,
                          p1_ref, p2_ref,
                          oa_ref, q_ref, k_ref, v_ref, oc_ref, gate_ref, vn_ref, cz_ref, *, layer):
    n = SAMPLE_ROWS
    h = _rms(x_ref[...], gpre_ref[...]).astype(BF16)

    def proj(off, width):
        return _dot(h, w_ref[:, off:off + width])

    u = jax.nn.gelu(proj(OFF_U, A_WIDTH))
    vn = _layer_norm(jax.nn.gelu(proj(OFF_V, A_WIDTH)), lng_ref[...], lnb_ref[...])
    vn_ref[...] = vn
    t = lax.broadcasted_iota(jnp.int32, (n, 1), 0) & (DEC_SEQ - 1)
    for g in range(A_GROUPS):
        cs = slice(g * A_GROUP_DIM, (g + 1) * A_GROUP_DIM)
        base = (layer * A_GROUPS + g) * DEC_SEQ
        vg = vn[:, cs]
        sa = _select_rows(t, [bs_ref[base + tt] for tt in range(DEC_SEQ)])
        for k in range(DEC_SEQ):
            coef = jnp.zeros((n, 1), F32)
            for tt in range(k, DEC_SEQ):
                coef = jnp.where(t == tt, ws_ref[(base + tt) * DEC_SEQ + tt - k], coef)
            sa = sa + coef * (vg if k == 0 else pltpu.roll(vg, k, 0))
        oa_ref[:, cs] = (u[:, cs] * sa).astype(BF16)

    q_ref[...] = proj(OFF_Q, Q_W)
    kv = proj(OFF_K, 2 * KV_W)
    k_ref[...] = kv[:, 0:KV_W]
    v_ref[...] = kv[:, KV_W:2 * KV_W]

    cz = proj(OFF_CC, C_WIDTH) * proj(OFF_CX, C_WIDTH)
    cz_ref[...] = cz
    z1 = jnp.where(t >= 1, pltpu.roll(cz, 1, 0), p1_ref[...])
    z2 = jnp.where(t >= 2, pltpu.roll(cz, 2, 0), p2_ref[...])
    y = cbias_ref[...] + cw_ref[0:1, :] * z2 + cw_ref[1:2, :] * z1 + cw_ref[2:3, :] * cz
    oc_ref[...] = (proj(OFF_CB, C_WIDTH) * y).astype(BF16)

    for j in range(3):
        cs = slice(j * D_MODEL, (j + 1) * D_MODEL)
        gate_ref[:, cs] = jax.nn.sigmoid(proj(OFF_G + j * D_MODEL, D_MODEL) + bg_ref[:, cs]).astype(BF16)


def _inproj_sample(layer, ws4, bs4, x, gpre, w_in, lng, lnb, cw, cbias, bg, p1, p2):
    n = SAMPLE_ROWS
    smem = pl.BlockSpec(memory_space=pltpu.SMEM)

    def full(shape):
        nd = len(shape)
        return pl.BlockSpec(shape, lambda i: (0,) * nd)

    widths_dtypes = [(A_WIDTH, BF16), (Q_W, F32), (KV_W, F32), (KV_W, F32), (C_WIDTH, BF16),
                     (3 * D_MODEL, BF16), (A_WIDTH, F32), (C_WIDTH, F32)]
    return pl.pallas_call(
        functools.partial(_inproj_sample_kernel, layer=layer),
        grid=(1,),
        in_specs=[smem, smem, full((n, D_MODEL)), _layer_spec(layer, (1, D_MODEL)),
                  _layer_spec(layer, (D_MODEL, IN_W)), _layer_spec(layer, (1, A_WIDTH)),
                  _layer_spec(layer, (1, A_WIDTH)), _layer_spec(layer, (CONV_W, C_WIDTH)),
                  _layer_spec(layer, (1, C_WIDTH)), _layer_spec(layer, (1, 3 * D_MODEL)),
                  _layer_spec(layer, (n, C_WIDTH)), _layer_spec(layer, (n, C_WIDTH))],
        out_specs=[full((n, w)) for w, _ in widths_dtypes],
        out_shape=[jax.ShapeDtypeStruct((n, w), dt) for w, dt in widths_dtypes],
        compiler_params=_params(("arbitrary",)),
        name="inproj_sample",
    )(ws4, bs4, x, gpre, w_in, lng, lnb, cw, cbias, bg, p1, p2)


def _attn_sample_kernel(qh_ref, kn_ref, vn_ref, kt_ref, vt_ref, o_ref, bias_ref):
    nr = N_DIL * SUBLANES
    width = MAX_WINDOW

    @pl.when(pl.program_id(0) == 0)
    def _():
        row = lax.broadcasted_iota(jnp.int32, (nr, width), 0)
        pos = lax.broadcasted_iota(jnp.int32, (nr, width), 1)
        grp = row >> 3
        delta = width + (row & (SUBLANES - 1)) - pos
        dil_m1 = _select_rows(grp, [d - 1 for d in DILS])
        reach = _select_rows(grp, [BAND * d for d in DILS])
        ok = jnp.logical_and((delta & dil_m1) == 0, delta <= reach)
        for h in range(KV_HEADS):
            slope = _select_rows(grp, [jnp.float32(SLOPES[g][h]) for g in range(N_DIL)])
            bias_ref[h * nr:(h + 1) * nr, :] = jnp.where(ok, -(slope * delta.astype(F32)), NEG)

    ridx = lax.broadcasted_iota(jnp.int32, (nr, 1), 0)
    g_of = ridx >> 3
    t_of = ridx & (SUBLANES - 1)
    scale = HEAD_DIM ** -0.5
    slopes = [_select_rows(g_of, [jnp.float32(SLOPES[g][h]) for g in range(N_DIL)]) for h in range(KV_HEADS)]
    for e in range(qh_ref.shape[0]):
        qs = [qh_ref[e, h] * scale for h in range(KV_HEADS)]
        s = jnp.concatenate([_dot(qs[h].astype(BF16), kt_ref[0, e, h].astype(BF16)) for h in range(KV_HEADS)],
                            axis=0)
        s = s + bias_ref[...]
        m = jnp.max(s, axis=-1, keepdims=True)
        s_new = []
        for n in range(DEC_SEQ):
            dn = t_of - n
            ok = jnp.logical_or(dn == 0, jnp.logical_and(g_of == 0, dn > 0))
            rows = []
            for h in range(KV_HEADS):
                raw = jnp.sum(qs[h] * kn_ref[e, h, n:n + 1, :], axis=-1, keepdims=True)
                rows.append(jnp.where(ok, raw - slopes[h] * dn.astype(F32), NEG))
            sn = jnp.concatenate(rows, axis=0)
            s_new.append(sn)
            m = jnp.maximum(m, sn)
        p = jnp.exp(s - m)
        l_all = jnp.sum(p, axis=-1, keepdims=True)
        p16 = p.astype(BF16)
        p_new = [jnp.exp(sn - m) for sn in s_new]
        for pn in p_new:
            l_all = l_all + pn
        for h in range(KV_HEADS):
            hr = slice(h * nr, (h + 1) * nr)
            r = _dot_nt(p16[hr], vt_ref[0, e, h].astype(BF16))
            for n in range(DEC_SEQ):
                r = r + p_new[n][hr] * vn_ref[e, h, n:n + 1, :]
            m_h = m[hr]
            l_h = l_all[hr]
            parts = [(r[g * SUBLANES:(g + 1) * SUBLANES], m_h[g * SUBLANES:(g + 1) * SUBLANES],
                      l_h[g * SUBLANES:(g + 1) * SUBLANES]) for g in range(N_DIL)]
            m_all = jnp.maximum(jnp.maximum(parts[0][1], parts[1][1]), parts[2][1])
            num = jnp.zeros((SUBLANES, HEAD_DIM), F32)
            den = jnp.zeros((SUBLANES, 1), F32)
            for o_g, m_g, l_g in parts:
                w = jnp.exp(m_g - m_all)
                num = num + o_g * w
                den = den + l_g * w
            o_ref[e, :, h * HEAD_DIM:(h + 1) * HEAD_DIM] = (num / den)[0:DEC_SEQ]


def _attn_sample(layer, qh, knh, vnh, ckt, cvt):
    nr = N_DIL * SUBLANES
    eb = SAMPLE_EB
    cache_spec = pl.BlockSpec((1, eb, KV_HEADS, HEAD_DIM, MAX_WINDOW), lambda b: (layer, b, 0, 0, 0))
    new_spec = pl.BlockSpec((eb, KV_HEADS, SUBLANES, HEAD_DIM), lambda b: (b, 0, 0, 0))
    return pl.pallas_call(
        _attn_sample_kernel,
        grid=(DEC_BATCH // eb,),
        in_specs=[pl.BlockSpec((eb, KV_HEADS, nr, HEAD_DIM), lambda b: (b, 0, 0, 0)),
                  new_spec, new_spec, cache_spec, cache_spec],
        out_specs=pl.BlockSpec((eb, DEC_SEQ, KV_W), lambda b: (b, 0, 0)),
        out_shape=jax.ShapeDtypeStruct((DEC_BATCH, DEC_SEQ, KV_W), F32),
        scratch_shapes=[pltpu.VMEM((KV_HEADS * nr, MAX_WINDOW), F32)],
        compiler_params=_params(("arbitrary",)),
        name="attn_sample",
    )(qh, knh, vnh, ckt, cvt)


def kernel(x_prompt, x_sample, cache_k_win, cache_v_win, state_conv, g_pre_mix, g_post_mix, g_pre_ffn, g_post_ffn,
           w_in, a_ln_g, a_ln_b, a_ws, a_bs, c_conv_w, c_conv_b, w_br_a, w_br_b, w_br_c, b_gate, w_o,
           w_ff_gate, w_ff_up, w_ff_down):
    xp = x_prompt.reshape(BATCH * SEQ, D_MODEL)
    xs = x_sample.reshape(SAMPLE_ROWS, D_MODEL)
    ckt = jnp.transpose(cache_k_win, (0, 1, 3, 4, 2))
    cvt = jnp.transpose(cache_v_win, (0, 1, 3, 4, 2))

    def vec(a):
        return a.reshape(DEPTH, 1, -1)

    w_in_b = w_in.astype(BF16)
    wkt = jnp.swapaxes(w_in[:, :, OFF_K:OFF_K + KV_W], 1, 2).astype(BF16)
    wvt = jnp.swapaxes(w_in[:, :, OFF_VV:OFF_VV + KV_W], 1, 2).astype(BF16)
    wa, wb, wc, wo = (w.astype(BF16) for w in (w_br_a, w_br_b, w_br_c, w_o))
    wg, wu, wd = (w.astype(BF16) for w in (w_ff_gate, w_ff_up, w_ff_down))
    gpm, gqm, gpf, gqf = vec(g_pre_mix), vec(g_post_mix), vec(g_pre_ffn), vec(g_post_ffn)
    lng, lnb, cbias, bg = vec(a_ln_g), vec(a_ln_b), vec(c_conv_b), vec(b_gate)
    bs_t = jnp.swapaxes(a_bs, 1, 2)
    ws4 = a_ws[:, :, :DEC_SEQ, :DEC_SEQ].reshape(-1)
    bs4 = a_bs[:, :, :DEC_SEQ].reshape(-1)
    zeros = jnp.zeros((DEPTH, DEC_BATCH, 1, C_WIDTH), F32)
    p1 = jnp.concatenate([state_conv[:, :, 1:2], zeros, zeros, zeros], axis=2).reshape(DEPTH, SAMPLE_ROWS, C_WIDTH)
    p2 = jnp.concatenate([state_conv, zeros, zeros], axis=2).reshape(DEPTH, SAMPLE_ROWS, C_WIDTH)

    kp_l, vp_l, ks_l, vs_l, cp_l, cs_l, av_l = [], [], [], [], [], [], []
    for l in range(DEPTH):
        oa, q, k, v, kt, vt, oc, gates, cs8 = _inproj_prompt(
            l, xp, gpm, w_in_b, wkt, wvt, lng, lnb, a_ws, bs_t, c_conv_w, cbias, bg)
        ob = _attn_prompt(q, k, v)
        xp = _merge_ffn(l, xp, oa, ob, oc, gates, wa, wb, wc, wo, gqm, gpf, wg, wu, wd, gqf, PROMPT_TM)

        def window(t):
            return jnp.transpose(t.reshape(BATCH, KV_HEADS, HEAD_DIM, WIN), (0, 3, 1, 2))

        kp_l.append(window(kt))
        vp_l.append(window(vt))
        cp_l.append(cs8[:, SUBLANES - (CONV_W - 1):])

        oa_s, q_s, k_s, v_s, oc_s, gates_s, vn_s, cz_s = _inproj_sample(
            l, ws4, bs4, xs, gpm, w_in_b, lng, lnb, c_conv_w, cbias, bg, p1, p2)

        pad_t = ((0, 0), (0, 0), (0, 0), (0, SUBLANES - DEC_SEQ), (0, 0))
        qh = q_s.reshape(DEC_BATCH, DEC_SEQ, N_DIL, KV_HEADS, HEAD_DIM).transpose(0, 3, 2, 1, 4)
        qh = jnp.pad(qh, pad_t).reshape(DEC_BATCH, KV_HEADS, N_DIL * SUBLANES, HEAD_DIM)

        def new_rows(a):
            a = a.reshape(DEC_BATCH, DEC_SEQ, KV_HEADS, HEAD_DIM).transpose(0, 2, 1, 3)
            return jnp.pad(a, pad_t[1:])

        ob_s = _attn_sample(l, qh, new_rows(k_s), new_rows(v_s), ckt, cvt)
        ob_s = ob_s.reshape(SAMPLE_ROWS, KV_W)
        xs = _merge_ffn(l, xs, oa_s, ob_s, oc_s, gates_s, wa, wb, wc, wo, gqm, gpf, wg, wu, wd, gqf, SAMPLE_ROWS)

        ks_l.append(k_s.reshape(DEC_BATCH, DEC_SEQ, KV_HEADS, HEAD_DIM))
        vs_l.append(v_s.reshape(DEC_BATCH, DEC_SEQ, KV_HEADS, HEAD_DIM))
        cs_l.append(cz_s.reshape(DEC_BATCH, DEC_SEQ, C_WIDTH)[:, DEC_SEQ - (CONV_W - 1):])
        av_l.append(vn_s.reshape(DEC_BATCH, DEC_SEQ, A_WIDTH))

    return (xp.reshape(BATCH, SEQ, D_MODEL), xs.reshape(DEC_BATCH, DEC_SEQ, D_MODEL),
            jnp.stack(kp_l), jnp.stack(vp_l), jnp.stack(ks_l), jnp.stack(vs_l),
            jnp.stack(cp_l), jnp.stack(cs_l), jnp.stack(av_l))
```

```python
import functools

import jax
import jax.numpy as jnp
from jax import lax
from jax.experimental import pallas as pl
from jax.experimental.pallas import tpu as pltpu

F32 = jnp.float32
BF16 = jnp.bfloat16

D_MODEL = 1024
BATCH = 2
SEQ = 8192
DEPTH = 2
DEC_BATCH = 32
DEC_SEQ = 4
CHUNK = 128
A_GROUPS = 4
A_GROUP_DIM = 128
A_WIDTH = A_GROUPS * A_GROUP_DIM
HEAD_DIM = 64
KV_HEADS = 4
DILS = (1, 4, 16)
N_DIL = len(DILS)
Q_W = N_DIL * KV_HEADS * HEAD_DIM
KV_W = KV_HEADS * HEAD_DIM
BAND = 128
MAX_WINDOW = 2048
C_WIDTH = 512
CONV_W = 3
D_FF = 2816
EPS = 1e-6

OFF_U = 0
OFF_V = OFF_U + A_WIDTH
OFF_Q = OFF_V + A_WIDTH
OFF_K = OFF_Q + Q_W
OFF_VV = OFF_K + KV_W
OFF_CX = OFF_VV + KV_W
OFF_CB = OFF_CX + C_WIDTH
OFF_CC = OFF_CB + C_WIDTH
OFF_G = OFF_CC + C_WIDTH
IN_W = OFF_G + 3 * D_MODEL

LANES = 128
SUBLANES = 8
VMEM_LIMIT = 56 * 1024 * 1024
NEG = -1e30

SAMPLE_ROWS = DEC_BATCH * DEC_SEQ
WIN = min(MAX_WINDOW, SEQ)
SPAN = BAND * DILS[-1]
RES = DILS[1]
PROMPT_TM = 512
ROW_SUB = 256
INPROJ_SUB = ROW_SUB
INPROJ_LAG = 2
MERGE_FFN_LAG = 3
HEAD_STACK = KV_HEADS
SAMPLE_EB = 4
MXU_N = 256
FF_CHUNKS = ((0, 6 * MXU_N), (6 * MXU_N, D_FF))
LOG2E = 1.4426950408889634
QSCALE = LOG2E * HEAD_DIM ** -0.5

SLOPES = tuple(
    tuple(2.0 ** (-8.0 * (g * KV_HEADS + h + 1) / (N_DIL * KV_HEADS)) for h in range(KV_HEADS))
    for g in range(N_DIL)
)


def _rms(x, g):
    return x * lax.rsqrt(jnp.mean(x * x, axis=-1, keepdims=True) + EPS) * g


def _layer_norm(x, g, b):
    mu = jnp.mean(x, axis=-1, keepdims=True)
    xc = x - mu
    return xc * lax.rsqrt(jnp.mean(xc * xc, axis=-1, keepdims=True) + EPS) * g + b


def _dot(a, b):
    return jnp.dot(a, b, preferred_element_type=F32)


def _dot_nt(a, b):
    return lax.dot_general(a, b, (((1,), (1,)), ((), ())), preferred_element_type=F32)


def _select_rows(hid, vals):
    out = vals[-1]
    for h in range(len(vals) - 2, -1, -1):
        out = jnp.where(hid == h, vals[h], out)
    return out


def _layer_spec(layer, shape):
    nd = len(shape)
    return pl.BlockSpec((None,) + tuple(shape), lambda *_: (layer,) + (0,) * nd, pipeline_mode=pl.Buffered(1))


VEC_SLOTS = {"bg": (3 * D_MODEL, 0), "gpm": (D_MODEL, 3), "gqm": (D_MODEL, 4), "gpf": (D_MODEL, 5),
             "gqf": (D_MODEL, 6), "lng": (A_WIDTH, 14), "lnb": (A_WIDTH, 15), "cbias": (C_WIDTH, 16)}


def _vec_spec(layer, name):
    width, idx = VEC_SLOTS[name]
    return pl.BlockSpec((None, 1, width), lambda *_: (layer, 0, idx), pipeline_mode=pl.Buffered(1))


def _trace_staggered(stage_gens, lag):
    live = [True] * len(stage_gens)
    tick = 0
    while any(live):
        for j, gen in enumerate(stage_gens):
            if live[j] and tick >= j * lag:
                live[j] = next(gen, "done") != "done"
        tick += 1


def _params(sem):
    return pltpu.CompilerParams(dimension_semantics=sem, vmem_limit_bytes=VMEM_LIMIT)


def _inproj_prompt_kernel(x_ref, gpre_ref, w_ref, wkt_ref, wvt_ref, lng_ref, lnb_ref, ws_ref, bs_ref, cw_ref,
                          cbias_ref, bg_ref,
                          oa_ref, q_ref, k_ref, v_ref, kt_ref, vt_ref, oc_ref, gate_ref, cs_ref, prev_ref):
    tm = x_ref.shape[0]
    i = pl.program_id(1)

    @pl.when(i == 0)
    def _():
        prev_ref[...] = jnp.zeros_like(prev_ref)

    row = lax.broadcasted_iota(jnp.int32, (CHUNK, CHUNK), 0)
    col = lax.broadcasted_iota(jnp.int32, (CHUNK, CHUNK), 1)
    wt = [jnp.where(row >= col, ws_ref[g], 0.0).astype(BF16) for g in range(A_GROUPS)]
    rsub = lax.broadcasted_iota(jnp.int32, (INPROJ_SUB, 1), 0)

    carry = {"tail": prev_ref[...], "h": []}

    def sub_tile(r0):
        rows = slice(r0, r0 + INPROJ_SUB)
        h = _rms(x_ref[rows, :], gpre_ref[...]).astype(BF16)
        carry["h"].append(h)

        def proj(off, width):
            return _dot(h, w_ref[:, off:off + width])

        yield
        u = jax.nn.gelu(proj(OFF_U, A_WIDTH))
        yield
        vn = _layer_norm(jax.nn.gelu(proj(OFF_V, A_WIDTH)), lng_ref[...], lnb_ref[...])
        for j in range(3):
            yield
            cs = slice(j * D_MODEL, (j + 1) * D_MODEL)
            gate_ref[rows, cs] = jax.nn.sigmoid(proj(OFF_G + j * D_MODEL, D_MODEL) + bg_ref[:, cs]).astype(BF16)
        yield
        for g in range(A_GROUPS):
            cs = slice(g * A_GROUP_DIM, (g + 1) * A_GROUP_DIM)
            for c in range(0, INPROJ_SUB // CHUNK, 2):
                pair = jnp.concatenate([vn[(c + j) * CHUNK:(c + j + 1) * CHUNK, cs] for j in range(2)], axis=1)
                sa2 = _dot(wt[g], pair.astype(BF16)) + bs_ref[:, g:g + 1]
                for j in range(2):
                    rs = slice((c + j) * CHUNK, (c + j + 1) * CHUNK)
                    sa = sa2[:, j * A_GROUP_DIM:(j + 1) * A_GROUP_DIM]
                    oa_ref[r0 + (c + j) * CHUNK:r0 + (c + j + 1) * CHUNK, cs] = (u[rs, cs] * sa).astype(BF16)
        yield
        cz = proj(OFF_CC, C_WIDTH) * proj(OFF_CX, C_WIDTH)
        p2 = carry["tail"][SUBLANES - 2:SUBLANES - 1, :]
        p1 = carry["tail"][SUBLANES - 1:SUBLANES, :]
        carry["tail"] = cz[INPROJ_SUB - SUBLANES:INPROJ_SUB, :]
        z1 = jnp.where(rsub == 0, p1, pltpu.roll(cz, 1, 0))
        z2 = jnp.where(rsub == 0, p2, jnp.where(rsub == 1, p1, pltpu.roll(cz, 2, 0)))
        y = cbias_ref[...] + cw_ref[0:1, :] * z2 + cw_ref[1:2, :] * z1 + cw_ref[2:3, :] * cz
        yield
        oc_ref[rows, :] = (proj(OFF_CB, C_WIDTH) * y).astype(BF16)
        yield
        qv = proj(OFF_Q, Q_W)
        for c in range(Q_W // LANES):
            q_ref[c, rows, :] = qv[:, c * LANES:(c + 1) * LANES] * QSCALE
        yield
        kv = proj(OFF_K, 2 * KV_W)
        for c in range(KV_W // LANES):
            k_ref[c, rows, :] = kv[:, c * LANES:(c + 1) * LANES]
            v_ref[c, rows, :] = kv[:, KV_W + c * LANES:KV_W + (c + 1) * LANES]

    _trace_staggered([sub_tile(r0) for r0 in range(0, tm, INPROJ_SUB)], INPROJ_LAG)
    prev_ref[...] = carry["tail"]
    cs_ref[0] = carry["tail"]
    hs = carry["h"]

    @pl.when(i >= pl.num_programs(1) - WIN // tm)
    def _():
        for j, h in enumerate(hs):
            cols = slice(j * INPROJ_SUB, (j + 1) * INPROJ_SUB)
            kt_ref[0, :, cols] = _dot_nt(wkt_ref[...], h)
            vt_ref[0, :, cols] = _dot_nt(wvt_ref[...], h)


def _inproj_prompt(layer, x, vecs, w_in, wkt, wvt, ws, bs_t, cw):
    tm = PROMPT_TM
    nt = SEQ // tm
    rows = BATCH * SEQ
    win_spec = pl.BlockSpec((1, KV_W, tm), lambda b, i: (b, 0, jnp.maximum(i - (nt - WIN // tm), 0)))

    def rowblk(width):
        return pl.BlockSpec((tm, width), lambda b, i: (b * nt + i, 0))

    def slab(n):
        return pl.BlockSpec((n, tm, LANES), lambda b, i: (0, b * nt + i, 0))

    return pl.pallas_call(
        _inproj_prompt_kernel,
        grid=(BATCH, nt),
        in_specs=[
            rowblk(D_MODEL),
            _vec_spec(layer, "gpm"),
            _layer_spec(layer, (D_MODEL, IN_W)),
            _layer_spec(layer, (KV_W, D_MODEL)),
            _layer_spec(layer, (KV_W, D_MODEL)),
            _vec_spec(layer, "lng"),
            _vec_spec(layer, "lnb"),
            _layer_spec(layer, (A_GROUPS, CHUNK, CHUNK)),
            _layer_spec(layer, (CHUNK, A_GROUPS)),
            _layer_spec(layer, (CONV_W, C_WIDTH)),
            _vec_spec(layer, "cbias"),
            _vec_spec(layer, "bg"),
        ],
        out_specs=[
            rowblk(A_WIDTH),
            slab(Q_W // LANES),
            slab(KV_W // LANES),
            slab(KV_W // LANES),
            win_spec,
            win_spec,
            rowblk(C_WIDTH),
            rowblk(3 * D_MODEL),
            pl.BlockSpec((1, SUBLANES, C_WIDTH), lambda b, i: (b, 0, 0)),
        ],
        out_shape=[
            jax.ShapeDtypeStruct((rows, A_WIDTH), BF16),
            jax.ShapeDtypeStruct((Q_W // LANES, rows, LANES), F32),
            jax.ShapeDtypeStruct((KV_W // LANES, rows, LANES), F32),
            jax.ShapeDtypeStruct((KV_W // LANES, rows, LANES), F32),
            jax.ShapeDtypeStruct((BATCH, KV_W, WIN), F32),
            jax.ShapeDtypeStruct((BATCH, KV_W, WIN), F32),
            jax.ShapeDtypeStruct((rows, C_WIDTH), BF16),
            jax.ShapeDtypeStruct((rows, 3 * D_MODEL), BF16),
            jax.ShapeDtypeStruct((BATCH, SUBLANES, C_WIDTH), F32),
        ],
        scratch_shapes=[pltpu.VMEM((SUBLANES, C_WIDTH), F32)],
        compiler_params=_params(("arbitrary", "arbitrary")),
        name="inproj_prompt",
    )(x, vecs, w_in, wkt, wvt, vecs, vecs, ws, bs_t, cw, vecs, vecs)


def _attn_prompt_kernel(q_ref, k_ref, v_ref, o_ref, ktail, vtail, k4, v4, q24, acc, mrun, lrun, tmp, bias_ref):
    b = pl.program_id(0)
    i = pl.program_id(1)
    first = i == 0
    nslab = KV_W // LANES
    quarter = SPAN // RES

    @pl.when(jnp.logical_and(b == 0, first))
    def _():
        rr = lax.broadcasted_iota(jnp.int32, (KV_HEADS * BAND, 2 * BAND), 0)
        kk = lax.broadcasted_iota(jnp.int32, (KV_HEADS * BAND, 2 * BAND), 1)
        dist = (rr & (BAND - 1)) + BAND - kk
        hrow = rr >> 7
        ok = jnp.logical_and(dist >= 0, dist <= BAND)
        for g, dil in enumerate(DILS):
            coef = _select_rows(hrow, [jnp.float32(SLOPES[g][h] * dil * LOG2E) for h in range(KV_HEADS)])
            full = jnp.where(ok, -(coef * dist.astype(F32)), NEG)
            bias_ref[2 * g] = full
            bias_ref[2 * g + 1] = jnp.where(kk >= BAND, full, NEG)

    @pl.when(first)
    def _():
        for r in range(RES):
            base = r * 2 * quarter
            k4[:, base:base + quarter, :] = jnp.zeros((nslab, quarter, LANES), F32)
            v4[:, base:base + quarter, :] = jnp.zeros((nslab, quarter, LANES), F32)
        ktail[...] = jnp.zeros_like(ktail)
        vtail[...] = jnp.zeros_like(vtail)

    @pl.when(i > 0)
    def _():
        for r in range(RES):
            base = r * 2 * quarter
            k4[:, base:base + quarter, :] = k4[:, base + quarter:base + 2 * quarter, :]
            v4[:, base:base + quarter, :] = v4[:, base + quarter:base + 2 * quarter, :]

    for r in range(RES):
        base = r * 2 * quarter + quarter
        for c in range(nslab):
            k4[c, base:base + quarter, :] = k_ref[c, pl.ds(r, quarter, stride=RES), :]
            v4[c, base:base + quarter, :] = v_ref[c, pl.ds(r, quarter, stride=RES), :]
            q24[c, r * quarter:(r + 1) * quarter, :] = q_ref[2 * nslab + c, pl.ds(r, quarter, stride=RES), :]

    lane = lax.broadcasted_iota(jnp.int32, (1, KV_W), 1)
    hid = lane >> 6
    headmask = [jnp.where(hid == h, 1.0, 0.0).astype(BF16) for h in range(KV_HEADS)]

    def wide(ref, rows, slab0=0):
        return jnp.concatenate([ref[slab0 + c, rows, :] for c in range(nslab)], axis=1)

    def softmax_unit(qb, kb, vb, bias):
        qb16 = qb.astype(BF16)
        kb16 = kb.astype(BF16)
        vb16 = vb.astype(BF16)
        rs, ms, ls = [], [], []
        for h0 in range(0, KV_HEADS, HEAD_STACK):
            qs = jnp.concatenate([qb16 * headmask[h0 + j] for j in range(HEAD_STACK)], axis=0)
            s = _dot_nt(qs, kb16)
            ps = []
            for j in range(HEAD_STACK):
                sh = s[j * BAND:(j + 1) * BAND] + bias_ref[bias, (h0 + j) * BAND:(h0 + j + 1) * BAND, :]
                mh = jnp.max(sh, axis=-1, keepdims=True)
                ph = jnp.exp2(sh - mh)
                ps.append(ph.astype(BF16))
                ms.append(mh)
                ls.append(jnp.sum(ph, axis=-1, keepdims=True))
            r = _dot(jnp.concatenate(ps, axis=0), vb16)
            rs += [r[j * BAND:(j + 1) * BAND] for j in range(HEAD_STACK)]
        return _select_rows(hid, rs), _select_rows(hid, ms), _select_rows(hid, ls)

    def store_stats(rows, o, m, l):
        for c in range(nslab):
            ls = slice(c * LANES, (c + 1) * LANES)
            acc[c, rows, :] = o[:, ls]
            mrun[c, rows, :] = m[:, ls]
            lrun[c, rows, :] = l[:, ls]

    def merge_stats(rows, o_u, m_u, l_u):
        m_old = wide(mrun, rows)
        m_n = jnp.maximum(m_old, m_u)
        a_old = jnp.exp2(m_old - m_n)
        a_new = jnp.exp2(m_u - m_n)
        store_stats(rows, wide(acc, rows) * a_old + o_u * a_new, m_n, wide(lrun, rows) * a_old + l_u * a_new)

    def unit0(n, kb, vb, bias):
        q0 = n * BAND if isinstance(n, int) else pl.multiple_of(n * BAND, BAND)
        stats = softmax_unit(wide(q_ref, pl.ds(q0, BAND)), kb, vb, bias)
        sub = BAND // RES
        for j, x in enumerate(stats):
            for c in range(nslab):
                tmp[j * nslab + c, pl.ds(q0, BAND), :] = x[:, c * LANES:(c + 1) * LANES]
        for r in range(RES):
            src = pl.ds(q0 + r, sub, stride=RES)
            dst = pl.ds(r * quarter + n * sub, sub)
            for c in range(nslab):
                acc[c, dst, :] = tmp[c, src, :]
                mrun[c, dst, :] = tmp[nslab + c, src, :]
                lrun[c, dst, :] = tmp[2 * nslab + c, src, :]

    kb0 = jnp.concatenate([wide(ktail, pl.ds(0, BAND)), wide(k_ref, pl.ds(0, BAND))], axis=0)
    vb0 = jnp.concatenate([wide(vtail, pl.ds(0, BAND)), wide(v_ref, pl.ds(0, BAND))], axis=0)
    unit0(0, kb0, vb0, jnp.where(first, 1, 0))

    def body0(n, carry):
        keys = pl.ds(pl.multiple_of((n - 1) * BAND, BAND), 2 * BAND)
        unit0(n, wide(k_ref, keys), wide(v_ref, keys), 0)
        return carry

    lax.fori_loop(1, SPAN // BAND, body0, 0, unroll=5)

    def body1(s, carry):
        bias = 2 + jnp.where(jnp.logical_and(first, s == 0), 1, 0)
        for r in range(RES):
            qb = wide(q_ref, pl.ds(s * (BAND * RES) + r, BAND, stride=RES), nslab)
            keys = pl.ds(r * 2 * quarter + quarter + (s - 1) * BAND, 2 * BAND)
            o_u, m_u, l_u = softmax_unit(qb, wide(k4, keys), wide(v4, keys), bias)
            merge_stats(pl.ds(r * quarter + s * BAND, BAND), o_u, m_u, l_u)
        return carry

    lax.fori_loop(0, quarter // BAND, body1, 0, unroll=2)

    def body2(r, carry):
        bias = 4 + jnp.where(first, 1, 0)
        for a in range(DILS[2] // RES):
            rows = pl.ds(r * quarter + a, BAND, stride=RES)
            keys = pl.ds(r * 2 * quarter + a, 2 * BAND, stride=RES)
            o_u, m_u, l_u = softmax_unit(wide(q24, rows), wide(k4, keys), wide(v4, keys), bias)
            merge_stats(rows, o_u, m_u, l_u)
        return carry

    lax.fori_loop(0, RES, body2, 0, unroll=2)

    for r in range(RES):
        rows = pl.ds(r * quarter, quarter)
        for c in range(nslab):
            tmp[c, pl.ds(r, quarter, stride=RES), :] = acc[c, rows, :] / lrun[c, rows, :]
    for c in range(nslab):
        o_ref[:, c * LANES:(c + 1) * LANES] = tmp[c].astype(BF16)
    ktail[...] = k_ref[:, SPAN - BAND:SPAN, :]
    vtail[...] = v_ref[:, SPAN - BAND:SPAN, :]


def _attn_prompt(q, k, v):
    ns = SEQ // SPAN
    nslab = KV_W // LANES
    rows = BATCH * SEQ

    def slab(n):
        return pl.BlockSpec((n, SPAN, LANES), lambda b, i: (0, b * ns + i, 0))

    return pl.pallas_call(
        _attn_prompt_kernel,
        grid=(BATCH, ns),
        in_specs=[slab(Q_W // LANES), slab(nslab), slab(nslab)],
        out_specs=pl.BlockSpec((SPAN, KV_W), lambda b, i: (b * ns + i, 0)),
        out_shape=jax.ShapeDtypeStruct((rows, KV_W), BF16),
        scratch_shapes=[
            pltpu.VMEM((nslab, BAND, LANES), F32),
            pltpu.VMEM((nslab, BAND, LANES), F32),
            pltpu.VMEM((nslab, 2 * SPAN, LANES), F32),
            pltpu.VMEM((nslab, 2 * SPAN, LANES), F32),
            pltpu.VMEM((nslab, SPAN, LANES), F32),
            pltpu.VMEM((nslab, SPAN, LANES), F32),
            pltpu.VMEM((nslab, SPAN, LANES), F32),
            pltpu.VMEM((nslab, SPAN, LANES), F32),
            pltpu.VMEM((3 * nslab, SPAN, LANES), F32),
            pltpu.VMEM((2 * N_DIL, KV_HEADS * BAND, 2 * BAND), F32),
        ],
        compiler_params=_params(("arbitrary", "arbitrary")),
        name="attn_prompt",
    )(q, k, v)


def _merge_ffn_kernel(x_ref, oa_ref, ob_ref, oc_ref, gate_ref, wa_ref, wb_ref, wc_ref, wo_ref, gqm_ref,
                      gpf_ref, wg_ref, wu_ref, wd_ref, gqf_ref, o_ref):
    tm = x_ref.shape[0]
    sub = min(tm, ROW_SUB)

    def sub_tile(r0):
        rows = slice(r0, r0 + sub)

        def gate(j):
            return gate_ref[rows, j * D_MODEL:(j + 1) * D_MODEL].astype(F32)

        merged = gate(0) * _dot(oa_ref[rows, :], wa_ref[...])
        yield
        merged = merged + gate(1) * _dot(ob_ref[rows, :].astype(BF16), wb_ref[...])
        yield
        merged = merged + gate(2) * _dot(oc_ref[rows, :], wc_ref[...])
        yield
        x = x_ref[rows, :] + _rms(_dot(merged.astype(BF16), wo_ref[...]), gqm_ref[...])
        yield
        h = _rms(x, gpf_ref[...]).astype(BF16)
        y = None
        for lo, hi in FF_CHUNKS:
            yield
            a = jax.nn.silu(_dot(h, wg_ref[:, lo:hi]))
            yield
            act = (a * _dot(h, wu_ref[:, lo:hi])).astype(BF16)
            yield
            part = _dot(act, wd_ref[lo:hi, :])
            y = part if y is None else y + part
        yield
        o_ref[rows, :] = x + _rms(y, gqf_ref[...])

    _trace_staggered([sub_tile(r0) for r0 in range(0, tm, sub)], MERGE_FFN_LAG)


def _merge_ffn(layer, x, oa, ob, oc, gates, vecs, wa, wb, wc, wo, wg, wu, wd, tm):
    rows = x.shape[0]

    def rowblk(width):
        return pl.BlockSpec((tm, width), lambda i: (i, 0))

    return pl.pallas_call(
        _merge_ffn_kernel,
        grid=(rows // tm,),
        in_specs=[
            rowblk(D_MODEL), rowblk(A_WIDTH), rowblk(KV_W), rowblk(C_WIDTH), rowblk(3 * D_MODEL),
            _layer_spec(layer, (A_WIDTH, D_MODEL)), _layer_spec(layer, (KV_W, D_MODEL)),
            _layer_spec(layer, (C_WIDTH, D_MODEL)), _layer_spec(layer, (D_MODEL, D_MODEL)),
            _vec_spec(layer, "gqm"),
            _vec_spec(layer, "gpf"), _layer_spec(layer, (D_MODEL, D_FF)),
            _layer_spec(layer, (D_MODEL, D_FF)), _layer_spec(layer, (D_FF, D_MODEL)),
            _vec_spec(layer, "gqf"),
        ],
        out_specs=rowblk(D_MODEL),
        out_shape=jax.ShapeDtypeStruct((rows, D_MODEL), F32),
        compiler_params=_params(("parallel",)),
        name="merge_ffn",
    )(x, oa, ob, oc, gates, wa, wb, wc, wo, vecs, vecs, wg, wu, wd, vecs)


def _inproj_sample_kernel(ws_ref, bs_ref, x_ref, gpre_ref, w_ref, lng_ref, lnb_ref, cw_ref, cbias_ref, bg_ref,
                          p1_ref, p2_ref,
                          oa_ref, q_ref, k_ref, v_ref, oc_ref, gate_ref, vn_ref, cz_ref, *, layer):
    n = SAMPLE_ROWS
    h = _rms(x_ref[...], gpre_ref[...]).astype(BF16)

    def proj(off, width):
        return _dot(h, w_ref[:, off:off + width])

    u = jax.nn.gelu(proj(OFF_U, A_WIDTH))
    vn = _layer_norm(jax.nn.gelu(proj(OFF_V, A_WIDTH)), lng_ref[...], lnb_ref[...])
    vn_ref[...] = vn
    t = lax.broadcasted_iota(jnp.int32, (n, 1), 0) & (DEC_SEQ - 1)
    for g in range(A_GROUPS):
        cs = slice(g * A_GROUP_DIM, (g + 1) * A_GROUP_DIM)
        base = (layer * A_GROUPS + g) * DEC_SEQ
        vg = vn[:, cs]
        sa = _select_rows(t, [bs_ref[base + tt] for tt in range(DEC_SEQ)])
        for k in range(DEC_SEQ):
            coef = jnp.zeros((n, 1), F32)
            for tt in range(k, DEC_SEQ):
                coef = jnp.where(t == tt, ws_ref[(base + tt) * DEC_SEQ + tt - k], coef)
            sa = sa + coef * (vg if k == 0 else pltpu.roll(vg, k, 0))
        oa_ref[:, cs] = (u[:, cs] * sa).astype(BF16)

    q_ref[...] = proj(OFF_Q, Q_W)
    kv = proj(OFF_K, 2 * KV_W)
    k_ref[...] = kv[:, 0:KV_W]
    v_ref[...] = kv[:, KV_W:2 * KV_W]

    cz = proj(OFF_CC, C_WIDTH) * proj(OFF_CX, C_WIDTH)
    cz_ref[...] = cz
    z1 = jnp.where(t >= 1, pltpu.roll(cz, 1, 0), p1_ref[...])
    z2 = jnp.where(t >= 2, pltpu.roll(cz, 2, 0), p2_ref[...])
    y = cbias_ref[...] + cw_ref[0:1, :] * z2 + cw_ref[1:2, :] * z1 + cw_ref[2:3, :] * cz
    oc_ref[...] = (proj(OFF_CB, C_WIDTH) * y).astype(BF16)

    for j in range(3):
        cs = slice(j * D_MODEL, (j + 1) * D_MODEL)
        gate_ref[:, cs] = jax.nn.sigmoid(proj(OFF_G + j * D_MODEL, D_MODEL) + bg_ref[:, cs]).astype(BF16)


def _inproj_sample(layer, ws4, bs4, x, vecs, w_in, cw, p1, p2):
    n = SAMPLE_ROWS
    smem = pl.BlockSpec(memory_space=pltpu.SMEM)

    def full(shape):
        nd = len(shape)
        return pl.BlockSpec(shape, lambda i: (0,) * nd)

    widths_dtypes = [(A_WIDTH, BF16), (Q_W, F32), (KV_W, F32), (KV_W, F32), (C_WIDTH, BF16),
                     (3 * D_MODEL, BF16), (A_WIDTH, F32), (C_WIDTH, F32)]
    return pl.pallas_call(
        functools.partial(_inproj_sample_kernel, layer=layer),
        grid=(1,),
        in_specs=[smem, smem, full((n, D_MODEL)), _vec_spec(layer, "gpm"),
                  _layer_spec(layer, (D_MODEL, IN_W)), _vec_spec(layer, "lng"),
                  _vec_spec(layer, "lnb"), _layer_spec(layer, (CONV_W, C_WIDTH)),
                  _vec_spec(layer, "cbias"), _vec_spec(layer, "bg"),
                  _layer_spec(layer, (n, C_WIDTH)), _layer_spec(layer, (n, C_WIDTH))],
        out_specs=[full((n, w)) for w, _ in widths_dtypes],
        out_shape=[jax.ShapeDtypeStruct((n, w), dt) for w, dt in widths_dtypes],
        compiler_params=_params(("arbitrary",)),
        name="inproj_sample",
    )(ws4, bs4, x, vecs, w_in, vecs, vecs, cw, vecs, vecs, p1, p2)


def _attn_sample_kernel(qh_ref, kn_ref, vn_ref, kt_ref, vt_ref, o_ref, bias_ref):
    nr = N_DIL * SUBLANES
    width = MAX_WINDOW

    @pl.when(pl.program_id(0) == 0)
    def _():
        row = lax.broadcasted_iota(jnp.int32, (nr, width), 0)
        pos = lax.broadcasted_iota(jnp.int32, (nr, width), 1)
        grp = row >> 3
        delta = width + (row & (SUBLANES - 1)) - pos
        dil_m1 = _select_rows(grp, [d - 1 for d in DILS])
        reach = _select_rows(grp, [BAND * d for d in DILS])
        ok = jnp.logical_and((delta & dil_m1) == 0, delta <= reach)
        for h in range(KV_HEADS):
            slope = _select_rows(grp, [jnp.float32(SLOPES[g][h]) for g in range(N_DIL)])
            bias_ref[h * nr:(h + 1) * nr, :] = jnp.where(ok, -(slope * delta.astype(F32)), NEG)

    ridx = lax.broadcasted_iota(jnp.int32, (nr, 1), 0)
    g_of = ridx >> 3
    t_of = ridx & (SUBLANES - 1)
    scale = HEAD_DIM ** -0.5
    slopes = [_select_rows(g_of, [jnp.float32(SLOPES[g][h]) for g in range(N_DIL)]) for h in range(KV_HEADS)]
    for e in range(qh_ref.shape[0]):
        qs = [qh_ref[e, h] * scale for h in range(KV_HEADS)]
        s = jnp.concatenate([_dot(qs[h].astype(BF16), kt_ref[0, e, h].astype(BF16)) for h in range(KV_HEADS)],
                            axis=0)
        s = s + bias_ref[...]
        m = jnp.max(s, axis=-1, keepdims=True)
        s_new = []
        for n in range(DEC_SEQ):
            dn = t_of - n
            ok = jnp.logical_or(dn == 0, jnp.logical_and(g_of == 0, dn > 0))
            rows = []
            for h in range(KV_HEADS):
                raw = jnp.sum(qs[h] * kn_ref[e, h, n:n + 1, :], axis=-1, keepdims=True)
                rows.append(jnp.where(ok, raw - slopes[h] * dn.astype(F32), NEG))
            sn = jnp.concatenate(rows, axis=0)
            s_new.append(sn)
            m = jnp.maximum(m, sn)
        p = jnp.exp(s - m)
        l_all = jnp.sum(p, axis=-1, keepdims=True)
        p16 = p.astype(BF16)
        p_new = [jnp.exp(sn - m) for sn in s_new]
        for pn in p_new:
            l_all = l_all + pn
        for h in range(KV_HEADS):
            hr = slice(h * nr, (h + 1) * nr)
            r = _dot_nt(p16[hr], vt_ref[0, e, h].astype(BF16))
            for n in range(DEC_SEQ):
                r = r + p_new[n][hr] * vn_ref[e, h, n:n + 1, :]
            m_h = m[hr]
            l_h = l_all[hr]
            parts = [(r[g * SUBLANES:(g + 1) * SUBLANES], m_h[g * SUBLANES:(g + 1) * SUBLANES],
                      l_h[g * SUBLANES:(g + 1) * SUBLANES]) for g in range(N_DIL)]
            m_all = jnp.maximum(jnp.maximum(parts[0][1], parts[1][1]), parts[2][1])
            num = jnp.zeros((SUBLANES, HEAD_DIM), F32)
            den = jnp.zeros((SUBLANES, 1), F32)
            for o_g, m_g, l_g in parts:
                w = jnp.exp(m_g - m_all)
                num = num + o_g * w
                den = den + l_g * w
            o_ref[e, :, h * HEAD_DIM:(h + 1) * HEAD_DIM] = (num / den)[0:DEC_SEQ]


def _attn_sample(layer, qh, knh, vnh, ckt, cvt):
    nr = N_DIL * SUBLANES
    eb = SAMPLE_EB
    cache_spec = pl.BlockSpec((1, eb, KV_HEADS, HEAD_DIM, MAX_WINDOW), lambda b: (layer, b, 0, 0, 0))
    new_spec = pl.BlockSpec((eb, KV_HEADS, SUBLANES, HEAD_DIM), lambda b: (b, 0, 0, 0))
    return pl.pallas_call(
        _attn_sample_kernel,
        grid=(DEC_BATCH // eb,),
        in_specs=[pl.BlockSpec((eb, KV_HEADS, nr, HEAD_DIM), lambda b: (b, 0, 0, 0)),
                  new_spec, new_spec, cache_spec, cache_spec],
        out_specs=pl.BlockSpec((eb, DEC_SEQ, KV_W), lambda b: (b, 0, 0)),
        out_shape=jax.ShapeDtypeStruct((DEC_BATCH, DEC_SEQ, KV_W), F32),
        scratch_shapes=[pltpu.VMEM((KV_HEADS * nr, MAX_WINDOW), F32)],
        compiler_params=_params(("arbitrary",)),
        name="attn_sample",
    )(qh, knh, vnh, ckt, cvt)


def kernel(x_prompt, x_sample, cache_k_win, cache_v_win, state_conv, g_pre_mix, g_post_mix, g_pre_ffn, g_post_ffn,
           w_in, a_ln_g, a_ln_b, a_ws, a_bs, c_conv_w, c_conv_b, w_br_a, w_br_b, w_br_c, b_gate, w_o,
           w_ff_gate, w_ff_up, w_ff_down):
    xp = x_prompt.reshape(BATCH * SEQ, D_MODEL)
    xs = x_sample.reshape(SAMPLE_ROWS, D_MODEL)
    ckt = jnp.transpose(cache_k_win, (0, 1, 3, 4, 2))
    cvt = jnp.transpose(cache_v_win, (0, 1, 3, 4, 2))

    w_in_b = w_in.astype(BF16)
    wkt = jnp.swapaxes(w_in[:, :, OFF_K:OFF_K + KV_W], 1, 2).astype(BF16)
    wvt = jnp.swapaxes(w_in[:, :, OFF_VV:OFF_VV + KV_W], 1, 2).astype(BF16)
    wa, wb, wc, wo = (w.astype(BF16) for w in (w_br_a, w_br_b, w_br_c, w_o))
    wg, wu, wd = (w.astype(BF16) for w in (w_ff_gate, w_ff_up, w_ff_down))
    vecs = jnp.concatenate([b_gate, g_pre_mix, g_post_mix, g_pre_ffn, g_post_ffn, a_ln_g, a_ln_b, c_conv_b],
                           axis=1).reshape(DEPTH, 1, -1)
    bs_t = jnp.swapaxes(a_bs, 1, 2)
    ws4 = a_ws[:, :, :DEC_SEQ, :DEC_SEQ].reshape(-1)
    bs4 = a_bs[:, :, :DEC_SEQ].reshape(-1)
    zeros = jnp.zeros((DEPTH, DEC_BATCH, 1, C_WIDTH), F32)
    p1 = jnp.concatenate([state_conv[:, :, 1:2], zeros, zeros, zeros], axis=2).reshape(DEPTH, SAMPLE_ROWS, C_WIDTH)
    p2 = jnp.concatenate([state_conv, zeros, zeros], axis=2).reshape(DEPTH, SAMPLE_ROWS, C_WIDTH)

    kp_l, vp_l, ks_l, vs_l, cp_l, cs_l, av_l = [], [], [], [], [], [], []
    for l in range(DEPTH):
        oa, q, k, v, kt, vt, oc, gates, cs8 = _inproj_prompt(
            l, xp, vecs, w_in_b, wkt, wvt, a_ws, bs_t, c_conv_w)
        ob = _attn_prompt(q, k, v)
        xp = _merge_ffn(l, xp, oa, ob, oc, gates, vecs, wa, wb, wc, wo, wg, wu, wd, PROMPT_TM)

        def window(t):
            return jnp.transpose(t.reshape(BATCH, KV_HEADS, HEAD_DIM, WIN), (0, 3, 1, 2))

        kp_l.append(window(kt))
        vp_l.append(window(vt))
        cp_l.append(cs8[:, SUBLANES - (CONV_W - 1):])

        oa_s, q_s, k_s, v_s, oc_s, gates_s, vn_s, cz_s = _inproj_sample(
            l, ws4, bs4, xs, vecs, w_in_b, c_conv_w, p1, p2)

        pad_t = ((0, 0), (0, 0), (0, 0), (0, SUBLANES - DEC_SEQ), (0, 0))
        qh = q_s.reshape(DEC_BATCH, DEC_SEQ, N_DIL, KV_HEADS, HEAD_DIM).transpose(0, 3, 2, 1, 4)
        qh = jnp.pad(qh, pad_t).reshape(DEC_BATCH, KV_HEADS, N_DIL * SUBLANES, HEAD_DIM)

        def new_rows(a):
            a = a.reshape(DEC_BATCH, DEC_SEQ, KV_HEADS, HEAD_DIM).transpose(0, 2, 1, 3)
            return jnp.pad(a, pad_t[1:])

        ob_s = _attn_sample(l, qh, new_rows(k_s), new_rows(v_s), ckt, cvt)
        ob_s = ob_s.reshape(SAMPLE_ROWS, KV_W)
        xs = _merge_ffn(l, xs, oa_s, ob_s, oc_s, gates_s, vecs, wa, wb, wc, wo, wg, wu, wd, SAMPLE_ROWS)

        ks_l.append(k_s.reshape(DEC_BATCH, DEC_SEQ, KV_HEADS, HEAD_DIM))
        vs_l.append(v_s.reshape(DEC_BATCH, DEC_SEQ, KV_HEADS, HEAD_DIM))
        cs_l.append(cz_s.reshape(DEC_BATCH, DEC_SEQ, C_WIDTH)[:, DEC_SEQ - (CONV_W - 1):])
        av_l.append(vn_s.reshape(DEC_BATCH, DEC_SEQ, A_WIDTH))

    return (xp.reshape(BATCH, SEQ, D_MODEL), xs.reshape(DEC_BATCH, DEC_SEQ, D_MODEL),
            jnp.stack(kp_l), jnp.stack(vp_l), jnp.stack(ks_l), jnp.stack(vs_l),
            jnp.stack(cp_l), jnp.stack(cs_l), jnp.stack(av_l))
```

```python
import functools

import jax
import jax.numpy as jnp
from jax import lax
from jax.experimental import pallas as pl
from jax.experimental.pallas import tpu as pltpu

F32 = jnp.float32
BF16 = jnp.bfloat16

D_MODEL = 1024
BATCH = 2
SEQ = 8192
DEPTH = 2
DEC_BATCH = 32
DEC_SEQ = 4
CHUNK = 128
A_GROUPS = 4
A_GROUP_DIM = 128
A_WIDTH = A_GROUPS * A_GROUP_DIM
HEAD_DIM = 64
KV_HEADS = 4
DILS = (1, 4, 16)
N_DIL = len(DILS)
Q_W = N_DIL * KV_HEADS * HEAD_DIM
KV_W = KV_HEADS * HEAD_DIM
BAND = 128
MAX_WINDOW = 2048
C_WIDTH = 512
CONV_W = 3
D_FF = 2816
EPS = 1e-6

OFF_U = 0
OFF_V = OFF_U + A_WIDTH
OFF_Q = OFF_V + A_WIDTH
OFF_K = OFF_Q + Q_W
OFF_VV = OFF_K + KV_W
OFF_CX = OFF_VV + KV_W
OFF_CB = OFF_CX + C_WIDTH
OFF_CC = OFF_CB + C_WIDTH
OFF_G = OFF_CC + C_WIDTH
IN_W = OFF_G + 3 * D_MODEL

LANES = 128
SUBLANES = 8
VMEM_LIMIT = 56 * 1024 * 1024
NEG = -1e30

SAMPLE_ROWS = DEC_BATCH * DEC_SEQ
WIN = min(MAX_WINDOW, SEQ)
SPAN = BAND * DILS[-1]
RES = DILS[1]
PROMPT_TM = 512
ROW_SUB = 256
INPROJ_SUB = ROW_SUB
INPROJ_LAG = 2
MERGE_FFN_LAG = 3
WCHUNK = 128
SAMPLE_EB = 4
MXU_N = 256
FF_CHUNKS = ((0, 6 * MXU_N), (6 * MXU_N, D_FF))
LOG2E = 1.4426950408889634
QSCALE = LOG2E * HEAD_DIM ** -0.5

SLOPES = tuple(
    tuple(2.0 ** (-8.0 * (g * KV_HEADS + h + 1) / (N_DIL * KV_HEADS)) for h in range(KV_HEADS))
    for g in range(N_DIL)
)


def _rms(x, g):
    return x * lax.rsqrt(jnp.mean(x * x, axis=-1, keepdims=True) + EPS) * g


def _layer_norm(x, g, b):
    mu = jnp.mean(x, axis=-1, keepdims=True)
    xc = x - mu
    return xc * lax.rsqrt(jnp.mean(xc * xc, axis=-1, keepdims=True) + EPS) * g + b


def _dot(a, b):
    return jnp.dot(a, b, preferred_element_type=F32)


def _dot_nt(a, b):
    return lax.dot_general(a, b, (((1,), (1,)), ((), ())), preferred_element_type=F32)


def _select_rows(hid, vals):
    out = vals[-1]
    for h in range(len(vals) - 2, -1, -1):
        out = jnp.where(hid == h, vals[h], out)
    return out


def _layer_spec(layer, shape):
    nd = len(shape)
    return pl.BlockSpec((None,) + tuple(shape), lambda *_: (layer,) + (0,) * nd, pipeline_mode=pl.Buffered(1))


VEC_SLOTS = {"bg": (3 * D_MODEL, 0), "gpm": (D_MODEL, 3), "gqm": (D_MODEL, 4), "gpf": (D_MODEL, 5),
             "gqf": (D_MODEL, 6), "lng": (A_WIDTH, 14), "lnb": (A_WIDTH, 15), "cbias": (C_WIDTH, 16)}


def _vec_spec(layer, name):
    width, idx = VEC_SLOTS[name]
    return pl.BlockSpec((None, 1, width), lambda *_: (layer, 0, idx), pipeline_mode=pl.Buffered(1))


def _trace_staggered(stage_gens, lag):
    live = [True] * len(stage_gens)
    tick = 0
    while any(live):
        for j, gen in enumerate(stage_gens):
            if live[j] and tick >= j * lag:
                live[j] = next(gen, "done") != "done"
        tick += 1


def _params(sem):
    return pltpu.CompilerParams(dimension_semantics=sem, vmem_limit_bytes=VMEM_LIMIT)


def _inproj_prompt_kernel(x_ref, gpre_ref, w_ref, wkt_ref, wvt_ref, lng_ref, lnb_ref, ws_ref, bs_ref, cw_ref,
                          cbias_ref, bg_ref,
                          oa_ref, q_ref, k_ref, v_ref, kt_ref, vt_ref, oc_ref, gate_ref, cs_ref, prev_ref):
    tm = x_ref.shape[0]
    i = pl.program_id(1)

    @pl.when(i == 0)
    def _():
        prev_ref[...] = jnp.zeros_like(prev_ref)

    row = lax.broadcasted_iota(jnp.int32, (CHUNK, CHUNK), 0)
    col = lax.broadcasted_iota(jnp.int32, (CHUNK, CHUNK), 1)
    wt = [jnp.where(row >= col, ws_ref[g], 0.0).astype(BF16) for g in range(A_GROUPS)]
    rsub = lax.broadcasted_iota(jnp.int32, (INPROJ_SUB, 1), 0)

    carry = {"tail": prev_ref[...], "h": []}

    def sub_tile(r0):
        rows = slice(r0, r0 + INPROJ_SUB)
        h = _rms(x_ref[rows, :], gpre_ref[...]).astype(BF16)
        carry["h"].append(h)

        def proj(off, width):
            return _dot(h, w_ref[:, off:off + width])

        yield
        u = jax.nn.gelu(proj(OFF_U, A_WIDTH))
        yield
        vn = _layer_norm(jax.nn.gelu(proj(OFF_V, A_WIDTH)), lng_ref[...], lnb_ref[...])
        for j in range(3):
            yield
            cs = slice(j * D_MODEL, (j + 1) * D_MODEL)
            gate_ref[rows, cs] = jax.nn.sigmoid(proj(OFF_G + j * D_MODEL, D_MODEL) + bg_ref[:, cs]).astype(BF16)
        yield
        for g in range(A_GROUPS):
            cs = slice(g * A_GROUP_DIM, (g + 1) * A_GROUP_DIM)
            for c in range(0, INPROJ_SUB // CHUNK, 2):
                pair = jnp.concatenate([vn[(c + j) * CHUNK:(c + j + 1) * CHUNK, cs] for j in range(2)], axis=1)
                sa2 = _dot(wt[g], pair.astype(BF16)) + bs_ref[:, g:g + 1]
                for j in range(2):
                    rs = slice((c + j) * CHUNK, (c + j + 1) * CHUNK)
                    sa = sa2[:, j * A_GROUP_DIM:(j + 1) * A_GROUP_DIM]
                    oa_ref[r0 + (c + j) * CHUNK:r0 + (c + j + 1) * CHUNK, cs] = (u[rs, cs] * sa).astype(BF16)
        yield
        cz = proj(OFF_CC, C_WIDTH) * proj(OFF_CX, C_WIDTH)
        p2 = carry["tail"][SUBLANES - 2:SUBLANES - 1, :]
        p1 = carry["tail"][SUBLANES - 1:SUBLANES, :]
        carry["tail"] = cz[INPROJ_SUB - SUBLANES:INPROJ_SUB, :]
        z1 = jnp.where(rsub == 0, p1, pltpu.roll(cz, 1, 0))
        z2 = jnp.where(rsub == 0, p2, jnp.where(rsub == 1, p1, pltpu.roll(cz, 2, 0)))
        y = cbias_ref[...] + cw_ref[0:1, :] * z2 + cw_ref[1:2, :] * z1 + cw_ref[2:3, :] * cz
        yield
        oc_ref[rows, :] = (proj(OFF_CB, C_WIDTH) * y).astype(BF16)
        yield
        qv = proj(OFF_Q, Q_W)
        for c in range(Q_W // LANES):
            q_ref[c, rows, :] = qv[:, c * LANES:(c + 1) * LANES] * QSCALE
        yield
        kv = proj(OFF_K, 2 * KV_W)
        for c in range(KV_W // LANES):
            k_ref[c, rows, :] = kv[:, c * LANES:(c + 1) * LANES]
            v_ref[c, rows, :] = kv[:, KV_W + c * LANES:KV_W + (c + 1) * LANES]

    _trace_staggered([sub_tile(r0) for r0 in range(0, tm, INPROJ_SUB)], INPROJ_LAG)
    prev_ref[...] = carry["tail"]
    cs_ref[0] = carry["tail"]
    hs = carry["h"]

    @pl.when(i >= pl.num_programs(1) - WIN // tm)
    def _():
        for j, h in enumerate(hs):
            cols = slice(j * INPROJ_SUB, (j + 1) * INPROJ_SUB)
            kt_ref[0, :, cols] = _dot_nt(wkt_ref[...], h)
            vt_ref[0, :, cols] = _dot_nt(wvt_ref[...], h)


def _inproj_prompt(layer, x, vecs, w_in, wkt, wvt, ws, bs_t, cw):
    tm = PROMPT_TM
    nt = SEQ // tm
    rows = BATCH * SEQ
    win_spec = pl.BlockSpec((1, KV_W, tm), lambda b, i: (b, 0, jnp.maximum(i - (nt - WIN // tm), 0)))

    def rowblk(width):
        return pl.BlockSpec((tm, width), lambda b, i: (b * nt + i, 0))

    def slab(n):
        return pl.BlockSpec((n, tm, LANES), lambda b, i: (0, b * nt + i, 0))

    return pl.pallas_call(
        _inproj_prompt_kernel,
        grid=(BATCH, nt),
        in_specs=[
            rowblk(D_MODEL),
            _vec_spec(layer, "gpm"),
            _layer_spec(layer, (D_MODEL, IN_W)),
            _layer_spec(layer, (KV_W, D_MODEL)),
            _layer_spec(layer, (KV_W, D_MODEL)),
            _vec_spec(layer, "lng"),
            _vec_spec(layer, "lnb"),
            _layer_spec(layer, (A_GROUPS, CHUNK, CHUNK)),
            _layer_spec(layer, (CHUNK, A_GROUPS)),
            _layer_spec(layer, (CONV_W, C_WIDTH)),
            _vec_spec(layer, "cbias"),
            _vec_spec(layer, "bg"),
        ],
        out_specs=[
            rowblk(A_WIDTH),
            slab(Q_W // LANES),
            slab(KV_W // LANES),
            slab(KV_W // LANES),
            win_spec,
            win_spec,
            rowblk(C_WIDTH),
            rowblk(3 * D_MODEL),
            pl.BlockSpec((1, SUBLANES, C_WIDTH), lambda b, i: (b, 0, 0)),
        ],
        out_shape=[
            jax.ShapeDtypeStruct((rows, A_WIDTH), BF16),
            jax.ShapeDtypeStruct((Q_W // LANES, rows, LANES), F32),
            jax.ShapeDtypeStruct((KV_W // LANES, rows, LANES), F32),
            jax.ShapeDtypeStruct((KV_W // LANES, rows, LANES), F32),
            jax.ShapeDtypeStruct((BATCH, KV_W, WIN), F32),
            jax.ShapeDtypeStruct((BATCH, KV_W, WIN), F32),
            jax.ShapeDtypeStruct((rows, C_WIDTH), BF16),
            jax.ShapeDtypeStruct((rows, 3 * D_MODEL), BF16),
            jax.ShapeDtypeStruct((BATCH, SUBLANES, C_WIDTH), F32),
        ],
        scratch_shapes=[pltpu.VMEM((SUBLANES, C_WIDTH), F32)],
        compiler_params=_params(("arbitrary", "arbitrary")),
        name="inproj_prompt",
    )(x, vecs, w_in, wkt, wvt, vecs, vecs, ws, bs_t, cw, vecs, vecs)


def _attn_prompt_kernel(q_ref, k_ref, v_ref, o_ref, ktail, vtail, k4, v4, q24, acc, mrun, lrun, tmp, bias_ref):
    b = pl.program_id(0)
    i = pl.program_id(1)
    first = i == 0
    nslab = KV_W // LANES
    quarter = SPAN // RES

    @pl.when(jnp.logical_and(b == 0, first))
    def _():
        rr = lax.broadcasted_iota(jnp.int32, (KV_HEADS * BAND, 2 * BAND), 0)
        kk = lax.broadcasted_iota(jnp.int32, (KV_HEADS * BAND, 2 * BAND), 1)
        dist = (rr & (BAND - 1)) + BAND - kk
        hrow = rr >> 7
        ok = jnp.logical_and(dist >= 0, dist <= BAND)
        for g, dil in enumerate(DILS):
            coef = _select_rows(hrow, [jnp.float32(SLOPES[g][h] * dil * LOG2E) for h in range(KV_HEADS)])
            full = jnp.where(ok, -(coef * dist.astype(F32)), NEG)
            bias_ref[2 * g] = full
            bias_ref[2 * g + 1] = jnp.where(kk >= BAND, full, NEG)

    @pl.when(first)
    def _():
        for r in range(RES):
            base = r * 2 * quarter
            k4[:, base:base + quarter, :] = jnp.zeros((nslab, quarter, LANES), F32)
            v4[:, base:base + quarter, :] = jnp.zeros((nslab, quarter, LANES), F32)
        ktail[...] = jnp.zeros_like(ktail)
        vtail[...] = jnp.zeros_like(vtail)

    @pl.when(i > 0)
    def _():
        for r in range(RES):
            base = r * 2 * quarter
            k4[:, base:base + quarter, :] = k4[:, base + quarter:base + 2 * quarter, :]
            v4[:, base:base + quarter, :] = v4[:, base + quarter:base + 2 * quarter, :]

    for r in range(RES):
        base = r * 2 * quarter + quarter
        for c in range(nslab):
            k4[c, base:base + quarter, :] = k_ref[c, pl.ds(r, quarter, stride=RES), :]
            v4[c, base:base + quarter, :] = v_ref[c, pl.ds(r, quarter, stride=RES), :]
            q24[c, r * quarter:(r + 1) * quarter, :] = q_ref[2 * nslab + c, pl.ds(r, quarter, stride=RES), :]

    lane = lax.broadcasted_iota(jnp.int32, (1, KV_W), 1)
    hid = lane >> 6
    headmask = [jnp.where(hid == h, 1.0, 0.0).astype(BF16) for h in range(KV_HEADS)]

    def wide(ref, rows, slab0=0):
        return jnp.concatenate([ref[slab0 + c, rows, :] for c in range(nslab)], axis=1)

    def softmax_unit(qb, kb, vb, bias):
        qb16 = qb.astype(BF16)
        kb16 = kb.astype(BF16)
        vb16 = vb.astype(BF16)
        qs = jnp.concatenate([qb16 * headmask[h] for h in range(KV_HEADS)], axis=0)
        s = _dot_nt(qs, kb16)
        ps, ms, ls = [], [], []
        for h in range(KV_HEADS):
            sh = s[h * BAND:(h + 1) * BAND] + bias_ref[bias, h * BAND:(h + 1) * BAND, :]
            mh = jnp.max(sh, axis=-1, keepdims=True)
            ph = jnp.exp2(sh - mh)
            ps.append(ph.astype(BF16))
            ms.append(mh)
            ls.append(jnp.sum(ph, axis=-1, keepdims=True))
        r = _dot(jnp.concatenate(ps, axis=0), vb16)
        rs = [r[h * BAND:(h + 1) * BAND] for h in range(KV_HEADS)]
        return _select_rows(hid, rs), _select_rows(hid, ms), _select_rows(hid, ls)

    def store_stats(rows, o, m, l):
        for c in range(nslab):
            ls = slice(c * LANES, (c + 1) * LANES)
            acc[c, rows, :] = o[:, ls]
            mrun[c, rows, :] = m[:, ls]
            lrun[c, rows, :] = l[:, ls]

    def merge_stats(rows, o_u, m_u, l_u):
        m_old = wide(mrun, rows)
        m_n = jnp.maximum(m_old, m_u)
        a_old = jnp.exp2(m_old - m_n)
        a_new = jnp.exp2(m_u - m_n)
        store_stats(rows, wide(acc, rows) * a_old + o_u * a_new, m_n, wide(lrun, rows) * a_old + l_u * a_new)

    def unit0(n, kb, vb, bias):
        q0 = n * BAND if isinstance(n, int) else pl.multiple_of(n * BAND, BAND)
        stats = softmax_unit(wide(q_ref, pl.ds(q0, BAND)), kb, vb, bias)
        sub = BAND // RES
        for j, x in enumerate(stats):
            for c in range(nslab):
                tmp[j * nslab + c, pl.ds(q0, BAND), :] = x[:, c * LANES:(c + 1) * LANES]
        for r in range(RES):
            src = pl.ds(q0 + r, sub, stride=RES)
            dst = pl.ds(r * quarter + n * sub, sub)
            for c in range(nslab):
                acc[c, dst, :] = tmp[c, src, :]
                mrun[c, dst, :] = tmp[nslab + c, src, :]
                lrun[c, dst, :] = tmp[2 * nslab + c, src, :]

    kb0 = jnp.concatenate([wide(ktail, pl.ds(0, BAND)), wide(k_ref, pl.ds(0, BAND))], axis=0)
    vb0 = jnp.concatenate([wide(vtail, pl.ds(0, BAND)), wide(v_ref, pl.ds(0, BAND))], axis=0)
    unit0(0, kb0, vb0, jnp.where(first, 1, 0))

    def body0(n, carry):
        keys = pl.ds(pl.multiple_of((n - 1) * BAND, BAND), 2 * BAND)
        unit0(n, wide(k_ref, keys), wide(v_ref, keys), 0)
        return carry

    lax.fori_loop(1, SPAN // BAND, body0, 0, unroll=5)

    def body1(s, carry):
        bias = 2 + jnp.where(jnp.logical_and(first, s == 0), 1, 0)
        for r in range(RES):
            qb = wide(q_ref, pl.ds(s * (BAND * RES) + r, BAND, stride=RES), nslab)
            keys = pl.ds(r * 2 * quarter + quarter + (s - 1) * BAND, 2 * BAND)
            o_u, m_u, l_u = softmax_unit(qb, wide(k4, keys), wide(v4, keys), bias)
            merge_stats(pl.ds(r * quarter + s * BAND, BAND), o_u, m_u, l_u)
        return carry

    lax.fori_loop(0, quarter // BAND, body1, 0, unroll=2)

    def body2(r, carry):
        bias = 4 + jnp.where(first, 1, 0)
        for a in range(DILS[2] // RES):
            rows = pl.ds(r * quarter + a, BAND, stride=RES)
            keys = pl.ds(r * 2 * quarter + a, 2 * BAND, stride=RES)
            o_u, m_u, l_u = softmax_unit(wide(q24, rows), wide(k4, keys), wide(v4, keys), bias)
            merge_stats(rows, o_u, m_u, l_u)
        return carry

    lax.fori_loop(0, RES, body2, 0, unroll=2)

    for r in range(RES):
        rows = pl.ds(r * quarter, quarter)
        for c in range(nslab):
            tmp[c, pl.ds(r, quarter, stride=RES), :] = acc[c, rows, :] / lrun[c, rows, :]
    for c in range(nslab):
        o_ref[:, c * LANES:(c + 1) * LANES] = tmp[c].astype(BF16)
    ktail[...] = k_ref[:, SPAN - BAND:SPAN, :]
    vtail[...] = v_ref[:, SPAN - BAND:SPAN, :]


def _attn_prompt(q, k, v):
    ns = SEQ // SPAN
    nslab = KV_W // LANES
    rows = BATCH * SEQ

    def slab(n):
        return pl.BlockSpec((n, SPAN, LANES), lambda b, i: (0, b * ns + i, 0))

    return pl.pallas_call(
        _attn_prompt_kernel,
        grid=(BATCH, ns),
        in_specs=[slab(Q_W // LANES), slab(nslab), slab(nslab)],
        out_specs=pl.BlockSpec((SPAN, KV_W), lambda b, i: (b * ns + i, 0)),
        out_shape=jax.ShapeDtypeStruct((rows, KV_W), BF16),
        scratch_shapes=[
            pltpu.VMEM((nslab, BAND, LANES), F32),
            pltpu.VMEM((nslab, BAND, LANES), F32),
            pltpu.VMEM((nslab, 2 * SPAN, LANES), F32),
            pltpu.VMEM((nslab, 2 * SPAN, LANES), F32),
            pltpu.VMEM((nslab, SPAN, LANES), F32),
            pltpu.VMEM((nslab, SPAN, LANES), F32),
            pltpu.VMEM((nslab, SPAN, LANES), F32),
            pltpu.VMEM((nslab, SPAN, LANES), F32),
            pltpu.VMEM((3 * nslab, SPAN, LANES), F32),
            pltpu.VMEM((2 * N_DIL, KV_HEADS * BAND, 2 * BAND), F32),
        ],
        compiler_params=_params(("arbitrary", "arbitrary")),
        name="attn_prompt",
    )(q, k, v)


def _stream_to_bf16(chunks):
    copies = [pltpu.make_async_copy(src, stage.at[slot], sems.at[slot]) for src, stage, sems, slot, _, _ in chunks]
    copies[0].start()
    for n, (_, stage, _, slot, dst, idx) in enumerate(chunks):
        if n + 1 < len(copies):
            copies[n + 1].start()
        copies[n].wait()
        dst[idx] = stage[slot].astype(BF16)


def _weight_chunks(layer, w_hbm, w_scr, axis, stage, sems, parity):
    out = []
    for c0 in range(0, w_scr.shape[axis], WCHUNK):
        piece = pl.ds(c0, WCHUNK)
        idx = (piece, slice(None)) if axis == 0 else (slice(None), piece)
        out.append((w_hbm.at[(layer,) + idx], stage, sems, parity[0], w_scr, idx))
        parity[0] = 1 - parity[0]
    return out


def _merge_ffn_kernel(xs_ref, oas_ref, obs_ref, ocs_ref, gates_ref, x_ref, oa_ref, ob_ref, oc_ref, gate_ref,
                      gqm_ref, gpf_ref, gqf_ref, wa_hbm, wb_hbm, wc_hbm, wo_hbm, wg_hbm, wu_hbm, wd_hbm,
                      os_ref, o_ref,
                      wa_ref, wb_ref, wc_ref, wo_ref, wg_ref, wu_ref, wd_ref, stage_r, stage_c, sems, *, layer):
    def run(x_ref, oa_ref, ob_ref, oc_ref, gate_ref, o_ref):
        tm = x_ref.shape[0]
        sub = min(tm, ROW_SUB)

        def sub_tile(r0):
            rows = slice(r0, r0 + sub)

            def gate(j):
                return gate_ref[rows, j * D_MODEL:(j + 1) * D_MODEL].astype(F32)

            merged = gate(0) * _dot(oa_ref[rows, :], wa_ref[...])
            yield
            merged = merged + gate(1) * _dot(ob_ref[rows, :].astype(BF16), wb_ref[...])
            yield
            merged = merged + gate(2) * _dot(oc_ref[rows, :], wc_ref[...])
            yield
            x = x_ref[rows, :] + _rms(_dot(merged.astype(BF16), wo_ref[...]), gqm_ref[...])
            yield
            h = _rms(x, gpf_ref[...]).astype(BF16)
            y = None
            for lo, hi in FF_CHUNKS:
                yield
                a = jax.nn.silu(_dot(h, wg_ref[:, lo:hi]))
                yield
                act = (a * _dot(h, wu_ref[:, lo:hi])).astype(BF16)
                yield
                part = _dot(act, wd_ref[lo:hi, :])
                y = part if y is None else y + part
            yield
            o_ref[rows, :] = x + _rms(y, gqf_ref[...])

        _trace_staggered([sub_tile(r0) for r0 in range(0, tm, sub)], MERGE_FFN_LAG)

    step = pl.program_id(0)

    @pl.when(step == 0)
    def _():
        pr, pc = [0], [0]
        chunks = []
        for w_hbm, w_scr in ((wa_hbm, wa_ref), (wb_hbm, wb_ref), (wc_hbm, wc_ref), (wo_hbm, wo_ref)):
            chunks += _weight_chunks(layer, w_hbm, w_scr, 0, stage_r, sems.at[0], pr)
        for w_hbm, w_scr in ((wg_hbm, wg_ref), (wu_hbm, wu_ref)):
            chunks += _weight_chunks(layer, w_hbm, w_scr, 1, stage_c, sems.at[1], pc)
        chunks += _weight_chunks(layer, wd_hbm, wd_ref, 0, stage_r, sems.at[0], pr)
        _stream_to_bf16(chunks)
        run(xs_ref, oas_ref, obs_ref, ocs_ref, gates_ref, os_ref)

    @pl.when(step > 0)
    def _():
        run(x_ref, oa_ref, ob_ref, oc_ref, gate_ref, o_ref)


def _merge_ffn(layer, xs, oa_s, ob_s, oc_s, gates_s, x, oa, ob, oc, gates, vecs, wa, wb, wc, wo, wg, wu, wd):
    tm = PROMPT_TM
    rows = x.shape[0]
    ns = xs.shape[0]

    def rowblk(width):
        return pl.BlockSpec((tm, width), lambda i: (jnp.maximum(i - 1, 0), 0))

    def whole(width):
        return pl.BlockSpec((ns, width), lambda i: (0, 0), pipeline_mode=pl.Buffered(1))

    hbm = pl.BlockSpec(memory_space=pl.ANY)
    return pl.pallas_call(
        functools.partial(_merge_ffn_kernel, layer=layer),
        grid=(1 + rows // tm,),
        in_specs=[
            whole(D_MODEL), whole(A_WIDTH), whole(KV_W), whole(C_WIDTH), whole(3 * D_MODEL),
            rowblk(D_MODEL), rowblk(A_WIDTH), rowblk(KV_W), rowblk(C_WIDTH), rowblk(3 * D_MODEL),
            _vec_spec(layer, "gqm"), _vec_spec(layer, "gpf"), _vec_spec(layer, "gqf"),
            hbm, hbm, hbm, hbm, hbm, hbm, hbm,
        ],
        out_specs=[pl.BlockSpec((ns, D_MODEL), lambda i: (0, 0)), rowblk(D_MODEL)],
        out_shape=[jax.ShapeDtypeStruct((ns, D_MODEL), F32), jax.ShapeDtypeStruct((rows, D_MODEL), F32)],
        scratch_shapes=[
            pltpu.VMEM((A_WIDTH, D_MODEL), BF16), pltpu.VMEM((KV_W, D_MODEL), BF16),
            pltpu.VMEM((C_WIDTH, D_MODEL), BF16), pltpu.VMEM((D_MODEL, D_MODEL), BF16),
            pltpu.VMEM((D_MODEL, D_FF), BF16), pltpu.VMEM((D_MODEL, D_FF), BF16),
            pltpu.VMEM((D_FF, D_MODEL), BF16),
            pltpu.VMEM((2, WCHUNK, D_MODEL), F32), pltpu.VMEM((2, D_MODEL, WCHUNK), F32),
            pltpu.SemaphoreType.DMA((2, 2)),
        ],
        compiler_params=_params(("arbitrary",)),
        name="merge_ffn",
    )(xs, oa_s, ob_s, oc_s, gates_s, x, oa, ob, oc, gates, vecs, vecs, vecs, wa, wb, wc, wo, wg, wu, wd)


def _inproj_sample_kernel(ws_ref, bs_ref, x_ref, gpre_ref, w_ref, lng_ref, lnb_ref, cw_ref, cbias_ref, bg_ref,
                          p1_ref, p2_ref,
                          oa_ref, q_ref, k_ref, v_ref, oc_ref, gate_ref, vn_ref, cz_ref, *, layer):
    n = SAMPLE_ROWS
    h = _rms(x_ref[...], gpre_ref[...]).astype(BF16)

    def proj(off, width):
        return _dot(h, w_ref[:, off:off + width])

    u = jax.nn.gelu(proj(OFF_U, A_WIDTH))
    vn = _layer_norm(jax.nn.gelu(proj(OFF_V, A_WIDTH)), lng_ref[...], lnb_ref[...])
    vn_ref[...] = vn
    t = lax.broadcasted_iota(jnp.int32, (n, 1), 0) & (DEC_SEQ - 1)
    for g in range(A_GROUPS):
        cs = slice(g * A_GROUP_DIM, (g + 1) * A_GROUP_DIM)
        base = (layer * A_GROUPS + g) * DEC_SEQ
        vg = vn[:, cs]
        sa = _select_rows(t, [bs_ref[base + tt] for tt in range(DEC_SEQ)])
        for k in range(DEC_SEQ):
            coef = jnp.zeros((n, 1), F32)
            for tt in range(k, DEC_SEQ):
                coef = jnp.where(t == tt, ws_ref[(base + tt) * DEC_SEQ + tt - k], coef)
            sa = sa + coef * (vg if k == 0 else pltpu.roll(vg, k, 0))
        oa_ref[:, cs] = (u[:, cs] * sa).astype(BF16)

    q_ref[...] = proj(OFF_Q, Q_W)
    kv = proj(OFF_K, 2 * KV_W)
    k_ref[...] = kv[:, 0:KV_W]
    v_ref[...] = kv[:, KV_W:2 * KV_W]

    cz = proj(OFF_CC, C_WIDTH) * proj(OFF_CX, C_WIDTH)
    cz_ref[...] = cz
    z1 = jnp.where(t >= 1, pltpu.roll(cz, 1, 0), p1_ref[...])
    z2 = jnp.where(t >= 2, pltpu.roll(cz, 2, 0), p2_ref[...])
    y = cbias_ref[...] + cw_ref[0:1, :] * z2 + cw_ref[1:2, :] * z1 + cw_ref[2:3, :] * cz
    oc_ref[...] = (proj(OFF_CB, C_WIDTH) * y).astype(BF16)

    for j in range(3):
        cs = slice(j * D_MODEL, (j + 1) * D_MODEL)
        gate_ref[:, cs] = jax.nn.sigmoid(proj(OFF_G + j * D_MODEL, D_MODEL) + bg_ref[:, cs]).astype(BF16)


def _inproj_sample(layer, ws4, bs4, x, vecs, w_in, cw, p1, p2):
    n = SAMPLE_ROWS
    smem = pl.BlockSpec(memory_space=pltpu.SMEM)

    def full(shape):
        nd = len(shape)
        return pl.BlockSpec(shape, lambda i: (0,) * nd)

    widths_dtypes = [(A_WIDTH, BF16), (Q_W, F32), (KV_W, F32), (KV_W, F32), (C_WIDTH, BF16),
                     (3 * D_MODEL, BF16), (A_WIDTH, F32), (C_WIDTH, F32)]
    return pl.pallas_call(
        functools.partial(_inproj_sample_kernel, layer=layer),
        grid=(1,),
        in_specs=[smem, smem, full((n, D_MODEL)), _vec_spec(layer, "gpm"),
                  _layer_spec(layer, (D_MODEL, IN_W)), _vec_spec(layer, "lng"),
                  _vec_spec(layer, "lnb"), _layer_spec(layer, (CONV_W, C_WIDTH)),
                  _vec_spec(layer, "cbias"), _vec_spec(layer, "bg"),
                  _layer_spec(layer, (n, C_WIDTH)), _layer_spec(layer, (n, C_WIDTH))],
        out_specs=[full((n, w)) for w, _ in widths_dtypes],
        out_shape=[jax.ShapeDtypeStruct((n, w), dt) for w, dt in widths_dtypes],
        compiler_params=_params(("arbitrary",)),
        name="inproj_sample",
    )(ws4, bs4, x, vecs, w_in, vecs, vecs, cw, vecs, vecs, p1, p2)


def _attn_sample_kernel(qh_ref, kn_ref, vn_ref, kt_ref, vt_ref, o_ref, bias_ref):
    nr = N_DIL * SUBLANES
    width = MAX_WINDOW

    @pl.when(pl.program_id(0) == 0)
    def _():
        row = lax.broadcasted_iota(jnp.int32, (nr, width), 0)
        pos = lax.broadcasted_iota(jnp.int32, (nr, width), 1)
        grp = row >> 3
        delta = width + (row & (SUBLANES - 1)) - pos
        dil_m1 = _select_rows(grp, [d - 1 for d in DILS])
        reach = _select_rows(grp, [BAND * d for d in DILS])
        ok = jnp.logical_and((delta & dil_m1) == 0, delta <= reach)
        for h in range(KV_HEADS):
            slope = _select_rows(grp, [jnp.float32(SLOPES[g][h]) for g in range(N_DIL)])
            bias_ref[h * nr:(h + 1) * nr, :] = jnp.where(ok, -(slope * delta.astype(F32)), NEG)

    ridx = lax.broadcasted_iota(jnp.int32, (nr, 1), 0)
    g_of = ridx >> 3
    t_of = ridx & (SUBLANES - 1)
    scale = HEAD_DIM ** -0.5
    slopes = [_select_rows(g_of, [jnp.float32(SLOPES[g][h]) for g in range(N_DIL)]) for h in range(KV_HEADS)]
    for e in range(qh_ref.shape[0]):
        qs = [qh_ref[e, h] * scale for h in range(KV_HEADS)]
        s = jnp.concatenate([_dot(qs[h].astype(BF16), kt_ref[0, e, h].astype(BF16)) for h in range(KV_HEADS)],
                            axis=0)
        s = s + bias_ref[...]
        m = jnp.max(s, axis=-1, keepdims=True)
        s_new = []
        for n in range(DEC_SEQ):
            dn = t_of - n
            ok = jnp.logical_or(dn == 0, jnp.logical_and(g_of == 0, dn > 0))
            rows = []
            for h in range(KV_HEADS):
                raw = jnp.sum(qs[h] * kn_ref[e, h, n:n + 1, :], axis=-1, keepdims=True)
                rows.append(jnp.where(ok, raw - slopes[h] * dn.astype(F32), NEG))
            sn = jnp.concatenate(rows, axis=0)
            s_new.append(sn)
            m = jnp.maximum(m, sn)
        p = jnp.exp(s - m)
        l_all = jnp.sum(p, axis=-1, keepdims=True)
        p16 = p.astype(BF16)
        p_new = [jnp.exp(sn - m) for sn in s_new]
        for pn in p_new:
            l_all = l_all + pn
        for h in range(KV_HEADS):
            hr = slice(h * nr, (h + 1) * nr)
            r = _dot_nt(p16[hr], vt_ref[0, e, h].astype(BF16))
            for n in range(DEC_SEQ):
                r = r + p_new[n][hr] * vn_ref[e, h, n:n + 1, :]
            m_h = m[hr]
            l_h = l_all[hr]
            parts = [(r[g * SUBLANES:(g + 1) * SUBLANES], m_h[g * SUBLANES:(g + 1) * SUBLANES],
                      l_h[g * SUBLANES:(g + 1) * SUBLANES]) for g in range(N_DIL)]
            m_all = jnp.maximum(jnp.maximum(parts[0][1], parts[1][1]), parts[2][1])
            num = jnp.zeros((SUBLANES, HEAD_DIM), F32)
            den = jnp.zeros((SUBLANES, 1), F32)
            for o_g, m_g, l_g in parts:
                w = jnp.exp(m_g - m_all)
                num = num + o_g * w
                den = den + l_g * w
            o_ref[e, :, h * HEAD_DIM:(h + 1) * HEAD_DIM] = (num / den)[0:DEC_SEQ]


def _attn_sample(layer, qh, knh, vnh, ckt, cvt):
    nr = N_DIL * SUBLANES
    eb = SAMPLE_EB
    cache_spec = pl.BlockSpec((1, eb, KV_HEADS, HEAD_DIM, MAX_WINDOW), lambda b: (layer, b, 0, 0, 0))
    new_spec = pl.BlockSpec((eb, KV_HEADS, SUBLANES, HEAD_DIM), lambda b: (b, 0, 0, 0))
    return pl.pallas_call(
        _attn_sample_kernel,
        grid=(DEC_BATCH // eb,),
        in_specs=[pl.BlockSpec((eb, KV_HEADS, nr, HEAD_DIM), lambda b: (b, 0, 0, 0)),
                  new_spec, new_spec, cache_spec, cache_spec],
        out_specs=pl.BlockSpec((eb, DEC_SEQ, KV_W), lambda b: (b, 0, 0)),
        out_shape=jax.ShapeDtypeStruct((DEC_BATCH, DEC_SEQ, KV_W), F32),
        scratch_shapes=[pltpu.VMEM((KV_HEADS * nr, MAX_WINDOW), F32)],
        compiler_params=_params(("arbitrary",)),
        name="attn_sample",
    )(qh, knh, vnh, ckt, cvt)


def kernel(x_prompt, x_sample, cache_k_win, cache_v_win, state_conv, g_pre_mix, g_post_mix, g_pre_ffn, g_post_ffn,
           w_in, a_ln_g, a_ln_b, a_ws, a_bs, c_conv_w, c_conv_b, w_br_a, w_br_b, w_br_c, b_gate, w_o,
           w_ff_gate, w_ff_up, w_ff_down):
    xp = x_prompt.reshape(BATCH * SEQ, D_MODEL)
    xs = x_sample.reshape(SAMPLE_ROWS, D_MODEL)
    ckt = jnp.transpose(cache_k_win, (0, 1, 3, 4, 2))
    cvt = jnp.transpose(cache_v_win, (0, 1, 3, 4, 2))

    w_in_b = w_in.astype(BF16)
    wkt = jnp.swapaxes(w_in[:, :, OFF_K:OFF_K + KV_W], 1, 2).astype(BF16)
    wvt = jnp.swapaxes(w_in[:, :, OFF_VV:OFF_VV + KV_W], 1, 2).astype(BF16)
    vecs = jnp.concatenate([b_gate, g_pre_mix, g_post_mix, g_pre_ffn, g_post_ffn, a_ln_g, a_ln_b, c_conv_b],
                           axis=1).reshape(DEPTH, 1, -1)
    bs_t = jnp.swapaxes(a_bs, 1, 2)
    ws4 = a_ws[:, :, :DEC_SEQ, :DEC_SEQ].reshape(-1)
    bs4 = a_bs[:, :, :DEC_SEQ].reshape(-1)
    zeros = jnp.zeros((DEPTH, DEC_BATCH, 1, C_WIDTH), F32)
    p1 = jnp.concatenate([state_conv[:, :, 1:2], zeros, zeros, zeros], axis=2).reshape(DEPTH, SAMPLE_ROWS, C_WIDTH)
    p2 = jnp.concatenate([state_conv, zeros, zeros], axis=2).reshape(DEPTH, SAMPLE_ROWS, C_WIDTH)

    kp_l, vp_l, ks_l, vs_l, cp_l, cs_l, av_l = [], [], [], [], [], [], []
    for l in range(DEPTH):
        oa, q, k, v, kt, vt, oc, gates, cs8 = _inproj_prompt(
            l, xp, vecs, w_in_b, wkt, wvt, a_ws, bs_t, c_conv_w)
        ob = _attn_prompt(q, k, v)

        def window(t):
            return jnp.transpose(t.reshape(BATCH, KV_HEADS, HEAD_DIM, WIN), (0, 3, 1, 2))

        kp_l.append(window(kt))
        vp_l.append(window(vt))
        cp_l.append(cs8[:, SUBLANES - (CONV_W - 1):])

        oa_s, q_s, k_s, v_s, oc_s, gates_s, vn_s, cz_s = _inproj_sample(
            l, ws4, bs4, xs, vecs, w_in_b, c_conv_w, p1, p2)

        pad_t = ((0, 0), (0, 0), (0, 0), (0, SUBLANES - DEC_SEQ), (0, 0))
        qh = q_s.reshape(DEC_BATCH, DEC_SEQ, N_DIL, KV_HEADS, HEAD_DIM).transpose(0, 3, 2, 1, 4)
        qh = jnp.pad(qh, pad_t).reshape(DEC_BATCH, KV_HEADS, N_DIL * SUBLANES, HEAD_DIM)

        def new_rows(a):
            a = a.reshape(DEC_BATCH, DEC_SEQ, KV_HEADS, HEAD_DIM).transpose(0, 2, 1, 3)
            return jnp.pad(a, pad_t[1:])

        ob_s = _attn_sample(l, qh, new_rows(k_s), new_rows(v_s), ckt, cvt)
        ob_s = ob_s.reshape(SAMPLE_ROWS, KV_W)
        xs, xp = _merge_ffn(l, xs, oa_s, ob_s, oc_s, gates_s, xp, oa, ob, oc, gates, vecs,
                            w_br_a, w_br_b, w_br_c, w_o, w_ff_gate, w_ff_up, w_ff_down)

        ks_l.append(k_s.reshape(DEC_BATCH, DEC_SEQ, KV_HEADS, HEAD_DIM))
        vs_l.append(v_s.reshape(DEC_BATCH, DEC_SEQ, KV_HEADS, HEAD_DIM))
        cs_l.append(cz_s.reshape(DEC_BATCH, DEC_SEQ, C_WIDTH)[:, DEC_SEQ - (CONV_W - 1):])
        av_l.append(vn_s.reshape(DEC_BATCH, DEC_SEQ, A_WIDTH))

    return (xp.reshape(BATCH, SEQ, D_MODEL), xs.reshape(DEC_BATCH, DEC_SEQ, D_MODEL),
            jnp.stack(kp_l), jnp.stack(vp_l), jnp.stack(ks_l), jnp.stack(vs_l),
            jnp.stack(cp_l), jnp.stack(cs_l), jnp.stack(av_l))
```

```python
import functools

import jax
import jax.numpy as jnp
from jax import lax
from jax.experimental import pallas as pl
from jax.experimental.pallas import tpu as pltpu

F32 = jnp.float32
BF16 = jnp.bfloat16

D_MODEL = 1024
BATCH = 2
SEQ = 8192
DEPTH = 2
DEC_BATCH = 32
DEC_SEQ = 4
CHUNK = 128
A_GROUPS = 4
A_GROUP_DIM = 128
A_WIDTH = A_GROUPS * A_GROUP_DIM
HEAD_DIM = 64
KV_HEADS = 4
DILS = (1, 4, 16)
N_DIL = len(DILS)
Q_W = N_DIL * KV_HEADS * HEAD_DIM
KV_W = KV_HEADS * HEAD_DIM
BAND = 128
MAX_WINDOW = 2048
C_WIDTH = 512
CONV_W = 3
D_FF = 2816
EPS = 1e-6

OFF_U = 0
OFF_V = OFF_U + A_WIDTH
OFF_Q = OFF_V + A_WIDTH
OFF_K = OFF_Q + Q_W
OFF_VV = OFF_K + KV_W
OFF_CX = OFF_VV + KV_W
OFF_CB = OFF_CX + C_WIDTH
OFF_CC = OFF_CB + C_WIDTH
OFF_G = OFF_CC + C_WIDTH
IN_W = OFF_G + 3 * D_MODEL

LANES = 128
SUBLANES = 8
VMEM_LIMIT = 56 * 1024 * 1024
NEG = -1e30

SAMPLE_ROWS = DEC_BATCH * DEC_SEQ
WIN = min(MAX_WINDOW, SEQ)
SPAN = BAND * DILS[-1]
RES = DILS[1]
PROMPT_TM = 512
ROW_SUB = 256
INPROJ_SUB = ROW_SUB
INPROJ_LAG = 2
MERGE_FFN_LAG = 3
SAMPLE_EB = 4
MXU_N = 256
FF_CHUNKS = ((0, 6 * MXU_N), (6 * MXU_N, D_FF))
LOG2E = 1.4426950408889634
QSCALE = LOG2E * HEAD_DIM ** -0.5

SLOPES = tuple(
    tuple(2.0 ** (-8.0 * (g * KV_HEADS + h + 1) / (N_DIL * KV_HEADS)) for h in range(KV_HEADS))
    for g in range(N_DIL)
)


def _rms(x, g):
    return x * lax.rsqrt(jnp.mean(x * x, axis=-1, keepdims=True) + EPS) * g


def _layer_norm(x, g, b):
    mu = jnp.mean(x, axis=-1, keepdims=True)
    xc = x - mu
    return xc * lax.rsqrt(jnp.mean(xc * xc, axis=-1, keepdims=True) + EPS) * g + b


def _dot(a, b):
    return jnp.dot(a, b, preferred_element_type=F32)


def _dot_nt(a, b):
    return lax.dot_general(a, b, (((1,), (1,)), ((), ())), preferred_element_type=F32)


def _select_rows(hid, vals):
    out = vals[-1]
    for h in range(len(vals) - 2, -1, -1):
        out = jnp.where(hid == h, vals[h], out)
    return out


def _layer_spec(layer, shape):
    nd = len(shape)
    return pl.BlockSpec((None,) + tuple(shape), lambda *_: (layer,) + (0,) * nd, pipeline_mode=pl.Buffered(1))


VEC_SLOTS = {"bg": (3 * D_MODEL, 0), "gpm": (D_MODEL, 3), "gqm": (D_MODEL, 4), "gpf": (D_MODEL, 5),
             "gqf": (D_MODEL, 6), "lng": (A_WIDTH, 14), "lnb": (A_WIDTH, 15), "cbias": (C_WIDTH, 16)}


def _vec_spec(layer, name):
    width, idx = VEC_SLOTS[name]
    return pl.BlockSpec((None, 1, width), lambda *_: (layer, 0, idx), pipeline_mode=pl.Buffered(1))


def _trace_staggered(stage_gens, lag):
    live = [True] * len(stage_gens)
    tick = 0
    while any(live):
        for j, gen in enumerate(stage_gens):
            if live[j] and tick >= j * lag:
                live[j] = next(gen, "done") != "done"
        tick += 1


def _params(sem):
    return pltpu.CompilerParams(dimension_semantics=sem, vmem_limit_bytes=VMEM_LIMIT)


def _inproj_prompt_kernel(x_ref, gpre_ref, w_ref, wkt_ref, wvt_ref, lng_ref, lnb_ref, ws_ref, bs_ref, cw_ref,
                          cbias_ref, bg_ref,
                          oa_ref, q_ref, k_ref, v_ref, kt_ref, vt_ref, oc_ref, gate_ref, cs_ref, prev_ref):
    tm = x_ref.shape[0]
    i = pl.program_id(1)

    @pl.when(i == 0)
    def _():
        prev_ref[...] = jnp.zeros_like(prev_ref)

    row = lax.broadcasted_iota(jnp.int32, (CHUNK, CHUNK), 0)
    col = lax.broadcasted_iota(jnp.int32, (CHUNK, CHUNK), 1)
    wt = [jnp.where(row >= col, ws_ref[g], 0.0).astype(BF16) for g in range(A_GROUPS)]
    rsub = lax.broadcasted_iota(jnp.int32, (INPROJ_SUB, 1), 0)

    carry = {"tail": prev_ref[...], "h": []}

    def sub_tile(r0):
        rows = slice(r0, r0 + INPROJ_SUB)
        h = _rms(x_ref[rows, :], gpre_ref[...]).astype(BF16)
        carry["h"].append(h)

        def proj(off, width):
            return _dot(h, w_ref[:, off:off + width])

        yield
        u = jax.nn.gelu(proj(OFF_U, A_WIDTH))
        yield
        vn = _layer_norm(jax.nn.gelu(proj(OFF_V, A_WIDTH)), lng_ref[...], lnb_ref[...])
        for j in range(3):
            yield
            cs = slice(j * D_MODEL, (j + 1) * D_MODEL)
            gate_ref[rows, cs] = jax.nn.sigmoid(proj(OFF_G + j * D_MODEL, D_MODEL) + bg_ref[:, cs]).astype(BF16)
        yield
        for g in range(A_GROUPS):
            cs = slice(g * A_GROUP_DIM, (g + 1) * A_GROUP_DIM)
            for c in range(0, INPROJ_SUB // CHUNK, 2):
                pair = jnp.concatenate([vn[(c + j) * CHUNK:(c + j + 1) * CHUNK, cs] for j in range(2)], axis=1)
                sa2 = _dot(wt[g], pair.astype(BF16)) + bs_ref[:, g:g + 1]
                for j in range(2):
                    rs = slice((c + j) * CHUNK, (c + j + 1) * CHUNK)
                    sa = sa2[:, j * A_GROUP_DIM:(j + 1) * A_GROUP_DIM]
                    oa_ref[r0 + (c + j) * CHUNK:r0 + (c + j + 1) * CHUNK, cs] = (u[rs, cs] * sa).astype(BF16)
        yield
        cz = proj(OFF_CC, C_WIDTH) * proj(OFF_CX, C_WIDTH)
        p2 = carry["tail"][SUBLANES - 2:SUBLANES - 1, :]
        p1 = carry["tail"][SUBLANES - 1:SUBLANES, :]
        carry["tail"] = cz[INPROJ_SUB - SUBLANES:INPROJ_SUB, :]
        z1 = jnp.where(rsub == 0, p1, pltpu.roll(cz, 1, 0))
        z2 = jnp.where(rsub == 0, p2, jnp.where(rsub == 1, p1, pltpu.roll(cz, 2, 0)))
        y = cbias_ref[...] + cw_ref[0:1, :] * z2 + cw_ref[1:2, :] * z1 + cw_ref[2:3, :] * cz
        yield
        oc_ref[rows, :] = (proj(OFF_CB, C_WIDTH) * y).astype(BF16)
        yield
        qv = proj(OFF_Q, Q_W)
        for c in range(Q_W // LANES):
            q_ref[c, rows, :] = qv[:, c * LANES:(c + 1) * LANES] * QSCALE
        yield
        kv = proj(OFF_K, 2 * KV_W)
        for c in range(KV_W // LANES):
            k_ref[c, rows, :] = kv[:, c * LANES:(c + 1) * LANES]
            v_ref[c, rows, :] = kv[:, KV_W + c * LANES:KV_W + (c + 1) * LANES]

    _trace_staggered([sub_tile(r0) for r0 in range(0, tm, INPROJ_SUB)], INPROJ_LAG)
    prev_ref[...] = carry["tail"]
    cs_ref[0] = carry["tail"]
    hs = carry["h"]

    @pl.when(i >= pl.num_programs(1) - WIN // tm)
    def _():
        for j, h in enumerate(hs):
            cols = slice(j * INPROJ_SUB, (j + 1) * INPROJ_SUB)
            kt_ref[0, :, cols] = _dot_nt(wkt_ref[...], h)
            vt_ref[0, :, cols] = _dot_nt(wvt_ref[...], h)


def _inproj_prompt(layer, x, vecs, w_in, wkt, wvt, ws, bs_t, cw):
    tm = PROMPT_TM
    nt = SEQ // tm
    rows = BATCH * SEQ
    win_spec = pl.BlockSpec((1, KV_W, tm), lambda b, i: (b, 0, jnp.maximum(i - (nt - WIN // tm), 0)))

    def rowblk(width):
        return pl.BlockSpec((tm, width), lambda b, i: (b * nt + i, 0))

    def slab(n):
        return pl.BlockSpec((n, tm, LANES), lambda b, i: (0, b * nt + i, 0))

    return pl.pallas_call(
        _inproj_prompt_kernel,
        grid=(BATCH, nt),
        in_specs=[
            rowblk(D_MODEL),
            _vec_spec(layer, "gpm"),
            _layer_spec(layer, (D_MODEL, IN_W)),
            _layer_spec(layer, (KV_W, D_MODEL)),
            _layer_spec(layer, (KV_W, D_MODEL)),
            _vec_spec(layer, "lng"),
            _vec_spec(layer, "lnb"),
            _layer_spec(layer, (A_GROUPS, CHUNK, CHUNK)),
            _layer_spec(layer, (CHUNK, A_GROUPS)),
            _layer_spec(layer, (CONV_W, C_WIDTH)),
            _vec_spec(layer, "cbias"),
            _vec_spec(layer, "bg"),
        ],
        out_specs=[
            rowblk(A_WIDTH),
            slab(Q_W // LANES),
            slab(KV_W // LANES),
            slab(KV_W // LANES),
            win_spec,
            win_spec,
            rowblk(C_WIDTH),
            rowblk(3 * D_MODEL),
            pl.BlockSpec((1, SUBLANES, C_WIDTH), lambda b, i: (b, 0, 0)),
        ],
        out_shape=[
            jax.ShapeDtypeStruct((rows, A_WIDTH), BF16),
            jax.ShapeDtypeStruct((Q_W // LANES, rows, LANES), F32),
            jax.ShapeDtypeStruct((KV_W // LANES, rows, LANES), F32),
            jax.ShapeDtypeStruct((KV_W // LANES, rows, LANES), F32),
            jax.ShapeDtypeStruct((BATCH, KV_W, WIN), F32),
            jax.ShapeDtypeStruct((BATCH, KV_W, WIN), F32),
            jax.ShapeDtypeStruct((rows, C_WIDTH), BF16),
            jax.ShapeDtypeStruct((rows, 3 * D_MODEL), BF16),
            jax.ShapeDtypeStruct((BATCH, SUBLANES, C_WIDTH), F32),
        ],
        scratch_shapes=[pltpu.VMEM((SUBLANES, C_WIDTH), F32)],
        compiler_params=_params(("arbitrary", "arbitrary")),
        name="inproj_prompt",
    )(x, vecs, w_in, wkt, wvt, vecs, vecs, ws, bs_t, cw, vecs, vecs)


def _attn_prompt_kernel(q_ref, k_ref, v_ref, o_ref, ktail, vtail, k4, v4, q24, acc, mrun, lrun, tmp, bias_ref):
    b = pl.program_id(0)
    i = pl.program_id(1)
    first = i == 0
    nslab = KV_W // LANES
    quarter = SPAN // RES

    @pl.when(jnp.logical_and(b == 0, first))
    def _():
        rr = lax.broadcasted_iota(jnp.int32, (KV_HEADS * BAND, 2 * BAND), 0)
        kk = lax.broadcasted_iota(jnp.int32, (KV_HEADS * BAND, 2 * BAND), 1)
        dist = (rr & (BAND - 1)) + BAND - kk
        hrow = rr >> 7
        ok = jnp.logical_and(dist >= 0, dist <= BAND)
        for g, dil in enumerate(DILS):
            coef = _select_rows(hrow, [jnp.float32(SLOPES[g][h] * dil * LOG2E) for h in range(KV_HEADS)])
            full = jnp.where(ok, -(coef * dist.astype(F32)), NEG)
            bias_ref[2 * g] = full
            bias_ref[2 * g + 1] = jnp.where(kk >= BAND, full, NEG)

    @pl.when(first)
    def _():
        for r in range(RES):
            base = r * 2 * quarter
            k4[:, base:base + quarter, :] = jnp.zeros((nslab, quarter, LANES), F32)
            v4[:, base:base + quarter, :] = jnp.zeros((nslab, quarter, LANES), F32)
        ktail[...] = jnp.zeros_like(ktail)
        vtail[...] = jnp.zeros_like(vtail)

    @pl.when(i > 0)
    def _():
        for r in range(RES):
            base = r * 2 * quarter
            k4[:, base:base + quarter, :] = k4[:, base + quarter:base + 2 * quarter, :]
            v4[:, base:base + quarter, :] = v4[:, base + quarter:base + 2 * quarter, :]

    for r in range(RES):
        base = r * 2 * quarter + quarter
        for c in range(nslab):
            k4[c, base:base + quarter, :] = k_ref[c, pl.ds(r, quarter, stride=RES), :]
            v4[c, base:base + quarter, :] = v_ref[c, pl.ds(r, quarter, stride=RES), :]
            q24[c, r * quarter:(r + 1) * quarter, :] = q_ref[2 * nslab + c, pl.ds(r, quarter, stride=RES), :]

    lane = lax.broadcasted_iota(jnp.int32, (1, KV_W), 1)
    hid = lane >> 6
    headmask = [jnp.where(hid == h, 1.0, 0.0).astype(BF16) for h in range(KV_HEADS)]

    def wide(ref, rows, slab0=0):
        return jnp.concatenate([ref[slab0 + c, rows, :] for c in range(nslab)], axis=1)

    def softmax_unit(qb, kb, vb, bias):
        qb16 = qb.astype(BF16)
        kb16 = kb.astype(BF16)
        vb16 = vb.astype(BF16)
        qs = jnp.concatenate([qb16 * headmask[h] for h in range(KV_HEADS)], axis=0)
        s = _dot_nt(qs, kb16)
        ps, ms, ls = [], [], []
        for h in range(KV_HEADS):
            sh = s[h * BAND:(h + 1) * BAND] + bias_ref[bias, h * BAND:(h + 1) * BAND, :]
            mh = jnp.max(sh, axis=-1, keepdims=True)
            ph = jnp.exp2(sh - mh)
            ps.append(ph.astype(BF16))
            ms.append(mh)
            ls.append(jnp.sum(ph, axis=-1, keepdims=True))
        r = _dot(jnp.concatenate(ps, axis=0), vb16)
        rs = [r[h * BAND:(h + 1) * BAND] for h in range(KV_HEADS)]
        return _select_rows(hid, rs), _select_rows(hid, ms), _select_rows(hid, ls)

    def store_stats(rows, o, m, l):
        for c in range(nslab):
            ls = slice(c * LANES, (c + 1) * LANES)
            acc[c, rows, :] = o[:, ls]
            mrun[c, rows, :] = m[:, ls]
            lrun[c, rows, :] = l[:, ls]

    def merge_stats(rows, o_u, m_u, l_u):
        m_old = wide(mrun, rows)
        m_n = jnp.maximum(m_old, m_u)
        a_old = jnp.exp2(m_old - m_n)
        a_new = jnp.exp2(m_u - m_n)
        store_stats(rows, wide(acc, rows) * a_old + o_u * a_new, m_n, wide(lrun, rows) * a_old + l_u * a_new)

    def unit0(n, kb, vb, bias):
        q0 = n * BAND if isinstance(n, int) else pl.multiple_of(n * BAND, BAND)
        stats = softmax_unit(wide(q_ref, pl.ds(q0, BAND)), kb, vb, bias)
        sub = BAND // RES
        for j, x in enumerate(stats):
            for c in range(nslab):
                tmp[j * nslab + c, pl.ds(q0, BAND), :] = x[:, c * LANES:(c + 1) * LANES]
        for r in range(RES):
            src = pl.ds(q0 + r, sub, stride=RES)
            dst = pl.ds(r * quarter + n * sub, sub)
            for c in range(nslab):
                acc[c, dst, :] = tmp[c, src, :]
                mrun[c, dst, :] = tmp[nslab + c, src, :]
                lrun[c, dst, :] = tmp[2 * nslab + c, src, :]

    kb0 = jnp.concatenate([wide(ktail, pl.ds(0, BAND)), wide(k_ref, pl.ds(0, BAND))], axis=0)
    vb0 = jnp.concatenate([wide(vtail, pl.ds(0, BAND)), wide(v_ref, pl.ds(0, BAND))], axis=0)
    unit0(0, kb0, vb0, jnp.where(first, 1, 0))

    def body0(n, carry):
        keys = pl.ds(pl.multiple_of((n - 1) * BAND, BAND), 2 * BAND)
        unit0(n, wide(k_ref, keys), wide(v_ref, keys), 0)
        return carry

    lax.fori_loop(1, SPAN // BAND, body0, 0, unroll=5)

    def body1(s, carry):
        bias = 2 + jnp.where(jnp.logical_and(first, s == 0), 1, 0)
        for r in range(RES):
            qb = wide(q_ref, pl.ds(s * (BAND * RES) + r, BAND, stride=RES), nslab)
            keys = pl.ds(r * 2 * quarter + quarter + (s - 1) * BAND, 2 * BAND)
            o_u, m_u, l_u = softmax_unit(qb, wide(k4, keys), wide(v4, keys), bias)
            merge_stats(pl.ds(r * quarter + s * BAND, BAND), o_u, m_u, l_u)
        return carry

    lax.fori_loop(0, quarter // BAND, body1, 0, unroll=2)

    def body2(r, carry):
        bias = 4 + jnp.where(first, 1, 0)
        for a in range(DILS[2] // RES):
            rows = pl.ds(r * quarter + a, BAND, stride=RES)
            keys = pl.ds(r * 2 * quarter + a, 2 * BAND, stride=RES)
            o_u, m_u, l_u = softmax_unit(wide(q24, rows), wide(k4, keys), wide(v4, keys), bias)
            merge_stats(rows, o_u, m_u, l_u)
        return carry

    lax.fori_loop(0, RES, body2, 0, unroll=2)

    for r in range(RES):
        rows = pl.ds(r * quarter, quarter)
        for c in range(nslab):
            tmp[c, pl.ds(r, quarter, stride=RES), :] = acc[c, rows, :] / lrun[c, rows, :]
    for c in range(nslab):
        o_ref[:, c * LANES:(c + 1) * LANES] = tmp[c].astype(BF16)
    ktail[...] = k_ref[:, SPAN - BAND:SPAN, :]
    vtail[...] = v_ref[:, SPAN - BAND:SPAN, :]


def _attn_prompt(q, k, v):
    ns = SEQ // SPAN
    nslab = KV_W // LANES
    rows = BATCH * SEQ

    def slab(n):
        return pl.BlockSpec((n, SPAN, LANES), lambda b, i: (0, b * ns + i, 0))

    return pl.pallas_call(
        _attn_prompt_kernel,
        grid=(BATCH, ns),
        in_specs=[slab(Q_W // LANES), slab(nslab), slab(nslab)],
        out_specs=pl.BlockSpec((SPAN, KV_W), lambda b, i: (b * ns + i, 0)),
        out_shape=jax.ShapeDtypeStruct((rows, KV_W), BF16),
        scratch_shapes=[
            pltpu.VMEM((nslab, BAND, LANES), F32),
            pltpu.VMEM((nslab, BAND, LANES), F32),
            pltpu.VMEM((nslab, 2 * SPAN, LANES), F32),
            pltpu.VMEM((nslab, 2 * SPAN, LANES), F32),
            pltpu.VMEM((nslab, SPAN, LANES), F32),
            pltpu.VMEM((nslab, SPAN, LANES), F32),
            pltpu.VMEM((nslab, SPAN, LANES), F32),
            pltpu.VMEM((nslab, SPAN, LANES), F32),
            pltpu.VMEM((3 * nslab, SPAN, LANES), F32),
            pltpu.VMEM((2 * N_DIL, KV_HEADS * BAND, 2 * BAND), F32),
        ],
        compiler_params=_params(("arbitrary", "arbitrary")),
        name="attn_prompt",
    )(q, k, v)


def _merge_ffn_kernel(xs_ref, oas_ref, obs_ref, ocs_ref, gates_ref, x_ref, oa_ref, ob_ref, oc_ref, gate_ref,
                      gqm_ref, gpf_ref, gqf_ref, wa_ref, wb_ref, wc_ref, wo_ref, wg_ref, wu_ref, wd_ref,
                      os_ref, o_ref):
    def run(x_ref, oa_ref, ob_ref, oc_ref, gate_ref, o_ref):
        tm = x_ref.shape[0]
        sub = min(tm, ROW_SUB)

        def sub_tile(r0):
            rows = slice(r0, r0 + sub)

            def gate(j):
                return gate_ref[rows, j * D_MODEL:(j + 1) * D_MODEL].astype(F32)

            merged = gate(0) * _dot(oa_ref[rows, :], wa_ref[...])
            yield
            merged = merged + gate(1) * _dot(ob_ref[rows, :].astype(BF16), wb_ref[...])
            yield
            merged = merged + gate(2) * _dot(oc_ref[rows, :], wc_ref[...])
            yield
            x = x_ref[rows, :] + _rms(_dot(merged.astype(BF16), wo_ref[...]), gqm_ref[...])
            yield
            h = _rms(x, gpf_ref[...]).astype(BF16)
            y = None
            for lo, hi in FF_CHUNKS:
                yield
                a = jax.nn.silu(_dot(h, wg_ref[:, lo:hi]))
                yield
                act = (a * _dot(h, wu_ref[:, lo:hi])).astype(BF16)
                yield
                part = _dot(act, wd_ref[lo:hi, :])
                y = part if y is None else y + part
            yield
            o_ref[rows, :] = x + _rms(y, gqf_ref[...])

        _trace_staggered([sub_tile(r0) for r0 in range(0, tm, sub)], MERGE_FFN_LAG)

    step = pl.program_id(0)

    @pl.when(step == 0)
    def _():
        run(xs_ref, oas_ref, obs_ref, ocs_ref, gates_ref, os_ref)

    @pl.when(step > 0)
    def _():
        run(x_ref, oa_ref, ob_ref, oc_ref, gate_ref, o_ref)


def _merge_ffn(layer, xs, oa_s, ob_s, oc_s, gates_s, x, oa, ob, oc, gates, vecs, wa, wb, wc, wo, wg, wu, wd):
    tm = PROMPT_TM
    rows = x.shape[0]
    ns = xs.shape[0]

    def rowblk(width):
        return pl.BlockSpec((tm, width), lambda i: (jnp.maximum(i - 1, 0), 0))

    def whole(width):
        return pl.BlockSpec((ns, width), lambda i: (0, 0), pipeline_mode=pl.Buffered(1))

    return pl.pallas_call(
        _merge_ffn_kernel,
        grid=(1 + rows // tm,),
        in_specs=[
            whole(D_MODEL), whole(A_WIDTH), whole(KV_W), whole(C_WIDTH), whole(3 * D_MODEL),
            rowblk(D_MODEL), rowblk(A_WIDTH), rowblk(KV_W), rowblk(C_WIDTH), rowblk(3 * D_MODEL),
            _vec_spec(layer, "gqm"), _vec_spec(layer, "gpf"), _vec_spec(layer, "gqf"),
            _layer_spec(layer, (A_WIDTH, D_MODEL)), _layer_spec(layer, (KV_W, D_MODEL)),
            _layer_spec(layer, (C_WIDTH, D_MODEL)), _layer_spec(layer, (D_MODEL, D_MODEL)),
            _layer_spec(layer, (D_MODEL, D_FF)), _layer_spec(layer, (D_MODEL, D_FF)),
            _layer_spec(layer, (D_FF, D_MODEL)),
        ],
        out_specs=[pl.BlockSpec((ns, D_MODEL), lambda i: (0, 0)), rowblk(D_MODEL)],
        out_shape=[jax.ShapeDtypeStruct((ns, D_MODEL), F32), jax.ShapeDtypeStruct((rows, D_MODEL), F32)],
        compiler_params=_params(("arbitrary",)),
        name="merge_ffn",
    )(xs, oa_s, ob_s, oc_s, gates_s, x, oa, ob, oc, gates, vecs, vecs, vecs, wa, wb, wc, wo, wg, wu, wd)


def _inproj_sample_kernel(ws_ref, bs_ref, x_ref, gpre_ref, w_ref, lng_ref, lnb_ref, cw_ref, cbias_ref, bg_ref,
                          p1_ref, p2_ref,
                          oa_ref, q_ref, k_ref, v_ref, oc_ref, gate_ref, vn_ref, cz_ref, *, layer):
    n = SAMPLE_ROWS
    h = _rms(x_ref[...], gpre_ref[...]).astype(BF16)

    def proj(off, width):
        return _dot(h, w_ref[:, off:off + width])

    u = jax.nn.gelu(proj(OFF_U, A_WIDTH))
    vn = _layer_norm(jax.nn.gelu(proj(OFF_V, A_WIDTH)), lng_ref[...], lnb_ref[...])
    vn_ref[...] = vn
    t = lax.broadcasted_iota(jnp.int32, (n, 1), 0) & (DEC_SEQ - 1)
    for g in range(A_GROUPS):
        cs = slice(g * A_GROUP_DIM, (g + 1) * A_GROUP_DIM)
        base = (layer * A_GROUPS + g) * DEC_SEQ
        vg = vn[:, cs]
        sa = _select_rows(t, [bs_ref[base + tt] for tt in range(DEC_SEQ)])
        for k in range(DEC_SEQ):
            coef = jnp.zeros((n, 1), F32)
            for tt in range(k, DEC_SEQ):
                coef = jnp.where(t == tt, ws_ref[(base + tt) * DEC_SEQ + tt - k], coef)
            sa = sa + coef * (vg if k == 0 else pltpu.roll(vg, k, 0))
        oa_ref[:, cs] = (u[:, cs] * sa).astype(BF16)

    q_ref[...] = proj(OFF_Q, Q_W)
    kv = proj(OFF_K, 2 * KV_W)
    k_ref[...] = kv[:, 0:KV_W]
    v_ref[...] = kv[:, KV_W:2 * KV_W]

    cz = proj(OFF_CC, C_WIDTH) * proj(OFF_CX, C_WIDTH)
    cz_ref[...] = cz
    z1 = jnp.where(t >= 1, pltpu.roll(cz, 1, 0), p1_ref[...])
    z2 = jnp.where(t >= 2, pltpu.roll(cz, 2, 0), p2_ref[...])
    y = cbias_ref[...] + cw_ref[0:1, :] * z2 + cw_ref[1:2, :] * z1 + cw_ref[2:3, :] * cz
    oc_ref[...] = (proj(OFF_CB, C_WIDTH) * y).astype(BF16)

    for j in range(3):
        cs = slice(j * D_MODEL, (j + 1) * D_MODEL)
        gate_ref[:, cs] = jax.nn.sigmoid(proj(OFF_G + j * D_MODEL, D_MODEL) + bg_ref[:, cs]).astype(BF16)


def _inproj_sample(layer, ws4, bs4, x, vecs, w_in, cw, p1, p2):
    n = SAMPLE_ROWS
    smem = pl.BlockSpec(memory_space=pltpu.SMEM)

    def full(shape):
        nd = len(shape)
        return pl.BlockSpec(shape, lambda i: (0,) * nd)

    widths_dtypes = [(A_WIDTH, BF16), (Q_W, F32), (KV_W, F32), (KV_W, F32), (C_WIDTH, BF16),
                     (3 * D_MODEL, BF16), (A_WIDTH, F32), (C_WIDTH, F32)]
    return pl.pallas_call(
        functools.partial(_inproj_sample_kernel, layer=layer),
        grid=(1,),
        in_specs=[smem, smem, full((n, D_MODEL)), _vec_spec(layer, "gpm"),
                  _layer_spec(layer, (D_MODEL, IN_W)), _vec_spec(layer, "lng"),
                  _vec_spec(layer, "lnb"), _layer_spec(layer, (CONV_W, C_WIDTH)),
                  _vec_spec(layer, "cbias"), _vec_spec(layer, "bg"),
                  _layer_spec(layer, (n, C_WIDTH)), _layer_spec(layer, (n, C_WIDTH))],
        out_specs=[full((n, w)) for w, _ in widths_dtypes],
        out_shape=[jax.ShapeDtypeStruct((n, w), dt) for w, dt in widths_dtypes],
        compiler_params=_params(("arbitrary",)),
        name="inproj_sample",
    )(ws4, bs4, x, vecs, w_in, vecs, vecs, cw, vecs, vecs, p1, p2)


def _attn_sample_kernel(qh_ref, kn_ref, vn_ref, kt_ref, vt_ref, o_ref, bias_ref):
    nr = N_DIL * SUBLANES
    width = MAX_WINDOW

    @pl.when(pl.program_id(0) == 0)
    def _():
        row = lax.broadcasted_iota(jnp.int32, (nr, width), 0)
        pos = lax.broadcasted_iota(jnp.int32, (nr, width), 1)
        grp = row >> 3
        delta = width + (row & (SUBLANES - 1)) - pos
        dil_m1 = _select_rows(grp, [d - 1 for d in DILS])
        reach = _select_rows(grp, [BAND * d for d in DILS])
        ok = jnp.logical_and((delta & dil_m1) == 0, delta <= reach)
        for h in range(KV_HEADS):
            slope = _select_rows(grp, [jnp.float32(SLOPES[g][h]) for g in range(N_DIL)])
            bias_ref[h * nr:(h + 1) * nr, :] = jnp.where(ok, -(slope * delta.astype(F32)), NEG)

    ridx = lax.broadcasted_iota(jnp.int32, (nr, 1), 0)
    g_of = ridx >> 3
    t_of = ridx & (SUBLANES - 1)
    scale = HEAD_DIM ** -0.5
    slopes = [_select_rows(g_of, [jnp.float32(SLOPES[g][h]) for g in range(N_DIL)]) for h in range(KV_HEADS)]
    for e in range(qh_ref.shape[0]):
        qs = [qh_ref[e, h] * scale for h in range(KV_HEADS)]
        s = jnp.concatenate([_dot(qs[h].astype(BF16), kt_ref[0, e, h].astype(BF16)) for h in range(KV_HEADS)],
                            axis=0)
        s = s + bias_ref[...]
        m = jnp.max(s, axis=-1, keepdims=True)
        s_new = []
        for n in range(DEC_SEQ):
            dn = t_of - n
            ok = jnp.logical_or(dn == 0, jnp.logical_and(g_of == 0, dn > 0))
            rows = []
            for h in range(KV_HEADS):
                raw = jnp.sum(qs[h] * kn_ref[e, h, n:n + 1, :], axis=-1, keepdims=True)
                rows.append(jnp.where(ok, raw - slopes[h] * dn.astype(F32), NEG))
            sn = jnp.concatenate(rows, axis=0)
            s_new.append(sn)
            m = jnp.maximum(m, sn)
        p = jnp.exp(s - m)
        l_all = jnp.sum(p, axis=-1, keepdims=True)
        p16 = p.astype(BF16)
        p_new = [jnp.exp(sn - m) for sn in s_new]
        for pn in p_new:
            l_all = l_all + pn
        for h in range(KV_HEADS):
            hr = slice(h * nr, (h + 1) * nr)
            r = _dot_nt(p16[hr], vt_ref[0, e, h].astype(BF16))
            for n in range(DEC_SEQ):
                r = r + p_new[n][hr] * vn_ref[e, h, n:n + 1, :]
            m_h = m[hr]
            l_h = l_all[hr]
            parts = [(r[g * SUBLANES:(g + 1) * SUBLANES], m_h[g * SUBLANES:(g + 1) * SUBLANES],
                      l_h[g * SUBLANES:(g + 1) * SUBLANES]) for g in range(N_DIL)]
            m_all = jnp.maximum(jnp.maximum(parts[0][1], parts[1][1]), parts[2][1])
            num = jnp.zeros((SUBLANES, HEAD_DIM), F32)
            den = jnp.zeros((SUBLANES, 1), F32)
            for o_g, m_g, l_g in parts:
                w = jnp.exp(m_g - m_all)
                num = num + o_g * w
                den = den + l_g * w
            o_ref[e, :, h * HEAD_DIM:(h + 1) * HEAD_DIM] = (num / den)[0:DEC_SEQ]


def _attn_sample(layer, qh, knh, vnh, ckt, cvt):
    nr = N_DIL * SUBLANES
    eb = SAMPLE_EB
    cache_spec = pl.BlockSpec((1, eb, KV_HEADS, HEAD_DIM, MAX_WINDOW), lambda b: (layer, b, 0, 0, 0))
    new_spec = pl.BlockSpec((eb, KV_HEADS, SUBLANES, HEAD_DIM), lambda b: (b, 0, 0, 0))
    return pl.pallas_call(
        _attn_sample_kernel,
        grid=(DEC_BATCH // eb,),
        in_specs=[pl.BlockSpec((eb, KV_HEADS, nr, HEAD_DIM), lambda b: (b, 0, 0, 0)),
                  new_spec, new_spec, cache_spec, cache_spec],
        out_specs=pl.BlockSpec((eb, DEC_SEQ, KV_W), lambda b: (b, 0, 0)),
        out_shape=jax.ShapeDtypeStruct((DEC_BATCH, DEC_SEQ, KV_W), F32),
        scratch_shapes=[pltpu.VMEM((KV_HEADS * nr, MAX_WINDOW), F32)],
        compiler_params=_params(("arbitrary",)),
        name="attn_sample",
    )(qh, knh, vnh, ckt, cvt)


def kernel(x_prompt, x_sample, cache_k_win, cache_v_win, state_conv, g_pre_mix, g_post_mix, g_pre_ffn, g_post_ffn,
           w_in, a_ln_g, a_ln_b, a_ws, a_bs, c_conv_w, c_conv_b, w_br_a, w_br_b, w_br_c, b_gate, w_o,
           w_ff_gate, w_ff_up, w_ff_down):
    xp = x_prompt.reshape(BATCH * SEQ, D_MODEL)
    xs = x_sample.reshape(SAMPLE_ROWS, D_MODEL)
    ckt = jnp.transpose(cache_k_win, (0, 1, 3, 4, 2))
    cvt = jnp.transpose(cache_v_win, (0, 1, 3, 4, 2))

    w_in_b = w_in.astype(BF16)
    wa, wb, wc, wo = (w.astype(BF16) for w in (w_br_a, w_br_b, w_br_c, w_o))
    wg, wu, wd = (w.astype(BF16) for w in (w_ff_gate, w_ff_up, w_ff_down))
    wkt = jnp.swapaxes(w_in[:, :, OFF_K:OFF_K + KV_W], 1, 2).astype(BF16)
    wvt = jnp.swapaxes(w_in[:, :, OFF_VV:OFF_VV + KV_W], 1, 2).astype(BF16)
    vecs = jnp.concatenate([b_gate, g_pre_mix, g_post_mix, g_pre_ffn, g_post_ffn, a_ln_g, a_ln_b, c_conv_b],
                           axis=1).reshape(DEPTH, 1, -1)
    bs_t = jnp.swapaxes(a_bs, 1, 2)
    ws4 = a_ws[:, :, :DEC_SEQ, :DEC_SEQ].reshape(-1)
    bs4 = a_bs[:, :, :DEC_SEQ].reshape(-1)
    zeros = jnp.zeros((DEPTH, DEC_BATCH, 1, C_WIDTH), F32)
    p1 = jnp.concatenate([state_conv[:, :, 1:2], zeros, zeros, zeros], axis=2).reshape(DEPTH, SAMPLE_ROWS, C_WIDTH)
    p2 = jnp.concatenate([state_conv, zeros, zeros], axis=2).reshape(DEPTH, SAMPLE_ROWS, C_WIDTH)

    kp_l, vp_l, ks_l, vs_l, cp_l, cs_l, av_l = [], [], [], [], [], [], []
    for l in range(DEPTH):
        oa, q, k, v, kt, vt, oc, gates, cs8 = _inproj_prompt(
            l, xp, vecs, w_in_b, wkt, wvt, a_ws, bs_t, c_conv_w)
        ob = _attn_prompt(q, k, v)

        def window(t):
            return jnp.transpose(t.reshape(BATCH, KV_HEADS, HEAD_DIM, WIN), (0, 3, 1, 2))

        kp_l.append(window(kt))
        vp_l.append(window(vt))
        cp_l.append(cs8[:, SUBLANES - (CONV_W - 1):])

        oa_s, q_s, k_s, v_s, oc_s, gates_s, vn_s, cz_s = _inproj_sample(
            l, ws4, bs4, xs, vecs, w_in_b, c_conv_w, p1, p2)

        pad_t = ((0, 0), (0, 0), (0, 0), (0, SUBLANES - DEC_SEQ), (0, 0))
        qh = q_s.reshape(DEC_BATCH, DEC_SEQ, N_DIL, KV_HEADS, HEAD_DIM).transpose(0, 3, 2, 1, 4)
        qh = jnp.pad(qh, pad_t).reshape(DEC_BATCH, KV_HEADS, N_DIL * SUBLANES, HEAD_DIM)

        def new_rows(a):
            a = a.reshape(DEC_BATCH, DEC_SEQ, KV_HEADS, HEAD_DIM).transpose(0, 2, 1, 3)
            return jnp.pad(a, pad_t[1:])

        ob_s = _attn_sample(l, qh, new_rows(k_s), new_rows(v_s), ckt, cvt)
        ob_s = ob_s.reshape(SAMPLE_ROWS, KV_W)
        xs, xp = _merge_ffn(l, xs, oa_s, ob_s, oc_s, gates_s, xp, oa, ob, oc, gates, vecs,
                            wa, wb, wc, wo, wg, wu, wd)

        ks_l.append(k_s.reshape(DEC_BATCH, DEC_SEQ, KV_HEADS, HEAD_DIM))
        vs_l.append(v_s.reshape(DEC_BATCH, DEC_SEQ, KV_HEADS, HEAD_DIM))
        cs_l.append(cz_s.reshape(DEC_BATCH, DEC_SEQ, C_WIDTH)[:, DEC_SEQ - (CONV_W - 1):])
        av_l.append(vn_s.reshape(DEC_BATCH, DEC_SEQ, A_WIDTH))

    return (xp.reshape(BATCH, SEQ, D_MODEL), xs.reshape(DEC_BATCH, DEC_SEQ, D_MODEL),
            jnp.stack(kp_l), jnp.stack(vp_l), jnp.stack(ks_l), jnp.stack(vs_l),
            jnp.stack(cp_l), jnp.stack(cs_l), jnp.stack(av_l))
```

```python
import functools

import jax
import jax.numpy as jnp
from jax import lax
from jax.experimental import pallas as pl
from jax.experimental.pallas import tpu as pltpu

F32 = jnp.float32
BF16 = jnp.bfloat16

D_MODEL = 1024
BATCH = 2
SEQ = 8192
DEPTH = 2
DEC_BATCH = 32
DEC_SEQ = 4
CHUNK = 128
A_GROUPS = 4
A_GROUP_DIM = 128
A_WIDTH = A_GROUPS * A_GROUP_DIM
HEAD_DIM = 64
KV_HEADS = 4
DILS = (1, 4, 16)
N_DIL = len(DILS)
Q_W = N_DIL * KV_HEADS * HEAD_DIM
KV_W = KV_HEADS * HEAD_DIM
BAND = 128
MAX_WINDOW = 2048
C_WIDTH = 512
CONV_W = 3
D_FF = 2816
EPS = 1e-6

OFF_U = 0
OFF_V = OFF_U + A_WIDTH
OFF_Q = OFF_V + A_WIDTH
OFF_K = OFF_Q + Q_W
OFF_VV = OFF_K + KV_W
OFF_CX = OFF_VV + KV_W
OFF_CB = OFF_CX + C_WIDTH
OFF_CC = OFF_CB + C_WIDTH
OFF_G = OFF_CC + C_WIDTH
IN_W = OFF_G + 3 * D_MODEL

LANES = 128
SUBLANES = 8
VMEM_LIMIT = 56 * 1024 * 1024
NEG = -1e30

SAMPLE_ROWS = DEC_BATCH * DEC_SEQ
WIN = min(MAX_WINDOW, SEQ)
SPAN = BAND * DILS[-1]
RES = DILS[1]
PROMPT_TM = 512
ROW_SUB = 256
INPROJ_SUB = ROW_SUB
INPROJ_LAG = 2
MERGE_FFN_LAG = 3
SAMPLE_EB = 4
MXU_N = 256
FF_CHUNKS = ((0, 6 * MXU_N), (6 * MXU_N, D_FF))
LOG2E = 1.4426950408889634
QSCALE = LOG2E * HEAD_DIM ** -0.5

SLOPES = tuple(
    tuple(2.0 ** (-8.0 * (g * KV_HEADS + h + 1) / (N_DIL * KV_HEADS)) for h in range(KV_HEADS))
    for g in range(N_DIL)
)


def _rms(x, g):
    return x * lax.rsqrt(jnp.mean(x * x, axis=-1, keepdims=True) + EPS) * g


def _layer_norm(x, g, b):
    mu = jnp.mean(x, axis=-1, keepdims=True)
    xc = x - mu
    return xc * lax.rsqrt(jnp.mean(xc * xc, axis=-1, keepdims=True) + EPS) * g + b


def _dot(a, b):
    return jnp.dot(a, b, preferred_element_type=F32)


def _dot_nt(a, b):
    return lax.dot_general(a, b, (((1,), (1,)), ((), ())), preferred_element_type=F32)


def _select_rows(hid, vals):
    out = vals[-1]
    for h in range(len(vals) - 2, -1, -1):
        out = jnp.where(hid == h, vals[h], out)
    return out


def _layer_spec(layer, shape):
    nd = len(shape)
    return pl.BlockSpec((None,) + tuple(shape), lambda *_: (layer,) + (0,) * nd, pipeline_mode=pl.Buffered(1))


VEC_SLOTS = {"bg": (3 * D_MODEL, 0), "gpm": (D_MODEL, 3), "gqm": (D_MODEL, 4), "gpf": (D_MODEL, 5),
             "gqf": (D_MODEL, 6), "lng": (A_WIDTH, 14), "lnb": (A_WIDTH, 15), "cbias": (C_WIDTH, 16)}


def _vec_spec(layer, name):
    width, idx = VEC_SLOTS[name]
    return pl.BlockSpec((None, 1, width), lambda *_: (layer, 0, idx), pipeline_mode=pl.Buffered(1))


def _trace_staggered(stage_gens, lag):
    live = [True] * len(stage_gens)
    tick = 0
    while any(live):
        for j, gen in enumerate(stage_gens):
            if live[j] and tick >= j * lag:
                live[j] = next(gen, "done") != "done"
        tick += 1


def _params(sem):
    return pltpu.CompilerParams(dimension_semantics=sem, vmem_limit_bytes=VMEM_LIMIT)


def _inproj_prompt_kernel(x_ref, gpre_ref, w_ref, lng_ref, lnb_ref, ws_ref, bs_ref, cw_ref,
                          cbias_ref, bg_ref,
                          oa_ref, q_ref, k_ref, v_ref, kt_ref, vt_ref, oc_ref, gate_ref, cs_ref, prev_ref):
    tm = x_ref.shape[0]
    i = pl.program_id(1)

    @pl.when(i == 0)
    def _():
        prev_ref[...] = jnp.zeros_like(prev_ref)

    row = lax.broadcasted_iota(jnp.int32, (CHUNK, CHUNK), 0)
    col = lax.broadcasted_iota(jnp.int32, (CHUNK, CHUNK), 1)
    wt = [jnp.where(row >= col, ws_ref[g], 0.0).astype(BF16) for g in range(A_GROUPS)]
    rsub = lax.broadcasted_iota(jnp.int32, (INPROJ_SUB, 1), 0)

    carry = {"tail": prev_ref[...]}

    def sub_tile(r0):
        rows = slice(r0, r0 + INPROJ_SUB)
        h = _rms(x_ref[rows, :], gpre_ref[...]).astype(BF16)

        def proj(off, width):
            return _dot(h, w_ref[:, off:off + width])

        yield
        u = jax.nn.gelu(proj(OFF_U, A_WIDTH))
        yield
        vn = _layer_norm(jax.nn.gelu(proj(OFF_V, A_WIDTH)), lng_ref[...], lnb_ref[...])
        for j in range(3):
            yield
            cs = slice(j * D_MODEL, (j + 1) * D_MODEL)
            gate_ref[rows, cs] = jax.nn.sigmoid(proj(OFF_G + j * D_MODEL, D_MODEL) + bg_ref[:, cs]).astype(BF16)
        yield
        for g in range(A_GROUPS):
            cs = slice(g * A_GROUP_DIM, (g + 1) * A_GROUP_DIM)
            for c in range(0, INPROJ_SUB // CHUNK, 2):
                pair = jnp.concatenate([vn[(c + j) * CHUNK:(c + j + 1) * CHUNK, cs] for j in range(2)], axis=1)
                sa2 = _dot(wt[g], pair.astype(BF16)) + bs_ref[:, g:g + 1]
                for j in range(2):
                    rs = slice((c + j) * CHUNK, (c + j + 1) * CHUNK)
                    sa = sa2[:, j * A_GROUP_DIM:(j + 1) * A_GROUP_DIM]
                    oa_ref[r0 + (c + j) * CHUNK:r0 + (c + j + 1) * CHUNK, cs] = (u[rs, cs] * sa).astype(BF16)
        yield
        cz = proj(OFF_CC, C_WIDTH) * proj(OFF_CX, C_WIDTH)
        p2 = carry["tail"][SUBLANES - 2:SUBLANES - 1, :]
        p1 = carry["tail"][SUBLANES - 1:SUBLANES, :]
        carry["tail"] = cz[INPROJ_SUB - SUBLANES:INPROJ_SUB, :]
        z1 = jnp.where(rsub == 0, p1, pltpu.roll(cz, 1, 0))
        z2 = jnp.where(rsub == 0, p2, jnp.where(rsub == 1, p1, pltpu.roll(cz, 2, 0)))
        y = cbias_ref[...] + cw_ref[0:1, :] * z2 + cw_ref[1:2, :] * z1 + cw_ref[2:3, :] * cz
        yield
        oc_ref[rows, :] = (proj(OFF_CB, C_WIDTH) * y).astype(BF16)
        yield
        qv = proj(OFF_Q, Q_W)
        for c in range(Q_W // LANES):
            q_ref[c, rows, :] = qv[:, c * LANES:(c + 1) * LANES] * QSCALE
        yield
        kv = proj(OFF_K, 2 * KV_W)
        for c in range(KV_W // LANES):
            k_ref[c, rows, :] = kv[:, c * LANES:(c + 1) * LANES]
            v_ref[c, rows, :] = kv[:, KV_W + c * LANES:KV_W + (c + 1) * LANES]
        kt_ref[0, :, rows] = kv[:, 0:KV_W].T
        vt_ref[0, :, rows] = kv[:, KV_W:2 * KV_W].T

    _trace_staggered([sub_tile(r0) for r0 in range(0, tm, INPROJ_SUB)], INPROJ_LAG)
    prev_ref[...] = carry["tail"]
    cs_ref[0] = carry["tail"]


def _inproj_prompt(layer, x, vecs, w_in, ws, bs_t, cw):
    tm = PROMPT_TM
    nt = SEQ // tm
    rows = BATCH * SEQ
    win_spec = pl.BlockSpec((1, KV_W, tm), lambda b, i: (b, 0, jnp.maximum(i - (nt - WIN // tm), 0)))

    def rowblk(width):
        return pl.BlockSpec((tm, width), lambda b, i: (b * nt + i, 0))

    def slab(n):
        return pl.BlockSpec((n, tm, LANES), lambda b, i: (0, b * nt + i, 0))

    return pl.pallas_call(
        _inproj_prompt_kernel,
        grid=(BATCH, nt),
        in_specs=[
            rowblk(D_MODEL),
            _vec_spec(layer, "gpm"),
            _layer_spec(layer, (D_MODEL, IN_W)),
            _vec_spec(layer, "lng"),
            _vec_spec(layer, "lnb"),
            _layer_spec(layer, (A_GROUPS, CHUNK, CHUNK)),
            _layer_spec(layer, (CHUNK, A_GROUPS)),
            _layer_spec(layer, (CONV_W, C_WIDTH)),
            _vec_spec(layer, "cbias"),
            _vec_spec(layer, "bg"),
        ],
        out_specs=[
            rowblk(A_WIDTH),
            slab(Q_W // LANES),
            slab(KV_W // LANES),
            slab(KV_W // LANES),
            win_spec,
            win_spec,
            rowblk(C_WIDTH),
            rowblk(3 * D_MODEL),
            pl.BlockSpec((1, SUBLANES, C_WIDTH), lambda b, i: (b, 0, 0)),
        ],
        out_shape=[
            jax.ShapeDtypeStruct((rows, A_WIDTH), BF16),
            jax.ShapeDtypeStruct((Q_W // LANES, rows, LANES), F32),
            jax.ShapeDtypeStruct((KV_W // LANES, rows, LANES), F32),
            jax.ShapeDtypeStruct((KV_W // LANES, rows, LANES), F32),
            jax.ShapeDtypeStruct((BATCH, KV_W, WIN), F32),
            jax.ShapeDtypeStruct((BATCH, KV_W, WIN), F32),
            jax.ShapeDtypeStruct((rows, C_WIDTH), BF16),
            jax.ShapeDtypeStruct((rows, 3 * D_MODEL), BF16),
            jax.ShapeDtypeStruct((BATCH, SUBLANES, C_WIDTH), F32),
        ],
        scratch_shapes=[pltpu.VMEM((SUBLANES, C_WIDTH), F32)],
        compiler_params=_params(("arbitrary", "arbitrary")),
        name="inproj_prompt",
    )(x, vecs, w_in, vecs, vecs, ws, bs_t, cw, vecs, vecs)


def _attn_prompt_kernel(q_ref, k_ref, v_ref, o_ref, ktail, vtail, k4, v4, q24, acc, mrun, lrun, tmp, bias_ref):
    b = pl.program_id(0)
    i = pl.program_id(1)
    first = i == 0
    nslab = KV_W // LANES
    quarter = SPAN // RES

    @pl.when(jnp.logical_and(b == 0, first))
    def _():
        rr = lax.broadcasted_iota(jnp.int32, (KV_HEADS * BAND, 2 * BAND), 0)
        kk = lax.broadcasted_iota(jnp.int32, (KV_HEADS * BAND, 2 * BAND), 1)
        dist = (rr & (BAND - 1)) + BAND - kk
        hrow = rr >> 7
        ok = jnp.logical_and(dist >= 0, dist <= BAND)
        for g, dil in enumerate(DILS):
            coef = _select_rows(hrow, [jnp.float32(SLOPES[g][h] * dil * LOG2E) for h in range(KV_HEADS)])
            full = jnp.where(ok, -(coef * dist.astype(F32)), NEG)
            bias_ref[2 * g] = full
            bias_ref[2 * g + 1] = jnp.where(kk >= BAND, full, NEG)

    @pl.when(first)
    def _():
        for r in range(RES):
            base = r * 2 * quarter
            k4[:, base:base + quarter, :] = jnp.zeros((nslab, quarter, LANES), F32)
            v4[:, base:base + quarter, :] = jnp.zeros((nslab, quarter, LANES), F32)
        ktail[...] = jnp.zeros_like(ktail)
        vtail[...] = jnp.zeros_like(vtail)

    @pl.when(i > 0)
    def _():
        for r in range(RES):
            base = r * 2 * quarter
            k4[:, base:base + quarter, :] = k4[:, base + quarter:base + 2 * quarter, :]
            v4[:, base:base + quarter, :] = v4[:, base + quarter:base + 2 * quarter, :]

    for r in range(RES):
        base = r * 2 * quarter + quarter
        for c in range(nslab):
            k4[c, base:base + quarter, :] = k_ref[c, pl.ds(r, quarter, stride=RES), :]
            v4[c, base:base + quarter, :] = v_ref[c, pl.ds(r, quarter, stride=RES), :]
            q24[c, r * quarter:(r + 1) * quarter, :] = q_ref[2 * nslab + c, pl.ds(r, quarter, stride=RES), :]

    lane = lax.broadcasted_iota(jnp.int32, (1, KV_W), 1)
    hid = lane >> 6
    headmask = [jnp.where(hid == h, 1.0, 0.0).astype(BF16) for h in range(KV_HEADS)]

    def wide(ref, rows, slab0=0):
        return jnp.concatenate([ref[slab0 + c, rows, :] for c in range(nslab)], axis=1)

    def softmax_unit(qb, kb, vb, bias):
        qb16 = qb.astype(BF16)
        kb16 = kb.astype(BF16)
        vb16 = vb.astype(BF16)
        qs = jnp.concatenate([qb16 * headmask[h] for h in range(KV_HEADS)], axis=0)
        s = _dot_nt(qs, kb16)
        ps, ms, ls = [], [], []
        for h in range(KV_HEADS):
            sh = s[h * BAND:(h + 1) * BAND] + bias_ref[bias, h * BAND:(h + 1) * BAND, :]
            mh = jnp.max(sh, axis=-1, keepdims=True)
            ph = jnp.exp2(sh - mh)
            ps.append(ph.astype(BF16))
            ms.append(mh)
            ls.append(jnp.sum(ph, axis=-1, keepdims=True))
        r = _dot(jnp.concatenate(ps, axis=0), vb16)
        rs = [r[h * BAND:(h + 1) * BAND] for h in range(KV_HEADS)]
        return _select_rows(hid, rs), _select_rows(hid, ms), _select_rows(hid, ls)

    def store_stats(rows, o, m, l):
        for c in range(nslab):
            ls = slice(c * LANES, (c + 1) * LANES)
            acc[c, rows, :] = o[:, ls]
            mrun[c, rows, :] = m[:, ls]
            lrun[c, rows, :] = l[:, ls]

    def merge_stats(rows, o_u, m_u, l_u):
        m_old = wide(mrun, rows)
        m_n = jnp.maximum(m_old, m_u)
        a_old = jnp.exp2(m_old - m_n)
        a_new = jnp.exp2(m_u - m_n)
        store_stats(rows, wide(acc, rows) * a_old + o_u * a_new, m_n, wide(lrun, rows) * a_old + l_u * a_new)

    def unit0(n, kb, vb, bias):
        q0 = n * BAND if isinstance(n, int) else pl.multiple_of(n * BAND, BAND)
        stats = softmax_unit(wide(q_ref, pl.ds(q0, BAND)), kb, vb, bias)
        sub = BAND // RES
        for j, x in enumerate(stats):
            for c in range(nslab):
                tmp[j * nslab + c, pl.ds(q0, BAND), :] = x[:, c * LANES:(c + 1) * LANES]
        for r in range(RES):
            src = pl.ds(q0 + r, sub, stride=RES)
            dst = pl.ds(r * quarter + n * sub, sub)
            for c in range(nslab):
                acc[c, dst, :] = tmp[c, src, :]
                mrun[c, dst, :] = tmp[nslab + c, src, :]
                lrun[c, dst, :] = tmp[2 * nslab + c, src, :]

    kb0 = jnp.concatenate([wide(ktail, pl.ds(0, BAND)), wide(k_ref, pl.ds(0, BAND))], axis=0)
    vb0 = jnp.concatenate([wide(vtail, pl.ds(0, BAND)), wide(v_ref, pl.ds(0, BAND))], axis=0)
    unit0(0, kb0, vb0, jnp.where(first, 1, 0))

    def body0(n, carry):
        keys = pl.ds(pl.multiple_of((n - 1) * BAND, BAND), 2 * BAND)
        unit0(n, wide(k_ref, keys), wide(v_ref, keys), 0)
        return carry

    lax.fori_loop(1, SPAN // BAND, body0, 0, unroll=5)

    def body1(s, carry):
        bias = 2 + jnp.where(jnp.logical_and(first, s == 0), 1, 0)
        for r in range(RES):
            qb = wide(q_ref, pl.ds(s * (BAND * RES) + r, BAND, stride=RES), nslab)
            keys = pl.ds(r * 2 * quarter + quarter + (s - 1) * BAND, 2 * BAND)
            o_u, m_u, l_u = softmax_unit(qb, wide(k4, keys), wide(v4, keys), bias)
            merge_stats(pl.ds(r * quarter + s * BAND, BAND), o_u, m_u, l_u)
        return carry

    lax.fori_loop(0, quarter // BAND, body1, 0, unroll=2)

    def body2(r, carry):
        bias = 4 + jnp.where(first, 1, 0)
        for a in range(DILS[2] // RES):
            rows = pl.ds(r * quarter + a, BAND, stride=RES)
            keys = pl.ds(r * 2 * quarter + a, 2 * BAND, stride=RES)
            o_u, m_u, l_u = softmax_unit(wide(q24, rows), wide(k4, keys), wide(v4, keys), bias)
            merge_stats(rows, o_u, m_u, l_u)
        return carry

    lax.fori_loop(0, RES, body2, 0, unroll=2)

    for r in range(RES):
        rows = pl.ds(r * quarter, quarter)
        for c in range(nslab):
            tmp[c, pl.ds(r, quarter, stride=RES), :] = acc[c, rows, :] / lrun[c, rows, :]
    for c in range(nslab):
        o_ref[:, c * LANES:(c + 1) * LANES] = tmp[c].astype(BF16)
    ktail[...] = k_ref[:, SPAN - BAND:SPAN, :]
    vtail[...] = v_ref[:, SPAN - BAND:SPAN, :]


def _attn_prompt(q, k, v):
    ns = SEQ // SPAN
    nslab = KV_W // LANES
    rows = BATCH * SEQ

    def slab(n):
        return pl.BlockSpec((n, SPAN, LANES), lambda b, i: (0, b * ns + i, 0))

    return pl.pallas_call(
        _attn_prompt_kernel,
        grid=(BATCH, ns),
        in_specs=[slab(Q_W // LANES), slab(nslab), slab(nslab)],
        out_specs=pl.BlockSpec((SPAN, KV_W), lambda b, i: (b * ns + i, 0)),
        out_shape=jax.ShapeDtypeStruct((rows, KV_W), BF16),
        scratch_shapes=[
            pltpu.VMEM((nslab, BAND, LANES), F32),
            pltpu.VMEM((nslab, BAND, LANES), F32),
            pltpu.VMEM((nslab, 2 * SPAN, LANES), F32),
            pltpu.VMEM((nslab, 2 * SPAN, LANES), F32),
            pltpu.VMEM((nslab, SPAN, LANES), F32),
            pltpu.VMEM((nslab, SPAN, LANES), F32),
            pltpu.VMEM((nslab, SPAN, LANES), F32),
            pltpu.VMEM((nslab, SPAN, LANES), F32),
            pltpu.VMEM((3 * nslab, SPAN, LANES), F32),
            pltpu.VMEM((2 * N_DIL, KV_HEADS * BAND, 2 * BAND), F32),
        ],
        compiler_params=_params(("arbitrary", "arbitrary")),
        name="attn_prompt",
    )(q, k, v)


def _merge_ffn_kernel(xs_ref, oas_ref, obs_ref, ocs_ref, gates_ref, x_ref, oa_ref, ob_ref, oc_ref, gate_ref,
                      gqm_ref, gpf_ref, gqf_ref, wa_ref, wb_ref, wc_ref, wo_ref, wg_ref, wu_ref, wd_ref,
                      os_ref, o_ref):
    def run(x_ref, oa_ref, ob_ref, oc_ref, gate_ref, o_ref):
        tm = x_ref.shape[0]
        sub = min(tm, ROW_SUB)

        def sub_tile(r0):
            rows = slice(r0, r0 + sub)

            def gate(j):
                return gate_ref[rows, j * D_MODEL:(j + 1) * D_MODEL].astype(F32)

            merged = gate(0) * _dot(oa_ref[rows, :], wa_ref[...])
            yield
            merged = merged + gate(1) * _dot(ob_ref[rows, :].astype(BF16), wb_ref[...])
            yield
            merged = merged + gate(2) * _dot(oc_ref[rows, :], wc_ref[...])
            yield
            x = x_ref[rows, :] + _rms(_dot(merged.astype(BF16), wo_ref[...]), gqm_ref[...])
            yield
            h = _rms(x, gpf_ref[...]).astype(BF16)
            y = None
            for lo, hi in FF_CHUNKS:
                yield
                a = jax.nn.silu(_dot(h, wg_ref[:, lo:hi]))
                yield
                act = (a * _dot(h, wu_ref[:, lo:hi])).astype(BF16)
                yield
                part = _dot(act, wd_ref[lo:hi, :])
                y = part if y is None else y + part
            yield
            o_ref[rows, :] = x + _rms(y, gqf_ref[...])

        _trace_staggered([sub_tile(r0) for r0 in range(0, tm, sub)], MERGE_FFN_LAG)

    step = pl.program_id(0)

    @pl.when(step == 0)
    def _():
        run(xs_ref, oas_ref, obs_ref, ocs_ref, gates_ref, os_ref)

    @pl.when(step > 0)
    def _():
        run(x_ref, oa_ref, ob_ref, oc_ref, gate_ref, o_ref)


def _merge_ffn(layer, xs, oa_s, ob_s, oc_s, gates_s, x, oa, ob, oc, gates, vecs, wa, wb, wc, wo, wg, wu, wd):
    tm = PROMPT_TM
    rows = x.shape[0]
    ns = xs.shape[0]

    def rowblk(width):
        return pl.BlockSpec((tm, width), lambda i: (jnp.maximum(i - 1, 0), 0))

    def whole(width):
        return pl.BlockSpec((ns, width), lambda i: (0, 0), pipeline_mode=pl.Buffered(1))

    return pl.pallas_call(
        _merge_ffn_kernel,
        grid=(1 + rows // tm,),
        in_specs=[
            whole(D_MODEL), whole(A_WIDTH), whole(KV_W), whole(C_WIDTH), whole(3 * D_MODEL),
            rowblk(D_MODEL), rowblk(A_WIDTH), rowblk(KV_W), rowblk(C_WIDTH), rowblk(3 * D_MODEL),
            _vec_spec(layer, "gqm"), _vec_spec(layer, "gpf"), _vec_spec(layer, "gqf"),
            _layer_spec(layer, (A_WIDTH, D_MODEL)), _layer_spec(layer, (KV_W, D_MODEL)),
            _layer_spec(layer, (C_WIDTH, D_MODEL)), _layer_spec(layer, (D_MODEL, D_MODEL)),
            _layer_spec(layer, (D_MODEL, D_FF)), _layer_spec(layer, (D_MODEL, D_FF)),
            _layer_spec(layer, (D_FF, D_MODEL)),
        ],
        out_specs=[pl.BlockSpec((ns, D_MODEL), lambda i: (0, 0)), rowblk(D_MODEL)],
        out_shape=[jax.ShapeDtypeStruct((ns, D_MODEL), F32), jax.ShapeDtypeStruct((rows, D_MODEL), F32)],
        compiler_params=_params(("arbitrary",)),
        name="merge_ffn",
    )(xs, oa_s, ob_s, oc_s, gates_s, x, oa, ob, oc, gates, vecs, vecs, vecs, wa, wb, wc, wo, wg, wu, wd)


def _inproj_sample_kernel(ws_ref, bs_ref, x_ref, gpre_ref, w_ref, lng_ref, lnb_ref, cw_ref, cbias_ref, bg_ref,
                          p1_ref, p2_ref,
                          oa_ref, q_ref, k_ref, v_ref, oc_ref, gate_ref, vn_ref, cz_ref, *, layer):
    n = SAMPLE_ROWS
    h = _rms(x_ref[...], gpre_ref[...]).astype(BF16)

    def proj(off, width):
        return _dot(h, w_ref[:, off:off + width])

    u = jax.nn.gelu(proj(OFF_U, A_WIDTH))
    vn = _layer_norm(jax.nn.gelu(proj(OFF_V, A_WIDTH)), lng_ref[...], lnb_ref[...])
    vn_ref[...] = vn
    t = lax.broadcasted_iota(jnp.int32, (n, 1), 0) & (DEC_SEQ - 1)
    for g in range(A_GROUPS):
        cs = slice(g * A_GROUP_DIM, (g + 1) * A_GROUP_DIM)
        base = (layer * A_GROUPS + g) * DEC_SEQ
        vg = vn[:, cs]
        sa = _select_rows(t, [bs_ref[base + tt] for tt in range(DEC_SEQ)])
        for k in range(DEC_SEQ):
            coef = jnp.zeros((n, 1), F32)
            for tt in range(k, DEC_SEQ):
                coef = jnp.where(t == tt, ws_ref[(base + tt) * DEC_SEQ + tt - k], coef)
            sa = sa + coef * (vg if k == 0 else pltpu.roll(vg, k, 0))
        oa_ref[:, cs] = (u[:, cs] * sa).astype(BF16)

    q_ref[...] = proj(OFF_Q, Q_W)
    kv = proj(OFF_K, 2 * KV_W)
    k_ref[...] = kv[:, 0:KV_W]
    v_ref[...] = kv[:, KV_W:2 * KV_W]

    cz = proj(OFF_CC, C_WIDTH) * proj(OFF_CX, C_WIDTH)
    cz_ref[...] = cz
    z1 = jnp.where(t >= 1, pltpu.roll(cz, 1, 0), p1_ref[...])
    z2 = jnp.where(t >= 2, pltpu.roll(cz, 2, 0), p2_ref[...])
    y = cbias_ref[...] + cw_ref[0:1, :] * z2 + cw_ref[1:2, :] * z1 + cw_ref[2:3, :] * cz
    oc_ref[...] = (proj(OFF_CB, C_WIDTH) * y).astype(BF16)

    for j in range(3):
        cs = slice(j * D_MODEL, (j + 1) * D_MODEL)
        gate_ref[:, cs] = jax.nn.sigmoid(proj(OFF_G + j * D_MODEL, D_MODEL) + bg_ref[:, cs]).astype(BF16)


def _inproj_sample(layer, ws4, bs4, x, vecs, w_in, cw, p1, p2):
    n = SAMPLE_ROWS
    smem = pl.BlockSpec(memory_space=pltpu.SMEM)

    def full(shape):
        nd = len(shape)
        return pl.BlockSpec(shape, lambda i: (0,) * nd)

    widths_dtypes = [(A_WIDTH, BF16), (Q_W, F32), (KV_W, F32), (KV_W, F32), (C_WIDTH, BF16),
                     (3 * D_MODEL, BF16), (A_WIDTH, F32), (C_WIDTH, F32)]
    return pl.pallas_call(
        functools.partial(_inproj_sample_kernel, layer=layer),
        grid=(1,),
        in_specs=[smem, smem, full((n, D_MODEL)), _vec_spec(layer, "gpm"),
                  _layer_spec(layer, (D_MODEL, IN_W)), _vec_spec(layer, "lng"),
                  _vec_spec(layer, "lnb"), _layer_spec(layer, (CONV_W, C_WIDTH)),
                  _vec_spec(layer, "cbias"), _vec_spec(layer, "bg"),
                  _layer_spec(layer, (n, C_WIDTH)), _layer_spec(layer, (n, C_WIDTH))],
        out_specs=[full((n, w)) for w, _ in widths_dtypes],
        out_shape=[jax.ShapeDtypeStruct((n, w), dt) for w, dt in widths_dtypes],
        compiler_params=_params(("arbitrary",)),
        name="inproj_sample",
    )(ws4, bs4, x, vecs, w_in, vecs, vecs, cw, vecs, vecs, p1, p2)


def _attn_sample_kernel(qh_ref, kn_ref, vn_ref, kt_ref, vt_ref, o_ref, bias_ref):
    nr = N_DIL * SUBLANES
    width = MAX_WINDOW

    @pl.when(pl.program_id(0) == 0)
    def _():
        row = lax.broadcasted_iota(jnp.int32, (nr, width), 0)
        pos = lax.broadcasted_iota(jnp.int32, (nr, width), 1)
        grp = row >> 3
        delta = width + (row & (SUBLANES - 1)) - pos
        dil_m1 = _select_rows(grp, [d - 1 for d in DILS])
        reach = _select_rows(grp, [BAND * d for d in DILS])
        ok = jnp.logical_and((delta & dil_m1) == 0, delta <= reach)
        for h in range(KV_HEADS):
            slope = _select_rows(grp, [jnp.float32(SLOPES[g][h]) for g in range(N_DIL)])
            bias_ref[h * nr:(h + 1) * nr, :] = jnp.where(ok, -(slope * delta.astype(F32)), NEG)

    ridx = lax.broadcasted_iota(jnp.int32, (nr, 1), 0)
    g_of = ridx >> 3
    t_of = ridx & (SUBLANES - 1)
    scale = HEAD_DIM ** -0.5
    slopes = [_select_rows(g_of, [jnp.float32(SLOPES[g][h]) for g in range(N_DIL)]) for h in range(KV_HEADS)]
    for e in range(qh_ref.shape[0]):
        qs = [qh_ref[e, h] * scale for h in range(KV_HEADS)]
        s = jnp.concatenate([_dot(qs[h].astype(BF16), kt_ref[0, e, h].astype(BF16)) for h in range(KV_HEADS)],
                            axis=0)
        s = s + bias_ref[...]
        m = jnp.max(s, axis=-1, keepdims=True)
        s_new = []
        for n in range(DEC_SEQ):
            dn = t_of - n
            ok = jnp.logical_or(dn == 0, jnp.logical_and(g_of == 0, dn > 0))
            rows = []
            for h in range(KV_HEADS):
                raw = jnp.sum(qs[h] * kn_ref[e, h, n:n + 1, :], axis=-1, keepdims=True)
                rows.append(jnp.where(ok, raw - slopes[h] * dn.astype(F32), NEG))
            sn = jnp.concatenate(rows, axis=0)
            s_new.append(sn)
            m = jnp.maximum(m, sn)
        p = jnp.exp(s - m)
        l_all = jnp.sum(p, axis=-1, keepdims=True)
        p16 = p.astype(BF16)
        p_new = [jnp.exp(sn - m) for sn in s_new]
        for pn in p_new:
            l_all = l_all + pn
        for h in range(KV_HEADS):
            hr = slice(h * nr, (h + 1) * nr)
            r = _dot_nt(p16[hr], vt_ref[0, e, h].astype(BF16))
            for n in range(DEC_SEQ):
                r = r + p_new[n][hr] * vn_ref[e, h, n:n + 1, :]
            m_h = m[hr]
            l_h = l_all[hr]
            parts = [(r[g * SUBLANES:(g + 1) * SUBLANES], m_h[g * SUBLANES:(g + 1) * SUBLANES],
                      l_h[g * SUBLANES:(g + 1) * SUBLANES]) for g in range(N_DIL)]
            m_all = jnp.maximum(jnp.maximum(parts[0][1], parts[1][1]), parts[2][1])
            num = jnp.zeros((SUBLANES, HEAD_DIM), F32)
            den = jnp.zeros((SUBLANES, 1), F32)
            for o_g, m_g, l_g in parts:
                w = jnp.exp(m_g - m_all)
                num = num + o_g * w
                den = den + l_g * w
            o_ref[e, :, h * HEAD_DIM:(h + 1) * HEAD_DIM] = (num / den)[0:DEC_SEQ]


def _attn_sample(layer, qh, knh, vnh, ckt, cvt):
    nr = N_DIL * SUBLANES
    eb = SAMPLE_EB
    cache_spec = pl.BlockSpec((1, eb, KV_HEADS, HEAD_DIM, MAX_WINDOW), lambda b: (layer, b, 0, 0, 0))
    new_spec = pl.BlockSpec((eb, KV_HEADS, SUBLANES, HEAD_DIM), lambda b: (b, 0, 0, 0))
    return pl.pallas_call(
        _attn_sample_kernel,
        grid=(DEC_BATCH // eb,),
        in_specs=[pl.BlockSpec((eb, KV_HEADS, nr, HEAD_DIM), lambda b: (b, 0, 0, 0)),
                  new_spec, new_spec, cache_spec, cache_spec],
        out_specs=pl.BlockSpec((eb, DEC_SEQ, KV_W), lambda b: (b, 0, 0)),
        out_shape=jax.ShapeDtypeStruct((DEC_BATCH, DEC_SEQ, KV_W), F32),
        scratch_shapes=[pltpu.VMEM((KV_HEADS * nr, MAX_WINDOW), F32)],
        compiler_params=_params(("arbitrary",)),
        name="attn_sample",
    )(qh, knh, vnh, ckt, cvt)


def kernel(x_prompt, x_sample, cache_k_win, cache_v_win, state_conv, g_pre_mix, g_post_mix, g_pre_ffn, g_post_ffn,
           w_in, a_ln_g, a_ln_b, a_ws, a_bs, c_conv_w, c_conv_b, w_br_a, w_br_b, w_br_c, b_gate, w_o,
           w_ff_gate, w_ff_up, w_ff_down):
    xp = x_prompt.reshape(BATCH * SEQ, D_MODEL)
    xs = x_sample.reshape(SAMPLE_ROWS, D_MODEL)
    ckt = jnp.transpose(cache_k_win, (0, 1, 3, 4, 2))
    cvt = jnp.transpose(cache_v_win, (0, 1, 3, 4, 2))

    w_in_b = w_in.astype(BF16)
    wa, wb, wc, wo = (w.astype(BF16) for w in (w_br_a, w_br_b, w_br_c, w_o))
    wg, wu, wd = (w.astype(BF16) for w in (w_ff_gate, w_ff_up, w_ff_down))
    vecs = jnp.concatenate([b_gate, g_pre_mix, g_post_mix, g_pre_ffn, g_post_ffn, a_ln_g, a_ln_b, c_conv_b],
                           axis=1).reshape(DEPTH, 1, -1)
    bs_t = jnp.swapaxes(a_bs, 1, 2)
    ws4 = a_ws[:, :, :DEC_SEQ, :DEC_SEQ].reshape(-1)
    bs4 = a_bs[:, :, :DEC_SEQ].reshape(-1)
    zeros = jnp.zeros((DEPTH, DEC_BATCH, 1, C_WIDTH), F32)
    p1 = jnp.concatenate([state_conv[:, :, 1:2], zeros, zeros, zeros], axis=2).reshape(DEPTH, SAMPLE_ROWS, C_WIDTH)
    p2 = jnp.concatenate([state_conv, zeros, zeros], axis=2).reshape(DEPTH, SAMPLE_ROWS, C_WIDTH)

    kp_l, vp_l, ks_l, vs_l, cp_l, cs_l, av_l = [], [], [], [], [], [], []
    for l in range(DEPTH):
        oa, q, k, v, kt, vt, oc, gates, cs8 = _inproj_prompt(
            l, xp, vecs, w_in_b, a_ws, bs_t, c_conv_w)
        ob = _attn_prompt(q, k, v)

        def window(t):
            return jnp.transpose(t.reshape(BATCH, KV_HEADS, HEAD_DIM, WIN), (0, 3, 1, 2))

        kp_l.append(window(kt))
        vp_l.append(window(vt))
        cp_l.append(cs8[:, SUBLANES - (CONV_W - 1):])

        oa_s, q_s, k_s, v_s, oc_s, gates_s, vn_s, cz_s = _inproj_sample(
            l, ws4, bs4, xs, vecs, w_in_b, c_conv_w, p1, p2)

        pad_t = ((0, 0), (0, 0), (0, 0), (0, SUBLANES - DEC_SEQ), (0, 0))
        qh = q_s.reshape(DEC_BATCH, DEC_SEQ, N_DIL, KV_HEADS, HEAD_DIM).transpose(0, 3, 2, 1, 4)
        qh = jnp.pad(qh, pad_t).reshape(DEC_BATCH, KV_HEADS, N_DIL * SUBLANES, HEAD_DIM)

        def new_rows(a):
            a = a.reshape(DEC_BATCH, DEC_SEQ, KV_HEADS, HEAD_DIM).transpose(0, 2, 1, 3)
            return jnp.pad(a, pad_t[1:])

        ob_s = _attn_sample(l, qh, new_rows(k_s), new_rows(v_s), ckt, cvt)
        ob_s = ob_s.reshape(SAMPLE_ROWS, KV_W)
        xs, xp = _merge_ffn(l, xs, oa_s, ob_s, oc_s, gates_s, xp, oa, ob, oc, gates, vecs,
                            wa, wb, wc, wo, wg, wu, wd)

        ks_l.append(k_s.reshape(DEC_BATCH, DEC_SEQ, KV_HEADS, HEAD_DIM))
        vs_l.append(v_s.reshape(DEC_BATCH, DEC_SEQ, KV_HEADS, HEAD_DIM))
        cs_l.append(cz_s.reshape(DEC_BATCH, DEC_SEQ, C_WIDTH)[:, DEC_SEQ - (CONV_W - 1):])
        av_l.append(vn_s.reshape(DEC_BATCH, DEC_SEQ, A_WIDTH))

    return (xp.reshape(BATCH, SEQ, D_MODEL), xs.reshape(DEC_BATCH, DEC_SEQ, D_MODEL),
            jnp.stack(kp_l), jnp.stack(vp_l), jnp.stack(ks_l), jnp.stack(vs_l),
            jnp.stack(cp_l), jnp.stack(cs_l), jnp.stack(av_l))
```

```python
import functools

import jax
import jax.numpy as jnp
from jax import lax
from jax.experimental import pallas as pl
from jax.experimental.pallas import tpu as pltpu

F32 = jnp.float32
BF16 = jnp.bfloat16

D_MODEL = 1024
BATCH = 2
SEQ = 8192
DEPTH = 2
DEC_BATCH = 32
DEC_SEQ = 4
CHUNK = 128
A_GROUPS = 4
A_GROUP_DIM = 128
A_WIDTH = A_GROUPS * A_GROUP_DIM
HEAD_DIM = 64
KV_HEADS = 4
DILS = (1, 4, 16)
N_DIL = len(DILS)
Q_W = N_DIL * KV_HEADS * HEAD_DIM
KV_W = KV_HEADS * HEAD_DIM
BAND = 128
MAX_WINDOW = 2048
C_WIDTH = 512
CONV_W = 3
D_FF = 2816
EPS = 1e-6

OFF_U = 0
OFF_V = OFF_U + A_WIDTH
OFF_Q = OFF_V + A_WIDTH
OFF_K = OFF_Q + Q_W
OFF_VV = OFF_K + KV_W
OFF_CX = OFF_VV + KV_W
OFF_CB = OFF_CX + C_WIDTH
OFF_CC = OFF_CB + C_WIDTH
OFF_G = OFF_CC + C_WIDTH
IN_W = OFF_G + 3 * D_MODEL

LANES = 128
SUBLANES = 8
VMEM_LIMIT = 56 * 1024 * 1024
NEG = -1e30

SAMPLE_ROWS = DEC_BATCH * DEC_SEQ
WIN = min(MAX_WINDOW, SEQ)
SPAN = BAND * DILS[-1]
RES = DILS[1]
PROMPT_TM = 512
ROW_SUB = 256
INPROJ_SUB = ROW_SUB
INPROJ_LAG = 2
MERGE_FFN_LAG = 3
SAMPLE_EB = 4
MXU_N = 256
FF_CHUNKS = ((0, 6 * MXU_N), (6 * MXU_N, D_FF))
LOG2E = 1.4426950408889634
QSCALE = LOG2E * HEAD_DIM ** -0.5

SLOPES = tuple(
    tuple(2.0 ** (-8.0 * (g * KV_HEADS + h + 1) / (N_DIL * KV_HEADS)) for h in range(KV_HEADS))
    for g in range(N_DIL)
)


def _rms(x, g):
    return x * lax.rsqrt(jnp.mean(x * x, axis=-1, keepdims=True) + EPS) * g


def _layer_norm(x, g, b):
    mu = jnp.mean(x, axis=-1, keepdims=True)
    xc = x - mu
    return xc * lax.rsqrt(jnp.mean(xc * xc, axis=-1, keepdims=True) + EPS) * g + b


def _dot(a, b):
    return jnp.dot(a, b, preferred_element_type=F32)


def _dot_nt(a, b):
    return lax.dot_general(a, b, (((1,), (1,)), ((), ())), preferred_element_type=F32)


def _select_rows(hid, vals):
    out = vals[-1]
    for h in range(len(vals) - 2, -1, -1):
        out = jnp.where(hid == h, vals[h], out)
    return out


def _layer_spec(layer, shape):
    nd = len(shape)
    return pl.BlockSpec((None,) + tuple(shape), lambda *_: (layer,) + (0,) * nd, pipeline_mode=pl.Buffered(1))


VEC_SLOTS = {"bg": (3 * D_MODEL, 0), "gpm": (D_MODEL, 3), "gqm": (D_MODEL, 4), "gpf": (D_MODEL, 5),
             "gqf": (D_MODEL, 6), "lng": (A_WIDTH, 14), "lnb": (A_WIDTH, 15), "cbias": (C_WIDTH, 16)}


def _vec_spec(layer, name):
    width, idx = VEC_SLOTS[name]
    return pl.BlockSpec((None, 1, width), lambda *_: (layer, 0, idx), pipeline_mode=pl.Buffered(1))


def _trace_staggered(stage_gens, lag):
    live = [True] * len(stage_gens)
    tick = 0
    while any(live):
        for j, gen in enumerate(stage_gens):
            if live[j] and tick >= j * lag:
                live[j] = next(gen, "done") != "done"
        tick += 1


def _params(sem):
    return pltpu.CompilerParams(dimension_semantics=sem, vmem_limit_bytes=VMEM_LIMIT)


def _inproj_prompt_kernel(x_ref, gpre_ref, w_ref, lng_ref, lnb_ref, ws_ref, bs_ref, cw_ref,
                          cbias_ref, bg_ref,
                          oa_ref, q_ref, k_ref, v_ref, kt_ref, vt_ref, oc_ref, gate_ref, cs_ref, prev_ref, *, tile):
    tm = x_ref.shape[0]

    @pl.when(tile == 0)
    def _():
        prev_ref[...] = jnp.zeros_like(prev_ref)

    row = lax.broadcasted_iota(jnp.int32, (CHUNK, CHUNK), 0)
    col = lax.broadcasted_iota(jnp.int32, (CHUNK, CHUNK), 1)
    wt = [jnp.where(row >= col, ws_ref[g], 0.0).astype(BF16) for g in range(A_GROUPS)]
    rsub = lax.broadcasted_iota(jnp.int32, (INPROJ_SUB, 1), 0)

    carry = {"tail": prev_ref[...]}

    def sub_tile(r0):
        rows = slice(r0, r0 + INPROJ_SUB)
        h = _rms(x_ref[rows, :], gpre_ref[...]).astype(BF16)

        def proj(off, width):
            return _dot(h, w_ref[:, off:off + width])

        yield
        u = jax.nn.gelu(proj(OFF_U, A_WIDTH))
        yield
        vn = _layer_norm(jax.nn.gelu(proj(OFF_V, A_WIDTH)), lng_ref[...], lnb_ref[...])
        for j in range(3):
            yield
            cs = slice(j * D_MODEL, (j + 1) * D_MODEL)
            gate_ref[rows, cs] = jax.nn.sigmoid(proj(OFF_G + j * D_MODEL, D_MODEL) + bg_ref[:, cs]).astype(BF16)
        yield
        for g in range(A_GROUPS):
            cs = slice(g * A_GROUP_DIM, (g + 1) * A_GROUP_DIM)
            for c in range(0, INPROJ_SUB // CHUNK, 2):
                pair = jnp.concatenate([vn[(c + j) * CHUNK:(c + j + 1) * CHUNK, cs] for j in range(2)], axis=1)
                sa2 = _dot(wt[g], pair.astype(BF16)) + bs_ref[:, g:g + 1]
                for j in range(2):
                    rs = slice((c + j) * CHUNK, (c + j + 1) * CHUNK)
                    sa = sa2[:, j * A_GROUP_DIM:(j + 1) * A_GROUP_DIM]
                    oa_ref[r0 + (c + j) * CHUNK:r0 + (c + j + 1) * CHUNK, cs] = (u[rs, cs] * sa).astype(BF16)
        yield
        cz = proj(OFF_CC, C_WIDTH) * proj(OFF_CX, C_WIDTH)
        p2 = carry["tail"][SUBLANES - 2:SUBLANES - 1, :]
        p1 = carry["tail"][SUBLANES - 1:SUBLANES, :]
        carry["tail"] = cz[INPROJ_SUB - SUBLANES:INPROJ_SUB, :]
        z1 = jnp.where(rsub == 0, p1, pltpu.roll(cz, 1, 0))
        z2 = jnp.where(rsub == 0, p2, jnp.where(rsub == 1, p1, pltpu.roll(cz, 2, 0)))
        y = cbias_ref[...] + cw_ref[0:1, :] * z2 + cw_ref[1:2, :] * z1 + cw_ref[2:3, :] * cz
        yield
        oc_ref[rows, :] = (proj(OFF_CB, C_WIDTH) * y).astype(BF16)
        yield
        qv = proj(OFF_Q, Q_W)
        for c in range(Q_W // LANES):
            q_ref[c, rows, :] = qv[:, c * LANES:(c + 1) * LANES] * QSCALE
        yield
        kv = proj(OFF_K, 2 * KV_W)
        for c in range(KV_W // LANES):
            k_ref[c, rows, :] = kv[:, c * LANES:(c + 1) * LANES]
            v_ref[c, rows, :] = kv[:, KV_W + c * LANES:KV_W + (c + 1) * LANES]
        kt_ref[0, :, rows] = kv[:, 0:KV_W].T
        vt_ref[0, :, rows] = kv[:, KV_W:2 * KV_W].T

    _trace_staggered([sub_tile(r0) for r0 in range(0, tm, INPROJ_SUB)], INPROJ_LAG)
    prev_ref[...] = carry["tail"]
    cs_ref[0] = carry["tail"]


def _attn_prompt_kernel(q_ref, k_ref, v_ref, o_ref, ktail, vtail, k4, v4, q24, acc, mrun, lrun, tmp, bias_ref):
    b = pl.program_id(0)
    i = pl.program_id(1)
    first = i == 0
    nslab = KV_W // LANES
    quarter = SPAN // RES

    @pl.when(jnp.logical_and(b == 0, first))
    def _():
        rr = lax.broadcasted_iota(jnp.int32, (KV_HEADS * BAND, 2 * BAND), 0)
        kk = lax.broadcasted_iota(jnp.int32, (KV_HEADS * BAND, 2 * BAND), 1)
        dist = (rr & (BAND - 1)) + BAND - kk
        hrow = rr >> 7
        ok = jnp.logical_and(dist >= 0, dist <= BAND)
        for g, dil in enumerate(DILS):
            coef = _select_rows(hrow, [jnp.float32(SLOPES[g][h] * dil * LOG2E) for h in range(KV_HEADS)])
            full = jnp.where(ok, -(coef * dist.astype(F32)), NEG)
            bias_ref[2 * g] = full
            bias_ref[2 * g + 1] = jnp.where(kk >= BAND, full, NEG)

    @pl.when(first)
    def _():
        for r in range(RES):
            base = r * 2 * quarter
            k4[:, base:base + quarter, :] = jnp.zeros((nslab, quarter, LANES), F32)
            v4[:, base:base + quarter, :] = jnp.zeros((nslab, quarter, LANES), F32)
        ktail[...] = jnp.zeros_like(ktail)
        vtail[...] = jnp.zeros_like(vtail)

    @pl.when(i > 0)
    def _():
        for r in range(RES):
            base = r * 2 * quarter
            k4[:, base:base + quarter, :] = k4[:, base + quarter:base + 2 * quarter, :]
            v4[:, base:base + quarter, :] = v4[:, base + quarter:base + 2 * quarter, :]

    for r in range(RES):
        base = r * 2 * quarter + quarter
        for c in range(nslab):
            k4[c, base:base + quarter, :] = k_ref[c, pl.ds(r, quarter, stride=RES), :]
            v4[c, base:base + quarter, :] = v_ref[c, pl.ds(r, quarter, stride=RES), :]
            q24[c, r * quarter:(r + 1) * quarter, :] = q_ref[2 * nslab + c, pl.ds(r, quarter, stride=RES), :]

    lane = lax.broadcasted_iota(jnp.int32, (1, KV_W), 1)
    hid = lane >> 6
    headmask = [jnp.where(hid == h, 1.0, 0.0).astype(BF16) for h in range(KV_HEADS)]

    def wide(ref, rows, slab0=0):
        return jnp.concatenate([ref[slab0 + c, rows, :] for c in range(nslab)], axis=1)

    def softmax_unit(qb, kb, vb, bias):
        qb16 = qb.astype(BF16)
        kb16 = kb.astype(BF16)
        vb16 = vb.astype(BF16)
        qs = jnp.concatenate([qb16 * headmask[h] for h in range(KV_HEADS)], axis=0)
        s = _dot_nt(qs, kb16)
        ps, ms, ls = [], [], []
        for h in range(KV_HEADS):
            sh = s[h * BAND:(h + 1) * BAND] + bias_ref[bias, h * BAND:(h + 1) * BAND, :]
            mh = jnp.max(sh, axis=-1, keepdims=True)
            ph = jnp.exp2(sh - mh)
            ps.append(ph.astype(BF16))
            ms.append(mh)
            ls.append(jnp.sum(ph, axis=-1, keepdims=True))
        r = _dot(jnp.concatenate(ps, axis=0), vb16)
        rs = [r[h * BAND:(h + 1) * BAND] for h in range(KV_HEADS)]
        return _select_rows(hid, rs), _select_rows(hid, ms), _select_rows(hid, ls)

    def store_stats(rows, o, m, l):
        for c in range(nslab):
            ls = slice(c * LANES, (c + 1) * LANES)
            acc[c, rows, :] = o[:, ls]
            mrun[c, rows, :] = m[:, ls]
            lrun[c, rows, :] = l[:, ls]

    def merge_stats(rows, o_u, m_u, l_u):
        m_old = wide(mrun, rows)
        m_n = jnp.maximum(m_old, m_u)
        a_old = jnp.exp2(m_old - m_n)
        a_new = jnp.exp2(m_u - m_n)
        store_stats(rows, wide(acc, rows) * a_old + o_u * a_new, m_n, wide(lrun, rows) * a_old + l_u * a_new)

    def unit0(n, kb, vb, bias):
        q0 = n * BAND if isinstance(n, int) else pl.multiple_of(n * BAND, BAND)
        stats = softmax_unit(wide(q_ref, pl.ds(q0, BAND)), kb, vb, bias)
        sub = BAND // RES
        for j, x in enumerate(stats):
            for c in range(nslab):
                tmp[j * nslab + c, pl.ds(q0, BAND), :] = x[:, c * LANES:(c + 1) * LANES]
        for r in range(RES):
            src = pl.ds(q0 + r, sub, stride=RES)
            dst = pl.ds(r * quarter + n * sub, sub)
            for c in range(nslab):
                acc[c, dst, :] = tmp[c, src, :]
                mrun[c, dst, :] = tmp[nslab + c, src, :]
                lrun[c, dst, :] = tmp[2 * nslab + c, src, :]

    kb0 = jnp.concatenate([wide(ktail, pl.ds(0, BAND)), wide(k_ref, pl.ds(0, BAND))], axis=0)
    vb0 = jnp.concatenate([wide(vtail, pl.ds(0, BAND)), wide(v_ref, pl.ds(0, BAND))], axis=0)
    unit0(0, kb0, vb0, jnp.where(first, 1, 0))

    def body0(n, carry):
        keys = pl.ds(pl.multiple_of((n - 1) * BAND, BAND), 2 * BAND)
        unit0(n, wide(k_ref, keys), wide(v_ref, keys), 0)
        return carry

    lax.fori_loop(1, SPAN // BAND, body0, 0, unroll=5)

    def body1(s, carry):
        bias = 2 + jnp.where(jnp.logical_and(first, s == 0), 1, 0)
        for r in range(RES):
            qb = wide(q_ref, pl.ds(s * (BAND * RES) + r, BAND, stride=RES), nslab)
            keys = pl.ds(r * 2 * quarter + quarter + (s - 1) * BAND, 2 * BAND)
            o_u, m_u, l_u = softmax_unit(qb, wide(k4, keys), wide(v4, keys), bias)
            merge_stats(pl.ds(r * quarter + s * BAND, BAND), o_u, m_u, l_u)
        return carry

    lax.fori_loop(0, quarter // BAND, body1, 0, unroll=2)

    def body2(r, carry):
        bias = 4 + jnp.where(first, 1, 0)
        for a in range(DILS[2] // RES):
            rows = pl.ds(r * quarter + a, BAND, stride=RES)
            keys = pl.ds(r * 2 * quarter + a, 2 * BAND, stride=RES)
            o_u, m_u, l_u = softmax_unit(wide(q24, rows), wide(k4, keys), wide(v4, keys), bias)
            merge_stats(rows, o_u, m_u, l_u)
        return carry

    lax.fori_loop(0, RES, body2, 0, unroll=2)

    for r in range(RES):
        rows = pl.ds(r * quarter, quarter)
        for c in range(nslab):
            tmp[c, pl.ds(r, quarter, stride=RES), :] = acc[c, rows, :] / lrun[c, rows, :]
    for c in range(nslab):
        o_ref[:, c * LANES:(c + 1) * LANES] = tmp[c].astype(BF16)
    ktail[...] = k_ref[:, SPAN - BAND:SPAN, :]
    vtail[...] = v_ref[:, SPAN - BAND:SPAN, :]


def _attn_prompt(q, k, v):
    ns = SEQ // SPAN
    nslab = KV_W // LANES
    rows = BATCH * SEQ

    def slab(n):
        return pl.BlockSpec((n, SPAN, LANES), lambda b, i: (0, b * ns + i, 0))

    return pl.pallas_call(
        _attn_prompt_kernel,
        grid=(BATCH, ns),
        in_specs=[slab(Q_W // LANES), slab(nslab), slab(nslab)],
        out_specs=pl.BlockSpec((SPAN, KV_W), lambda b, i: (b * ns + i, 0)),
        out_shape=jax.ShapeDtypeStruct((rows, KV_W), BF16),
        scratch_shapes=[
            pltpu.VMEM((nslab, BAND, LANES), F32),
            pltpu.VMEM((nslab, BAND, LANES), F32),
            pltpu.VMEM((nslab, 2 * SPAN, LANES), F32),
            pltpu.VMEM((nslab, 2 * SPAN, LANES), F32),
            pltpu.VMEM((nslab, SPAN, LANES), F32),
            pltpu.VMEM((nslab, SPAN, LANES), F32),
            pltpu.VMEM((nslab, SPAN, LANES), F32),
            pltpu.VMEM((nslab, SPAN, LANES), F32),
            pltpu.VMEM((3 * nslab, SPAN, LANES), F32),
            pltpu.VMEM((2 * N_DIL, KV_HEADS * BAND, 2 * BAND), F32),
        ],
        compiler_params=_params(("arbitrary", "arbitrary")),
        name="attn_prompt",
    )(q, k, v)


def _merge_ffn_kernel(xs_ref, oas_ref, obs_ref, ocs_ref, gates_ref, x_ref, oa_ref, ob_ref, oc_ref, gate_ref,
                      gqm_ref, gpf_ref, gqf_ref, wa_ref, wb_ref, wc_ref, wo_ref, wg_ref, wu_ref, wd_ref,
                      os_ref, o_ref):
    def run(x_ref, oa_ref, ob_ref, oc_ref, gate_ref, o_ref):
        tm = x_ref.shape[0]
        sub = min(tm, ROW_SUB)

        def sub_tile(r0):
            rows = slice(r0, r0 + sub)

            def gate(j):
                return gate_ref[rows, j * D_MODEL:(j + 1) * D_MODEL].astype(F32)

            merged = gate(0) * _dot(oa_ref[rows, :], wa_ref[...])
            yield
            merged = merged + gate(1) * _dot(ob_ref[rows, :].astype(BF16), wb_ref[...])
            yield
            merged = merged + gate(2) * _dot(oc_ref[rows, :], wc_ref[...])
            yield
            x = x_ref[rows, :] + _rms(_dot(merged.astype(BF16), wo_ref[...]), gqm_ref[...])
            yield
            h = _rms(x, gpf_ref[...]).astype(BF16)
            y = None
            for lo, hi in FF_CHUNKS:
                yield
                a = jax.nn.silu(_dot(h, wg_ref[:, lo:hi]))
                yield
                act = (a * _dot(h, wu_ref[:, lo:hi])).astype(BF16)
                yield
                part = _dot(act, wd_ref[lo:hi, :])
                y = part if y is None else y + part
            yield
            o_ref[rows, :] = x + _rms(y, gqf_ref[...])

        _trace_staggered([sub_tile(r0) for r0 in range(0, tm, sub)], MERGE_FFN_LAG)

    step = pl.program_id(0)

    @pl.when(step == 0)
    def _():
        run(xs_ref, oas_ref, obs_ref, ocs_ref, gates_ref, os_ref)

    @pl.when(step > 0)
    def _():
        run(x_ref, oa_ref, ob_ref, oc_ref, gate_ref, o_ref)


def _merge_ffn(layer, xs, oa_s, ob_s, oc_s, gates_s, x, oa, ob, oc, gates, vecs, wa, wb, wc, wo, wg, wu, wd):
    tm = PROMPT_TM
    rows = x.shape[0]
    ns = xs.shape[0]

    def rowblk(width):
        return pl.BlockSpec((tm, width), lambda i: (jnp.maximum(i - 1, 0), 0))

    def whole(width):
        return pl.BlockSpec((ns, width), lambda i: (0, 0), pipeline_mode=pl.Buffered(1))

    return pl.pallas_call(
        _merge_ffn_kernel,
        grid=(1 + rows // tm,),
        in_specs=[
            whole(D_MODEL), whole(A_WIDTH), whole(KV_W), whole(C_WIDTH), whole(3 * D_MODEL),
            rowblk(D_MODEL), rowblk(A_WIDTH), rowblk(KV_W), rowblk(C_WIDTH), rowblk(3 * D_MODEL),
            _vec_spec(layer, "gqm"), _vec_spec(layer, "gpf"), _vec_spec(layer, "gqf"),
            _layer_spec(layer, (A_WIDTH, D_MODEL)), _layer_spec(layer, (KV_W, D_MODEL)),
            _layer_spec(layer, (C_WIDTH, D_MODEL)), _layer_spec(layer, (D_MODEL, D_MODEL)),
            _layer_spec(layer, (D_MODEL, D_FF)), _layer_spec(layer, (D_MODEL, D_FF)),
            _layer_spec(layer, (D_FF, D_MODEL)),
        ],
        out_specs=[pl.BlockSpec((ns, D_MODEL), lambda i: (0, 0)), rowblk(D_MODEL)],
        out_shape=[jax.ShapeDtypeStruct((ns, D_MODEL), F32), jax.ShapeDtypeStruct((rows, D_MODEL), F32)],
        compiler_params=_params(("arbitrary",)),
        name="merge_ffn",
    )(xs, oa_s, ob_s, oc_s, gates_s, x, oa, ob, oc, gates, vecs, vecs, vecs, wa, wb, wc, wo, wg, wu, wd)


def _inproj_sample_kernel(ws_ref, bs_ref, x_ref, gpre_ref, w_ref, lng_ref, lnb_ref, cw_ref, cbias_ref, bg_ref,
                          p1_ref, p2_ref,
                          oa_ref, q_ref, k_ref, v_ref, oc_ref, gate_ref, vn_ref, cz_ref, *, layer):
    n = SAMPLE_ROWS
    h = _rms(x_ref[...], gpre_ref[...]).astype(BF16)

    def proj(off, width):
        return _dot(h, w_ref[:, off:off + width])

    u = jax.nn.gelu(proj(OFF_U, A_WIDTH))
    vn = _layer_norm(jax.nn.gelu(proj(OFF_V, A_WIDTH)), lng_ref[...], lnb_ref[...])
    vn_ref[...] = vn
    t = lax.broadcasted_iota(jnp.int32, (n, 1), 0) & (DEC_SEQ - 1)
    for g in range(A_GROUPS):
        cs = slice(g * A_GROUP_DIM, (g + 1) * A_GROUP_DIM)
        base = (layer * A_GROUPS + g) * DEC_SEQ
        vg = vn[:, cs]
        sa = _select_rows(t, [bs_ref[base + tt] for tt in range(DEC_SEQ)])
        for k in range(DEC_SEQ):
            coef = jnp.zeros((n, 1), F32)
            for tt in range(k, DEC_SEQ):
                coef = jnp.where(t == tt, ws_ref[(base + tt) * DEC_SEQ + tt - k], coef)
            sa = sa + coef * (vg if k == 0 else pltpu.roll(vg, k, 0))
        oa_ref[:, cs] = (u[:, cs] * sa).astype(BF16)

    q_ref[...] = proj(OFF_Q, Q_W)
    kv = proj(OFF_K, 2 * KV_W)
    k_ref[...] = kv[:, 0:KV_W]
    v_ref[...] = kv[:, KV_W:2 * KV_W]

    cz = proj(OFF_CC, C_WIDTH) * proj(OFF_CX, C_WIDTH)
    cz_ref[...] = cz
    z1 = jnp.where(t >= 1, pltpu.roll(cz, 1, 0), p1_ref[...])
    z2 = jnp.where(t >= 2, pltpu.roll(cz, 2, 0), p2_ref[...])
    y = cbias_ref[...] + cw_ref[0:1, :] * z2 + cw_ref[1:2, :] * z1 + cw_ref[2:3, :] * cz
    oc_ref[...] = (proj(OFF_CB, C_WIDTH) * y).astype(BF16)

    for j in range(3):
        cs = slice(j * D_MODEL, (j + 1) * D_MODEL)
        gate_ref[:, cs] = jax.nn.sigmoid(proj(OFF_G + j * D_MODEL, D_MODEL) + bg_ref[:, cs]).astype(BF16)


def _inproj_kernel(ws4_ref, bs4_ref, xs_ref, p1_ref, p2_ref, x_ref, gpre_ref, w_ref, lng_ref, lnb_ref, ws_ref,
                   bst_ref, cw_ref, cbias_ref, bg_ref,
                   oa_s, q_s, k_s, v_s, oc_s, gate_s, vn_s, cz_s,
                   oa_ref, q_ref, k_ref, v_ref, kt_ref, vt_ref, oc_ref, gate_ref, cs_ref, prev_ref, *, layer):
    step = pl.program_id(0)

    @pl.when(step == 0)
    def _():
        _inproj_sample_kernel(ws4_ref, bs4_ref, xs_ref, gpre_ref, w_ref, lng_ref, lnb_ref, cw_ref, cbias_ref, bg_ref,
                              p1_ref, p2_ref, oa_s, q_s, k_s, v_s, oc_s, gate_s, vn_s, cz_s, layer=layer)

    @pl.when(step > 0)
    def _():
        _inproj_prompt_kernel(x_ref, gpre_ref, w_ref, lng_ref, lnb_ref, ws_ref, bst_ref, cw_ref, cbias_ref, bg_ref,
                              oa_ref, q_ref, k_ref, v_ref, kt_ref, vt_ref, oc_ref, gate_ref, cs_ref, prev_ref,
                              tile=(step - 1) % (SEQ // PROMPT_TM))


def _inproj(layer, ws4, bs4, xs, p1, p2, x, vecs, w_in, ws, bs_t, cw):
    tm = PROMPT_TM
    nt = SEQ // tm
    rows = BATCH * SEQ
    n = SAMPLE_ROWS
    smem = pl.BlockSpec(memory_space=pltpu.SMEM)

    def tile_of(s):
        return jnp.maximum(s - 1, 0)

    def rowblk(width):
        return pl.BlockSpec((tm, width), lambda s: (tile_of(s), 0))

    def slab(k):
        return pl.BlockSpec((k, tm, LANES), lambda s: (0, tile_of(s), 0))

    def whole(width):
        return pl.BlockSpec((n, width), lambda s: (0, 0))

    win_spec = pl.BlockSpec((1, KV_W, tm),
                            lambda s: (tile_of(s) // nt, 0, jnp.maximum(tile_of(s) % nt - (nt - WIN // tm), 0)))
    sample_out = [(A_WIDTH, BF16), (Q_W, F32), (KV_W, F32), (KV_W, F32), (C_WIDTH, BF16),
                  (3 * D_MODEL, BF16), (A_WIDTH, F32), (C_WIDTH, F32)]
    return pl.pallas_call(
        functools.partial(_inproj_kernel, layer=layer),
        grid=(1 + BATCH * nt,),
        in_specs=[
            smem, smem,
            pl.BlockSpec((n, D_MODEL), lambda s: (0, 0), pipeline_mode=pl.Buffered(1)),
            _layer_spec(layer, (n, C_WIDTH)), _layer_spec(layer, (n, C_WIDTH)),
            rowblk(D_MODEL),
            _vec_spec(layer, "gpm"),
            _layer_spec(layer, (D_MODEL, IN_W)),
            _vec_spec(layer, "lng"),
            _vec_spec(layer, "lnb"),
            _layer_spec(layer, (A_GROUPS, CHUNK, CHUNK)),
            _layer_spec(layer, (CHUNK, A_GROUPS)),
            _layer_spec(layer, (CONV_W, C_WIDTH)),
            _vec_spec(layer, "cbias"),
            _vec_spec(layer, "bg"),
        ],
        out_specs=[whole(w) for w, _ in sample_out] + [
            rowblk(A_WIDTH),
            slab(Q_W // LANES),
            slab(KV_W // LANES),
            slab(KV_W // LANES),
            win_spec,
            win_spec,
            rowblk(C_WIDTH),
            rowblk(3 * D_MODEL),
            pl.BlockSpec((1, SUBLANES, C_WIDTH), lambda s: (tile_of(s) // nt, 0, 0)),
        ],
        out_shape=[jax.ShapeDtypeStruct((n, w), dt) for w, dt in sample_out] + [
            jax.ShapeDtypeStruct((rows, A_WIDTH), BF16),
            jax.ShapeDtypeStruct((Q_W // LANES, rows, LANES), F32),
            jax.ShapeDtypeStruct((KV_W // LANES, rows, LANES), F32),
            jax.ShapeDtypeStruct((KV_W // LANES, rows, LANES), F32),
            jax.ShapeDtypeStruct((BATCH, KV_W, WIN), F32),
            jax.ShapeDtypeStruct((BATCH, KV_W, WIN), F32),
            jax.ShapeDtypeStruct((rows, C_WIDTH), BF16),
            jax.ShapeDtypeStruct((rows, 3 * D_MODEL), BF16),
            jax.ShapeDtypeStruct((BATCH, SUBLANES, C_WIDTH), F32),
        ],
        scratch_shapes=[pltpu.VMEM((SUBLANES, C_WIDTH), F32)],
        compiler_params=_params(("arbitrary",)),
        name="inproj",
    )(ws4, bs4, xs, p1, p2, x, vecs, w_in, vecs, vecs, ws, bs_t, cw, vecs, vecs)


def _attn_sample_kernel(qh_ref, kn_ref, vn_ref, kt_ref, vt_ref, o_ref, bias_ref):
    nr = N_DIL * SUBLANES
    width = MAX_WINDOW

    @pl.when(pl.program_id(0) == 0)
    def _():
        row = lax.broadcasted_iota(jnp.int32, (nr, width), 0)
        pos = lax.broadcasted_iota(jnp.int32, (nr, width), 1)
        grp = row >> 3
        delta = width + (row & (SUBLANES - 1)) - pos
        dil_m1 = _select_rows(grp, [d - 1 for d in DILS])
        reach = _select_rows(grp, [BAND * d for d in DILS])
        ok = jnp.logical_and((delta & dil_m1) == 0, delta <= reach)
        for h in range(KV_HEADS):
            slope = _select_rows(grp, [jnp.float32(SLOPES[g][h]) for g in range(N_DIL)])
            bias_ref[h * nr:(h + 1) * nr, :] = jnp.where(ok, -(slope * delta.astype(F32)), NEG)

    ridx = lax.broadcasted_iota(jnp.int32, (nr, 1), 0)
    g_of = ridx >> 3
    t_of = ridx & (SUBLANES - 1)
    scale = HEAD_DIM ** -0.5
    slopes = [_select_rows(g_of, [jnp.float32(SLOPES[g][h]) for g in range(N_DIL)]) for h in range(KV_HEADS)]
    for e in range(qh_ref.shape[0]):
        qs = [qh_ref[e, h] * scale for h in range(KV_HEADS)]
        s = jnp.concatenate([_dot(qs[h].astype(BF16), kt_ref[0, e, h].astype(BF16)) for h in range(KV_HEADS)],
                            axis=0)
        s = s + bias_ref[...]
        m = jnp.max(s, axis=-1, keepdims=True)
        s_new = []
        for n in range(DEC_SEQ):
            dn = t_of - n
            ok = jnp.logical_or(dn == 0, jnp.logical_and(g_of == 0, dn > 0))
            rows = []
            for h in range(KV_HEADS):
                raw = jnp.sum(qs[h] * kn_ref[e, h, n:n + 1, :], axis=-1, keepdims=True)
                rows.append(jnp.where(ok, raw - slopes[h] * dn.astype(F32), NEG))
            sn = jnp.concatenate(rows, axis=0)
            s_new.append(sn)
            m = jnp.maximum(m, sn)
        p = jnp.exp(s - m)
        l_all = jnp.sum(p, axis=-1, keepdims=True)
        p16 = p.astype(BF16)
        p_new = [jnp.exp(sn - m) for sn in s_new]
        for pn in p_new:
            l_all = l_all + pn
        for h in range(KV_HEADS):
            hr = slice(h * nr, (h + 1) * nr)
            r = _dot_nt(p16[hr], vt_ref[0, e, h].astype(BF16))
            for n in range(DEC_SEQ):
                r = r + p_new[n][hr] * vn_ref[e, h, n:n + 1, :]
            m_h = m[hr]
            l_h = l_all[hr]
            parts = [(r[g * SUBLANES:(g + 1) * SUBLANES], m_h[g * SUBLANES:(g + 1) * SUBLANES],
                      l_h[g * SUBLANES:(g + 1) * SUBLANES]) for g in range(N_DIL)]
            m_all = jnp.maximum(jnp.maximum(parts[0][1], parts[1][1]), parts[2][1])
            num = jnp.zeros((SUBLANES, HEAD_DIM), F32)
            den = jnp.zeros((SUBLANES, 1), F32)
            for o_g, m_g, l_g in parts:
                w = jnp.exp(m_g - m_all)
                num = num + o_g * w
                den = den + l_g * w
            o_ref[e, :, h * HEAD_DIM:(h + 1) * HEAD_DIM] = (num / den)[0:DEC_SEQ]


def _attn_sample(layer, qh, knh, vnh, ckt, cvt):
    nr = N_DIL * SUBLANES
    eb = SAMPLE_EB
    cache_spec = pl.BlockSpec((1, eb, KV_HEADS, HEAD_DIM, MAX_WINDOW), lambda b: (layer, b, 0, 0, 0))
    new_spec = pl.BlockSpec((eb, KV_HEADS, SUBLANES, HEAD_DIM), lambda b: (b, 0, 0, 0))
    return pl.pallas_call(
        _attn_sample_kernel,
        grid=(DEC_BATCH // eb,),
        in_specs=[pl.BlockSpec((eb, KV_HEADS, nr, HEAD_DIM), lambda b: (b, 0, 0, 0)),
                  new_spec, new_spec, cache_spec, cache_spec],
        out_specs=pl.BlockSpec((eb, DEC_SEQ, KV_W), lambda b: (b, 0, 0)),
        out_shape=jax.ShapeDtypeStruct((DEC_BATCH, DEC_SEQ, KV_W), F32),
        scratch_shapes=[pltpu.VMEM((KV_HEADS * nr, MAX_WINDOW), F32)],
        compiler_params=_params(("arbitrary",)),
        name="attn_sample",
    )(qh, knh, vnh, ckt, cvt)


def kernel(x_prompt, x_sample, cache_k_win, cache_v_win, state_conv, g_pre_mix, g_post_mix, g_pre_ffn, g_post_ffn,
           w_in, a_ln_g, a_ln_b, a_ws, a_bs, c_conv_w, c_conv_b, w_br_a, w_br_b, w_br_c, b_gate, w_o,
           w_ff_gate, w_ff_up, w_ff_down):
    xp = x_prompt.reshape(BATCH * SEQ, D_MODEL)
    xs = x_sample.reshape(SAMPLE_ROWS, D_MODEL)
    ckt = jnp.transpose(cache_k_win, (0, 1, 3, 4, 2))
    cvt = jnp.transpose(cache_v_win, (0, 1, 3, 4, 2))

    w_in_b = w_in.astype(BF16)
    wa, wb, wc, wo = (w.astype(BF16) for w in (w_br_a, w_br_b, w_br_c, w_o))
    wg, wu, wd = (w.astype(BF16) for w in (w_ff_gate, w_ff_up, w_ff_down))
    vecs = jnp.concatenate([b_gate, g_pre_mix, g_post_mix, g_pre_ffn, g_post_ffn, a_ln_g, a_ln_b, c_conv_b],
                           axis=1).reshape(DEPTH, 1, -1)
    bs_t = jnp.swapaxes(a_bs, 1, 2)
    ws4 = a_ws[:, :, :DEC_SEQ, :DEC_SEQ].reshape(-1)
    bs4 = a_bs[:, :, :DEC_SEQ].reshape(-1)
    zeros = jnp.zeros((DEPTH, DEC_BATCH, 1, C_WIDTH), F32)
    p1 = jnp.concatenate([state_conv[:, :, 1:2], zeros, zeros, zeros], axis=2).reshape(DEPTH, SAMPLE_ROWS, C_WIDTH)
    p2 = jnp.concatenate([state_conv, zeros, zeros], axis=2).reshape(DEPTH, SAMPLE_ROWS, C_WIDTH)

    kp_l, vp_l, ks_l, vs_l, cp_l, cs_l, av_l = [], [], [], [], [], [], []
    for l in range(DEPTH):
        (oa_s, q_s, k_s, v_s, oc_s, gates_s, vn_s, cz_s,
         oa, q, k, v, kt, vt, oc, gates, cs8) = _inproj(l, ws4, bs4, xs, p1, p2, xp, vecs, w_in_b, a_ws, bs_t, c_conv_w)
        ob = _attn_prompt(q, k, v)

        def window(t):
            return jnp.transpose(t.reshape(BATCH, KV_HEADS, HEAD_DIM, WIN), (0, 3, 1, 2))

        kp_l.append(window(kt))
        vp_l.append(window(vt))
        cp_l.append(cs8[:, SUBLANES - (CONV_W - 1):])


        pad_t = ((0, 0), (0, 0), (0, 0), (0, SUBLANES - DEC_SEQ), (0, 0))
        qh = q_s.reshape(DEC_BATCH, DEC_SEQ, N_DIL, KV_HEADS, HEAD_DIM).transpose(0, 3, 2, 1, 4)
        qh = jnp.pad(qh, pad_t).reshape(DEC_BATCH, KV_HEADS, N_DIL * SUBLANES, HEAD_DIM)

        def new_rows(a):
            a = a.reshape(DEC_BATCH, DEC_SEQ, KV_HEADS, HEAD_DIM).transpose(0, 2, 1, 3)
            return jnp.pad(a, pad_t[1:])

        ob_s = _attn_sample(l, qh, new_rows(k_s), new_rows(v_s), ckt, cvt)
        ob_s = ob_s.reshape(SAMPLE_ROWS, KV_W)
        xs, xp = _merge_ffn(l, xs, oa_s, ob_s, oc_s, gates_s, xp, oa, ob, oc, gates, vecs,
                            wa, wb, wc, wo, wg, wu, wd)

        ks_l.append(k_s.reshape(DEC_BATCH, DEC_SEQ, KV_HEADS, HEAD_DIM))
        vs_l.append(v_s.reshape(DEC_BATCH, DEC_SEQ, KV_HEADS, HEAD_DIM))
        cs_l.append(cz_s.reshape(DEC_BATCH, DEC_SEQ, C_WIDTH)[:, DEC_SEQ - (CONV_W - 1):])
        av_l.append(vn_s.reshape(DEC_BATCH, DEC_SEQ, A_WIDTH))

    return (xp.reshape(BATCH, SEQ, D_MODEL), xs.reshape(DEC_BATCH, DEC_SEQ, D_MODEL),
            jnp.stack(kp_l), jnp.stack(vp_l), jnp.stack(ks_l), jnp.stack(vs_l),
            jnp.stack(cp_l), jnp.stack(cs_l), jnp.stack(av_l))
```

```python
import functools

import jax
import jax.numpy as jnp
from jax import lax
from jax.experimental import pallas as pl
from jax.experimental.pallas import tpu as pltpu

F32 = jnp.float32
BF16 = jnp.bfloat16

D_MODEL = 1024
BATCH = 2
SEQ = 8192
DEPTH = 2
DEC_BATCH = 32
DEC_SEQ = 4
CHUNK = 128
A_GROUPS = 4
A_GROUP_DIM = 128
A_WIDTH = A_GROUPS * A_GROUP_DIM
HEAD_DIM = 64
KV_HEADS = 4
DILS = (1, 4, 16)
N_DIL = len(DILS)
Q_W = N_DIL * KV_HEADS * HEAD_DIM
KV_W = KV_HEADS * HEAD_DIM
BAND = 128
MAX_WINDOW = 2048
C_WIDTH = 512
CONV_W = 3
D_FF = 2816
EPS = 1e-6

OFF_U = 0
OFF_V = OFF_U + A_WIDTH
OFF_Q = OFF_V + A_WIDTH
OFF_K = OFF_Q + Q_W
OFF_VV = OFF_K + KV_W
OFF_CX = OFF_VV + KV_W
OFF_CB = OFF_CX + C_WIDTH
OFF_CC = OFF_CB + C_WIDTH
OFF_G = OFF_CC + C_WIDTH
IN_W = OFF_G + 3 * D_MODEL

LANES = 128
SUBLANES = 8
VMEM_LIMIT = 56 * 1024 * 1024
NEG = -1e30

SAMPLE_ROWS = DEC_BATCH * DEC_SEQ
WIN = min(MAX_WINDOW, SEQ)
SPAN = BAND * DILS[-1]
RES = DILS[1]
PROMPT_TM = 512
ROW_SUB = 256
INPROJ_SUB = ROW_SUB
INPROJ_LAG = 2
MERGE_FFN_LAG = 3
CAST_ROWS = (16, 16, 16, 32, 32, 32, 128)
WIN_CAST_ROWS = 32
SAMPLE_EB = 4
MXU_N = 256
FF_CHUNKS = ((0, 6 * MXU_N), (6 * MXU_N, D_FF))
LOG2E = 1.4426950408889634
QSCALE = LOG2E * HEAD_DIM ** -0.5

SLOPES = tuple(
    tuple(2.0 ** (-8.0 * (g * KV_HEADS + h + 1) / (N_DIL * KV_HEADS)) for h in range(KV_HEADS))
    for g in range(N_DIL)
)


def _rms(x, g):
    return x * lax.rsqrt(jnp.mean(x * x, axis=-1, keepdims=True) + EPS) * g


def _layer_norm(x, g, b):
    mu = jnp.mean(x, axis=-1, keepdims=True)
    xc = x - mu
    return xc * lax.rsqrt(jnp.mean(xc * xc, axis=-1, keepdims=True) + EPS) * g + b


def _dot(a, b):
    return jnp.dot(a, b, preferred_element_type=F32)


def _dot_nt(a, b):
    return lax.dot_general(a, b, (((1,), (1,)), ((), ())), preferred_element_type=F32)


def _select_rows(hid, vals):
    out = vals[-1]
    for h in range(len(vals) - 2, -1, -1):
        out = jnp.where(hid == h, vals[h], out)
    return out


def _layer_spec(layer, shape):
    nd = len(shape)
    return pl.BlockSpec((None,) + tuple(shape), lambda *_: (layer,) + (0,) * nd, pipeline_mode=pl.Buffered(1))


VEC_SLOTS = {"bg": (3 * D_MODEL, 0), "gpm": (D_MODEL, 3), "gqm": (D_MODEL, 4), "gpf": (D_MODEL, 5),
             "gqf": (D_MODEL, 6), "lng": (A_WIDTH, 14), "lnb": (A_WIDTH, 15), "cbias": (C_WIDTH, 16)}


def _vec_spec(layer, name):
    width, idx = VEC_SLOTS[name]
    return pl.BlockSpec((None, 1, width), lambda *_: (layer, 0, idx), pipeline_mode=pl.Buffered(1))


def _trace_staggered(stage_gens, lag):
    live = [True] * len(stage_gens)
    tick = 0
    while any(live):
        for j, gen in enumerate(stage_gens):
            if live[j] and tick >= j * lag:
                live[j] = next(gen, "done") != "done"
        tick += 1


def _resident(shape):
    nd = len(shape)
    return pl.BlockSpec(tuple(shape), lambda *_: (0,) * nd, pipeline_mode=pl.Buffered(1))


def _cast_specs(layer, shape, rows_per_step):
    last = shape[0] // rows_per_step - 1

    def piece(s):
        return jnp.minimum(jnp.maximum(s - 1, 0), last)

    return (pl.BlockSpec((None, rows_per_step, shape[1]), lambda s: (layer, piece(s), 0)),
            pl.BlockSpec((rows_per_step, shape[1]), lambda s: (piece(s), 0)))


def _params(sem):
    return pltpu.CompilerParams(dimension_semantics=sem, vmem_limit_bytes=VMEM_LIMIT)


def _inproj_prompt_kernel(x_ref, gpre_ref, w_ref, lng_ref, lnb_ref, ws_ref, bs_ref, cw_ref,
                          cbias_ref, bg_ref,
                          oa_ref, q_ref, k_ref, v_ref, kt_ref, vt_ref, oc_ref, gate_ref, cs_ref, prev_ref, *, tile):
    tm = x_ref.shape[0]

    @pl.when(tile == 0)
    def _():
        prev_ref[...] = jnp.zeros_like(prev_ref)

    row = lax.broadcasted_iota(jnp.int32, (CHUNK, CHUNK), 0)
    col = lax.broadcasted_iota(jnp.int32, (CHUNK, CHUNK), 1)
    wt = [jnp.where(row >= col, ws_ref[g], 0.0).astype(BF16) for g in range(A_GROUPS)]
    rsub = lax.broadcasted_iota(jnp.int32, (INPROJ_SUB, 1), 0)

    carry = {"tail": prev_ref[...]}

    def sub_tile(r0):
        rows = slice(r0, r0 + INPROJ_SUB)
        h = _rms(x_ref[rows, :], gpre_ref[...]).astype(BF16)

        def proj(off, width):
            return _dot(h, w_ref[:, off:off + width])

        yield
        u = jax.nn.gelu(proj(OFF_U, A_WIDTH))
        yield
        vn = _layer_norm(jax.nn.gelu(proj(OFF_V, A_WIDTH)), lng_ref[...], lnb_ref[...])
        for j in range(3):
            yield
            cs = slice(j * D_MODEL, (j + 1) * D_MODEL)
            gate_ref[rows, cs] = jax.nn.sigmoid(proj(OFF_G + j * D_MODEL, D_MODEL) + bg_ref[:, cs]).astype(BF16)
        yield
        for g in range(A_GROUPS):
            cs = slice(g * A_GROUP_DIM, (g + 1) * A_GROUP_DIM)
            for c in range(0, INPROJ_SUB // CHUNK, 2):
                pair = jnp.concatenate([vn[(c + j) * CHUNK:(c + j + 1) * CHUNK, cs] for j in range(2)], axis=1)
                sa2 = _dot(wt[g], pair.astype(BF16)) + bs_ref[:, g:g + 1]
                for j in range(2):
                    rs = slice((c + j) * CHUNK, (c + j + 1) * CHUNK)
                    sa = sa2[:, j * A_GROUP_DIM:(j + 1) * A_GROUP_DIM]
                    oa_ref[r0 + (c + j) * CHUNK:r0 + (c + j + 1) * CHUNK, cs] = (u[rs, cs] * sa).astype(BF16)
        yield
        cz = proj(OFF_CC, C_WIDTH) * proj(OFF_CX, C_WIDTH)
        p2 = carry["tail"][SUBLANES - 2:SUBLANES - 1, :]
        p1 = carry["tail"][SUBLANES - 1:SUBLANES, :]
        carry["tail"] = cz[INPROJ_SUB - SUBLANES:INPROJ_SUB, :]
        z1 = jnp.where(rsub == 0, p1, pltpu.roll(cz, 1, 0))
        z2 = jnp.where(rsub == 0, p2, jnp.where(rsub == 1, p1, pltpu.roll(cz, 2, 0)))
        y = cbias_ref[...] + cw_ref[0:1, :] * z2 + cw_ref[1:2, :] * z1 + cw_ref[2:3, :] * cz
        yield
        oc_ref[rows, :] = (proj(OFF_CB, C_WIDTH) * y).astype(BF16)
        yield
        qv = proj(OFF_Q, Q_W)
        for c in range(Q_W // LANES):
            q_ref[c, rows, :] = qv[:, c * LANES:(c + 1) * LANES] * QSCALE
        yield
        kv = proj(OFF_K, 2 * KV_W)
        for c in range(KV_W // LANES):
            k_ref[c, rows, :] = kv[:, c * LANES:(c + 1) * LANES]
            v_ref[c, rows, :] = kv[:, KV_W + c * LANES:KV_W + (c + 1) * LANES]
        kt_ref[0, :, rows] = kv[:, 0:KV_W].T
        vt_ref[0, :, rows] = kv[:, KV_W:2 * KV_W].T

    _trace_staggered([sub_tile(r0) for r0 in range(0, tm, INPROJ_SUB)], INPROJ_LAG)
    prev_ref[...] = carry["tail"]
    cs_ref[0] = carry["tail"]


def _attn_prompt_kernel(q_ref, k_ref, v_ref, o_ref, ktail, vtail, k4, v4, q24, acc, mrun, lrun, tmp, bias_ref):
    b = pl.program_id(0)
    i = pl.program_id(1)
    first = i == 0
    nslab = KV_W // LANES
    quarter = SPAN // RES

    @pl.when(jnp.logical_and(b == 0, first))
    def _():
        rr = lax.broadcasted_iota(jnp.int32, (KV_HEADS * BAND, 2 * BAND), 0)
        kk = lax.broadcasted_iota(jnp.int32, (KV_HEADS * BAND, 2 * BAND), 1)
        dist = (rr & (BAND - 1)) + BAND - kk
        hrow = rr >> 7
        ok = jnp.logical_and(dist >= 0, dist <= BAND)
        for g, dil in enumerate(DILS):
            coef = _select_rows(hrow, [jnp.float32(SLOPES[g][h] * dil * LOG2E) for h in range(KV_HEADS)])
            full = jnp.where(ok, -(coef * dist.astype(F32)), NEG)
            bias_ref[2 * g] = full
            bias_ref[2 * g + 1] = jnp.where(kk >= BAND, full, NEG)

    @pl.when(first)
    def _():
        for r in range(RES):
            base = r * 2 * quarter
            k4[:, base:base + quarter, :] = jnp.zeros((nslab, quarter, LANES), F32)
            v4[:, base:base + quarter, :] = jnp.zeros((nslab, quarter, LANES), F32)
        ktail[...] = jnp.zeros_like(ktail)
        vtail[...] = jnp.zeros_like(vtail)

    @pl.when(i > 0)
    def _():
        for r in range(RES):
            base = r * 2 * quarter
            k4[:, base:base + quarter, :] = k4[:, base + quarter:base + 2 * quarter, :]
            v4[:, base:base + quarter, :] = v4[:, base + quarter:base + 2 * quarter, :]

    for r in range(RES):
        base = r * 2 * quarter + quarter
        for c in range(nslab):
            k4[c, base:base + quarter, :] = k_ref[c, pl.ds(r, quarter, stride=RES), :]
            v4[c, base:base + quarter, :] = v_ref[c, pl.ds(r, quarter, stride=RES), :]
            q24[c, r * quarter:(r + 1) * quarter, :] = q_ref[2 * nslab + c, pl.ds(r, quarter, stride=RES), :]

    lane = lax.broadcasted_iota(jnp.int32, (1, KV_W), 1)
    hid = lane >> 6
    headmask = [jnp.where(hid == h, 1.0, 0.0).astype(BF16) for h in range(KV_HEADS)]

    def wide(ref, rows, slab0=0):
        return jnp.concatenate([ref[slab0 + c, rows, :] for c in range(nslab)], axis=1)

    def softmax_unit(qb, kb, vb, bias):
        qb16 = qb.astype(BF16)
        kb16 = kb.astype(BF16)
        vb16 = vb.astype(BF16)
        qs = jnp.concatenate([qb16 * headmask[h] for h in range(KV_HEADS)], axis=0)
        s = _dot_nt(qs, kb16)
        ps, ms, ls = [], [], []
        for h in range(KV_HEADS):
            sh = s[h * BAND:(h + 1) * BAND] + bias_ref[bias, h * BAND:(h + 1) * BAND, :]
            mh = jnp.max(sh, axis=-1, keepdims=True)
            ph = jnp.exp2(sh - mh)
            ps.append(ph.astype(BF16))
            ms.append(mh)
            ls.append(jnp.sum(ph, axis=-1, keepdims=True))
        r = _dot(jnp.concatenate(ps, axis=0), vb16)
        rs = [r[h * BAND:(h + 1) * BAND] for h in range(KV_HEADS)]
        return _select_rows(hid, rs), _select_rows(hid, ms), _select_rows(hid, ls)

    def store_stats(rows, o, m, l):
        for c in range(nslab):
            ls = slice(c * LANES, (c + 1) * LANES)
            acc[c, rows, :] = o[:, ls]
            mrun[c, rows, :] = m[:, ls]
            lrun[c, rows, :] = l[:, ls]

    def merge_stats(rows, o_u, m_u, l_u):
        m_old = wide(mrun, rows)
        m_n = jnp.maximum(m_old, m_u)
        a_old = jnp.exp2(m_old - m_n)
        a_new = jnp.exp2(m_u - m_n)
        store_stats(rows, wide(acc, rows) * a_old + o_u * a_new, m_n, wide(lrun, rows) * a_old + l_u * a_new)

    def unit0(n, kb, vb, bias):
        q0 = n * BAND if isinstance(n, int) else pl.multiple_of(n * BAND, BAND)
        stats = softmax_unit(wide(q_ref, pl.ds(q0, BAND)), kb, vb, bias)
        sub = BAND // RES
        for j, x in enumerate(stats):
            for c in range(nslab):
                tmp[j * nslab + c, pl.ds(q0, BAND), :] = x[:, c * LANES:(c + 1) * LANES]
        for r in range(RES):
            src = pl.ds(q0 + r, sub, stride=RES)
            dst = pl.ds(r * quarter + n * sub, sub)
            for c in range(nslab):
                acc[c, dst, :] = tmp[c, src, :]
                mrun[c, dst, :] = tmp[nslab + c, src, :]
                lrun[c, dst, :] = tmp[2 * nslab + c, src, :]

    kb0 = jnp.concatenate([wide(ktail, pl.ds(0, BAND)), wide(k_ref, pl.ds(0, BAND))], axis=0)
    vb0 = jnp.concatenate([wide(vtail, pl.ds(0, BAND)), wide(v_ref, pl.ds(0, BAND))], axis=0)
    unit0(0, kb0, vb0, jnp.where(first, 1, 0))

    def body0(n, carry):
        keys = pl.ds(pl.multiple_of((n - 1) * BAND, BAND), 2 * BAND)
        unit0(n, wide(k_ref, keys), wide(v_ref, keys), 0)
        return carry

    lax.fori_loop(1, SPAN // BAND, body0, 0, unroll=5)

    def body1(s, carry):
        bias = 2 + jnp.where(jnp.logical_and(first, s == 0), 1, 0)
        for r in range(RES):
            qb = wide(q_ref, pl.ds(s * (BAND * RES) + r, BAND, stride=RES), nslab)
            keys = pl.ds(r * 2 * quarter + quarter + (s - 1) * BAND, 2 * BAND)
            o_u, m_u, l_u = softmax_unit(qb, wide(k4, keys), wide(v4, keys), bias)
            merge_stats(pl.ds(r * quarter + s * BAND, BAND), o_u, m_u, l_u)
        return carry

    lax.fori_loop(0, quarter // BAND, body1, 0, unroll=2)

    def body2(r, carry):
        bias = 4 + jnp.where(first, 1, 0)
        for a in range(DILS[2] // RES):
            rows = pl.ds(r * quarter + a, BAND, stride=RES)
            keys = pl.ds(r * 2 * quarter + a, 2 * BAND, stride=RES)
            o_u, m_u, l_u = softmax_unit(wide(q24, rows), wide(k4, keys), wide(v4, keys), bias)
            merge_stats(rows, o_u, m_u, l_u)
        return carry

    lax.fori_loop(0, RES, body2, 0, unroll=2)

    for r in range(RES):
        rows = pl.ds(r * quarter, quarter)
        for c in range(nslab):
            tmp[c, pl.ds(r, quarter, stride=RES), :] = acc[c, rows, :] / lrun[c, rows, :]
    for c in range(nslab):
        o_ref[:, c * LANES:(c + 1) * LANES] = tmp[c].astype(BF16)
    ktail[...] = k_ref[:, SPAN - BAND:SPAN, :]
    vtail[...] = v_ref[:, SPAN - BAND:SPAN, :]


def _attn_prompt(q, k, v):
    ns = SEQ // SPAN
    nslab = KV_W // LANES
    rows = BATCH * SEQ

    def slab(n):
        return pl.BlockSpec((n, SPAN, LANES), lambda b, i: (0, b * ns + i, 0))

    return pl.pallas_call(
        _attn_prompt_kernel,
        grid=(BATCH, ns),
        in_specs=[slab(Q_W // LANES), slab(nslab), slab(nslab)],
        out_specs=pl.BlockSpec((SPAN, KV_W), lambda b, i: (b * ns + i, 0)),
        out_shape=jax.ShapeDtypeStruct((rows, KV_W), BF16),
        scratch_shapes=[
            pltpu.VMEM((nslab, BAND, LANES), F32),
            pltpu.VMEM((nslab, BAND, LANES), F32),
            pltpu.VMEM((nslab, 2 * SPAN, LANES), F32),
            pltpu.VMEM((nslab, 2 * SPAN, LANES), F32),
            pltpu.VMEM((nslab, SPAN, LANES), F32),
            pltpu.VMEM((nslab, SPAN, LANES), F32),
            pltpu.VMEM((nslab, SPAN, LANES), F32),
            pltpu.VMEM((nslab, SPAN, LANES), F32),
            pltpu.VMEM((3 * nslab, SPAN, LANES), F32),
            pltpu.VMEM((2 * N_DIL, KV_HEADS * BAND, 2 * BAND), F32),
        ],
        compiler_params=_params(("arbitrary", "arbitrary")),
        name="attn_prompt",
    )(q, k, v)


def _merge_ffn_kernel(xs_ref, oas_ref, obs_ref, ocs_ref, gates_ref, x_ref, oa_ref, ob_ref, oc_ref, gate_ref,
                      gqm_ref, gpf_ref, gqf_ref, wa_ref, wb_ref, wc_ref, wo_ref, wg_ref, wu_ref, wd_ref, *rest):
    os_ref, o_ref = rest[-3:-1] if len(rest) == 4 else rest
    def run(x_ref, oa_ref, ob_ref, oc_ref, gate_ref, o_ref):
        tm = x_ref.shape[0]
        sub = min(tm, ROW_SUB)

        def sub_tile(r0):
            rows = slice(r0, r0 + sub)

            def gate(j):
                return gate_ref[rows, j * D_MODEL:(j + 1) * D_MODEL].astype(F32)

            merged = gate(0) * _dot(oa_ref[rows, :], wa_ref[...])
            yield
            merged = merged + gate(1) * _dot(ob_ref[rows, :].astype(BF16), wb_ref[...])
            yield
            merged = merged + gate(2) * _dot(oc_ref[rows, :], wc_ref[...])
            yield
            x = x_ref[rows, :] + _rms(_dot(merged.astype(BF16), wo_ref[...]), gqm_ref[...])
            yield
            h = _rms(x, gpf_ref[...]).astype(BF16)
            y = None
            for lo, hi in FF_CHUNKS:
                yield
                a = jax.nn.silu(_dot(h, wg_ref[:, lo:hi]))
                yield
                act = (a * _dot(h, wu_ref[:, lo:hi])).astype(BF16)
                yield
                part = _dot(act, wd_ref[lo:hi, :])
                y = part if y is None else y + part
            yield
            o_ref[rows, :] = x + _rms(y, gqf_ref[...])

        _trace_staggered([sub_tile(r0) for r0 in range(0, tm, sub)], MERGE_FFN_LAG)

    step = pl.program_id(0)

    @pl.when(step == 0)
    def _():
        run(xs_ref, oas_ref, obs_ref, ocs_ref, gates_ref, os_ref)

    @pl.when(step > 0)
    def _():
        run(x_ref, oa_ref, ob_ref, oc_ref, gate_ref, o_ref)
        if len(rest) == 4:
            rest[3][...] = rest[0][...].astype(BF16)


def _merge_ffn(layer, xs, oa_s, ob_s, oc_s, gates_s, x, oa, ob, oc, gates, vecs, weights, w_in_f32):
    tm = PROMPT_TM
    rows = x.shape[0]
    ns = xs.shape[0]

    def rowblk(width):
        return pl.BlockSpec((tm, width), lambda i: (jnp.maximum(i - 1, 0), 0))

    def whole(width):
        return pl.BlockSpec((ns, width), lambda i: (0, 0), pipeline_mode=pl.Buffered(1))

    nxt = [] if w_in_f32 is None else [_cast_specs(layer + 1, (D_MODEL, IN_W), WIN_CAST_ROWS)]
    return pl.pallas_call(
        _merge_ffn_kernel,
        grid=(1 + rows // tm,),
        in_specs=[
            whole(D_MODEL), whole(A_WIDTH), whole(KV_W), whole(C_WIDTH), whole(3 * D_MODEL),
            rowblk(D_MODEL), rowblk(A_WIDTH), rowblk(KV_W), rowblk(C_WIDTH), rowblk(3 * D_MODEL),
            _vec_spec(layer, "gqm"), _vec_spec(layer, "gpf"), _vec_spec(layer, "gqf"),
        ] + [_resident(w.shape) for w in weights] + [c[0] for c in nxt],
        out_specs=[pl.BlockSpec((ns, D_MODEL), lambda i: (0, 0)), rowblk(D_MODEL)] + [c[1] for c in nxt],
        out_shape=[jax.ShapeDtypeStruct((ns, D_MODEL), F32), jax.ShapeDtypeStruct((rows, D_MODEL), F32)]
        + [jax.ShapeDtypeStruct((D_MODEL, IN_W), BF16) for _ in nxt],
        compiler_params=_params(("arbitrary",)),
        name="merge_ffn",
    )(xs, oa_s, ob_s, oc_s, gates_s, x, oa, ob, oc, gates, vecs, vecs, vecs, *weights,
      *([] if w_in_f32 is None else [w_in_f32]))


def _inproj_sample_kernel(ws_ref, bs_ref, x_ref, gpre_ref, w_ref, lng_ref, lnb_ref, cw_ref, cbias_ref, bg_ref,
                          p1_ref, p2_ref,
                          oa_ref, q_ref, k_ref, v_ref, oc_ref, gate_ref, vn_ref, cz_ref, *, layer):
    n = SAMPLE_ROWS
    h = _rms(x_ref[...], gpre_ref[...]).astype(BF16)

    def proj(off, width):
        return _dot(h, w_ref[:, off:off + width])

    u = jax.nn.gelu(proj(OFF_U, A_WIDTH))
    vn = _layer_norm(jax.nn.gelu(proj(OFF_V, A_WIDTH)), lng_ref[...], lnb_ref[...])
    vn_ref[...] = vn
    t = lax.broadcasted_iota(jnp.int32, (n, 1), 0) & (DEC_SEQ - 1)
    for g in range(A_GROUPS):
        cs = slice(g * A_GROUP_DIM, (g + 1) * A_GROUP_DIM)
        base = (layer * A_GROUPS + g) * DEC_SEQ
        vg = vn[:, cs]
        sa = _select_rows(t, [bs_ref[base + tt] for tt in range(DEC_SEQ)])
        for k in range(DEC_SEQ):
            coef = jnp.zeros((n, 1), F32)
            for tt in range(k, DEC_SEQ):
                coef = jnp.where(t == tt, ws_ref[(base + tt) * DEC_SEQ + tt - k], coef)
            sa = sa + coef * (vg if k == 0 else pltpu.roll(vg, k, 0))
        oa_ref[:, cs] = (u[:, cs] * sa).astype(BF16)

    q_ref[...] = proj(OFF_Q, Q_W)
    kv = proj(OFF_K, 2 * KV_W)
    k_ref[...] = kv[:, 0:KV_W]
    v_ref[...] = kv[:, KV_W:2 * KV_W]

    cz = proj(OFF_CC, C_WIDTH) * proj(OFF_CX, C_WIDTH)
    cz_ref[...] = cz
    z1 = jnp.where(t >= 1, pltpu.roll(cz, 1, 0), p1_ref[...])
    z2 = jnp.where(t >= 2, pltpu.roll(cz, 2, 0), p2_ref[...])
    y = cbias_ref[...] + cw_ref[0:1, :] * z2 + cw_ref[1:2, :] * z1 + cw_ref[2:3, :] * cz
    oc_ref[...] = (proj(OFF_CB, C_WIDTH) * y).astype(BF16)

    for j in range(3):
        cs = slice(j * D_MODEL, (j + 1) * D_MODEL)
        gate_ref[:, cs] = jax.nn.sigmoid(proj(OFF_G + j * D_MODEL, D_MODEL) + bg_ref[:, cs]).astype(BF16)


def _inproj_kernel(ws4_ref, bs4_ref, xs_ref, p1_ref, p2_ref, x_ref, gpre_ref, w_ref, lng_ref, lnb_ref, ws_ref,
                   bst_ref, cw_ref, cbias_ref, bg_ref, *rest, layer):
    nw = len(CAST_ROWS)
    f32_pieces, rest = rest[:nw], rest[nw:]
    oa_s, q_s, k_s, v_s, oc_s, gate_s, vn_s, cz_s = rest[:8]
    oa_ref, q_ref, k_ref, v_ref, kt_ref, vt_ref, oc_ref, gate_ref, cs_ref = rest[8:17]
    bf16_pieces, prev_ref = rest[17:17 + nw], rest[17 + nw]
    step = pl.program_id(0)

    @pl.when(step == 0)
    def _():
        _inproj_sample_kernel(ws4_ref, bs4_ref, xs_ref, gpre_ref, w_ref, lng_ref, lnb_ref, cw_ref, cbias_ref, bg_ref,
                              p1_ref, p2_ref, oa_s, q_s, k_s, v_s, oc_s, gate_s, vn_s, cz_s, layer=layer)

    @pl.when(step > 0)
    def _():
        _inproj_prompt_kernel(x_ref, gpre_ref, w_ref, lng_ref, lnb_ref, ws_ref, bst_ref, cw_ref, cbias_ref, bg_ref,
                              oa_ref, q_ref, k_ref, v_ref, kt_ref, vt_ref, oc_ref, gate_ref, cs_ref, prev_ref,
                              tile=(step - 1) % (SEQ // PROMPT_TM))
        for src_ref, dst_ref in zip(f32_pieces, bf16_pieces):
            dst_ref[...] = src_ref[...].astype(BF16)


def _inproj(layer, ws4, bs4, xs, p1, p2, x, vecs, w_in, ws, bs_t, cw, f32_weights):
    tm = PROMPT_TM
    nt = SEQ // tm
    rows = BATCH * SEQ
    n = SAMPLE_ROWS
    smem = pl.BlockSpec(memory_space=pltpu.SMEM)

    def tile_of(s):
        return jnp.maximum(s - 1, 0)

    def rowblk(width):
        return pl.BlockSpec((tm, width), lambda s: (tile_of(s), 0))

    def slab(k):
        return pl.BlockSpec((k, tm, LANES), lambda s: (0, tile_of(s), 0))

    def whole(width):
        return pl.BlockSpec((n, width), lambda s: (0, 0))

    win_spec = pl.BlockSpec((1, KV_W, tm),
                            lambda s: (tile_of(s) // nt, 0, jnp.maximum(tile_of(s) % nt - (nt - WIN // tm), 0)))
    sample_out = [(A_WIDTH, BF16), (Q_W, F32), (KV_W, F32), (KV_W, F32), (C_WIDTH, BF16),
                  (3 * D_MODEL, BF16), (A_WIDTH, F32), (C_WIDTH, F32)]
    cast = [_cast_specs(layer, w.shape[1:], r) for w, r in zip(f32_weights, CAST_ROWS)]
    return pl.pallas_call(
        functools.partial(_inproj_kernel, layer=layer),
        grid=(1 + BATCH * nt,),
        in_specs=[
            smem, smem,
            pl.BlockSpec((n, D_MODEL), lambda s: (0, 0), pipeline_mode=pl.Buffered(1)),
            _layer_spec(layer, (n, C_WIDTH)), _layer_spec(layer, (n, C_WIDTH)),
            rowblk(D_MODEL),
            _vec_spec(layer, "gpm"),
            _resident((D_MODEL, IN_W)),
            _vec_spec(layer, "lng"),
            _vec_spec(layer, "lnb"),
            _layer_spec(layer, (A_GROUPS, CHUNK, CHUNK)),
            _layer_spec(layer, (CHUNK, A_GROUPS)),
            _layer_spec(layer, (CONV_W, C_WIDTH)),
            _vec_spec(layer, "cbias"),
            _vec_spec(layer, "bg"),
        ] + [c[0] for c in cast],
        out_specs=[whole(w) for w, _ in sample_out] + [
            rowblk(A_WIDTH),
            slab(Q_W // LANES),
            slab(KV_W // LANES),
            slab(KV_W // LANES),
            win_spec,
            win_spec,
            rowblk(C_WIDTH),
            rowblk(3 * D_MODEL),
            pl.BlockSpec((1, SUBLANES, C_WIDTH), lambda s: (tile_of(s) // nt, 0, 0)),
        ] + [c[1] for c in cast],
        out_shape=[jax.ShapeDtypeStruct((n, w), dt) for w, dt in sample_out] + [
            jax.ShapeDtypeStruct((rows, A_WIDTH), BF16),
            jax.ShapeDtypeStruct((Q_W // LANES, rows, LANES), F32),
            jax.ShapeDtypeStruct((KV_W // LANES, rows, LANES), F32),
            jax.ShapeDtypeStruct((KV_W // LANES, rows, LANES), F32),
            jax.ShapeDtypeStruct((BATCH, KV_W, WIN), F32),
            jax.ShapeDtypeStruct((BATCH, KV_W, WIN), F32),
            jax.ShapeDtypeStruct((rows, C_WIDTH), BF16),
            jax.ShapeDtypeStruct((rows, 3 * D_MODEL), BF16),
            jax.ShapeDtypeStruct((BATCH, SUBLANES, C_WIDTH), F32),
        ] + [jax.ShapeDtypeStruct(w.shape[1:], BF16) for w in f32_weights],
        scratch_shapes=[pltpu.VMEM((SUBLANES, C_WIDTH), F32)],
        compiler_params=_params(("arbitrary",)),
        name="inproj",
    )(ws4, bs4, xs, p1, p2, x, vecs, w_in, vecs, vecs, ws, bs_t, cw, vecs, vecs, *f32_weights)


def _attn_sample_kernel(qh_ref, kn_ref, vn_ref, kt_ref, vt_ref, o_ref, bias_ref):
    nr = N_DIL * SUBLANES
    width = MAX_WINDOW

    @pl.when(pl.program_id(0) == 0)
    def _():
        row = lax.broadcasted_iota(jnp.int32, (nr, width), 0)
        pos = lax.broadcasted_iota(jnp.int32, (nr, width), 1)
        grp = row >> 3
        delta = width + (row & (SUBLANES - 1)) - pos
        dil_m1 = _select_rows(grp, [d - 1 for d in DILS])
        reach = _select_rows(grp, [BAND * d for d in DILS])
        ok = jnp.logical_and((delta & dil_m1) == 0, delta <= reach)
        for h in range(KV_HEADS):
            slope = _select_rows(grp, [jnp.float32(SLOPES[g][h]) for g in range(N_DIL)])
            bias_ref[h * nr:(h + 1) * nr, :] = jnp.where(ok, -(slope * delta.astype(F32)), NEG)

    ridx = lax.broadcasted_iota(jnp.int32, (nr, 1), 0)
    g_of = ridx >> 3
    t_of = ridx & (SUBLANES - 1)
    scale = HEAD_DIM ** -0.5
    slopes = [_select_rows(g_of, [jnp.float32(SLOPES[g][h]) for g in range(N_DIL)]) for h in range(KV_HEADS)]
    for e in range(qh_ref.shape[0]):
        qs = [qh_ref[e, h] * scale for h in range(KV_HEADS)]
        s = jnp.concatenate([_dot(qs[h].astype(BF16), kt_ref[0, e, h].astype(BF16)) for h in range(KV_HEADS)],
                            axis=0)
        s = s + bias_ref[...]
        m = jnp.max(s, axis=-1, keepdims=True)
        s_new = []
        for n in range(DEC_SEQ):
            dn = t_of - n
            ok = jnp.logical_or(dn == 0, jnp.logical_and(g_of == 0, dn > 0))
            rows = []
            for h in range(KV_HEADS):
                raw = jnp.sum(qs[h] * kn_ref[e, h, n:n + 1, :], axis=-1, keepdims=True)
                rows.append(jnp.where(ok, raw - slopes[h] * dn.astype(F32), NEG))
            sn = jnp.concatenate(rows, axis=0)
            s_new.append(sn)
            m = jnp.maximum(m, sn)
        p = jnp.exp(s - m)
        l_all = jnp.sum(p, axis=-1, keepdims=True)
        p16 = p.astype(BF16)
        p_new = [jnp.exp(sn - m) for sn in s_new]
        for pn in p_new:
            l_all = l_all + pn
        for h in range(KV_HEADS):
            hr = slice(h * nr, (h + 1) * nr)
            r = _dot_nt(p16[hr], vt_ref[0, e, h].astype(BF16))
            for n in range(DEC_SEQ):
                r = r + p_new[n][hr] * vn_ref[e, h, n:n + 1, :]
            m_h = m[hr]
            l_h = l_all[hr]
            parts = [(r[g * SUBLANES:(g + 1) * SUBLANES], m_h[g * SUBLANES:(g + 1) * SUBLANES],
                      l_h[g * SUBLANES:(g + 1) * SUBLANES]) for g in range(N_DIL)]
            m_all = jnp.maximum(jnp.maximum(parts[0][1], parts[1][1]), parts[2][1])
            num = jnp.zeros((SUBLANES, HEAD_DIM), F32)
            den = jnp.zeros((SUBLANES, 1), F32)
            for o_g, m_g, l_g in parts:
                w = jnp.exp(m_g - m_all)
                num = num + o_g * w
                den = den + l_g * w
            o_ref[e, :, h * HEAD_DIM:(h + 1) * HEAD_DIM] = (num / den)[0:DEC_SEQ]


def _attn_sample(layer, qh, knh, vnh, ckt, cvt):
    nr = N_DIL * SUBLANES
    eb = SAMPLE_EB
    cache_spec = pl.BlockSpec((1, eb, KV_HEADS, HEAD_DIM, MAX_WINDOW), lambda b: (layer, b, 0, 0, 0))
    new_spec = pl.BlockSpec((eb, KV_HEADS, SUBLANES, HEAD_DIM), lambda b: (b, 0, 0, 0))
    return pl.pallas_call(
        _attn_sample_kernel,
        grid=(DEC_BATCH // eb,),
        in_specs=[pl.BlockSpec((eb, KV_HEADS, nr, HEAD_DIM), lambda b: (b, 0, 0, 0)),
                  new_spec, new_spec, cache_spec, cache_spec],
        out_specs=pl.BlockSpec((eb, DEC_SEQ, KV_W), lambda b: (b, 0, 0)),
        out_shape=jax.ShapeDtypeStruct((DEC_BATCH, DEC_SEQ, KV_W), F32),
        scratch_shapes=[pltpu.VMEM((KV_HEADS * nr, MAX_WINDOW), F32)],
        compiler_params=_params(("arbitrary",)),
        name="attn_sample",
    )(qh, knh, vnh, ckt, cvt)


def kernel(x_prompt, x_sample, cache_k_win, cache_v_win, state_conv, g_pre_mix, g_post_mix, g_pre_ffn, g_post_ffn,
           w_in, a_ln_g, a_ln_b, a_ws, a_bs, c_conv_w, c_conv_b, w_br_a, w_br_b, w_br_c, b_gate, w_o,
           w_ff_gate, w_ff_up, w_ff_down):
    xp = x_prompt.reshape(BATCH * SEQ, D_MODEL)
    xs = x_sample.reshape(SAMPLE_ROWS, D_MODEL)
    ckt = jnp.transpose(cache_k_win, (0, 1, 3, 4, 2))
    cvt = jnp.transpose(cache_v_win, (0, 1, 3, 4, 2))

    w_in_b = w_in[0].astype(BF16)
    f32_weights = (w_br_a, w_br_b, w_br_c, w_o, w_ff_gate, w_ff_up, w_ff_down)
    vecs = jnp.concatenate([b_gate, g_pre_mix, g_post_mix, g_pre_ffn, g_post_ffn, a_ln_g, a_ln_b, c_conv_b],
                           axis=1).reshape(DEPTH, 1, -1)
    bs_t = jnp.swapaxes(a_bs, 1, 2)
    ws4 = a_ws[:, :, :DEC_SEQ, :DEC_SEQ].reshape(-1)
    bs4 = a_bs[:, :, :DEC_SEQ].reshape(-1)
    zeros = jnp.zeros((DEPTH, DEC_BATCH, 1, C_WIDTH), F32)
    p1 = jnp.concatenate([state_conv[:, :, 1:2], zeros, zeros, zeros], axis=2).reshape(DEPTH, SAMPLE_ROWS, C_WIDTH)
    p2 = jnp.concatenate([state_conv, zeros, zeros], axis=2).reshape(DEPTH, SAMPLE_ROWS, C_WIDTH)

    kp_l, vp_l, ks_l, vs_l, cp_l, cs_l, av_l = [], [], [], [], [], [], []
    for l in range(DEPTH):
        outs = _inproj(l, ws4, bs4, xs, p1, p2, xp, vecs, w_in_b, a_ws, bs_t, c_conv_w, f32_weights)
        oa_s, q_s, k_s, v_s, oc_s, gates_s, vn_s, cz_s, oa, q, k, v, kt, vt, oc, gates, cs8 = outs[:17]
        weights = outs[17:]
        ob = _attn_prompt(q, k, v)

        def window(t):
            return jnp.transpose(t.reshape(BATCH, KV_HEADS, HEAD_DIM, WIN), (0, 3, 1, 2))

        kp_l.append(window(kt))
        vp_l.append(window(vt))
        cp_l.append(cs8[:, SUBLANES - (CONV_W - 1):])


        pad_t = ((0, 0), (0, 0), (0, 0), (0, SUBLANES - DEC_SEQ), (0, 0))
        qh = q_s.reshape(DEC_BATCH, DEC_SEQ, N_DIL, KV_HEADS, HEAD_DIM).transpose(0, 3, 2, 1, 4)
        qh = jnp.pad(qh, pad_t).reshape(DEC_BATCH, KV_HEADS, N_DIL * SUBLANES, HEAD_DIM)

        def new_rows(a):
            a = a.reshape(DEC_BATCH, DEC_SEQ, KV_HEADS, HEAD_DIM).transpose(0, 2, 1, 3)
            return jnp.pad(a, pad_t[1:])

        ob_s = _attn_sample(l, qh, new_rows(k_s), new_rows(v_s), ckt, cvt)
        ob_s = ob_s.reshape(SAMPLE_ROWS, KV_W)
        last = l + 1 == DEPTH
        res = _merge_ffn(l, xs, oa_s, ob_s, oc_s, gates_s, xp, oa, ob, oc, gates, vecs, weights,
                         None if last else w_in)
        xs, xp = res[0], res[1]
        if not last:
            w_in_b = res[2]

        ks_l.append(k_s.reshape(DEC_BATCH, DEC_SEQ, KV_HEADS, HEAD_DIM))
        vs_l.append(v_s.reshape(DEC_BATCH, DEC_SEQ, KV_HEADS, HEAD_DIM))
        cs_l.append(cz_s.reshape(DEC_BATCH, DEC_SEQ, C_WIDTH)[:, DEC_SEQ - (CONV_W - 1):])
        av_l.append(vn_s.reshape(DEC_BATCH, DEC_SEQ, A_WIDTH))

    return (xp.reshape(BATCH, SEQ, D_MODEL), xs.reshape(DEC_BATCH, DEC_SEQ, D_MODEL),
            jnp.stack(kp_l), jnp.stack(vp_l), jnp.stack(ks_l), jnp.stack(vs_l),
            jnp.stack(cp_l), jnp.stack(cs_l), jnp.stack(av_l))
```

```python
import functools

import jax
import jax.numpy as jnp
from jax import lax
from jax.experimental import pallas as pl
from jax.experimental.pallas import tpu as pltpu

F32 = jnp.float32
BF16 = jnp.bfloat16

D_MODEL = 1024
BATCH = 2
SEQ = 8192
DEPTH = 2
DEC_BATCH = 32
DEC_SEQ = 4
CHUNK = 128
A_GROUPS = 4
A_GROUP_DIM = 128
A_WIDTH = A_GROUPS * A_GROUP_DIM
HEAD_DIM = 64
KV_HEADS = 4
DILS = (1, 4, 16)
N_DIL = len(DILS)
Q_W = N_DIL * KV_HEADS * HEAD_DIM
KV_W = KV_HEADS * HEAD_DIM
BAND = 128
MAX_WINDOW = 2048
C_WIDTH = 512
CONV_W = 3
D_FF = 2816
EPS = 1e-6

OFF_U = 0
OFF_V = OFF_U + A_WIDTH
OFF_Q = OFF_V + A_WIDTH
OFF_K = OFF_Q + Q_W
OFF_VV = OFF_K + KV_W
OFF_CX = OFF_VV + KV_W
OFF_CB = OFF_CX + C_WIDTH
OFF_CC = OFF_CB + C_WIDTH
OFF_G = OFF_CC + C_WIDTH
IN_W = OFF_G + 3 * D_MODEL

LANES = 128
SUBLANES = 8
VMEM_LIMIT = 56 * 1024 * 1024
NEG = -1e30

SAMPLE_ROWS = DEC_BATCH * DEC_SEQ
WIN = min(MAX_WINDOW, SEQ)
SPAN = BAND * DILS[-1]
RES = DILS[1]
PROMPT_TM = 512
ROW_SUB = 256
INPROJ_SUB = ROW_SUB
INPROJ_LAG = 2
MERGE_FFN_LAG = 3
CAST_ROWS = (16, 16, 16, 32, 32, 32, 128)
WIN_CAST_ROWS = 32
SAMPLE_EB = 4
MXU_N = 256
FF_CHUNKS = ((0, 6 * MXU_N), (6 * MXU_N, D_FF))
LOG2E = 1.4426950408889634
QSCALE = LOG2E * HEAD_DIM ** -0.5

SLOPES = tuple(
    tuple(2.0 ** (-8.0 * (g * KV_HEADS + h + 1) / (N_DIL * KV_HEADS)) for h in range(KV_HEADS))
    for g in range(N_DIL)
)


def _rms(x, g):
    return x * lax.rsqrt(jnp.mean(x * x, axis=-1, keepdims=True) + EPS) * g


def _layer_norm(x, g, b):
    mu = jnp.mean(x, axis=-1, keepdims=True)
    xc = x - mu
    return xc * lax.rsqrt(jnp.mean(xc * xc, axis=-1, keepdims=True) + EPS) * g + b


def _dot(a, b):
    return jnp.dot(a, b, preferred_element_type=F32)


def _dot_nt(a, b):
    return lax.dot_general(a, b, (((1,), (1,)), ((), ())), preferred_element_type=F32)


def _select_rows(hid, vals):
    out = vals[-1]
    for h in range(len(vals) - 2, -1, -1):
        out = jnp.where(hid == h, vals[h], out)
    return out


def _layer_spec(layer, shape):
    nd = len(shape)
    return pl.BlockSpec((None,) + tuple(shape), lambda *_: (layer,) + (0,) * nd, pipeline_mode=pl.Buffered(1))


VEC_SLOTS = {"bg": (3 * D_MODEL, 0), "gpm": (D_MODEL, 3), "gqm": (D_MODEL, 4), "gpf": (D_MODEL, 5),
             "gqf": (D_MODEL, 6), "lng": (A_WIDTH, 14), "lnb": (A_WIDTH, 15), "cbias": (C_WIDTH, 16)}


def _vec_spec(layer, name):
    width, idx = VEC_SLOTS[name]
    return pl.BlockSpec((None, 1, width), lambda *_: (layer, 0, idx), pipeline_mode=pl.Buffered(1))


def _trace_staggered(stage_gens, lag):
    live = [True] * len(stage_gens)
    tick = 0
    while any(live):
        for j, gen in enumerate(stage_gens):
            if live[j] and tick >= j * lag:
                live[j] = next(gen, "done") != "done"
        tick += 1


def _resident(shape):
    nd = len(shape)
    return pl.BlockSpec(tuple(shape), lambda *_: (0,) * nd, pipeline_mode=pl.Buffered(1))


def _cast_specs(layer, shape, rows_per_step):
    last = shape[0] // rows_per_step - 1

    def piece(s):
        return jnp.minimum(jnp.maximum(s - 1, 0), last)

    return (pl.BlockSpec((None, rows_per_step, shape[1]), lambda s: (layer, piece(s), 0)),
            pl.BlockSpec((rows_per_step, shape[1]), lambda s: (piece(s), 0)))


def _params(sem):
    return pltpu.CompilerParams(dimension_semantics=sem, vmem_limit_bytes=VMEM_LIMIT)


def _inproj_prompt_kernel(x_ref, gpre_ref, w_ref, lng_ref, lnb_ref, ws_ref, bs_ref, cw_ref,
                          cbias_ref, bg_ref,
                          oa_ref, q_ref, k_ref, v_ref, kt_ref, vt_ref, oc_ref, gate_ref, cs_ref, prev_ref,
                          xnext_ref, hnext_ref, *, tile):
    tm = x_ref.shape[0]

    @pl.when(tile == 0)
    def _():
        prev_ref[...] = jnp.zeros_like(prev_ref)

    row = lax.broadcasted_iota(jnp.int32, (CHUNK, CHUNK), 0)
    col = lax.broadcasted_iota(jnp.int32, (CHUNK, CHUNK), 1)
    wt = [jnp.where(row >= col, ws_ref[g], 0.0).astype(BF16) for g in range(A_GROUPS)]
    rsub = lax.broadcasted_iota(jnp.int32, (INPROJ_SUB, 1), 0)

    carry = {"tail": prev_ref[...]}

    def sub_tile(r0):
        rows = slice(r0, r0 + INPROJ_SUB)
        h = hnext_ref[...] if r0 == 0 else _rms(x_ref[rows, :], gpre_ref[...]).astype(BF16)

        def proj(off, width):
            return _dot(h, w_ref[:, off:off + width])

        yield
        u = jax.nn.gelu(proj(OFF_U, A_WIDTH))
        yield
        vn = _layer_norm(jax.nn.gelu(proj(OFF_V, A_WIDTH)), lng_ref[...], lnb_ref[...])
        for j in range(3):
            yield
            cs = slice(j * D_MODEL, (j + 1) * D_MODEL)
            gate_ref[rows, cs] = jax.nn.sigmoid(proj(OFF_G + j * D_MODEL, D_MODEL) + bg_ref[:, cs]).astype(BF16)
        yield
        for g in range(A_GROUPS):
            cs = slice(g * A_GROUP_DIM, (g + 1) * A_GROUP_DIM)
            for c in range(0, INPROJ_SUB // CHUNK, 2):
                pair = jnp.concatenate([vn[(c + j) * CHUNK:(c + j + 1) * CHUNK, cs] for j in range(2)], axis=1)
                sa2 = _dot(wt[g], pair.astype(BF16)) + bs_ref[:, g:g + 1]
                for j in range(2):
                    rs = slice((c + j) * CHUNK, (c + j + 1) * CHUNK)
                    sa = sa2[:, j * A_GROUP_DIM:(j + 1) * A_GROUP_DIM]
                    oa_ref[r0 + (c + j) * CHUNK:r0 + (c + j + 1) * CHUNK, cs] = (u[rs, cs] * sa).astype(BF16)
        yield
        cz = proj(OFF_CC, C_WIDTH) * proj(OFF_CX, C_WIDTH)
        p2 = carry["tail"][SUBLANES - 2:SUBLANES - 1, :]
        p1 = carry["tail"][SUBLANES - 1:SUBLANES, :]
        carry["tail"] = cz[INPROJ_SUB - SUBLANES:INPROJ_SUB, :]
        z1 = jnp.where(rsub == 0, p1, pltpu.roll(cz, 1, 0))
        z2 = jnp.where(rsub == 0, p2, jnp.where(rsub == 1, p1, pltpu.roll(cz, 2, 0)))
        y = cbias_ref[...] + cw_ref[0:1, :] * z2 + cw_ref[1:2, :] * z1 + cw_ref[2:3, :] * cz
        yield
        oc_ref[rows, :] = (proj(OFF_CB, C_WIDTH) * y).astype(BF16)
        yield
        qv = proj(OFF_Q, Q_W)
        for c in range(Q_W // LANES):
            q_ref[c, rows, :] = qv[:, c * LANES:(c + 1) * LANES] * QSCALE
        yield
        kv = proj(OFF_K, 2 * KV_W)
        for c in range(KV_W // LANES):
            k_ref[c, rows, :] = kv[:, c * LANES:(c + 1) * LANES]
            v_ref[c, rows, :] = kv[:, KV_W + c * LANES:KV_W + (c + 1) * LANES]
        kt_ref[0, :, rows] = kv[:, 0:KV_W].T
        vt_ref[0, :, rows] = kv[:, KV_W:2 * KV_W].T

    def prepare_next():
        yield
        hnext_ref[...] = _rms(xnext_ref[...], gpre_ref[...]).astype(BF16)

    _trace_staggered([sub_tile(r0) for r0 in range(0, tm, INPROJ_SUB)] + [prepare_next()], INPROJ_LAG)
    prev_ref[...] = carry["tail"]
    cs_ref[0] = carry["tail"]


def _attn_prompt_kernel(q_ref, k_ref, v_ref, o_ref, ktail, vtail, k4, v4, q24, acc, mrun, lrun, tmp, bias_ref):
    b = pl.program_id(0)
    i = pl.program_id(1)
    first = i == 0
    nslab = KV_W // LANES
    quarter = SPAN // RES

    @pl.when(jnp.logical_and(b == 0, first))
    def _():
        rr = lax.broadcasted_iota(jnp.int32, (KV_HEADS * BAND, 2 * BAND), 0)
        kk = lax.broadcasted_iota(jnp.int32, (KV_HEADS * BAND, 2 * BAND), 1)
        dist = (rr & (BAND - 1)) + BAND - kk
        hrow = rr >> 7
        ok = jnp.logical_and(dist >= 0, dist <= BAND)
        for g, dil in enumerate(DILS):
            coef = _select_rows(hrow, [jnp.float32(SLOPES[g][h] * dil * LOG2E) for h in range(KV_HEADS)])
            full = jnp.where(ok, -(coef * dist.astype(F32)), NEG)
            bias_ref[2 * g] = full
            bias_ref[2 * g + 1] = jnp.where(kk >= BAND, full, NEG)

    @pl.when(first)
    def _():
        for r in range(RES):
            base = r * 2 * quarter
            k4[:, base:base + quarter, :] = jnp.zeros((nslab, quarter, LANES), F32)
            v4[:, base:base + quarter, :] = jnp.zeros((nslab, quarter, LANES), F32)
        ktail[...] = jnp.zeros_like(ktail)
        vtail[...] = jnp.zeros_like(vtail)

    @pl.when(i > 0)
    def _():
        for r in range(RES):
            base = r * 2 * quarter
            k4[:, base:base + quarter, :] = k4[:, base + quarter:base + 2 * quarter, :]
            v4[:, base:base + quarter, :] = v4[:, base + quarter:base + 2 * quarter, :]

    for r in range(RES):
        base = r * 2 * quarter + quarter
        for c in range(nslab):
            k4[c, base:base + quarter, :] = k_ref[c, pl.ds(r, quarter, stride=RES), :]
            v4[c, base:base + quarter, :] = v_ref[c, pl.ds(r, quarter, stride=RES), :]
            q24[c, r * quarter:(r + 1) * quarter, :] = q_ref[2 * nslab + c, pl.ds(r, quarter, stride=RES), :]

    lane = lax.broadcasted_iota(jnp.int32, (1, KV_W), 1)
    hid = lane >> 6
    headmask = [jnp.where(hid == h, 1.0, 0.0).astype(BF16) for h in range(KV_HEADS)]

    def wide(ref, rows, slab0=0):
        return jnp.concatenate([ref[slab0 + c, rows, :] for c in range(nslab)], axis=1)

    def softmax_unit(qb, kb, vb, bias):
        qb16 = qb.astype(BF16)
        kb16 = kb.astype(BF16)
        vb16 = vb.astype(BF16)
        qs = jnp.concatenate([qb16 * headmask[h] for h in range(KV_HEADS)], axis=0)
        s = _dot_nt(qs, kb16)
        ps, ms, ls = [], [], []
        for h in range(KV_HEADS):
            sh = s[h * BAND:(h + 1) * BAND] + bias_ref[bias, h * BAND:(h + 1) * BAND, :]
            mh = jnp.max(sh, axis=-1, keepdims=True)
            ph = jnp.exp2(sh - mh)
            ps.append(ph.astype(BF16))
            ms.append(mh)
            ls.append(jnp.sum(ph, axis=-1, keepdims=True))
        r = _dot(jnp.concatenate(ps, axis=0), vb16)
        rs = [r[h * BAND:(h + 1) * BAND] for h in range(KV_HEADS)]
        return _select_rows(hid, rs), _select_rows(hid, ms), _select_rows(hid, ls)

    def store_stats(rows, o, m, l):
        for c in range(nslab):
            ls = slice(c * LANES, (c + 1) * LANES)
            acc[c, rows, :] = o[:, ls]
            mrun[c, rows, :] = m[:, ls]
            lrun[c, rows, :] = l[:, ls]

    def merge_stats(rows, o_u, m_u, l_u):
        m_old = wide(mrun, rows)
        m_n = jnp.maximum(m_old, m_u)
        a_old = jnp.exp2(m_old - m_n)
        a_new = jnp.exp2(m_u - m_n)
        store_stats(rows, wide(acc, rows) * a_old + o_u * a_new, m_n, wide(lrun, rows) * a_old + l_u * a_new)

    def unit0(n, kb, vb, bias):
        q0 = n * BAND if isinstance(n, int) else pl.multiple_of(n * BAND, BAND)
        stats = softmax_unit(wide(q_ref, pl.ds(q0, BAND)), kb, vb, bias)
        sub = BAND // RES
        for j, x in enumerate(stats):
            for c in range(nslab):
                tmp[j * nslab + c, pl.ds(q0, BAND), :] = x[:, c * LANES:(c + 1) * LANES]
        for r in range(RES):
            src = pl.ds(q0 + r, sub, stride=RES)
            dst = pl.ds(r * quarter + n * sub, sub)
            for c in range(nslab):
                acc[c, dst, :] = tmp[c, src, :]
                mrun[c, dst, :] = tmp[nslab + c, src, :]
                lrun[c, dst, :] = tmp[2 * nslab + c, src, :]

    kb0 = jnp.concatenate([wide(ktail, pl.ds(0, BAND)), wide(k_ref, pl.ds(0, BAND))], axis=0)
    vb0 = jnp.concatenate([wide(vtail, pl.ds(0, BAND)), wide(v_ref, pl.ds(0, BAND))], axis=0)
    unit0(0, kb0, vb0, jnp.where(first, 1, 0))

    def body0(n, carry):
        keys = pl.ds(pl.multiple_of((n - 1) * BAND, BAND), 2 * BAND)
        unit0(n, wide(k_ref, keys), wide(v_ref, keys), 0)
        return carry

    lax.fori_loop(1, SPAN // BAND, body0, 0, unroll=5)

    def body1(s, carry):
        bias = 2 + jnp.where(jnp.logical_and(first, s == 0), 1, 0)
        for r in range(RES):
            qb = wide(q_ref, pl.ds(s * (BAND * RES) + r, BAND, stride=RES), nslab)
            keys = pl.ds(r * 2 * quarter + quarter + (s - 1) * BAND, 2 * BAND)
            o_u, m_u, l_u = softmax_unit(qb, wide(k4, keys), wide(v4, keys), bias)
            merge_stats(pl.ds(r * quarter + s * BAND, BAND), o_u, m_u, l_u)
        return carry

    lax.fori_loop(0, quarter // BAND, body1, 0, unroll=2)

    def body2(r, carry):
        bias = 4 + jnp.where(first, 1, 0)
        for a in range(DILS[2] // RES):
            rows = pl.ds(r * quarter + a, BAND, stride=RES)
            keys = pl.ds(r * 2 * quarter + a, 2 * BAND, stride=RES)
            o_u, m_u, l_u = softmax_unit(wide(q24, rows), wide(k4, keys), wide(v4, keys), bias)
            merge_stats(rows, o_u, m_u, l_u)
        return carry

    lax.fori_loop(0, RES, body2, 0, unroll=2)

    for r in range(RES):
        rows = pl.ds(r * quarter, quarter)
        for c in range(nslab):
            tmp[c, pl.ds(r, quarter, stride=RES), :] = acc[c, rows, :] / lrun[c, rows, :]
    for c in range(nslab):
        o_ref[:, c * LANES:(c + 1) * LANES] = tmp[c].astype(BF16)
    ktail[...] = k_ref[:, SPAN - BAND:SPAN, :]
    vtail[...] = v_ref[:, SPAN - BAND:SPAN, :]


def _attn_prompt(q, k, v):
    ns = SEQ // SPAN
    nslab = KV_W // LANES
    rows = BATCH * SEQ

    def slab(n):
        return pl.BlockSpec((n, SPAN, LANES), lambda b, i: (0, b * ns + i, 0))

    return pl.pallas_call(
        _attn_prompt_kernel,
        grid=(BATCH, ns),
        in_specs=[slab(Q_W // LANES), slab(nslab), slab(nslab)],
        out_specs=pl.BlockSpec((SPAN, KV_W), lambda b, i: (b * ns + i, 0)),
        out_shape=jax.ShapeDtypeStruct((rows, KV_W), BF16),
        scratch_shapes=[
            pltpu.VMEM((nslab, BAND, LANES), F32),
            pltpu.VMEM((nslab, BAND, LANES), F32),
            pltpu.VMEM((nslab, 2 * SPAN, LANES), F32),
            pltpu.VMEM((nslab, 2 * SPAN, LANES), F32),
            pltpu.VMEM((nslab, SPAN, LANES), F32),
            pltpu.VMEM((nslab, SPAN, LANES), F32),
            pltpu.VMEM((nslab, SPAN, LANES), F32),
            pltpu.VMEM((nslab, SPAN, LANES), F32),
            pltpu.VMEM((3 * nslab, SPAN, LANES), F32),
            pltpu.VMEM((2 * N_DIL, KV_HEADS * BAND, 2 * BAND), F32),
        ],
        compiler_params=_params(("arbitrary", "arbitrary")),
        name="attn_prompt",
    )(q, k, v)


def _merge_ffn_kernel(xs_ref, oas_ref, obs_ref, ocs_ref, gates_ref, x_ref, oa_ref, ob_ref, oc_ref, gate_ref,
                      gqm_ref, gpf_ref, gqf_ref, wa_ref, wb_ref, wc_ref, wo_ref, wg_ref, wu_ref, wd_ref, *rest):
    os_ref, o_ref = rest[-3:-1] if len(rest) == 4 else rest
    def run(x_ref, oa_ref, ob_ref, oc_ref, gate_ref, o_ref):
        tm = x_ref.shape[0]
        sub = min(tm, ROW_SUB)

        def sub_tile(r0):
            rows = slice(r0, r0 + sub)

            def gate(j):
                return gate_ref[rows, j * D_MODEL:(j + 1) * D_MODEL].astype(F32)

            merged = gate(0) * _dot(oa_ref[rows, :], wa_ref[...])
            yield
            merged = merged + gate(1) * _dot(ob_ref[rows, :].astype(BF16), wb_ref[...])
            yield
            merged = merged + gate(2) * _dot(oc_ref[rows, :], wc_ref[...])
            yield
            x = x_ref[rows, :] + _rms(_dot(merged.astype(BF16), wo_ref[...]), gqm_ref[...])
            yield
            h = _rms(x, gpf_ref[...]).astype(BF16)
            y = None
            for lo, hi in FF_CHUNKS:
                yield
                a = jax.nn.silu(_dot(h, wg_ref[:, lo:hi]))
                yield
                act = (a * _dot(h, wu_ref[:, lo:hi])).astype(BF16)
                yield
                part = _dot(act, wd_ref[lo:hi, :])
                y = part if y is None else y + part
            yield
            o_ref[rows, :] = x + _rms(y, gqf_ref[...])

        _trace_staggered([sub_tile(r0) for r0 in range(0, tm, sub)], MERGE_FFN_LAG)

    step = pl.program_id(0)

    @pl.when(step == 0)
    def _():
        run(xs_ref, oas_ref, obs_ref, ocs_ref, gates_ref, os_ref)

    @pl.when(step > 0)
    def _():
        run(x_ref, oa_ref, ob_ref, oc_ref, gate_ref, o_ref)
        if len(rest) == 4:
            rest[3][...] = rest[0][...].astype(BF16)


def _merge_ffn(layer, xs, oa_s, ob_s, oc_s, gates_s, x, oa, ob, oc, gates, vecs, weights, w_in_f32):
    tm = PROMPT_TM
    rows = x.shape[0]
    ns = xs.shape[0]

    def rowblk(width):
        return pl.BlockSpec((tm, width), lambda i: (jnp.maximum(i - 1, 0), 0))

    def whole(width):
        return pl.BlockSpec((ns, width), lambda i: (0, 0), pipeline_mode=pl.Buffered(1))

    nxt = [] if w_in_f32 is None else [_cast_specs(layer + 1, (D_MODEL, IN_W), WIN_CAST_ROWS)]
    return pl.pallas_call(
        _merge_ffn_kernel,
        grid=(1 + rows // tm,),
        in_specs=[
            whole(D_MODEL), whole(A_WIDTH), whole(KV_W), whole(C_WIDTH), whole(3 * D_MODEL),
            rowblk(D_MODEL), rowblk(A_WIDTH), rowblk(KV_W), rowblk(C_WIDTH), rowblk(3 * D_MODEL),
            _vec_spec(layer, "gqm"), _vec_spec(layer, "gpf"), _vec_spec(layer, "gqf"),
        ] + [_resident(w.shape) for w in weights] + [c[0] for c in nxt],
        out_specs=[pl.BlockSpec((ns, D_MODEL), lambda i: (0, 0)), rowblk(D_MODEL)] + [c[1] for c in nxt],
        out_shape=[jax.ShapeDtypeStruct((ns, D_MODEL), F32), jax.ShapeDtypeStruct((rows, D_MODEL), F32)]
        + [jax.ShapeDtypeStruct((D_MODEL, IN_W), BF16) for _ in nxt],
        compiler_params=_params(("arbitrary",)),
        name="merge_ffn",
    )(xs, oa_s, ob_s, oc_s, gates_s, x, oa, ob, oc, gates, vecs, vecs, vecs, *weights,
      *([] if w_in_f32 is None else [w_in_f32]))


def _inproj_sample_kernel(ws_ref, bs_ref, x_ref, gpre_ref, w_ref, lng_ref, lnb_ref, cw_ref, cbias_ref, bg_ref,
                          p1_ref, p2_ref,
                          oa_ref, q_ref, k_ref, v_ref, oc_ref, gate_ref, vn_ref, cz_ref, *, layer):
    n = SAMPLE_ROWS
    h = _rms(x_ref[...], gpre_ref[...]).astype(BF16)

    def proj(off, width):
        return _dot(h, w_ref[:, off:off + width])

    u = jax.nn.gelu(proj(OFF_U, A_WIDTH))
    vn = _layer_norm(jax.nn.gelu(proj(OFF_V, A_WIDTH)), lng_ref[...], lnb_ref[...])
    vn_ref[...] = vn
    t = lax.broadcasted_iota(jnp.int32, (n, 1), 0) & (DEC_SEQ - 1)
    for g in range(A_GROUPS):
        cs = slice(g * A_GROUP_DIM, (g + 1) * A_GROUP_DIM)
        base = (layer * A_GROUPS + g) * DEC_SEQ
        vg = vn[:, cs]
        sa = _select_rows(t, [bs_ref[base + tt] for tt in range(DEC_SEQ)])
        for k in range(DEC_SEQ):
            coef = jnp.zeros((n, 1), F32)
            for tt in range(k, DEC_SEQ):
                coef = jnp.where(t == tt, ws_ref[(base + tt) * DEC_SEQ + tt - k], coef)
            sa = sa + coef * (vg if k == 0 else pltpu.roll(vg, k, 0))
        oa_ref[:, cs] = (u[:, cs] * sa).astype(BF16)

    q_ref[...] = proj(OFF_Q, Q_W)
    kv = proj(OFF_K, 2 * KV_W)
    k_ref[...] = kv[:, 0:KV_W]
    v_ref[...] = kv[:, KV_W:2 * KV_W]

    cz = proj(OFF_CC, C_WIDTH) * proj(OFF_CX, C_WIDTH)
    cz_ref[...] = cz
    z1 = jnp.where(t >= 1, pltpu.roll(cz, 1, 0), p1_ref[...])
    z2 = jnp.where(t >= 2, pltpu.roll(cz, 2, 0), p2_ref[...])
    y = cbias_ref[...] + cw_ref[0:1, :] * z2 + cw_ref[1:2, :] * z1 + cw_ref[2:3, :] * cz
    oc_ref[...] = (proj(OFF_CB, C_WIDTH) * y).astype(BF16)

    for j in range(3):
        cs = slice(j * D_MODEL, (j + 1) * D_MODEL)
        gate_ref[:, cs] = jax.nn.sigmoid(proj(OFF_G + j * D_MODEL, D_MODEL) + bg_ref[:, cs]).astype(BF16)


def _inproj_kernel(ws4_ref, bs4_ref, xs_ref, p1_ref, p2_ref, x_ref, gpre_ref, w_ref, lng_ref, lnb_ref, ws_ref,
                   bst_ref, cw_ref, cbias_ref, bg_ref, xnext_ref, *rest, layer):
    nw = len(CAST_ROWS)
    f32_pieces, rest = rest[:nw], rest[nw:]
    oa_s, q_s, k_s, v_s, oc_s, gate_s, vn_s, cz_s = rest[:8]
    oa_ref, q_ref, k_ref, v_ref, kt_ref, vt_ref, oc_ref, gate_ref, cs_ref = rest[8:17]
    bf16_pieces, prev_ref, hnext_ref = rest[17:17 + nw], rest[17 + nw], rest[18 + nw]
    step = pl.program_id(0)

    @pl.when(step == 0)
    def _():
        _inproj_sample_kernel(ws4_ref, bs4_ref, xs_ref, gpre_ref, w_ref, lng_ref, lnb_ref, cw_ref, cbias_ref, bg_ref,
                              p1_ref, p2_ref, oa_s, q_s, k_s, v_s, oc_s, gate_s, vn_s, cz_s, layer=layer)
        hnext_ref[...] = _rms(xnext_ref[...], gpre_ref[...]).astype(BF16)

    @pl.when(step > 0)
    def _():
        _inproj_prompt_kernel(x_ref, gpre_ref, w_ref, lng_ref, lnb_ref, ws_ref, bst_ref, cw_ref, cbias_ref, bg_ref,
                              oa_ref, q_ref, k_ref, v_ref, kt_ref, vt_ref, oc_ref, gate_ref, cs_ref, prev_ref,
                              xnext_ref, hnext_ref, tile=(step - 1) % (SEQ // PROMPT_TM))
        for src_ref, dst_ref in zip(f32_pieces, bf16_pieces):
            dst_ref[...] = src_ref[...].astype(BF16)


def _inproj(layer, ws4, bs4, xs, p1, p2, x, vecs, w_in, ws, bs_t, cw, f32_weights):
    tm = PROMPT_TM
    nt = SEQ // tm
    rows = BATCH * SEQ
    n = SAMPLE_ROWS
    smem = pl.BlockSpec(memory_space=pltpu.SMEM)

    def tile_of(s):
        return jnp.maximum(s - 1, 0)

    def rowblk(width):
        return pl.BlockSpec((tm, width), lambda s: (tile_of(s), 0))

    def slab(k):
        return pl.BlockSpec((k, tm, LANES), lambda s: (0, tile_of(s), 0))

    def whole(width):
        return pl.BlockSpec((n, width), lambda s: (0, 0))

    win_spec = pl.BlockSpec((1, KV_W, tm),
                            lambda s: (tile_of(s) // nt, 0, jnp.maximum(tile_of(s) % nt - (nt - WIN // tm), 0)))
    sample_out = [(A_WIDTH, BF16), (Q_W, F32), (KV_W, F32), (KV_W, F32), (C_WIDTH, BF16),
                  (3 * D_MODEL, BF16), (A_WIDTH, F32), (C_WIDTH, F32)]
    cast = [_cast_specs(layer, w.shape[1:], r) for w, r in zip(f32_weights, CAST_ROWS)]
    return pl.pallas_call(
        functools.partial(_inproj_kernel, layer=layer),
        grid=(1 + BATCH * nt,),
        in_specs=[
            smem, smem,
            pl.BlockSpec((n, D_MODEL), lambda s: (0, 0), pipeline_mode=pl.Buffered(1)),
            _layer_spec(layer, (n, C_WIDTH)), _layer_spec(layer, (n, C_WIDTH)),
            rowblk(D_MODEL),
            _vec_spec(layer, "gpm"),
            _resident((D_MODEL, IN_W)),
            _vec_spec(layer, "lng"),
            _vec_spec(layer, "lnb"),
            _layer_spec(layer, (A_GROUPS, CHUNK, CHUNK)),
            _layer_spec(layer, (CHUNK, A_GROUPS)),
            _layer_spec(layer, (CONV_W, C_WIDTH)),
            _vec_spec(layer, "cbias"),
            _vec_spec(layer, "bg"),
            pl.BlockSpec((INPROJ_SUB, D_MODEL), lambda s: (jnp.minimum(s, BATCH * nt - 1) * (tm // INPROJ_SUB), 0)),
        ] + [c[0] for c in cast],
        out_specs=[whole(w) for w, _ in sample_out] + [
            rowblk(A_WIDTH),
            slab(Q_W // LANES),
            slab(KV_W // LANES),
            slab(KV_W // LANES),
            win_spec,
            win_spec,
            rowblk(C_WIDTH),
            rowblk(3 * D_MODEL),
            pl.BlockSpec((1, SUBLANES, C_WIDTH), lambda s: (tile_of(s) // nt, 0, 0)),
        ] + [c[1] for c in cast],
        out_shape=[jax.ShapeDtypeStruct((n, w), dt) for w, dt in sample_out] + [
            jax.ShapeDtypeStruct((rows, A_WIDTH), BF16),
            jax.ShapeDtypeStruct((Q_W // LANES, rows, LANES), F32),
            jax.ShapeDtypeStruct((KV_W // LANES, rows, LANES), F32),
            jax.ShapeDtypeStruct((KV_W // LANES, rows, LANES), F32),
            jax.ShapeDtypeStruct((BATCH, KV_W, WIN), F32),
            jax.ShapeDtypeStruct((BATCH, KV_W, WIN), F32),
            jax.ShapeDtypeStruct((rows, C_WIDTH), BF16),
            jax.ShapeDtypeStruct((rows, 3 * D_MODEL), BF16),
            jax.ShapeDtypeStruct((BATCH, SUBLANES, C_WIDTH), F32),
        ] + [jax.ShapeDtypeStruct(w.shape[1:], BF16) for w in f32_weights],
        scratch_shapes=[pltpu.VMEM((SUBLANES, C_WIDTH), F32), pltpu.VMEM((INPROJ_SUB, D_MODEL), BF16)],
        compiler_params=_params(("arbitrary",)),
        name="inproj",
    )(ws4, bs4, xs, p1, p2, x, vecs, w_in, vecs, vecs, ws, bs_t, cw, vecs, vecs, x, *f32_weights)


def _attn_sample_kernel(qh_ref, kn_ref, vn_ref, kt_ref, vt_ref, o_ref, bias_ref):
    nr = N_DIL * SUBLANES
    width = MAX_WINDOW

    @pl.when(pl.program_id(0) == 0)
    def _():
        row = lax.broadcasted_iota(jnp.int32, (nr, width), 0)
        pos = lax.broadcasted_iota(jnp.int32, (nr, width), 1)
        grp = row >> 3
        delta = width + (row & (SUBLANES - 1)) - pos
        dil_m1 = _select_rows(grp, [d - 1 for d in DILS])
        reach = _select_rows(grp, [BAND * d for d in DILS])
        ok = jnp.logical_and((delta & dil_m1) == 0, delta <= reach)
        for h in range(KV_HEADS):
            slope = _select_rows(grp, [jnp.float32(SLOPES[g][h]) for g in range(N_DIL)])
            bias_ref[h * nr:(h + 1) * nr, :] = jnp.where(ok, -(slope * delta.astype(F32)), NEG)

    ridx = lax.broadcasted_iota(jnp.int32, (nr, 1), 0)
    g_of = ridx >> 3
    t_of = ridx & (SUBLANES - 1)
    scale = HEAD_DIM ** -0.5
    slopes = [_select_rows(g_of, [jnp.float32(SLOPES[g][h]) for g in range(N_DIL)]) for h in range(KV_HEADS)]
    for e in range(qh_ref.shape[0]):
        qs = [qh_ref[e, h] * scale for h in range(KV_HEADS)]
        s = jnp.concatenate([_dot(qs[h].astype(BF16), kt_ref[0, e, h].astype(BF16)) for h in range(KV_HEADS)],
                            axis=0)
        s = s + bias_ref[...]
        m = jnp.max(s, axis=-1, keepdims=True)
        s_new = []
        for n in range(DEC_SEQ):
            dn = t_of - n
            ok = jnp.logical_or(dn == 0, jnp.logical_and(g_of == 0, dn > 0))
            rows = []
            for h in range(KV_HEADS):
                raw = jnp.sum(qs[h] * kn_ref[e, h, n:n + 1, :], axis=-1, keepdims=True)
                rows.append(jnp.where(ok, raw - slopes[h] * dn.astype(F32), NEG))
            sn = jnp.concatenate(rows, axis=0)
            s_new.append(sn)
            m = jnp.maximum(m, sn)
        p = jnp.exp(s - m)
        l_all = jnp.sum(p, axis=-1, keepdims=True)
        p16 = p.astype(BF16)
        p_new = [jnp.exp(sn - m) for sn in s_new]
        for pn in p_new:
            l_all = l_all + pn
        for h in range(KV_HEADS):
            hr = slice(h * nr, (h + 1) * nr)
            r = _dot_nt(p16[hr], vt_ref[0, e, h].astype(BF16))
            for n in range(DEC_SEQ):
                r = r + p_new[n][hr] * vn_ref[e, h, n:n + 1, :]
            m_h = m[hr]
            l_h = l_all[hr]
            parts = [(r[g * SUBLANES:(g + 1) * SUBLANES], m_h[g * SUBLANES:(g + 1) * SUBLANES],
                      l_h[g * SUBLANES:(g + 1) * SUBLANES]) for g in range(N_DIL)]
            m_all = jnp.maximum(jnp.maximum(parts[0][1], parts[1][1]), parts[2][1])
            num = jnp.zeros((SUBLANES, HEAD_DIM), F32)
            den = jnp.zeros((SUBLANES, 1), F32)
            for o_g, m_g, l_g in parts:
                w = jnp.exp(m_g - m_all)
                num = num + o_g * w
                den = den + l_g * w
            o_ref[e, :, h * HEAD_DIM:(h + 1) * HEAD_DIM] = (num / den)[0:DEC_SEQ]


def _attn_sample(layer, qh, knh, vnh, ckt, cvt):
    nr = N_DIL * SUBLANES
    eb = SAMPLE_EB
    cache_spec = pl.BlockSpec((1, eb, KV_HEADS, HEAD_DIM, MAX_WINDOW), lambda b: (layer, b, 0, 0, 0))
    new_spec = pl.BlockSpec((eb, KV_HEADS, SUBLANES, HEAD_DIM), lambda b: (b, 0, 0, 0))
    return pl.pallas_call(
        _attn_sample_kernel,
        grid=(DEC_BATCH // eb,),
        in_specs=[pl.BlockSpec((eb, KV_HEADS, nr, HEAD_DIM), lambda b: (b, 0, 0, 0)),
                  new_spec, new_spec, cache_spec, cache_spec],
        out_specs=pl.BlockSpec((eb, DEC_SEQ, KV_W), lambda b: (b, 0, 0)),
        out_shape=jax.ShapeDtypeStruct((DEC_BATCH, DEC_SEQ, KV_W), F32),
        scratch_shapes=[pltpu.VMEM((KV_HEADS * nr, MAX_WINDOW), F32)],
        compiler_params=_params(("arbitrary",)),
        name="attn_sample",
    )(qh, knh, vnh, ckt, cvt)


def kernel(x_prompt, x_sample, cache_k_win, cache_v_win, state_conv, g_pre_mix, g_post_mix, g_pre_ffn, g_post_ffn,
           w_in, a_ln_g, a_ln_b, a_ws, a_bs, c_conv_w, c_conv_b, w_br_a, w_br_b, w_br_c, b_gate, w_o,
           w_ff_gate, w_ff_up, w_ff_down):
    xp = x_prompt.reshape(BATCH * SEQ, D_MODEL)
    xs = x_sample.reshape(SAMPLE_ROWS, D_MODEL)
    ckt = jnp.transpose(cache_k_win, (0, 1, 3, 4, 2))
    cvt = jnp.transpose(cache_v_win, (0, 1, 3, 4, 2))

    w_in_b = w_in[0].astype(BF16)
    f32_weights = (w_br_a, w_br_b, w_br_c, w_o, w_ff_gate, w_ff_up, w_ff_down)
    vecs = jnp.concatenate([b_gate, g_pre_mix, g_post_mix, g_pre_ffn, g_post_ffn, a_ln_g, a_ln_b, c_conv_b],
                           axis=1).reshape(DEPTH, 1, -1)
    bs_t = jnp.swapaxes(a_bs, 1, 2)
    ws4 = a_ws[:, :, :DEC_SEQ, :DEC_SEQ].reshape(-1)
    bs4 = a_bs[:, :, :DEC_SEQ].reshape(-1)
    zeros = jnp.zeros((DEPTH, DEC_BATCH, 1, C_WIDTH), F32)
    p1 = jnp.concatenate([state_conv[:, :, 1:2], zeros, zeros, zeros], axis=2).reshape(DEPTH, SAMPLE_ROWS, C_WIDTH)
    p2 = jnp.concatenate([state_conv, zeros, zeros], axis=2).reshape(DEPTH, SAMPLE_ROWS, C_WIDTH)

    kp_l, vp_l, ks_l, vs_l, cp_l, cs_l, av_l = [], [], [], [], [], [], []
    for l in range(DEPTH):
        outs = _inproj(l, ws4, bs4, xs, p1, p2, xp, vecs, w_in_b, a_ws, bs_t, c_conv_w, f32_weights)
        oa_s, q_s, k_s, v_s, oc_s, gates_s, vn_s, cz_s, oa, q, k, v, kt, vt, oc, gates, cs8 = outs[:17]
        weights = outs[17:]
        ob = _attn_prompt(q, k, v)

        def window(t):
            return jnp.transpose(t.reshape(BATCH, KV_HEADS, HEAD_DIM, WIN), (0, 3, 1, 2))

        kp_l.append(window(kt))
        vp_l.append(window(vt))
        cp_l.append(cs8[:, SUBLANES - (CONV_W - 1):])


        pad_t = ((0, 0), (0, 0), (0, 0), (0, SUBLANES - DEC_SEQ), (0, 0))
        qh = q_s.reshape(DEC_BATCH, DEC_SEQ, N_DIL, KV_HEADS, HEAD_DIM).transpose(0, 3, 2, 1, 4)
        qh = jnp.pad(qh, pad_t).reshape(DEC_BATCH, KV_HEADS, N_DIL * SUBLANES, HEAD_DIM)

        def new_rows(a):
            a = a.reshape(DEC_BATCH, DEC_SEQ, KV_HEADS, HEAD_DIM).transpose(0, 2, 1, 3)
            return jnp.pad(a, pad_t[1:])

        ob_s = _attn_sample(l, qh, new_rows(k_s), new_rows(v_s), ckt, cvt)
        ob_s = ob_s.reshape(SAMPLE_ROWS, KV_W)
        last = l + 1 == DEPTH
        res = _merge_ffn(l, xs, oa_s, ob_s, oc_s, gates_s, xp, oa, ob, oc, gates, vecs, weights,
                         None if last else w_in)
        xs, xp = res[0], res[1]
        if not last:
            w_in_b = res[2]

        ks_l.append(k_s.reshape(DEC_BATCH, DEC_SEQ, KV_HEADS, HEAD_DIM))
        vs_l.append(v_s.reshape(DEC_BATCH, DEC_SEQ, KV_HEADS, HEAD_DIM))
        cs_l.append(cz_s.reshape(DEC_BATCH, DEC_SEQ, C_WIDTH)[:, DEC_SEQ - (CONV_W - 1):])
        av_l.append(vn_s.reshape(DEC_BATCH, DEC_SEQ, A_WIDTH))

    return (xp.reshape(BATCH, SEQ, D_MODEL), xs.reshape(DEC_BATCH, DEC_SEQ, D_MODEL),
            jnp.stack(kp_l), jnp.stack(vp_l), jnp.stack(ks_l), jnp.stack(vs_l),
            jnp.stack(cp_l), jnp.stack(cs_l), jnp.stack(av_l))
```

```python
import functools

import jax
import jax.numpy as jnp
from jax import lax
from jax.experimental import pallas as pl
from jax.experimental.pallas import tpu as pltpu

F32 = jnp.float32
BF16 = jnp.bfloat16

D_MODEL = 1024
BATCH = 2
SEQ = 8192
DEPTH = 2
DEC_BATCH = 32
DEC_SEQ = 4
CHUNK = 128
A_GROUPS = 4
A_GROUP_DIM = 128
A_WIDTH = A_GROUPS * A_GROUP_DIM
HEAD_DIM = 64
KV_HEADS = 4
DILS = (1, 4, 16)
N_DIL = len(DILS)
Q_W = N_DIL * KV_HEADS * HEAD_DIM
KV_W = KV_HEADS * HEAD_DIM
BAND = 128
MAX_WINDOW = 2048
C_WIDTH = 512
CONV_W = 3
D_FF = 2816
EPS = 1e-6

OFF_U = 0
OFF_V = OFF_U + A_WIDTH
OFF_Q = OFF_V + A_WIDTH
OFF_K = OFF_Q + Q_W
OFF_VV = OFF_K + KV_W
OFF_CX = OFF_VV + KV_W
OFF_CB = OFF_CX + C_WIDTH
OFF_CC = OFF_CB + C_WIDTH
OFF_G = OFF_CC + C_WIDTH
IN_W = OFF_G + 3 * D_MODEL

LANES = 128
SUBLANES = 8
VMEM_LIMIT = 56 * 1024 * 1024
NEG = -1e30

SAMPLE_ROWS = DEC_BATCH * DEC_SEQ
WIN = min(MAX_WINDOW, SEQ)
SPAN = BAND * DILS[-1]
RES = DILS[1]
PROMPT_TM = 512
ROW_SUB = 256
INPROJ_SUB = ROW_SUB
INPROJ_LAG = 3
MERGE_FFN_LAG = 3
CAST_ROWS = (16, 16, 16, 32, 32, 32, 128)
WIN_CAST_ROWS = 32
SAMPLE_EB = 4
MXU_N = 256
FF_CHUNKS = ((0, 6 * MXU_N), (6 * MXU_N, D_FF))
LOG2E = 1.4426950408889634
QSCALE = LOG2E * HEAD_DIM ** -0.5

SLOPES = tuple(
    tuple(2.0 ** (-8.0 * (g * KV_HEADS + h + 1) / (N_DIL * KV_HEADS)) for h in range(KV_HEADS))
    for g in range(N_DIL)
)


def _rms(x, g):
    return x * lax.rsqrt(jnp.mean(x * x, axis=-1, keepdims=True) + EPS) * g


def _layer_norm(x, g, b):
    mu = jnp.mean(x, axis=-1, keepdims=True)
    xc = x - mu
    return xc * lax.rsqrt(jnp.mean(xc * xc, axis=-1, keepdims=True) + EPS) * g + b


def _dot(a, b):
    return jnp.dot(a, b, preferred_element_type=F32)


def _dot_nt(a, b):
    return lax.dot_general(a, b, (((1,), (1,)), ((), ())), preferred_element_type=F32)


def _select_rows(hid, vals):
    out = vals[-1]
    for h in range(len(vals) - 2, -1, -1):
        out = jnp.where(hid == h, vals[h], out)
    return out


def _layer_spec(layer, shape):
    nd = len(shape)
    return pl.BlockSpec((None,) + tuple(shape), lambda *_: (layer,) + (0,) * nd, pipeline_mode=pl.Buffered(1))


VEC_SLOTS = {"bg": (3 * D_MODEL, 0), "gpm": (D_MODEL, 3), "gqm": (D_MODEL, 4), "gpf": (D_MODEL, 5),
             "gqf": (D_MODEL, 6), "lng": (A_WIDTH, 14), "lnb": (A_WIDTH, 15), "cbias": (C_WIDTH, 16)}


def _vec_spec(layer, name):
    width, idx = VEC_SLOTS[name]
    return pl.BlockSpec((None, 1, width), lambda *_: (layer, 0, idx), pipeline_mode=pl.Buffered(1))


def _trace_staggered(stage_gens, lag):
    live = [True] * len(stage_gens)
    tick = 0
    while any(live):
        for j, gen in enumerate(stage_gens):
            if live[j] and tick >= j * lag:
                live[j] = next(gen, "done") != "done"
        tick += 1


def _resident(shape):
    nd = len(shape)
    return pl.BlockSpec(tuple(shape), lambda *_: (0,) * nd, pipeline_mode=pl.Buffered(1))


def _cast_specs(layer, shape, rows_per_step):
    last = shape[0] // rows_per_step - 1

    def piece(s):
        return jnp.minimum(jnp.maximum(s - 1, 0), last)

    return (pl.BlockSpec((None, rows_per_step, shape[1]), lambda s: (layer, piece(s), 0)),
            pl.BlockSpec((rows_per_step, shape[1]), lambda s: (piece(s), 0)))


def _params(sem):
    return pltpu.CompilerParams(dimension_semantics=sem, vmem_limit_bytes=VMEM_LIMIT)


def _inproj_prompt_kernel(x_ref, gpre_ref, w_ref, lng_ref, lnb_ref, ws_ref, bs_ref, cw_ref,
                          cbias_ref, bg_ref,
                          oa_ref, q_ref, k_ref, v_ref, kt_ref, vt_ref, oc_ref, gate_ref, cs_ref, prev_ref, *, tile,
                          side_stages=()):
    tm = x_ref.shape[0]

    @pl.when(tile == 0)
    def _():
        prev_ref[...] = jnp.zeros_like(prev_ref)

    row = lax.broadcasted_iota(jnp.int32, (CHUNK, CHUNK), 0)
    col = lax.broadcasted_iota(jnp.int32, (CHUNK, CHUNK), 1)
    wt = [jnp.where(row >= col, ws_ref[g], 0.0).astype(BF16) for g in range(A_GROUPS)]
    rsub = lax.broadcasted_iota(jnp.int32, (INPROJ_SUB, 1), 0)

    carry = {"tail": prev_ref[...]}

    def sub_tile(r0):
        rows = slice(r0, r0 + INPROJ_SUB)
        h = _rms(x_ref[rows, :], gpre_ref[...]).astype(BF16)

        def proj(off, width):
            return _dot(h, w_ref[:, off:off + width])

        yield
        u = jax.nn.gelu(proj(OFF_U, A_WIDTH))
        yield
        vn = _layer_norm(jax.nn.gelu(proj(OFF_V, A_WIDTH)), lng_ref[...], lnb_ref[...])
        for j in range(3):
            yield
            cs = slice(j * D_MODEL, (j + 1) * D_MODEL)
            gate_ref[rows, cs] = jax.nn.sigmoid(proj(OFF_G + j * D_MODEL, D_MODEL) + bg_ref[:, cs]).astype(BF16)
        yield
        for g in range(A_GROUPS):
            cs = slice(g * A_GROUP_DIM, (g + 1) * A_GROUP_DIM)
            for c in range(0, INPROJ_SUB // CHUNK, 2):
                pair = jnp.concatenate([vn[(c + j) * CHUNK:(c + j + 1) * CHUNK, cs] for j in range(2)], axis=1)
                sa2 = _dot(wt[g], pair.astype(BF16)) + bs_ref[:, g:g + 1]
                for j in range(2):
                    rs = slice((c + j) * CHUNK, (c + j + 1) * CHUNK)
                    sa = sa2[:, j * A_GROUP_DIM:(j + 1) * A_GROUP_DIM]
                    oa_ref[r0 + (c + j) * CHUNK:r0 + (c + j + 1) * CHUNK, cs] = (u[rs, cs] * sa).astype(BF16)
        yield
        cz = proj(OFF_CC, C_WIDTH) * proj(OFF_CX, C_WIDTH)
        p2 = carry["tail"][SUBLANES - 2:SUBLANES - 1, :]
        p1 = carry["tail"][SUBLANES - 1:SUBLANES, :]
        carry["tail"] = cz[INPROJ_SUB - SUBLANES:INPROJ_SUB, :]
        z1 = jnp.where(rsub == 0, p1, pltpu.roll(cz, 1, 0))
        z2 = jnp.where(rsub == 0, p2, jnp.where(rsub == 1, p1, pltpu.roll(cz, 2, 0)))
        y = cbias_ref[...] + cw_ref[0:1, :] * z2 + cw_ref[1:2, :] * z1 + cw_ref[2:3, :] * cz
        yield
        oc_ref[rows, :] = (proj(OFF_CB, C_WIDTH) * y).astype(BF16)
        yield
        qv = proj(OFF_Q, Q_W)
        for c in range(Q_W // LANES):
            q_ref[c, rows, :] = qv[:, c * LANES:(c + 1) * LANES] * QSCALE
        yield
        kv = proj(OFF_K, 2 * KV_W)
        for c in range(KV_W // LANES):
            k_ref[c, rows, :] = kv[:, c * LANES:(c + 1) * LANES]
            v_ref[c, rows, :] = kv[:, KV_W + c * LANES:KV_W + (c + 1) * LANES]
        kt_ref[0, :, rows] = kv[:, 0:KV_W].T
        vt_ref[0, :, rows] = kv[:, KV_W:2 * KV_W].T

    _trace_staggered([sub_tile(r0) for r0 in range(0, tm, INPROJ_SUB)] + list(side_stages), INPROJ_LAG)
    prev_ref[...] = carry["tail"]
    cs_ref[0] = carry["tail"]


def _attn_prompt_kernel(q_ref, k_ref, v_ref, o_ref, ktail, vtail, k4, v4, q24, acc, mrun, lrun, tmp, bias_ref):
    b = pl.program_id(0)
    i = pl.program_id(1)
    first = i == 0
    nslab = KV_W // LANES
    quarter = SPAN // RES

    @pl.when(jnp.logical_and(b == 0, first))
    def _():
        rr = lax.broadcasted_iota(jnp.int32, (KV_HEADS * BAND, 2 * BAND), 0)
        kk = lax.broadcasted_iota(jnp.int32, (KV_HEADS * BAND, 2 * BAND), 1)
        dist = (rr & (BAND - 1)) + BAND - kk
        hrow = rr >> 7
        ok = jnp.logical_and(dist >= 0, dist <= BAND)
        for g, dil in enumerate(DILS):
            coef = _select_rows(hrow, [jnp.float32(SLOPES[g][h] * dil * LOG2E) for h in range(KV_HEADS)])
            full = jnp.where(ok, -(coef * dist.astype(F32)), NEG)
            bias_ref[2 * g] = full
            bias_ref[2 * g + 1] = jnp.where(kk >= BAND, full, NEG)

    @pl.when(first)
    def _():
        for r in range(RES):
            base = r * 2 * quarter
            k4[:, base:base + quarter, :] = jnp.zeros((nslab, quarter, LANES), F32)
            v4[:, base:base + quarter, :] = jnp.zeros((nslab, quarter, LANES), F32)
        ktail[...] = jnp.zeros_like(ktail)
        vtail[...] = jnp.zeros_like(vtail)

    @pl.when(i > 0)
    def _():
        for r in range(RES):
            base = r * 2 * quarter
            k4[:, base:base + quarter, :] = k4[:, base + quarter:base + 2 * quarter, :]
            v4[:, base:base + quarter, :] = v4[:, base + quarter:base + 2 * quarter, :]

    for r in range(RES):
        base = r * 2 * quarter + quarter
        for c in range(nslab):
            k4[c, base:base + quarter, :] = k_ref[c, pl.ds(r, quarter, stride=RES), :]
            v4[c, base:base + quarter, :] = v_ref[c, pl.ds(r, quarter, stride=RES), :]
            q24[c, r * quarter:(r + 1) * quarter, :] = q_ref[2 * nslab + c, pl.ds(r, quarter, stride=RES), :]

    lane = lax.broadcasted_iota(jnp.int32, (1, KV_W), 1)
    hid = lane >> 6
    headmask = [jnp.where(hid == h, 1.0, 0.0).astype(BF16) for h in range(KV_HEADS)]

    def wide(ref, rows, slab0=0):
        return jnp.concatenate([ref[slab0 + c, rows, :] for c in range(nslab)], axis=1)

    def softmax_unit(qb, kb, vb, bias):
        qb16 = qb.astype(BF16)
        kb16 = kb.astype(BF16)
        vb16 = vb.astype(BF16)
        qs = jnp.concatenate([qb16 * headmask[h] for h in range(KV_HEADS)], axis=0)
        s = _dot_nt(qs, kb16)
        ps, ms, ls = [], [], []
        for h in range(KV_HEADS):
            sh = s[h * BAND:(h + 1) * BAND] + bias_ref[bias, h * BAND:(h + 1) * BAND, :]
            mh = jnp.max(sh, axis=-1, keepdims=True)
            ph = jnp.exp2(sh - mh)
            ps.append(ph.astype(BF16))
            ms.append(mh)
            ls.append(jnp.sum(ph, axis=-1, keepdims=True))
        r = _dot(jnp.concatenate(ps, axis=0), vb16)
        rs = [r[h * BAND:(h + 1) * BAND] for h in range(KV_HEADS)]
        return _select_rows(hid, rs), _select_rows(hid, ms), _select_rows(hid, ls)

    def store_stats(rows, o, m, l):
        for c in range(nslab):
            ls = slice(c * LANES, (c + 1) * LANES)
            acc[c, rows, :] = o[:, ls]
            mrun[c, rows, :] = m[:, ls]
            lrun[c, rows, :] = l[:, ls]

    def merge_stats(rows, o_u, m_u, l_u):
        m_old = wide(mrun, rows)
        m_n = jnp.maximum(m_old, m_u)
        a_old = jnp.exp2(m_old - m_n)
        a_new = jnp.exp2(m_u - m_n)
        store_stats(rows, wide(acc, rows) * a_old + o_u * a_new, m_n, wide(lrun, rows) * a_old + l_u * a_new)

    def unit0(n, kb, vb, bias):
        q0 = n * BAND if isinstance(n, int) else pl.multiple_of(n * BAND, BAND)
        stats = softmax_unit(wide(q_ref, pl.ds(q0, BAND)), kb, vb, bias)
        sub = BAND // RES
        for j, x in enumerate(stats):
            for c in range(nslab):
                tmp[j * nslab + c, pl.ds(q0, BAND), :] = x[:, c * LANES:(c + 1) * LANES]
        for r in range(RES):
            src = pl.ds(q0 + r, sub, stride=RES)
            dst = pl.ds(r * quarter + n * sub, sub)
            for c in range(nslab):
                acc[c, dst, :] = tmp[c, src, :]
                mrun[c, dst, :] = tmp[nslab + c, src, :]
                lrun[c, dst, :] = tmp[2 * nslab + c, src, :]

    kb0 = jnp.concatenate([wide(ktail, pl.ds(0, BAND)), wide(k_ref, pl.ds(0, BAND))], axis=0)
    vb0 = jnp.concatenate([wide(vtail, pl.ds(0, BAND)), wide(v_ref, pl.ds(0, BAND))], axis=0)
    unit0(0, kb0, vb0, jnp.where(first, 1, 0))

    def body0(n, carry):
        keys = pl.ds(pl.multiple_of((n - 1) * BAND, BAND), 2 * BAND)
        unit0(n, wide(k_ref, keys), wide(v_ref, keys), 0)
        return carry

    lax.fori_loop(1, SPAN // BAND, body0, 0, unroll=5)

    def body1(s, carry):
        bias = 2 + jnp.where(jnp.logical_and(first, s == 0), 1, 0)
        for r in range(RES):
            qb = wide(q_ref, pl.ds(s * (BAND * RES) + r, BAND, stride=RES), nslab)
            keys = pl.ds(r * 2 * quarter + quarter + (s - 1) * BAND, 2 * BAND)
            o_u, m_u, l_u = softmax_unit(qb, wide(k4, keys), wide(v4, keys), bias)
            merge_stats(pl.ds(r * quarter + s * BAND, BAND), o_u, m_u, l_u)
        return carry

    lax.fori_loop(0, quarter // BAND, body1, 0, unroll=2)

    def body2(r, carry):
        bias = 4 + jnp.where(first, 1, 0)
        for a in range(DILS[2] // RES):
            rows = pl.ds(r * quarter + a, BAND, stride=RES)
            keys = pl.ds(r * 2 * quarter + a, 2 * BAND, stride=RES)
            o_u, m_u, l_u = softmax_unit(wide(q24, rows), wide(k4, keys), wide(v4, keys), bias)
            merge_stats(rows, o_u, m_u, l_u)
        return carry

    lax.fori_loop(0, RES, body2, 0, unroll=2)

    for r in range(RES):
        rows = pl.ds(r * quarter, quarter)
        for c in range(nslab):
            tmp[c, pl.ds(r, quarter, stride=RES), :] = acc[c, rows, :] / lrun[c, rows, :]
    for c in range(nslab):
        o_ref[:, c * LANES:(c + 1) * LANES] = tmp[c].astype(BF16)
    ktail[...] = k_ref[:, SPAN - BAND:SPAN, :]
    vtail[...] = v_ref[:, SPAN - BAND:SPAN, :]


def _attn_prompt(q, k, v):
    ns = SEQ // SPAN
    nslab = KV_W // LANES
    rows = BATCH * SEQ

    def slab(n):
        return pl.BlockSpec((n, SPAN, LANES), lambda b, i: (0, b * ns + i, 0))

    return pl.pallas_call(
        _attn_prompt_kernel,
        grid=(BATCH, ns),
        in_specs=[slab(Q_W // LANES), slab(nslab), slab(nslab)],
        out_specs=pl.BlockSpec((SPAN, KV_W), lambda b, i: (b * ns + i, 0)),
        out_shape=jax.ShapeDtypeStruct((rows, KV_W), BF16),
        scratch_shapes=[
            pltpu.VMEM((nslab, BAND, LANES), F32),
            pltpu.VMEM((nslab, BAND, LANES), F32),
            pltpu.VMEM((nslab, 2 * SPAN, LANES), F32),
            pltpu.VMEM((nslab, 2 * SPAN, LANES), F32),
            pltpu.VMEM((nslab, SPAN, LANES), F32),
            pltpu.VMEM((nslab, SPAN, LANES), F32),
            pltpu.VMEM((nslab, SPAN, LANES), F32),
            pltpu.VMEM((nslab, SPAN, LANES), F32),
            pltpu.VMEM((3 * nslab, SPAN, LANES), F32),
            pltpu.VMEM((2 * N_DIL, KV_HEADS * BAND, 2 * BAND), F32),
        ],
        compiler_params=_params(("arbitrary", "arbitrary")),
        name="attn_prompt",
    )(q, k, v)


def _merge_ffn_kernel(xs_ref, oas_ref, obs_ref, ocs_ref, gates_ref, x_ref, oa_ref, ob_ref, oc_ref, gate_ref,
                      gqm_ref, gpf_ref, gqf_ref, wa_ref, wb_ref, wc_ref, wo_ref, wg_ref, wu_ref, wd_ref, *rest):
    os_ref, o_ref = rest[-3:-1] if len(rest) == 4 else rest
    def run(x_ref, oa_ref, ob_ref, oc_ref, gate_ref, o_ref):
        tm = x_ref.shape[0]
        sub = min(tm, ROW_SUB)

        def sub_tile(r0):
            rows = slice(r0, r0 + sub)

            def gate(j):
                return gate_ref[rows, j * D_MODEL:(j + 1) * D_MODEL].astype(F32)

            merged = gate(0) * _dot(oa_ref[rows, :], wa_ref[...])
            yield
            merged = merged + gate(1) * _dot(ob_ref[rows, :].astype(BF16), wb_ref[...])
            yield
            merged = merged + gate(2) * _dot(oc_ref[rows, :], wc_ref[...])
            yield
            x = x_ref[rows, :] + _rms(_dot(merged.astype(BF16), wo_ref[...]), gqm_ref[...])
            yield
            h = _rms(x, gpf_ref[...]).astype(BF16)
            y = None
            for lo, hi in FF_CHUNKS:
                yield
                a = jax.nn.silu(_dot(h, wg_ref[:, lo:hi]))
                yield
                act = (a * _dot(h, wu_ref[:, lo:hi])).astype(BF16)
                yield
                part = _dot(act, wd_ref[lo:hi, :])
                y = part if y is None else y + part
            yield
            o_ref[rows, :] = x + _rms(y, gqf_ref[...])

        _trace_staggered([sub_tile(r0) for r0 in range(0, tm, sub)], MERGE_FFN_LAG)

    step = pl.program_id(0)

    @pl.when(step == 0)
    def _():
        run(xs_ref, oas_ref, obs_ref, ocs_ref, gates_ref, os_ref)

    @pl.when(step > 0)
    def _():
        run(x_ref, oa_ref, ob_ref, oc_ref, gate_ref, o_ref)
        if len(rest) == 4:
            rest[3][...] = rest[0][...].astype(BF16)


def _merge_ffn(layer, xs, oa_s, ob_s, oc_s, gates_s, x, oa, ob, oc, gates, vecs, weights, w_in_f32):
    tm = PROMPT_TM
    rows = x.shape[0]
    ns = xs.shape[0]

    def rowblk(width):
        return pl.BlockSpec((tm, width), lambda i: (jnp.maximum(i - 1, 0), 0))

    def whole(width):
        return pl.BlockSpec((ns, width), lambda i: (0, 0), pipeline_mode=pl.Buffered(1))

    nxt = [] if w_in_f32 is None else [_cast_specs(layer + 1, (D_MODEL, IN_W), WIN_CAST_ROWS)]
    return pl.pallas_call(
        _merge_ffn_kernel,
        grid=(1 + rows // tm,),
        in_specs=[
            whole(D_MODEL), whole(A_WIDTH), whole(KV_W), whole(C_WIDTH), whole(3 * D_MODEL),
            rowblk(D_MODEL), rowblk(A_WIDTH), rowblk(KV_W), rowblk(C_WIDTH), rowblk(3 * D_MODEL),
            _vec_spec(layer, "gqm"), _vec_spec(layer, "gpf"), _vec_spec(layer, "gqf"),
        ] + [_resident(w.shape) for w in weights] + [c[0] for c in nxt],
        out_specs=[pl.BlockSpec((ns, D_MODEL), lambda i: (0, 0)), rowblk(D_MODEL)] + [c[1] for c in nxt],
        out_shape=[jax.ShapeDtypeStruct((ns, D_MODEL), F32), jax.ShapeDtypeStruct((rows, D_MODEL), F32)]
        + [jax.ShapeDtypeStruct((D_MODEL, IN_W), BF16) for _ in nxt],
        compiler_params=_params(("arbitrary",)),
        name="merge_ffn",
    )(xs, oa_s, ob_s, oc_s, gates_s, x, oa, ob, oc, gates, vecs, vecs, vecs, *weights,
      *([] if w_in_f32 is None else [w_in_f32]))


def _inproj_sample_kernel(ws_ref, bs_ref, x_ref, gpre_ref, w_ref, lng_ref, lnb_ref, cw_ref, cbias_ref, bg_ref,
                          p1_ref, p2_ref,
                          oa_ref, q_ref, k_ref, v_ref, oc_ref, gate_ref, vn_ref, cz_ref, *, layer):
    n = SAMPLE_ROWS
    h = _rms(x_ref[...], gpre_ref[...]).astype(BF16)

    def proj(off, width):
        return _dot(h, w_ref[:, off:off + width])

    u = jax.nn.gelu(proj(OFF_U, A_WIDTH))
    vn = _layer_norm(jax.nn.gelu(proj(OFF_V, A_WIDTH)), lng_ref[...], lnb_ref[...])
    vn_ref[...] = vn
    t = lax.broadcasted_iota(jnp.int32, (n, 1), 0) & (DEC_SEQ - 1)
    for g in range(A_GROUPS):
        cs = slice(g * A_GROUP_DIM, (g + 1) * A_GROUP_DIM)
        base = (layer * A_GROUPS + g) * DEC_SEQ
        vg = vn[:, cs]
        sa = _select_rows(t, [bs_ref[base + tt] for tt in range(DEC_SEQ)])
        for k in range(DEC_SEQ):
            coef = jnp.zeros((n, 1), F32)
            for tt in range(k, DEC_SEQ):
                coef = jnp.where(t == tt, ws_ref[(base + tt) * DEC_SEQ + tt - k], coef)
            sa = sa + coef * (vg if k == 0 else pltpu.roll(vg, k, 0))
        oa_ref[:, cs] = (u[:, cs] * sa).astype(BF16)

    q_ref[...] = proj(OFF_Q, Q_W)
    kv = proj(OFF_K, 2 * KV_W)
    k_ref[...] = kv[:, 0:KV_W]
    v_ref[...] = kv[:, KV_W:2 * KV_W]

    cz = proj(OFF_CC, C_WIDTH) * proj(OFF_CX, C_WIDTH)
    cz_ref[...] = cz
    z1 = jnp.where(t >= 1, pltpu.roll(cz, 1, 0), p1_ref[...])
    z2 = jnp.where(t >= 2, pltpu.roll(cz, 2, 0), p2_ref[...])
    y = cbias_ref[...] + cw_ref[0:1, :] * z2 + cw_ref[1:2, :] * z1 + cw_ref[2:3, :] * cz
    oc_ref[...] = (proj(OFF_CB, C_WIDTH) * y).astype(BF16)

    for j in range(3):
        cs = slice(j * D_MODEL, (j + 1) * D_MODEL)
        gate_ref[:, cs] = jax.nn.sigmoid(proj(OFF_G + j * D_MODEL, D_MODEL) + bg_ref[:, cs]).astype(BF16)


def _inproj_kernel(ws4_ref, bs4_ref, xs_ref, p1_ref, p2_ref, x_ref, gpre_ref, w_ref, lng_ref, lnb_ref, ws_ref,
                   bst_ref, cw_ref, cbias_ref, bg_ref, *rest, layer):
    nw = len(CAST_ROWS)
    f32_pieces, rest = rest[:nw], rest[nw:]
    oa_s, q_s, k_s, v_s, oc_s, gate_s, vn_s, cz_s = rest[:8]
    oa_ref, q_ref, k_ref, v_ref, kt_ref, vt_ref, oc_ref, gate_ref, cs_ref = rest[8:17]
    bf16_pieces, prev_ref = rest[17:17 + nw], rest[17 + nw]
    step = pl.program_id(0)

    @pl.when(step == 0)
    def _():
        _inproj_sample_kernel(ws4_ref, bs4_ref, xs_ref, gpre_ref, w_ref, lng_ref, lnb_ref, cw_ref, cbias_ref, bg_ref,
                              p1_ref, p2_ref, oa_s, q_s, k_s, v_s, oc_s, gate_s, vn_s, cz_s, layer=layer)

    def casts():
        for src_ref, dst_ref in zip(f32_pieces, bf16_pieces):
            yield
            dst_ref[...] = src_ref[...].astype(BF16)

    @pl.when(step > 0)
    def _():
        _inproj_prompt_kernel(x_ref, gpre_ref, w_ref, lng_ref, lnb_ref, ws_ref, bst_ref, cw_ref, cbias_ref, bg_ref,
                              oa_ref, q_ref, k_ref, v_ref, kt_ref, vt_ref, oc_ref, gate_ref, cs_ref, prev_ref,
                              tile=(step - 1) % (SEQ // PROMPT_TM), side_stages=[casts()])


def _inproj(layer, ws4, bs4, xs, p1, p2, x, vecs, w_in, ws, bs_t, cw, f32_weights):
    tm = PROMPT_TM
    nt = SEQ // tm
    rows = BATCH * SEQ
    n = SAMPLE_ROWS
    smem = pl.BlockSpec(memory_space=pltpu.SMEM)

    def tile_of(s):
        return jnp.maximum(s - 1, 0)

    def rowblk(width):
        return pl.BlockSpec((tm, width), lambda s: (tile_of(s), 0))

    def slab(k):
        return pl.BlockSpec((k, tm, LANES), lambda s: (0, tile_of(s), 0))

    def whole(width):
        return pl.BlockSpec((n, width), lambda s: (0, 0))

    win_spec = pl.BlockSpec((1, KV_W, tm),
                            lambda s: (tile_of(s) // nt, 0, jnp.maximum(tile_of(s) % nt - (nt - WIN // tm), 0)))
    sample_out = [(A_WIDTH, BF16), (Q_W, F32), (KV_W, F32), (KV_W, F32), (C_WIDTH, BF16),
                  (3 * D_MODEL, BF16), (A_WIDTH, F32), (C_WIDTH, F32)]
    cast = [_cast_specs(layer, w.shape[1:], r) for w, r in zip(f32_weights, CAST_ROWS)]
    return pl.pallas_call(
        functools.partial(_inproj_kernel, layer=layer),
        grid=(1 + BATCH * nt,),
        in_specs=[
            smem, smem,
            pl.BlockSpec((n, D_MODEL), lambda s: (0, 0), pipeline_mode=pl.Buffered(1)),
            _layer_spec(layer, (n, C_WIDTH)), _layer_spec(layer, (n, C_WIDTH)),
            rowblk(D_MODEL),
            _vec_spec(layer, "gpm"),
            _resident((D_MODEL, IN_W)),
            _vec_spec(layer, "lng"),
            _vec_spec(layer, "lnb"),
            _layer_spec(layer, (A_GROUPS, CHUNK, CHUNK)),
            _layer_spec(layer, (CHUNK, A_GROUPS)),
            _layer_spec(layer, (CONV_W, C_WIDTH)),
            _vec_spec(layer, "cbias"),
            _vec_spec(layer, "bg"),
        ] + [c[0] for c in cast],
        out_specs=[whole(w) for w, _ in sample_out] + [
            rowblk(A_WIDTH),
            slab(Q_W // LANES),
            slab(KV_W // LANES),
            slab(KV_W // LANES),
            win_spec,
            win_spec,
            rowblk(C_WIDTH),
            rowblk(3 * D_MODEL),
            pl.BlockSpec((1, SUBLANES, C_WIDTH), lambda s: (tile_of(s) // nt, 0, 0)),
        ] + [c[1] for c in cast],
        out_shape=[jax.ShapeDtypeStruct((n, w), dt) for w, dt in sample_out] + [
            jax.ShapeDtypeStruct((rows, A_WIDTH), BF16),
            jax.ShapeDtypeStruct((Q_W // LANES, rows, LANES), F32),
            jax.ShapeDtypeStruct((KV_W // LANES, rows, LANES), F32),
            jax.ShapeDtypeStruct((KV_W // LANES, rows, LANES), F32),
            jax.ShapeDtypeStruct((BATCH, KV_W, WIN), F32),
            jax.ShapeDtypeStruct((BATCH, KV_W, WIN), F32),
            jax.ShapeDtypeStruct((rows, C_WIDTH), BF16),
            jax.ShapeDtypeStruct((rows, 3 * D_MODEL), BF16),
            jax.ShapeDtypeStruct((BATCH, SUBLANES, C_WIDTH), F32),
        ] + [jax.ShapeDtypeStruct(w.shape[1:], BF16) for w in f32_weights],
        scratch_shapes=[pltpu.VMEM((SUBLANES, C_WIDTH), F32)],
        compiler_params=_params(("arbitrary",)),
        name="inproj",
    )(ws4, bs4, xs, p1, p2, x, vecs, w_in, vecs, vecs, ws, bs_t, cw, vecs, vecs, *f32_weights)


def _attn_sample_kernel(qh_ref, kn_ref, vn_ref, kt_ref, vt_ref, o_ref, bias_ref):
    nr = N_DIL * SUBLANES
    width = MAX_WINDOW

    @pl.when(pl.program_id(0) == 0)
    def _():
        row = lax.broadcasted_iota(jnp.int32, (nr, width), 0)
        pos = lax.broadcasted_iota(jnp.int32, (nr, width), 1)
        grp = row >> 3
        delta = width + (row & (SUBLANES - 1)) - pos
        dil_m1 = _select_rows(grp, [d - 1 for d in DILS])
        reach = _select_rows(grp, [BAND * d for d in DILS])
        ok = jnp.logical_and((delta & dil_m1) == 0, delta <= reach)
        for h in range(KV_HEADS):
            slope = _select_rows(grp, [jnp.float32(SLOPES[g][h]) for g in range(N_DIL)])
            bias_ref[h * nr:(h + 1) * nr, :] = jnp.where(ok, -(slope * delta.astype(F32)), NEG)

    ridx = lax.broadcasted_iota(jnp.int32, (nr, 1), 0)
    g_of = ridx >> 3
    t_of = ridx & (SUBLANES - 1)
    scale = HEAD_DIM ** -0.5
    slopes = [_select_rows(g_of, [jnp.float32(SLOPES[g][h]) for g in range(N_DIL)]) for h in range(KV_HEADS)]
    for e in range(qh_ref.shape[0]):
        qs = [qh_ref[e, h] * scale for h in range(KV_HEADS)]
        s = jnp.concatenate([_dot(qs[h].astype(BF16), kt_ref[0, e, h].astype(BF16)) for h in range(KV_HEADS)],
                            axis=0)
        s = s + bias_ref[...]
        m = jnp.max(s, axis=-1, keepdims=True)
        s_new = []
        for n in range(DEC_SEQ):
            dn = t_of - n
            ok = jnp.logical_or(dn == 0, jnp.logical_and(g_of == 0, dn > 0))
            rows = []
            for h in range(KV_HEADS):
                raw = jnp.sum(qs[h] * kn_ref[e, h, n:n + 1, :], axis=-1, keepdims=True)
                rows.append(jnp.where(ok, raw - slopes[h] * dn.astype(F32), NEG))
            sn = jnp.concatenate(rows, axis=0)
            s_new.append(sn)
            m = jnp.maximum(m, sn)
        p = jnp.exp(s - m)
        l_all = jnp.sum(p, axis=-1, keepdims=True)
        p16 = p.astype(BF16)
        p_new = [jnp.exp(sn - m) for sn in s_new]
        for pn in p_new:
            l_all = l_all + pn
        for h in range(KV_HEADS):
            hr = slice(h * nr, (h + 1) * nr)
            r = _dot_nt(p16[hr], vt_ref[0, e, h].astype(BF16))
            for n in range(DEC_SEQ):
                r = r + p_new[n][hr] * vn_ref[e, h, n:n + 1, :]
            m_h = m[hr]
            l_h = l_all[hr]
            parts = [(r[g * SUBLANES:(g + 1) * SUBLANES], m_h[g * SUBLANES:(g + 1) * SUBLANES],
                      l_h[g * SUBLANES:(g + 1) * SUBLANES]) for g in range(N_DIL)]
            m_all = jnp.maximum(jnp.maximum(parts[0][1], parts[1][1]), parts[2][1])
            num = jnp.zeros((SUBLANES, HEAD_DIM), F32)
            den = jnp.zeros((SUBLANES, 1), F32)
            for o_g, m_g, l_g in parts:
                w = jnp.exp(m_g - m_all)
                num = num + o_g * w
                den = den + l_g * w
            o_ref[e, :, h * HEAD_DIM:(h + 1) * HEAD_DIM] = (num / den)[0:DEC_SEQ]


def _attn_sample(layer, qh, knh, vnh, ckt, cvt):
    nr = N_DIL * SUBLANES
    eb = SAMPLE_EB
    cache_spec = pl.BlockSpec((1, eb, KV_HEADS, HEAD_DIM, MAX_WINDOW), lambda b: (layer, b, 0, 0, 0))
    new_spec = pl.BlockSpec((eb, KV_HEADS, SUBLANES, HEAD_DIM), lambda b: (b, 0, 0, 0))
    return pl.pallas_call(
        _attn_sample_kernel,
        grid=(DEC_BATCH // eb,),
        in_specs=[pl.BlockSpec((eb, KV_HEADS, nr, HEAD_DIM), lambda b: (b, 0, 0, 0)),
                  new_spec, new_spec, cache_spec, cache_spec],
        out_specs=pl.BlockSpec((eb, DEC_SEQ, KV_W), lambda b: (b, 0, 0)),
        out_shape=jax.ShapeDtypeStruct((DEC_BATCH, DEC_SEQ, KV_W), F32),
        scratch_shapes=[pltpu.VMEM((KV_HEADS * nr, MAX_WINDOW), F32)],
        compiler_params=_params(("arbitrary",)),
        name="attn_sample",
    )(qh, knh, vnh, ckt, cvt)


def kernel(x_prompt, x_sample, cache_k_win, cache_v_win, state_conv, g_pre_mix, g_post_mix, g_pre_ffn, g_post_ffn,
           w_in, a_ln_g, a_ln_b, a_ws, a_bs, c_conv_w, c_conv_b, w_br_a, w_br_b, w_br_c, b_gate, w_o,
           w_ff_gate, w_ff_up, w_ff_down):
    xp = x_prompt.reshape(BATCH * SEQ, D_MODEL)
    xs = x_sample.reshape(SAMPLE_ROWS, D_MODEL)
    ckt = jnp.transpose(cache_k_win, (0, 1, 3, 4, 2))
    cvt = jnp.transpose(cache_v_win, (0, 1, 3, 4, 2))

    w_in_b = w_in[0].astype(BF16)
    f32_weights = (w_br_a, w_br_b, w_br_c, w_o, w_ff_gate, w_ff_up, w_ff_down)
    vecs = jnp.concatenate([b_gate, g_pre_mix, g_post_mix, g_pre_ffn, g_post_ffn, a_ln_g, a_ln_b, c_conv_b],
                           axis=1).reshape(DEPTH, 1, -1)
    bs_t = jnp.swapaxes(a_bs, 1, 2)
    ws4 = a_ws[:, :, :DEC_SEQ, :DEC_SEQ].reshape(-1)
    bs4 = a_bs[:, :, :DEC_SEQ].reshape(-1)
    zeros = jnp.zeros((DEPTH, DEC_BATCH, 1, C_WIDTH), F32)
    p1 = jnp.concatenate([state_conv[:, :, 1:2], zeros, zeros, zeros], axis=2).reshape(DEPTH, SAMPLE_ROWS, C_WIDTH)
    p2 = jnp.concatenate([state_conv, zeros, zeros], axis=2).reshape(DEPTH, SAMPLE_ROWS, C_WIDTH)

    kp_l, vp_l, ks_l, vs_l, cp_l, cs_l, av_l = [], [], [], [], [], [], []
    for l in range(DEPTH):
        outs = _inproj(l, ws4, bs4, xs, p1, p2, xp, vecs, w_in_b, a_ws, bs_t, c_conv_w, f32_weights)
        oa_s, q_s, k_s, v_s, oc_s, gates_s, vn_s, cz_s, oa, q, k, v, kt, vt, oc, gates, cs8 = outs[:17]
        weights = outs[17:]
        ob = _attn_prompt(q, k, v)

        def window(t):
            return jnp.transpose(t.reshape(BATCH, KV_HEADS, HEAD_DIM, WIN), (0, 3, 1, 2))

        kp_l.append(window(kt))
        vp_l.append(window(vt))
        cp_l.append(cs8[:, SUBLANES - (CONV_W - 1):])


        pad_t = ((0, 0), (0, 0), (0, 0), (0, SUBLANES - DEC_SEQ), (0, 0))
        qh = q_s.reshape(DEC_BATCH, DEC_SEQ, N_DIL, KV_HEADS, HEAD_DIM).transpose(0, 3, 2, 1, 4)
        qh = jnp.pad(qh, pad_t).reshape(DEC_BATCH, KV_HEADS, N_DIL * SUBLANES, HEAD_DIM)

        def new_rows(a):
            a = a.reshape(DEC_BATCH, DEC_SEQ, KV_HEADS, HEAD_DIM).transpose(0, 2, 1, 3)
            return jnp.pad(a, pad_t[1:])

        ob_s = _attn_sample(l, qh, new_rows(k_s), new_rows(v_s), ckt, cvt)
        ob_s = ob_s.reshape(SAMPLE_ROWS, KV_W)
        last = l + 1 == DEPTH
        res = _merge_ffn(l, xs, oa_s, ob_s, oc_s, gates_s, xp, oa, ob, oc, gates, vecs, weights,
                         None if last else w_in)
        xs, xp = res[0], res[1]
        if not last:
            w_in_b = res[2]

        ks_l.append(k_s.reshape(DEC_BATCH, DEC_SEQ, KV_HEADS, HEAD_DIM))
        vs_l.append(v_s.reshape(DEC_BATCH, DEC_SEQ, KV_HEADS, HEAD_DIM))
        cs_l.append(cz_s.reshape(DEC_BATCH, DEC_SEQ, C_WIDTH)[:, DEC_SEQ - (CONV_W - 1):])
        av_l.append(vn_s.reshape(DEC_BATCH, DEC_SEQ, A_WIDTH))

    return (xp.reshape(BATCH, SEQ, D_MODEL), xs.reshape(DEC_BATCH, DEC_SEQ, D_MODEL),
            jnp.stack(kp_l), jnp.stack(vp_l), jnp.stack(ks_l), jnp.stack(vs_l),
            jnp.stack(cp_l), jnp.stack(cs_l), jnp.stack(av_l))
```

```python
import functools

import jax
import jax.numpy as jnp
from jax import lax
from jax.experimental import pallas as pl
from jax.experimental.pallas import tpu as pltpu

F32 = jnp.float32
BF16 = jnp.bfloat16

D_MODEL = 1024
BATCH = 2
SEQ = 8192
DEPTH = 2
DEC_BATCH = 32
DEC_SEQ = 4
CHUNK = 128
A_GROUPS = 4
A_GROUP_DIM = 128
A_WIDTH = A_GROUPS * A_GROUP_DIM
HEAD_DIM = 64
KV_HEADS = 4
DILS = (1, 4, 16)
N_DIL = len(DILS)
Q_W = N_DIL * KV_HEADS * HEAD_DIM
KV_W = KV_HEADS * HEAD_DIM
BAND = 128
MAX_WINDOW = 2048
C_WIDTH = 512
CONV_W = 3
D_FF = 2816
EPS = 1e-6

OFF_U = 0
OFF_V = OFF_U + A_WIDTH
OFF_Q = OFF_V + A_WIDTH
OFF_K = OFF_Q + Q_W
OFF_VV = OFF_K + KV_W
OFF_CX = OFF_VV + KV_W
OFF_CB = OFF_CX + C_WIDTH
OFF_CC = OFF_CB + C_WIDTH
OFF_G = OFF_CC + C_WIDTH
IN_W = OFF_G + 3 * D_MODEL

LANES = 128
SUBLANES = 8
VMEM_LIMIT = 56 * 1024 * 1024
NEG = -1e30

SAMPLE_ROWS = DEC_BATCH * DEC_SEQ
WIN = min(MAX_WINDOW, SEQ)
SPAN = BAND * DILS[-1]
RES = DILS[1]
PROMPT_TM = 512
ROW_SUB = 256
INPROJ_SUB = ROW_SUB
INPROJ_LAG = 3
MERGE_FFN_LAG = 3
CAST_ROWS = (16, 16, 16, 32, 32, 32, 128)
WIN_CAST_ROWS = 32
SAMPLE_EB = 4
MXU_N = 256
FF_CHUNKS = ((0, 6 * MXU_N), (6 * MXU_N, D_FF))
LOG2E = 1.4426950408889634
QSCALE = LOG2E * HEAD_DIM ** -0.5

SLOPES = tuple(
    tuple(2.0 ** (-8.0 * (g * KV_HEADS + h + 1) / (N_DIL * KV_HEADS)) for h in range(KV_HEADS))
    for g in range(N_DIL)
)


def _rms(x, g):
    return x * lax.rsqrt(jnp.mean(x * x, axis=-1, keepdims=True) + EPS) * g


def _layer_norm(x, g, b):
    mu = jnp.mean(x, axis=-1, keepdims=True)
    xc = x - mu
    return xc * lax.rsqrt(jnp.mean(xc * xc, axis=-1, keepdims=True) + EPS) * g + b


def _dot(a, b):
    return jnp.dot(a, b, preferred_element_type=F32)


def _dot_nt(a, b):
    return lax.dot_general(a, b, (((1,), (1,)), ((), ())), preferred_element_type=F32)


def _select_rows(hid, vals):
    out = vals[-1]
    for h in range(len(vals) - 2, -1, -1):
        out = jnp.where(hid == h, vals[h], out)
    return out


def _layer_spec(layer, shape):
    nd = len(shape)
    return pl.BlockSpec((None,) + tuple(shape), lambda *_: (layer,) + (0,) * nd, pipeline_mode=pl.Buffered(1))


VEC_SLOTS = {"bg": (3 * D_MODEL, 0), "gpm": (D_MODEL, 3), "gqm": (D_MODEL, 4), "gpf": (D_MODEL, 5),
             "gqf": (D_MODEL, 6), "lng": (A_WIDTH, 14), "lnb": (A_WIDTH, 15), "cbias": (C_WIDTH, 16)}


def _vec_spec(layer, name):
    width, idx = VEC_SLOTS[name]
    return pl.BlockSpec((None, 1, width), lambda *_: (layer, 0, idx), pipeline_mode=pl.Buffered(1))


def _trace_staggered(stage_gens, lag):
    live = [True] * len(stage_gens)
    tick = 0
    while any(live):
        for j, gen in enumerate(stage_gens):
            if live[j] and tick >= j * lag:
                live[j] = next(gen, "done") != "done"
        tick += 1


def _resident(shape):
    nd = len(shape)
    return pl.BlockSpec(tuple(shape), lambda *_: (0,) * nd, pipeline_mode=pl.Buffered(1))


def _cast_specs(layer, shape, rows_per_step):
    last = shape[0] // rows_per_step - 1

    def piece(s):
        return jnp.minimum(jnp.maximum(s - 1, 0), last)

    return (pl.BlockSpec((None, rows_per_step, shape[1]), lambda s: (layer, piece(s), 0)),
            pl.BlockSpec((rows_per_step, shape[1]), lambda s: (piece(s), 0)))


def _params(sem):
    return pltpu.CompilerParams(dimension_semantics=sem, vmem_limit_bytes=VMEM_LIMIT)


def _inproj_prompt_kernel(x_ref, gpre_ref, w_ref, lng_ref, lnb_ref, ws_ref, bs_ref, cw_ref,
                          cbias_ref, bg_ref,
                          oa_ref, q_ref, k_ref, v_ref, kt_ref, vt_ref, oc_ref, gate_ref, cs_ref, prev_ref, *, tile,
                          side_stages=()):
    tm = x_ref.shape[0]

    @pl.when(tile == 0)
    def _():
        prev_ref[...] = jnp.zeros_like(prev_ref)

    row = lax.broadcasted_iota(jnp.int32, (CHUNK, CHUNK), 0)
    col = lax.broadcasted_iota(jnp.int32, (CHUNK, CHUNK), 1)
    wt = [jnp.where(row >= col, ws_ref[g], 0.0).astype(BF16) for g in range(A_GROUPS)]
    rsub = lax.broadcasted_iota(jnp.int32, (INPROJ_SUB, 1), 0)

    carry = {"tail": prev_ref[...]}

    def sub_tile(r0):
        rows = slice(r0, r0 + INPROJ_SUB)
        h = _rms(x_ref[rows, :], gpre_ref[...]).astype(BF16)

        def proj(off, width):
            return _dot(h, w_ref[:, off:off + width])

        yield
        u = jax.nn.gelu(proj(OFF_U, A_WIDTH))
        yield
        vn = _layer_norm(jax.nn.gelu(proj(OFF_V, A_WIDTH)), lng_ref[...], lnb_ref[...])
        for j in range(3):
            yield
            cs = slice(j * D_MODEL, (j + 1) * D_MODEL)
            gate_ref[rows, cs] = jax.nn.sigmoid(proj(OFF_G + j * D_MODEL, D_MODEL) + bg_ref[:, cs]).astype(BF16)
        yield
        for g in range(A_GROUPS):
            cs = slice(g * A_GROUP_DIM, (g + 1) * A_GROUP_DIM)
            for c in range(0, INPROJ_SUB // CHUNK, 2):
                pair = jnp.concatenate([vn[(c + j) * CHUNK:(c + j + 1) * CHUNK, cs] for j in range(2)], axis=1)
                sa2 = _dot(wt[g], pair.astype(BF16)) + bs_ref[:, g:g + 1]
                for j in range(2):
                    rs = slice((c + j) * CHUNK, (c + j + 1) * CHUNK)
                    sa = sa2[:, j * A_GROUP_DIM:(j + 1) * A_GROUP_DIM]
                    oa_ref[r0 + (c + j) * CHUNK:r0 + (c + j + 1) * CHUNK, cs] = (u[rs, cs] * sa).astype(BF16)
        yield
        cz = proj(OFF_CC, C_WIDTH) * proj(OFF_CX, C_WIDTH)
        p2 = carry["tail"][SUBLANES - 2:SUBLANES - 1, :]
        p1 = carry["tail"][SUBLANES - 1:SUBLANES, :]
        carry["tail"] = cz[INPROJ_SUB - SUBLANES:INPROJ_SUB, :]
        z1 = jnp.where(rsub == 0, p1, pltpu.roll(cz, 1, 0))
        z2 = jnp.where(rsub == 0, p2, jnp.where(rsub == 1, p1, pltpu.roll(cz, 2, 0)))
        y = cbias_ref[...] + cw_ref[0:1, :] * z2 + cw_ref[1:2, :] * z1 + cw_ref[2:3, :] * cz
        yield
        oc_ref[rows, :] = (proj(OFF_CB, C_WIDTH) * y).astype(BF16)
        yield
        qv = proj(OFF_Q, Q_W)
        for c in range(Q_W // LANES):
            q_ref[c, rows, :] = qv[:, c * LANES:(c + 1) * LANES] * QSCALE
        yield
        kv = proj(OFF_K, 2 * KV_W)
        for c in range(KV_W // LANES):
            k_ref[c, rows, :] = kv[:, c * LANES:(c + 1) * LANES]
            v_ref[c, rows, :] = kv[:, KV_W + c * LANES:KV_W + (c + 1) * LANES]
        kt_ref[0, :, rows] = kv[:, 0:KV_W].T
        vt_ref[0, :, rows] = kv[:, KV_W:2 * KV_W].T

    _trace_staggered([sub_tile(r0) for r0 in range(0, tm, INPROJ_SUB)] + list(side_stages), INPROJ_LAG)
    prev_ref[...] = carry["tail"]
    cs_ref[0] = carry["tail"]


def _attn_prompt_kernel(q_ref, k_ref, v_ref, o_ref, ktail, vtail, k4, v4, q24, acc, mrun, lrun, tmp, bias_ref):
    b = pl.program_id(0)
    i = pl.program_id(1)
    first = i == 0
    nslab = KV_W // LANES
    quarter = SPAN // RES

    @pl.when(jnp.logical_and(b == 0, first))
    def _():
        rr = lax.broadcasted_iota(jnp.int32, (KV_HEADS * BAND, 2 * BAND), 0)
        kk = lax.broadcasted_iota(jnp.int32, (KV_HEADS * BAND, 2 * BAND), 1)
        dist = (rr & (BAND - 1)) + BAND - kk
        hrow = rr >> 7
        ok = jnp.logical_and(dist >= 0, dist <= BAND)
        for g, dil in enumerate(DILS):
            coef = _select_rows(hrow, [jnp.float32(SLOPES[g][h] * dil * LOG2E) for h in range(KV_HEADS)])
            full = jnp.where(ok, -(coef * dist.astype(F32)), NEG)
            bias_ref[2 * g] = full
            bias_ref[2 * g + 1] = jnp.where(kk >= BAND, full, NEG)

    @pl.when(first)
    def _():
        for r in range(RES):
            base = r * 2 * quarter
            k4[:, base:base + quarter, :] = jnp.zeros((nslab, quarter, LANES), F32)
            v4[:, base:base + quarter, :] = jnp.zeros((nslab, quarter, LANES), F32)
        ktail[...] = jnp.zeros_like(ktail)
        vtail[...] = jnp.zeros_like(vtail)

    @pl.when(i > 0)
    def _():
        for r in range(RES):
            base = r * 2 * quarter
            k4[:, base:base + quarter, :] = k4[:, base + quarter:base + 2 * quarter, :]
            v4[:, base:base + quarter, :] = v4[:, base + quarter:base + 2 * quarter, :]

    for r in range(RES):
        base = r * 2 * quarter + quarter
        for c in range(nslab):
            k4[c, base:base + quarter, :] = k_ref[c, pl.ds(r, quarter, stride=RES), :]
            v4[c, base:base + quarter, :] = v_ref[c, pl.ds(r, quarter, stride=RES), :]
            q24[c, r * quarter:(r + 1) * quarter, :] = q_ref[2 * nslab + c, pl.ds(r, quarter, stride=RES), :]

    lane = lax.broadcasted_iota(jnp.int32, (1, KV_W), 1)
    hid = lane >> 6
    headmask = [jnp.where(hid == h, 1.0, 0.0).astype(BF16) for h in range(KV_HEADS)]

    def wide(ref, rows, slab0=0):
        return jnp.concatenate([ref[slab0 + c, rows, :] for c in range(nslab)], axis=1)

    def softmax_unit(qb, kb, vb, bias):
        qb16 = qb.astype(BF16)
        kb16 = kb.astype(BF16)
        vb16 = vb.astype(BF16)
        qs = jnp.concatenate([qb16 * headmask[h] for h in range(KV_HEADS)], axis=0)
        s = _dot_nt(qs, kb16)
        ps, ms, ls = [], [], []
        for h in range(KV_HEADS):
            sh = s[h * BAND:(h + 1) * BAND] + bias_ref[bias, h * BAND:(h + 1) * BAND, :]
            mh = jnp.max(sh, axis=-1, keepdims=True)
            ph = jnp.exp2(sh - mh)
            ps.append(ph.astype(BF16))
            ms.append(mh)
            ls.append(jnp.sum(ph, axis=-1, keepdims=True))
        r = _dot(jnp.concatenate(ps, axis=0), vb16)
        rs = [r[h * BAND:(h + 1) * BAND] for h in range(KV_HEADS)]
        return _select_rows(hid, rs), _select_rows(hid, ms), _select_rows(hid, ls)

    def store_stats(rows, o, m, l):
        for c in range(nslab):
            ls = slice(c * LANES, (c + 1) * LANES)
            acc[c, rows, :] = o[:, ls]
            mrun[c, rows, :] = m[:, ls]
            lrun[c, rows, :] = l[:, ls]

    def merge_stats(rows, o_u, m_u, l_u):
        m_old = wide(mrun, rows)
        m_n = jnp.maximum(m_old, m_u)
        a_old = jnp.exp2(m_old - m_n)
        a_new = jnp.exp2(m_u - m_n)
        store_stats(rows, wide(acc, rows) * a_old + o_u * a_new, m_n, wide(lrun, rows) * a_old + l_u * a_new)

    def unit0(n, kb, vb, bias):
        q0 = n * BAND if isinstance(n, int) else pl.multiple_of(n * BAND, BAND)
        stats = softmax_unit(wide(q_ref, pl.ds(q0, BAND)), kb, vb, bias)
        sub = BAND // RES
        for j, x in enumerate(stats):
            for c in range(nslab):
                tmp[j * nslab + c, pl.ds(q0, BAND), :] = x[:, c * LANES:(c + 1) * LANES]
        for r in range(RES):
            src = pl.ds(q0 + r, sub, stride=RES)
            dst = pl.ds(r * quarter + n * sub, sub)
            for c in range(nslab):
                acc[c, dst, :] = tmp[c, src, :]
                mrun[c, dst, :] = tmp[nslab + c, src, :]
                lrun[c, dst, :] = tmp[2 * nslab + c, src, :]

    kb0 = jnp.concatenate([wide(ktail, pl.ds(0, BAND)), wide(k_ref, pl.ds(0, BAND))], axis=0)
    vb0 = jnp.concatenate([wide(vtail, pl.ds(0, BAND)), wide(v_ref, pl.ds(0, BAND))], axis=0)
    unit0(0, kb0, vb0, jnp.where(first, 1, 0))

    def body0(n, carry):
        keys = pl.ds(pl.multiple_of((n - 1) * BAND, BAND), 2 * BAND)
        unit0(n, wide(k_ref, keys), wide(v_ref, keys), 0)
        return carry

    lax.fori_loop(1, SPAN // BAND, body0, 0, unroll=15)

    def body1(s, carry):
        bias = 2 + jnp.where(jnp.logical_and(first, s == 0), 1, 0)
        for r in range(RES):
            qb = wide(q_ref, pl.ds(s * (BAND * RES) + r, BAND, stride=RES), nslab)
            keys = pl.ds(r * 2 * quarter + quarter + (s - 1) * BAND, 2 * BAND)
            o_u, m_u, l_u = softmax_unit(qb, wide(k4, keys), wide(v4, keys), bias)
            merge_stats(pl.ds(r * quarter + s * BAND, BAND), o_u, m_u, l_u)
        return carry

    lax.fori_loop(0, quarter // BAND, body1, 0, unroll=2)

    def body2(r, carry):
        bias = 4 + jnp.where(first, 1, 0)
        for a in range(DILS[2] // RES):
            rows = pl.ds(r * quarter + a, BAND, stride=RES)
            keys = pl.ds(r * 2 * quarter + a, 2 * BAND, stride=RES)
            o_u, m_u, l_u = softmax_unit(wide(q24, rows), wide(k4, keys), wide(v4, keys), bias)
            merge_stats(rows, o_u, m_u, l_u)
        return carry

    lax.fori_loop(0, RES, body2, 0, unroll=2)

    for r in range(RES):
        rows = pl.ds(r * quarter, quarter)
        for c in range(nslab):
            tmp[c, pl.ds(r, quarter, stride=RES), :] = acc[c, rows, :] / lrun[c, rows, :]
    for c in range(nslab):
        o_ref[:, c * LANES:(c + 1) * LANES] = tmp[c].astype(BF16)
    ktail[...] = k_ref[:, SPAN - BAND:SPAN, :]
    vtail[...] = v_ref[:, SPAN - BAND:SPAN, :]


def _attn_prompt(q, k, v):
    ns = SEQ // SPAN
    nslab = KV_W // LANES
    rows = BATCH * SEQ

    def slab(n):
        return pl.BlockSpec((n, SPAN, LANES), lambda b, i: (0, b * ns + i, 0))

    return pl.pallas_call(
        _attn_prompt_kernel,
        grid=(BATCH, ns),
        in_specs=[slab(Q_W // LANES), slab(nslab), slab(nslab)],
        out_specs=pl.BlockSpec((SPAN, KV_W), lambda b, i: (b * ns + i, 0)),
        out_shape=jax.ShapeDtypeStruct((rows, KV_W), BF16),
        scratch_shapes=[
            pltpu.VMEM((nslab, BAND, LANES), F32),
            pltpu.VMEM((nslab, BAND, LANES), F32),
            pltpu.VMEM((nslab, 2 * SPAN, LANES), F32),
            pltpu.VMEM((nslab, 2 * SPAN, LANES), F32),
            pltpu.VMEM((nslab, SPAN, LANES), F32),
            pltpu.VMEM((nslab, SPAN, LANES), F32),
            pltpu.VMEM((nslab, SPAN, LANES), F32),
            pltpu.VMEM((nslab, SPAN, LANES), F32),
            pltpu.VMEM((3 * nslab, SPAN, LANES), F32),
            pltpu.VMEM((2 * N_DIL, KV_HEADS * BAND, 2 * BAND), F32),
        ],
        compiler_params=_params(("arbitrary", "arbitrary")),
        name="attn_prompt",
    )(q, k, v)


def _merge_ffn_kernel(xs_ref, oas_ref, obs_ref, ocs_ref, gates_ref, x_ref, oa_ref, ob_ref, oc_ref, gate_ref,
                      gqm_ref, gpf_ref, gqf_ref, wa_ref, wb_ref, wc_ref, wo_ref, wg_ref, wu_ref, wd_ref, *rest):
    os_ref, o_ref = rest[-3:-1] if len(rest) == 4 else rest
    def run(x_ref, oa_ref, ob_ref, oc_ref, gate_ref, o_ref):
        tm = x_ref.shape[0]
        sub = min(tm, ROW_SUB)

        def sub_tile(r0):
            rows = slice(r0, r0 + sub)

            def gate(j):
                return gate_ref[rows, j * D_MODEL:(j + 1) * D_MODEL].astype(F32)

            merged = gate(0) * _dot(oa_ref[rows, :], wa_ref[...])
            yield
            merged = merged + gate(1) * _dot(ob_ref[rows, :].astype(BF16), wb_ref[...])
            yield
            merged = merged + gate(2) * _dot(oc_ref[rows, :], wc_ref[...])
            yield
            x = x_ref[rows, :] + _rms(_dot(merged.astype(BF16), wo_ref[...]), gqm_ref[...])
            yield
            h = _rms(x, gpf_ref[...]).astype(BF16)
            y = None
            for lo, hi in FF_CHUNKS:
                yield
                a = jax.nn.silu(_dot(h, wg_ref[:, lo:hi]))
                yield
                act = (a * _dot(h, wu_ref[:, lo:hi])).astype(BF16)
                yield
                part = _dot(act, wd_ref[lo:hi, :])
                y = part if y is None else y + part
            yield
            o_ref[rows, :] = x + _rms(y, gqf_ref[...])

        _trace_staggered([sub_tile(r0) for r0 in range(0, tm, sub)], MERGE_FFN_LAG)

    step = pl.program_id(0)

    @pl.when(step == 0)
    def _():
        run(xs_ref, oas_ref, obs_ref, ocs_ref, gates_ref, os_ref)

    @pl.when(step > 0)
    def _():
        run(x_ref, oa_ref, ob_ref, oc_ref, gate_ref, o_ref)
        if len(rest) == 4:
            rest[3][...] = rest[0][...].astype(BF16)


def _merge_ffn(layer, xs, oa_s, ob_s, oc_s, gates_s, x, oa, ob, oc, gates, vecs, weights, w_in_f32):
    tm = PROMPT_TM
    rows = x.shape[0]
    ns = xs.shape[0]

    def rowblk(width):
        return pl.BlockSpec((tm, width), lambda i: (jnp.maximum(i - 1, 0), 0))

    def whole(width):
        return pl.BlockSpec((ns, width), lambda i: (0, 0), pipeline_mode=pl.Buffered(1))

    nxt = [] if w_in_f32 is None else [_cast_specs(layer + 1, (D_MODEL, IN_W), WIN_CAST_ROWS)]
    return pl.pallas_call(
        _merge_ffn_kernel,
        grid=(1 + rows // tm,),
        in_specs=[
            whole(D_MODEL), whole(A_WIDTH), whole(KV_W), whole(C_WIDTH), whole(3 * D_MODEL),
            rowblk(D_MODEL), rowblk(A_WIDTH), rowblk(KV_W), rowblk(C_WIDTH), rowblk(3 * D_MODEL),
            _vec_spec(layer, "gqm"), _vec_spec(layer, "gpf"), _vec_spec(layer, "gqf"),
        ] + [_resident(w.shape) for w in weights] + [c[0] for c in nxt],
        out_specs=[pl.BlockSpec((ns, D_MODEL), lambda i: (0, 0)), rowblk(D_MODEL)] + [c[1] for c in nxt],
        out_shape=[jax.ShapeDtypeStruct((ns, D_MODEL), F32), jax.ShapeDtypeStruct((rows, D_MODEL), F32)]
        + [jax.ShapeDtypeStruct((D_MODEL, IN_W), BF16) for _ in nxt],
        compiler_params=_params(("arbitrary",)),
        name="merge_ffn",
    )(xs, oa_s, ob_s, oc_s, gates_s, x, oa, ob, oc, gates, vecs, vecs, vecs, *weights,
      *([] if w_in_f32 is None else [w_in_f32]))


def _inproj_sample_kernel(ws_ref, bs_ref, x_ref, gpre_ref, w_ref, lng_ref, lnb_ref, cw_ref, cbias_ref, bg_ref,
                          p1_ref, p2_ref,
                          oa_ref, q_ref, k_ref, v_ref, oc_ref, gate_ref, vn_ref, cz_ref, *, layer):
    n = SAMPLE_ROWS
    h = _rms(x_ref[...], gpre_ref[...]).astype(BF16)

    def proj(off, width):
        return _dot(h, w_ref[:, off:off + width])

    u = jax.nn.gelu(proj(OFF_U, A_WIDTH))
    vn = _layer_norm(jax.nn.gelu(proj(OFF_V, A_WIDTH)), lng_ref[...], lnb_ref[...])
    vn_ref[...] = vn
    t = lax.broadcasted_iota(jnp.int32, (n, 1), 0) & (DEC_SEQ - 1)
    for g in range(A_GROUPS):
        cs = slice(g * A_GROUP_DIM, (g + 1) * A_GROUP_DIM)
        base = (layer * A_GROUPS + g) * DEC_SEQ
        vg = vn[:, cs]
        sa = _select_rows(t, [bs_ref[base + tt] for tt in range(DEC_SEQ)])
        for k in range(DEC_SEQ):
            coef = jnp.zeros((n, 1), F32)
            for tt in range(k, DEC_SEQ):
                coef = jnp.where(t == tt, ws_ref[(base + tt) * DEC_SEQ + tt - k], coef)
            sa = sa + coef * (vg if k == 0 else pltpu.roll(vg, k, 0))
        oa_ref[:, cs] = (u[:, cs] * sa).astype(BF16)

    q_ref[...] = proj(OFF_Q, Q_W)
    kv = proj(OFF_K, 2 * KV_W)
    k_ref[...] = kv[:, 0:KV_W]
    v_ref[...] = kv[:, KV_W:2 * KV_W]

    cz = proj(OFF_CC, C_WIDTH) * proj(OFF_CX, C_WIDTH)
    cz_ref[...] = cz
    z1 = jnp.where(t >= 1, pltpu.roll(cz, 1, 0), p1_ref[...])
    z2 = jnp.where(t >= 2, pltpu.roll(cz, 2, 0), p2_ref[...])
    y = cbias_ref[...] + cw_ref[0:1, :] * z2 + cw_ref[1:2, :] * z1 + cw_ref[2:3, :] * cz
    oc_ref[...] = (proj(OFF_CB, C_WIDTH) * y).astype(BF16)

    for j in range(3):
        cs = slice(j * D_MODEL, (j + 1) * D_MODEL)
        gate_ref[:, cs] = jax.nn.sigmoid(proj(OFF_G + j * D_MODEL, D_MODEL) + bg_ref[:, cs]).astype(BF16)


def _inproj_kernel(ws4_ref, bs4_ref, xs_ref, p1_ref, p2_ref, x_ref, gpre_ref, w_ref, lng_ref, lnb_ref, ws_ref,
                   bst_ref, cw_ref, cbias_ref, bg_ref, *rest, layer):
    nw = len(CAST_ROWS)
    f32_pieces, rest = rest[:nw], rest[nw:]
    oa_s, q_s, k_s, v_s, oc_s, gate_s, vn_s, cz_s = rest[:8]
    oa_ref, q_ref, k_ref, v_ref, kt_ref, vt_ref, oc_ref, gate_ref, cs_ref = rest[8:17]
    bf16_pieces, prev_ref = rest[17:17 + nw], rest[17 + nw]
    step = pl.program_id(0)

    @pl.when(step == 0)
    def _():
        _inproj_sample_kernel(ws4_ref, bs4_ref, xs_ref, gpre_ref, w_ref, lng_ref, lnb_ref, cw_ref, cbias_ref, bg_ref,
                              p1_ref, p2_ref, oa_s, q_s, k_s, v_s, oc_s, gate_s, vn_s, cz_s, layer=layer)

    def casts():
        for src_ref, dst_ref in zip(f32_pieces, bf16_pieces):
            yield
            dst_ref[...] = src_ref[...].astype(BF16)

    @pl.when(step > 0)
    def _():
        _inproj_prompt_kernel(x_ref, gpre_ref, w_ref, lng_ref, lnb_ref, ws_ref, bst_ref, cw_ref, cbias_ref, bg_ref,
                              oa_ref, q_ref, k_ref, v_ref, kt_ref, vt_ref, oc_ref, gate_ref, cs_ref, prev_ref,
                              tile=(step - 1) % (SEQ // PROMPT_TM), side_stages=[casts()])


def _inproj(layer, ws4, bs4, xs, p1, p2, x, vecs, w_in, ws, bs_t, cw, f32_weights):
    tm = PROMPT_TM
    nt = SEQ // tm
    rows = BATCH * SEQ
    n = SAMPLE_ROWS
    smem = pl.BlockSpec(memory_space=pltpu.SMEM)

    def tile_of(s):
        return jnp.maximum(s - 1, 0)

    def rowblk(width):
        return pl.BlockSpec((tm, width), lambda s: (tile_of(s), 0))

    def slab(k):
        return pl.BlockSpec((k, tm, LANES), lambda s: (0, tile_of(s), 0))

    def whole(width):
        return pl.BlockSpec((n, width), lambda s: (0, 0))

    win_spec = pl.BlockSpec((1, KV_W, tm),
                            lambda s: (tile_of(s) // nt, 0, jnp.maximum(tile_of(s) % nt - (nt - WIN // tm), 0)))
    sample_out = [(A_WIDTH, BF16), (Q_W, F32), (KV_W, F32), (KV_W, F32), (C_WIDTH, BF16),
                  (3 * D_MODEL, BF16), (A_WIDTH, F32), (C_WIDTH, F32)]
    cast = [_cast_specs(layer, w.shape[1:], r) for w, r in zip(f32_weights, CAST_ROWS)]
    return pl.pallas_call(
        functools.partial(_inproj_kernel, layer=layer),
        grid=(1 + BATCH * nt,),
        in_specs=[
            smem, smem,
            pl.BlockSpec((n, D_MODEL), lambda s: (0, 0), pipeline_mode=pl.Buffered(1)),
            _layer_spec(layer, (n, C_WIDTH)), _layer_spec(layer, (n, C_WIDTH)),
            rowblk(D_MODEL),
            _vec_spec(layer, "gpm"),
            _resident((D_MODEL, IN_W)),
            _vec_spec(layer, "lng"),
            _vec_spec(layer, "lnb"),
            _layer_spec(layer, (A_GROUPS, CHUNK, CHUNK)),
            _layer_spec(layer, (CHUNK, A_GROUPS)),
            _layer_spec(layer, (CONV_W, C_WIDTH)),
            _vec_spec(layer, "cbias"),
            _vec_spec(layer, "bg"),
        ] + [c[0] for c in cast],
        out_specs=[whole(w) for w, _ in sample_out] + [
            rowblk(A_WIDTH),
            slab(Q_W // LANES),
            slab(KV_W // LANES),
            slab(KV_W // LANES),
            win_spec,
            win_spec,
            rowblk(C_WIDTH),
            rowblk(3 * D_MODEL),
            pl.BlockSpec((1, SUBLANES, C_WIDTH), lambda s: (tile_of(s) // nt, 0, 0)),
        ] + [c[1] for c in cast],
        out_shape=[jax.ShapeDtypeStruct((n, w), dt) for w, dt in sample_out] + [
            jax.ShapeDtypeStruct((rows, A_WIDTH), BF16),
            jax.ShapeDtypeStruct((Q_W // LANES, rows, LANES), F32),
            jax.ShapeDtypeStruct((KV_W // LANES, rows, LANES), F32),
            jax.ShapeDtypeStruct((KV_W // LANES, rows, LANES), F32),
            jax.ShapeDtypeStruct((BATCH, KV_W, WIN), F32),
            jax.ShapeDtypeStruct((BATCH, KV_W, WIN), F32),
            jax.ShapeDtypeStruct((rows, C_WIDTH), BF16),
            jax.ShapeDtypeStruct((rows, 3 * D_MODEL), BF16),
            jax.ShapeDtypeStruct((BATCH, SUBLANES, C_WIDTH), F32),
        ] + [jax.ShapeDtypeStruct(w.shape[1:], BF16) for w in f32_weights],
        scratch_shapes=[pltpu.VMEM((SUBLANES, C_WIDTH), F32)],
        compiler_params=_params(("arbitrary",)),
        name="inproj",
    )(ws4, bs4, xs, p1, p2, x, vecs, w_in, vecs, vecs, ws, bs_t, cw, vecs, vecs, *f32_weights)


def _attn_sample_kernel(qh_ref, kn_ref, vn_ref, kt_ref, vt_ref, o_ref, bias_ref):
    nr = N_DIL * SUBLANES
    width = MAX_WINDOW

    @pl.when(pl.program_id(0) == 0)
    def _():
        row = lax.broadcasted_iota(jnp.int32, (nr, width), 0)
        pos = lax.broadcasted_iota(jnp.int32, (nr, width), 1)
        grp = row >> 3
        delta = width + (row & (SUBLANES - 1)) - pos
        dil_m1 = _select_rows(grp, [d - 1 for d in DILS])
        reach = _select_rows(grp, [BAND * d for d in DILS])
        ok = jnp.logical_and((delta & dil_m1) == 0, delta <= reach)
        for h in range(KV_HEADS):
            slope = _select_rows(grp, [jnp.float32(SLOPES[g][h]) for g in range(N_DIL)])
            bias_ref[h * nr:(h + 1) * nr, :] = jnp.where(ok, -(slope * delta.astype(F32)), NEG)

    ridx = lax.broadcasted_iota(jnp.int32, (nr, 1), 0)
    g_of = ridx >> 3
    t_of = ridx & (SUBLANES - 1)
    scale = HEAD_DIM ** -0.5
    slopes = [_select_rows(g_of, [jnp.float32(SLOPES[g][h]) for g in range(N_DIL)]) for h in range(KV_HEADS)]
    for e in range(qh_ref.shape[0]):
        qs = [qh_ref[e, h] * scale for h in range(KV_HEADS)]
        s = jnp.concatenate([_dot(qs[h].astype(BF16), kt_ref[0, e, h].astype(BF16)) for h in range(KV_HEADS)],
                            axis=0)
        s = s + bias_ref[...]
        m = jnp.max(s, axis=-1, keepdims=True)
        s_new = []
        for n in range(DEC_SEQ):
            dn = t_of - n
            ok = jnp.logical_or(dn == 0, jnp.logical_and(g_of == 0, dn > 0))
            rows = []
            for h in range(KV_HEADS):
                raw = jnp.sum(qs[h] * kn_ref[e, h, n:n + 1, :], axis=-1, keepdims=True)
                rows.append(jnp.where(ok, raw - slopes[h] * dn.astype(F32), NEG))
            sn = jnp.concatenate(rows, axis=0)
            s_new.append(sn)
            m = jnp.maximum(m, sn)
        p = jnp.exp(s - m)
        l_all = jnp.sum(p, axis=-1, keepdims=True)
        p16 = p.astype(BF16)
        p_new = [jnp.exp(sn - m) for sn in s_new]
        for pn in p_new:
            l_all = l_all + pn
        for h in range(KV_HEADS):
            hr = slice(h * nr, (h + 1) * nr)
            r = _dot_nt(p16[hr], vt_ref[0, e, h].astype(BF16))
            for n in range(DEC_SEQ):
                r = r + p_new[n][hr] * vn_ref[e, h, n:n + 1, :]
            m_h = m[hr]
            l_h = l_all[hr]
            parts = [(r[g * SUBLANES:(g + 1) * SUBLANES], m_h[g * SUBLANES:(g + 1) * SUBLANES],
                      l_h[g * SUBLANES:(g + 1) * SUBLANES]) for g in range(N_DIL)]
            m_all = jnp.maximum(jnp.maximum(parts[0][1], parts[1][1]), parts[2][1])
            num = jnp.zeros((SUBLANES, HEAD_DIM), F32)
            den = jnp.zeros((SUBLANES, 1), F32)
            for o_g, m_g, l_g in parts:
                w = jnp.exp(m_g - m_all)
                num = num + o_g * w
                den = den + l_g * w
            o_ref[e, :, h * HEAD_DIM:(h + 1) * HEAD_DIM] = (num / den)[0:DEC_SEQ]


def _attn_sample(layer, qh, knh, vnh, ckt, cvt):
    nr = N_DIL * SUBLANES
    eb = SAMPLE_EB
    cache_spec = pl.BlockSpec((1, eb, KV_HEADS, HEAD_DIM, MAX_WINDOW), lambda b: (layer, b, 0, 0, 0))
    new_spec = pl.BlockSpec((eb, KV_HEADS, SUBLANES, HEAD_DIM), lambda b: (b, 0, 0, 0))
    return pl.pallas_call(
        _attn_sample_kernel,
        grid=(DEC_BATCH // eb,),
        in_specs=[pl.BlockSpec((eb, KV_HEADS, nr, HEAD_DIM), lambda b: (b, 0, 0, 0)),
                  new_spec, new_spec, cache_spec, cache_spec],
        out_specs=pl.BlockSpec((eb, DEC_SEQ, KV_W), lambda b: (b, 0, 0)),
        out_shape=jax.ShapeDtypeStruct((DEC_BATCH, DEC_SEQ, KV_W), F32),
        scratch_shapes=[pltpu.VMEM((KV_HEADS * nr, MAX_WINDOW), F32)],
        compiler_params=_params(("arbitrary",)),
        name="attn_sample",
    )(qh, knh, vnh, ckt, cvt)


def kernel(x_prompt, x_sample, cache_k_win, cache_v_win, state_conv, g_pre_mix, g_post_mix, g_pre_ffn, g_post_ffn,
           w_in, a_ln_g, a_ln_b, a_ws, a_bs, c_conv_w, c_conv_b, w_br_a, w_br_b, w_br_c, b_gate, w_o,
           w_ff_gate, w_ff_up, w_ff_down):
    xp = x_prompt.reshape(BATCH * SEQ, D_MODEL)
    xs = x_sample.reshape(SAMPLE_ROWS, D_MODEL)
    ckt = jnp.transpose(cache_k_win, (0, 1, 3, 4, 2))
    cvt = jnp.transpose(cache_v_win, (0, 1, 3, 4, 2))

    w_in_b = w_in[0].astype(BF16)
    f32_weights = (w_br_a, w_br_b, w_br_c, w_o, w_ff_gate, w_ff_up, w_ff_down)
    vecs = jnp.concatenate([b_gate, g_pre_mix, g_post_mix, g_pre_ffn, g_post_ffn, a_ln_g, a_ln_b, c_conv_b],
                           axis=1).reshape(DEPTH, 1, -1)
    bs_t = jnp.swapaxes(a_bs, 1, 2)
    ws4 = a_ws[:, :, :DEC_SEQ, :DEC_SEQ].reshape(-1)
    bs4 = a_bs[:, :, :DEC_SEQ].reshape(-1)
    zeros = jnp.zeros((DEPTH, DEC_BATCH, 1, C_WIDTH), F32)
    p1 = jnp.concatenate([state_conv[:, :, 1:2], zeros, zeros, zeros], axis=2).reshape(DEPTH, SAMPLE_ROWS, C_WIDTH)
    p2 = jnp.concatenate([state_conv, zeros, zeros], axis=2).reshape(DEPTH, SAMPLE_ROWS, C_WIDTH)

    kp_l, vp_l, ks_l, vs_l, cp_l, cs_l, av_l = [], [], [], [], [], [], []
    for l in range(DEPTH):
        outs = _inproj(l, ws4, bs4, xs, p1, p2, xp, vecs, w_in_b, a_ws, bs_t, c_conv_w, f32_weights)
        oa_s, q_s, k_s, v_s, oc_s, gates_s, vn_s, cz_s, oa, q, k, v, kt, vt, oc, gates, cs8 = outs[:17]
        weights = outs[17:]
        ob = _attn_prompt(q, k, v)

        def window(t):
            return jnp.transpose(t.reshape(BATCH, KV_HEADS, HEAD_DIM, WIN), (0, 3, 1, 2))

        kp_l.append(window(kt))
        vp_l.append(window(vt))
        cp_l.append(cs8[:, SUBLANES - (CONV_W - 1):])


        pad_t = ((0, 0), (0, 0), (0, 0), (0, SUBLANES - DEC_SEQ), (0, 0))
        qh = q_s.reshape(DEC_BATCH, DEC_SEQ, N_DIL, KV_HEADS, HEAD_DIM).transpose(0, 3, 2, 1, 4)
        qh = jnp.pad(qh, pad_t).reshape(DEC_BATCH, KV_HEADS, N_DIL * SUBLANES, HEAD_DIM)

        def new_rows(a):
            a = a.reshape(DEC_BATCH, DEC_SEQ, KV_HEADS, HEAD_DIM).transpose(0, 2, 1, 3)
            return jnp.pad(a, pad_t[1:])

        ob_s = _attn_sample(l, qh, new_rows(k_s), new_rows(v_s), ckt, cvt)
        ob_s = ob_s.reshape(SAMPLE_ROWS, KV_W)
        last = l + 1 == DEPTH
        res = _merge_ffn(l, xs, oa_s, ob_s, oc_s, gates_s, xp, oa, ob, oc, gates, vecs, weights,
                         None if last else w_in)
        xs, xp = res[0], res[1]
        if not last:
            w_in_b = res[2]

        ks_l.append(k_s.reshape(DEC_BATCH, DEC_SEQ, KV_HEADS, HEAD_DIM))
        vs_l.append(v_s.reshape(DEC_BATCH, DEC_SEQ, KV_HEADS, HEAD_DIM))
        cs_l.append(cz_s.reshape(DEC_BATCH, DEC_SEQ, C_WIDTH)[:, DEC_SEQ - (CONV_W - 1):])
        av_l.append(vn_s.reshape(DEC_BATCH, DEC_SEQ, A_WIDTH))

    return (xp.reshape(BATCH, SEQ, D_MODEL), xs.reshape(DEC_BATCH, DEC_SEQ, D_MODEL),
            jnp.stack(kp_l), jnp.stack(vp_l), jnp.stack(ks_l), jnp.stack(vs_l),
            jnp.stack(cp_l), jnp.stack(cs_l), jnp.stack(av_l))
```
